```python
import jax, jax.numpy as jnp
from jax import lax
import numpy as np

D_MODEL = 1024
BATCH = 2
SEQ = 8192
DEPTH = 2
DEC_BATCH = 128
DEC_SEQ = 1
PAST_LEN = 2048
PAGE_SIZE = 128

HEAD_DIM = 64
NSA_HEADS = 8
NSA_KV = 2
CMP_LEN = 32
CMP_STRIDE = 16
CMP_HID = 128
SEL_BLOCK = 64
SEL_TOPN = 16
WINDOW = 512
MOBA_HEADS = 4
MOBA_BLOCK = 256
MOBA_TOPK = 3
SB_HEADS = 4
D_FF = 2816
CONV_W = 3

QBLK = 128
RMS_EPS = 1e-6
NEG = -1e30
FORCE = 1e9

NSA_QW = NSA_HEADS * HEAD_DIM
NSA_KVW = 6 * NSA_KV * HEAD_DIM
NSA_GW = 3 * NSA_HEADS
MOBA_W = MOBA_HEADS * HEAD_DIM
SB_W = SB_HEADS * HEAD_DIM
SPLITS = (NSA_QW,
          NSA_QW + NSA_KVW,
          NSA_QW + NSA_KVW + NSA_GW,
          NSA_QW + NSA_KVW + NSA_GW + 3 * MOBA_W,
          NSA_QW + NSA_KVW + NSA_GW + 3 * MOBA_W + 3 * SB_W)
PROJ_OUT = SPLITS[-1] + 3 * D_MODEL

kernel_name = "nsa_moba_stickbreak_hybrid_decode_step"


def rmsnorm(x, g):
    xf = x.astype(jnp.float32)
    y = xf * lax.rsqrt(jnp.mean(xf * xf, axis=-1, keepdims=True) + RMS_EPS)
    return (y * g.astype(jnp.float32)).astype(x.dtype)


def alibi_slopes(n):
    return jnp.asarray(2.0 ** (-8.0 * (np.arange(n) + 1) / n), jnp.float32)


def masked_softmax(s, mask, axis):
    s = jnp.where(mask, s, NEG)
    e = jnp.where(mask, jnp.exp(s - jnp.max(s, axis=axis, keepdims=True)), 0.0)
    return e / jnp.maximum(jnp.sum(e, axis=axis, keepdims=True), 1e-30)


def sweep_queries(block_fn, tq, *q_arrays):
    qb = QBLK if tq % QBLK == 0 else tq
    nblk = tq // qb
    if nblk == 1:
        return block_fn(jnp.int32(0), *q_arrays)
    blocks = tuple(jnp.moveaxis(a.reshape(a.shape[0], nblk, qb, *a.shape[2:]), 1, 0) for a in q_arrays)
    starts = jnp.arange(nblk, dtype=jnp.int32) * qb
    out = lax.map(lambda args: block_fn(args[0], *args[1:]), (starts, *blocks))
    out = jnp.moveaxis(out, 0, 1)
    return out.reshape(out.shape[0], tq, *out.shape[3:])


def nsa_compress(rows, pe, w1, w2):
    B, L, G, dh = rows.shape
    n_cmp = (L - CMP_LEN) // CMP_STRIDE + 1
    idx = np.arange(n_cmp)[:, None] * CMP_STRIDE + np.arange(CMP_LEN)[None, :]
    blk = rows[:, idx] + pe.astype(jnp.float32)[:, None, :]
    flat = jnp.moveaxis(blk, 3, 2).reshape(B, n_cmp, G, CMP_LEN * dh)
    return jax.nn.gelu(flat @ w1.astype(jnp.float32)) @ w2.astype(jnp.float32)


def cmp_to_sel(n_cmp, n_sel):
    c0 = np.arange(n_cmp)[:, None] * CMP_STRIDE
    s0 = np.arange(n_sel)[None, :] * SEL_BLOCK
    ov = np.clip(np.minimum(c0 + CMP_LEN, s0 + SEL_BLOCK) - np.maximum(c0, s0), 0, None)
    return jnp.asarray(ov / CMP_LEN, jnp.float32)


def nsa_attention(q, gates, kv_full, win_full, q_pos0, win_pos0, pe, w1, w2):
    f32 = jnp.float32
    B, Tq, H, dh = q.shape
    L, G = kv_full.shape[1], kv_full.shape[3]
    hpg = H // G
    scale = dh ** -0.5
    slopes = alibi_slopes(H).reshape(G, hpg)
    kv_full = kv_full.astype(f32)
    kcmp = nsa_compress(kv_full[:, :, 0], pe[0], w1[0], w2[0])
    vcmp = nsa_compress(kv_full[:, :, 1], pe[1], w1[1], w2[1])
    n_cmp = kcmp.shape[1]
    cmp_end = jnp.asarray(np.arange(n_cmp) * CMP_STRIDE + CMP_LEN - 1, jnp.int32)
    n_sel = -(-L // SEL_BLOCK)
    agg = cmp_to_sel(n_cmp, n_sel)
    pad = ((0, 0), (0, n_sel * SEL_BLOCK - L), (0, 0), (0, 0))
    ksb = jnp.pad(kv_full[:, :, 2], pad).reshape(B, n_sel, SEL_BLOCK, G, dh).transpose(0, 3, 1, 2, 4)
    vsb = jnp.pad(kv_full[:, :, 3], pad).reshape(B, n_sel, SEL_BLOCK, G, dh).transpose(0, 3, 1, 2, 4)
    n_top = min(SEL_TOPN, n_sel)
    wpad = jnp.pad(win_full.astype(f32), ((0, 0), (WINDOW, 0), (0, 0), (0, 0), (0, 0)))
    bi = jnp.arange(B)[:, None, None, None]
    gi = jnp.arange(G)[None, None, :, None]
    blk_ids = jnp.arange(n_sel, dtype=jnp.int32)[None, :]
    r_sel = jnp.arange(SEL_BLOCK, dtype=jnp.int32)

    def block_fn(st, qblk, gblk):
        qb = qblk.shape[1]
        t0 = q_pos0 + st
        t = t0 + jnp.arange(qb, dtype=jnp.int32)
        qg = qblk.astype(f32).reshape(B, qb, G, hpg, dh) * scale
        sl = slopes[None, None, :, :, None]
        d_c = (t[:, None] - cmp_end[None, :])[None, :, None, None, :]
        s_c = jnp.einsum('bqghd,bngd->bqghn', qg, kcmp) - sl * d_c.astype(f32)
        p_c = masked_softmax(s_c, d_c >= 0, axis=-1)
        o_c = jnp.einsum('bqghn,bngd->bqghd', p_c, vcmp)
        imp = jnp.einsum('bqgn,nj->bqgj', jnp.sum(p_c, axis=3), agg)
        cur = (t // SEL_BLOCK)[:, None]
        forced = ((blk_ids == 0) | (blk_ids == cur) | (blk_ids == cur - 1))[None, :, None, :]
        causal_blk = (blk_ids <= cur)[None, :, None, :]
        score = jnp.where(causal_blk, jnp.where(forced, FORCE, imp), NEG)
        top_v, top_i = lax.top_k(score, n_top)
        kg = ksb[bi, gi, top_i]
        vg = vsb[bi, gi, top_i]
        d_s = t[None, :, None, None, None] - (top_i[..., None] * SEL_BLOCK + r_sel)
        m_s = ((top_v > NEG / 2)[..., None] & (d_s >= 0))[:, :, :, None]
        s_s = jnp.einsum('bqghd,bqgkrd->bqghkr', qg, kg) - sl[..., None] * d_s[:, :, :, None].astype(f32)
        p_s = masked_softmax(s_s, m_s, axis=(-2, -1))
        o_s = jnp.einsum('bqghkr,bqgkrd->bqghd', p_s, vg)
        wwin = lax.dynamic_slice_in_dim(wpad, t0 - win_pos0, WINDOW + qb, axis=1)
        spos = t0 - WINDOW + jnp.arange(WINDOW + qb, dtype=jnp.int32)
        d_w = t[:, None] - spos[None, :]
        m_w = ((d_w >= 0) & (d_w < WINDOW) & (spos >= win_pos0)[None, :])[None, :, None, None, :]
        s_w = jnp.einsum('bqghd,bsgd->bqghs', qg, wwin[:, :, 0]) - sl * d_w[None, :, None, None, :].astype(f32)
        p_w = masked_softmax(s_w, m_w, axis=-1)
        o_w = jnp.einsum('bqghs,bsgd->bqghd', p_w, wwin[:, :, 1])
        g = jax.nn.sigmoid(gblk.astype(f32)).reshape(B, qb, 3, G, hpg)[..., None]
        o = g[:, :, 0] * o_c + g[:, :, 1] * o_s + g[:, :, 2] * o_w
        return o.reshape(B, qb, H * dh)

    return sweep_queries(block_fn, Tq, q, gates).astype(q.dtype)


def moba_attention(q, k, v, q_pos0):
    f32 = jnp.float32
    B, L, H, dh = k.shape
    scale = dh ** -0.5
    slopes = alibi_slopes(H)
    nb = -(-L // MOBA_BLOCK)
    pad = ((0, 0), (0, nb * MOBA_BLOCK - L), (0, 0), (0, 0))
    kb = jnp.pad(k.astype(f32), pad).reshape(B, nb, MOBA_BLOCK, H, dh).transpose(0, 3, 1, 2, 4)
    vb = jnp.pad(v.astype(f32), pad).reshape(B, nb, MOBA_BLOCK, H, dh).transpose(0, 3, 1, 2, 4)
    kmean = jnp.mean(kb, axis=3)
    n_top = min(MOBA_TOPK, nb)
    bi = jnp.arange(B)[:, None, None, None]
    hi = jnp.arange(H)[None, None, :, None]
    blk_ids = jnp.arange(nb, dtype=jnp.int32)
    r = jnp.arange(MOBA_BLOCK, dtype=jnp.int32)

    def block_fn(st, qblk):
        qb = qblk.shape[1]
        t = q_pos0 + st + jnp.arange(qb, dtype=jnp.int32)
        own = t // MOBA_BLOCK
        qf = qblk.astype(f32) * scale
        gate = jnp.einsum('bqhd,bhnd->bqhn', qf, kmean)
        past = (blk_ids[None, :] < own[:, None])[None, :, None, :]
        top_v, top_i = lax.top_k(jnp.where(past, gate, NEG), n_top)
        own_i = jnp.broadcast_to(own[None, :, None, None], (B, qb, H, 1)).astype(top_i.dtype)
        idx = jnp.concatenate([top_i, own_i], axis=-1)
        ok = jnp.concatenate([top_v > NEG / 2, jnp.ones((B, qb, H, 1), bool)], axis=-1)
        kg = kb[bi, hi, idx]
        vg = vb[bi, hi, idx]
        dist = t[None, :, None, None, None] - (idx[..., None] * MOBA_BLOCK + r)
        s = jnp.einsum('bqhd,bqhkrd->bqhkr', qf, kg) - slopes[None, None, :, None, None] * dist.astype(f32)
        p = masked_softmax(s, ok[..., None] & (dist >= 0), axis=(-2, -1))
        o = jnp.einsum('bqhkr,bqhkrd->bqhd', p, vg)
        return o.reshape(B, qb, H * dh)

    return sweep_queries(block_fn, q.shape[1], q).astype(q.dtype)


def stick_breaking_attention(q, k, v, q_pos0):
    f32 = jnp.float32
    B, L, H, dh = k.shape
    scale = dh ** -0.5
    kf, vf = k.astype(f32), v.astype(f32)
    kpos = jnp.arange(L, dtype=jnp.int32)

    def block_fn(st, qblk):
        qb = qblk.shape[1]
        t = q_pos0 + st + jnp.arange(qb, dtype=jnp.int32)
        z = jnp.einsum('bqhd,bshd->bhqs', qblk.astype(f32) * scale, kf)
        past = (kpos[None, :] < t[:, None])[None, None]
        log_keep = jnp.where(past, jax.nn.log_sigmoid(-z), 0.0)
        between = lax.cumsum(log_keep, axis=3, reverse=True) - log_keep
        w = jnp.where(past, jnp.exp(jax.nn.log_sigmoid(z) + between), 0.0)
        o = jnp.einsum('bhqs,bshd->bqhd', w, vf)
        return o.reshape(B, qb, H * dh)

    return sweep_queries(block_fn, q.shape[1], q).astype(q.dtype)


def token_mixers(h, q_pos0, past, w_in, cmp_pe, cmp_w1, cmp_w2, w_br_a, w_br_b, w_br_c, w_o):
    B, T, _ = h.shape
    q_n, kv_n, g_n, mb, sb, g_m = jnp.split(h @ w_in, SPLITS, axis=-1)
    q_n = q_n.reshape(B, T, NSA_HEADS, HEAD_DIM)
    kv_n = kv_n.reshape(B, T, 6, NSA_KV, HEAD_DIM)
    nsa_rows, win_rows = kv_n[:, :, :4], kv_n[:, :, 4:]
    g_n = g_n.reshape(B, T, 3, NSA_HEADS)
    mb = mb.reshape(B, T, 3, MOBA_HEADS, HEAD_DIM)
    sb = sb.reshape(B, T, 3, SB_HEADS, HEAD_DIM)
    moba_rows, sb_rows = mb[:, :, 1:], sb[:, :, 1:]
    if past is None:
        nsa_full, moba_full, sb_full, win_full = nsa_rows, moba_rows, sb_rows, win_rows
        win_pos0, keep = 0, WINDOW
    else:
        nsa_past, moba_past, sb_past, win_buf = past
        nsa_full = jnp.concatenate([nsa_past.astype(h.dtype), nsa_rows], axis=1)
        moba_full = jnp.concatenate([moba_past.astype(h.dtype), moba_rows], axis=1)
        sb_full = jnp.concatenate([sb_past.astype(h.dtype), sb_rows], axis=1)
        win_full = jnp.concatenate([win_buf.astype(h.dtype), win_rows], axis=1)
        win_pos0, keep = q_pos0 - win_buf.shape[1], win_buf.shape[1]
    o_a = nsa_attention(q_n, g_n, nsa_full, win_full, q_pos0, win_pos0, cmp_pe, cmp_w1, cmp_w2)
    o_b = moba_attention(mb[:, :, 0], moba_full[:, :, 0], moba_full[:, :, 1], q_pos0)
    o_c = stick_breaking_attention(sb[:, :, 0], sb_full[:, :, 0], sb_full[:, :, 1], q_pos0)
    g = jax.nn.sigmoid(g_m.astype(jnp.float32)).reshape(B, T, 3, D_MODEL).astype(h.dtype)
    merged = g[:, :, 0] * (o_a @ w_br_a) + g[:, :, 1] * (o_b @ w_br_b) + g[:, :, 2] * (o_c @ w_br_c)
    new_win = win_full[:, win_full.shape[1] - min(keep, win_full.shape[1]):]
    return merged @ w_o, (nsa_rows, moba_rows, sb_rows, new_win)


def conv_ffn(h, conv_prev, w_up, conv_w, conv_b, w_down):
    T = h.shape[1]
    a, b = jnp.split(h @ w_up, 2, axis=-1)
    a_ext = jnp.concatenate([conv_prev.astype(a.dtype), a], axis=1)
    conv = conv_b
    for i in range(CONV_W):
        conv = conv + a_ext[:, i:i + T] * conv_w[i]
    y = (jax.nn.gelu(conv) * b) @ w_down
    return y, a_ext[:, a_ext.shape[1] - (CONV_W - 1):]


def setup_inputs(seed: int = 0) -> dict:
    key = jax.random.key(seed)
    ks = jax.random.split(key, 32)

    def nrm(i, shape, s=1.0):
        return s * jax.random.normal(ks[i], shape, jnp.float32)

    D = D_MODEL
    n_pages = PAST_LEN // PAGE_SIZE
    used = DEC_BATCH * n_pages
    n_phys = used + max(1, used // 4)
    win_buf = min(WINDOW, PAST_LEN)
    page_table = jax.random.permutation(ks[0], n_phys)[:used].reshape(DEC_BATCH, n_pages).astype(jnp.int32)
    return {
        "x_prompt": nrm(1, (BATCH, SEQ, D)),
        "x_sample": nrm(2, (DEC_BATCH, DEC_SEQ, D)),
        "cache_nsa": nrm(3, (DEPTH, n_phys, PAGE_SIZE, 4, NSA_KV, HEAD_DIM)),
        "cache_moba": nrm(4, (DEPTH, n_phys, PAGE_SIZE, 2, MOBA_HEADS, HEAD_DIM)),
        "cache_sb": nrm(5, (DEPTH, n_phys, PAGE_SIZE, 2, SB_HEADS, HEAD_DIM)),
        "state_win": nrm(6, (DEPTH, DEC_BATCH, win_buf, 2, NSA_KV, HEAD_DIM)),
        "state_conv": nrm(7, (DEPTH, DEC_BATCH, CONV_W - 1, D_FF)),
        "page_table": page_table,
        "c_prompt": nrm(8, (BATCH, D)),
        "c_sample": nrm(9, (DEC_BATCH, D)),
        "norm1_g": 1.0 + nrm(10, (DEPTH, D), 0.05),
        "norm2_g": 1.0 + nrm(11, (DEPTH, D), 0.05),
        "w_ada": nrm(12, (DEPTH, D, 6 * D), 0.5 * D ** -0.5),
        "b_ada": nrm(13, (DEPTH, 6 * D), 0.02),
        "w_in": nrm(14, (DEPTH, D, PROJ_OUT), D ** -0.5),
        "cmp_pe": nrm(15, (DEPTH, 2, CMP_LEN, HEAD_DIM), 0.1),
        "cmp_w1": nrm(16, (DEPTH, 2, CMP_LEN * HEAD_DIM, CMP_HID), (CMP_LEN * HEAD_DIM) ** -0.5),
        "cmp_w2": nrm(17, (DEPTH, 2, CMP_HID, HEAD_DIM), CMP_HID ** -0.5),
        "w_br_a": nrm(18, (DEPTH, NSA_QW, D), NSA_QW ** -0.5),
        "w_br_b": nrm(19, (DEPTH, MOBA_W, D), MOBA_W ** -0.5),
        "w_br_c": nrm(20, (DEPTH, SB_W, D), SB_W ** -0.5),
        "w_o": nrm(21, (DEPTH, D, D), D ** -0.5),
        "w_up": nrm(22, (DEPTH, D, 2 * D_FF), D ** -0.5),
        "conv_w": nrm(23, (DEPTH, CONV_W, D_FF), CONV_W ** -0.5),
        "conv_b": nrm(24, (DEPTH, D_FF), 0.02),
        "w_down": nrm(25, (DEPTH, D_FF, D), D_FF ** -0.5),
        "final_g": 1.0 + nrm(26, (D,), 0.05),
    }


def reference(x_prompt, x_sample, cache_nsa, cache_moba, cache_sb, state_win, state_conv, page_table,
              c_prompt, c_sample, norm1_g, norm2_g, w_ada, b_ada, w_in, cmp_pe, cmp_w1, cmp_w2,
              w_br_a, w_br_b, w_br_c, w_o, w_up, conv_w, conv_b, w_down, final_g):
    n_dec, n_pages = page_table.shape
    past_len = n_pages * PAGE_SIZE

    def paged_rows(pool):
        return pool[page_table].reshape(n_dec, past_len, *pool.shape[2:])

    def layer(l, x, c, past, q_pos0):
        B = x.shape[0]
        mod = (jax.nn.silu(c) @ w_ada[l] + b_ada[l]).reshape(B, 6, 1, D_MODEL)
        h = rmsnorm(x, norm1_g[l]) * (1.0 + mod[:, 1]) + mod[:, 0]
        a, (nsa_rows, moba_rows, sb_rows, new_win) = token_mixers(
            h, q_pos0, None if past is None else past[:4], w_in[l], cmp_pe[l], cmp_w1[l], cmp_w2[l],
            w_br_a[l], w_br_b[l], w_br_c[l], w_o[l])
        x = x + mod[:, 2] * a
        h = rmsnorm(x, norm2_g[l]) * (1.0 + mod[:, 4]) + mod[:, 3]
        conv_prev = jnp.zeros((B, CONV_W - 1, D_FF), x.dtype) if past is None else past[4]
        f, new_conv = conv_ffn(h, conv_prev, w_up[l], conv_w[l], conv_b[l], w_down[l])
        x = x + mod[:, 5] * f
        return x, (nsa_rows, moba_rows, sb_rows, new_win, new_conv)

    xp, xs = x_prompt, x_sample
    st_p, st_s = [], []
    for l in range(DEPTH):
        xp, sp = layer(l, xp, c_prompt, None, 0)
        past = (paged_rows(cache_nsa[l]), paged_rows(cache_moba[l]), paged_rows(cache_sb[l]),
                state_win[l], state_conv[l])
        xs, ss = layer(l, xs, c_sample, past, past_len)
        st_p.append(sp)
        st_s.append(ss)
    y_prompt = rmsnorm(xp, final_g)
    y_sample = rmsnorm(xs, final_g)

    def stack(lst, i):
        return jnp.stack([s[i] for s in lst])

    def to_pages(a):
        return a.reshape(a.shape[0], a.shape[1], a.shape[2] // PAGE_SIZE, PAGE_SIZE, *a.shape[3:])

    nsa_p, nsa_s = to_pages(stack(st_p, 0)), stack(st_s, 0)
    moba_p, moba_s = to_pages(stack(st_p, 1)), stack(st_s, 1)
    sb_p, sb_s = to_pages(stack(st_p, 2)), stack(st_s, 2)
    win_p, win_s = stack(st_p, 3), stack(st_s, 3)
    conv_p, conv_s = stack(st_p, 4), stack(st_s, 4)
    return (y_prompt, y_sample, nsa_p, nsa_s, moba_p, moba_s, sb_p, sb_s, win_p, win_s, conv_p, conv_s)
```

```python
import functools

import numpy as np
import jax
import jax.numpy as jnp
from jax import lax
from jax.experimental import pallas as pl
from jax.experimental.pallas import tpu as pltpu

f32 = jnp.float32
bf16 = jnp.bfloat16

D_MODEL = 1024
HEAD_DIM = 64
NSA_HEADS = 8
NSA_KV = 2
HPG = NSA_HEADS // NSA_KV
CMP_LEN = 32
CMP_STRIDE = 16
CMP_HID = 128
SEL_BLOCK = 64
SEL_TOPN = 16
WINDOW = 512
MOBA_HEADS = 4
MOBA_BLOCK = 256
MOBA_TOPK = 3
SB_HEADS = 4
D_FF = 2816
CONV_W = 3
PAGE_SIZE = 128
RMS_EPS = 1e-6
NEG = -1e30
FORCE = 1e9

LANES = 128
MASK_BIG = 2.0 ** 100
M_INIT = -1e29
REMOVED = -3e38
VMEM_LIMIT_MB = 56

NSA_SLOPES = [2.0 ** (-8.0 * (h + 1) / NSA_HEADS) for h in range(NSA_HEADS)]
MOBA_SLOPES = [2.0 ** (-8.0 * (h + 1) / MOBA_HEADS) for h in range(MOBA_HEADS)]

_Q_N, _NSA, _WIN, _MB_Q, _MOBA, _SB_Q, _SB, _G_M, _G_N, _PROJ_W = 0, 512, 1024, 1280, 1536, 2048, 2304, 2816, 5888, 6016


def _cparams(sem, vmem_mb=VMEM_LIMIT_MB):
    return pltpu.CompilerParams(dimension_semantics=sem, vmem_limit_bytes=vmem_mb * 2 ** 20)


def _const_spec(shape):
    nd = len(shape)
    return pl.BlockSpec(shape, lambda *_: (0,) * nd, pipeline_mode=pl.Buffered(1))


def _nt(a, b):
    return lax.dot_general(a, b, (((1,), (1,)), ((), ())), preferred_element_type=f32)


def _dot(a, b):
    return jnp.dot(a, b, preferred_element_type=f32)


def _gelu_tanh(x):
    return x * (0.5 * (1.0 + jnp.tanh(np.sqrt(2.0 / np.pi) * (x + 0.044715 * (x * x * x)))))


def _ref_softmax(s, valid):
    s = jnp.where(valid, s, NEG)
    e = jnp.where(valid, jnp.exp(s - jnp.max(s, axis=-1, keepdims=True)), 0.0)
    return e / jnp.maximum(jnp.sum(e, axis=-1, keepdims=True), 1e-30)


def _col_const(vals, reps):
    return jnp.concatenate([jnp.full((reps, 1), v, f32) for v in vals], axis=0)


def _ada_body(c_ref, w_ref, b_ref, o_ref):
    c = c_ref[...]
    s = c * jax.nn.sigmoid(c)
    o_ref[...] = _dot(s.astype(bf16), w_ref[...].astype(bf16)) + b_ref[...]


def _ada_mod(c_all, w_ada, b_ada):
    depth, d, n = w_ada.shape
    r = c_all.shape[0]
    tn = 1536
    return pl.pallas_call(
        _ada_body,
        out_shape=jax.ShapeDtypeStruct((depth, r, n), f32),
        grid=(depth, n // tn),
        in_specs=[pl.BlockSpec((r, d), lambda l, j: (0, 0)),
                  pl.BlockSpec((None, d, tn), lambda l, j: (l, 0, j)),
                  pl.BlockSpec((None, 1, tn), lambda l, j: (l, 0, j))],
        out_specs=pl.BlockSpec((None, r, tn), lambda l, j: (l, 0, j)),
        compiler_params=_cparams(("parallel", "parallel")),
        name="ada_mod",
    )(c_all, w_ada, b_ada.reshape(depth, 1, n))


def _nmm_body(x_ref, g_ref, sh_ref, sc_ref, w_ref, *o_refs, segs):
    x = x_ref[...]
    y = x * lax.rsqrt(jnp.mean(x * x, axis=-1, keepdims=True) + RMS_EPS)
    h = (y * g_ref[...]) * (1.0 + sc_ref[...]) + sh_ref[...]
    hb = h.astype(bf16)
    k = 0
    for off, width, scales in segs:
        outs = o_refs[k:k + len(scales)]
        k += len(scales)
        for c0 in range(0, width, 512):
            cw = min(512, width - c0)
            acc = _dot(hb, w_ref[:, off + c0:off + c0 + cw])
            for o, scale in zip(outs, scales):
                o[:, c0:c0 + cw] = (acc if scale == 1.0 else acc * scale).astype(o.dtype)


def _norm_mod_matmul(x, g, shift, scale, w_bf, segs, out_dtypes, tm, name):
    r, d = x.shape
    nb, rb, _ = shift.shape
    tiles_per_b = (r // nb) // tm
    out_shape, out_specs = [], []
    k = 0
    for off, width, scales in segs:
        for _ in scales:
            out_shape.append(jax.ShapeDtypeStruct((r, width), out_dtypes[k]))
            out_specs.append(pl.BlockSpec((tm, width), lambda i: (i, 0)))
            k += 1
    mod_spec = pl.BlockSpec((None, rb, d), lambda i: (i // tiles_per_b, 0, 0))
    return pl.pallas_call(
        functools.partial(_nmm_body, segs=segs),
        out_shape=out_shape,
        grid=(r // tm,),
        in_specs=[pl.BlockSpec((tm, d), lambda i: (i, 0)), _const_spec((1, d)), mod_spec, mod_spec,
                  _const_spec(w_bf.shape)],
        out_specs=out_specs,
        compiler_params=_cparams(("parallel",)),
        name=name,
    )(x, g.reshape(1, d), shift, scale, w_bf)


_IN_SEGS = ((_Q_N, 512, (0.125,)), (_NSA, 512, (1.0, 1.0)), (_WIN, 256, (1.0, 1.0)), (_MB_Q, 256, (0.125,)),
            (_MOBA, 512, (1.0, 1.0)), (_SB_Q, 256, (0.125,)), (_SB, 512, (1.0, 1.0)), (_G_M, 3072, (1.0,)),
            (_G_N, 128, (1.0,)))
_IN_DTYPES = (bf16, f32, bf16, f32, bf16, bf16, f32, bf16, bf16, f32, bf16, f32, f32)
_UP_SEGS = ((0, D_FF, (1.0,)), (D_FF, D_FF, (1.0,)))
_UP_DTYPES = (f32, f32)


def _cmp_core(rows_refs, pe_ref, w1_ref, w2_ref, nc):
    outs = []
    for kv in range(2):
        acc_a = jnp.zeros((nc, NSA_KV * CMP_HID), f32)
        acc_b = jnp.zeros((nc, NSA_KV * CMP_HID), f32)
        for r in range(CMP_STRIDE):
            y = rows_refs[kv][pl.ds(r, nc, stride=CMP_STRIDE), :]
            acc_a = acc_a + _dot((y + pe_ref[kv, r:r + 1, :]).astype(bf16), w1_ref[kv, r])
            acc_b = acc_b + _dot((y + pe_ref[kv, CMP_STRIDE + r:CMP_STRIDE + r + 1, :]).astype(bf16),
                                 w1_ref[kv, CMP_STRIDE + r])
        pre = acc_a + pltpu.roll(acc_b, nc - 1, 0)
        outs.append(_dot(_gelu_tanh(pre).astype(bf16), w2_ref[kv]))
    return jnp.concatenate(outs, axis=1)


def _cmp_prompt_body(k_ref, v_ref, pe_ref, w1_ref, w2_ref, o_ref, *, nc):
    o_ref[...] = _cmp_core((k_ref, v_ref), pe_ref, w1_ref, w2_ref, nc)


def _cmp_prompt(nsa_rows, pe2, w1bd, w2bd):
    b, l, _ = nsa_rows.shape
    nc = l // CMP_STRIDE
    return pl.pallas_call(
        functools.partial(_cmp_prompt_body, nc=nc),
        out_shape=jax.ShapeDtypeStruct((b, nc, 256), f32),
        grid=(b,),
        in_specs=[pl.BlockSpec((None, l, LANES), lambda i: (i, 0, 0)),
                  pl.BlockSpec((None, l, LANES), lambda i: (i, 0, 1)), _const_spec(pe2.shape),
                  _const_spec(w1bd.shape), _const_spec(w2bd.shape)],
        out_specs=pl.BlockSpec((None, nc, 256), lambda i: (i, 0, 0)),
        compiler_params=_cparams(("parallel",)),
        name="nsa_compress_prompt",
    )(nsa_rows, nsa_rows, pe2, w1bd, w2bd)


def _flash_update(s, vt, m_scr, l_scr, acc_scr):
    m_old = m_scr[...]
    m_new = jnp.maximum(m_old, jnp.max(s, axis=-1, keepdims=True))
    alpha = jnp.exp(m_old - m_new)
    p = jnp.exp(s - m_new)
    l_scr[...] = alpha * l_scr[...] + jnp.sum(p, axis=-1, keepdims=True)
    acc_scr[...] = alpha * acc_scr[...] + _dot(p.astype(bf16), vt)
    m_scr[...] = m_new


def _flash_init(m_scr, l_scr, acc_scr):
    m_scr[...] = jnp.full(m_scr.shape, M_INIT, f32)
    l_scr[...] = jnp.zeros(l_scr.shape, f32)
    acc_scr[...] = jnp.zeros(acc_scr.shape, f32)


def _topk_rows(score, ids, k, n_ids):
    picked = jnp.zeros(score.shape, f32)
    for _ in range(k):
        mx = jnp.max(score, axis=0, keepdims=True)
        idx = jnp.min(jnp.where(score == mx, ids, n_ids), axis=0, keepdims=True)
        pick = ids == idx
        picked = jnp.where(pick, 1.0, picked)
        score = jnp.where(pick, REMOVED, score)
    return picked


def _nsa_prompt_body(q_ref, gn_ref, cmp_ref, agg_ref, kv_ref, win_ref, o_ref, m_scr, l_scr, acc_scr,
                     *, tq, tk, n_cmp):
    i = pl.program_id(1)
    t0 = i * tq
    rows = HPG * tq
    nc = cmp_ref.shape[0]
    qf = q_ref[...].astype(f32)
    lane = lax.broadcasted_iota(jnp.int32, (tq, LANES), 1)
    t_row = t0 + lax.broadcasted_iota(jnp.int32, (rows, 1), 0) % tq
    sig = jax.nn.sigmoid(gn_ref[...])
    cmpv = cmp_ref[...]
    kc = cmpv[:, 0:LANES].astype(bf16)
    vc = cmpv[:, LANES:2 * LANES].astype(bf16)
    aggb = agg_ref[...].astype(bf16)
    n_id = lax.broadcasted_iota(jnp.int32, (1, nc), 1)
    blk_id = lax.broadcasted_iota(jnp.int32, (LANES, tq), 0)
    cur = (t0 + lax.broadcasted_iota(jnp.int32, (LANES, tq), 1)) // SEL_BLOCK
    causal_blk = blk_id <= cur
    forced = (blk_id == 0) | (blk_id == cur) | (blk_id == cur - 1)
    chunks = [jnp.zeros((tq, LANES), f32) for _ in range(NSA_HEADS // 2)]

    for g in range(NSA_KV):
        heads = [HPG * g + hh for hh in range(HPG)]
        pieces = []
        for h in heads:
            blk = qf[:, LANES * (h // 2):LANES * (h // 2 + 1)]
            if h % 2 != g:
                blk = pltpu.roll(blk, HEAD_DIM, 1)
            pieces.append(jnp.where(lane // HEAD_DIM == g, blk, 0.0))
        qg = jnp.concatenate(pieces, axis=0).astype(bf16)
        slope = _col_const([NSA_SLOPES[h] for h in heads], tq)

        d_c = t_row - (n_id * CMP_STRIDE + CMP_LEN - 1)
        p_c = _ref_softmax(_nt(qg, kc) - slope * d_c.astype(f32), (d_c >= 0) & (n_id < n_cmp))
        o_c = _dot(p_c.astype(bf16), vc)
        psum = p_c[0:tq] + p_c[tq:2 * tq] + p_c[2 * tq:3 * tq] + p_c[3 * tq:4 * tq]
        p_hi = psum.astype(bf16)
        p_lo = (psum - p_hi.astype(f32)).astype(bf16)
        imp = _dot(p_hi, aggb) + _dot(p_lo, aggb)

        score = jnp.where(causal_blk, jnp.where(forced, FORCE, imp.T), NEG)
        sel_t = jnp.where(causal_blk, _topk_rows(score, blk_id, SEL_TOPN, LANES), 0.0)
        notsel = (1.0 - sel_t).T
        q_aug = jnp.concatenate([qg, jnp.concatenate([notsel] * HPG, axis=0).astype(bf16)], axis=1)

        _flash_init(m_scr, l_scr, acc_scr)

        def sel_step(kt, masked):
            k0 = pl.multiple_of(kt * tk, tk)
            k_t = kv_ref[pl.ds(k0, tk), 0:LANES]
            v_t = kv_ref[pl.ds(k0, tk), LANES:2 * LANES]
            key_blk = (k0 + lax.broadcasted_iota(jnp.int32, (tk, LANES), 0)) // SEL_BLOCK
            onehot = jnp.where(lax.broadcasted_iota(jnp.int32, (tk, LANES), 1) == key_blk, -MASK_BIG, 0.0)
            s = _nt(q_aug, jnp.concatenate([k_t, onehot.astype(bf16)], axis=1))
            dist = t_row - (k0 + lax.broadcasted_iota(jnp.int32, (1, tk), 1))
            s = s - slope * dist.astype(f32)
            if masked:
                s = jnp.where(dist >= 0, s, -MASK_BIG)
            _flash_update(s, v_t, m_scr, l_scr, acc_scr)

        kd = t0 // tk

        def sel_loop(kt, carry):
            sel_step(kt, False)
            return carry

        lax.fori_loop(0, kd, sel_loop, 0)
        sel_step(kd, True)
        o_s = acc_scr[...] / l_scr[...]

        wl = WINDOW + tq
        s0 = pl.multiple_of(jnp.maximum(t0 - WINDOW, 0), tq)
        k_w = win_ref[pl.ds(s0, wl), 0:LANES]
        v_w = win_ref[pl.ds(s0, wl), LANES:2 * LANES]
        d_w = t_row - (s0 + lax.broadcasted_iota(jnp.int32, (1, wl), 1))
        p_w = _ref_softmax(_nt(qg, k_w) - slope * d_w.astype(f32), (d_w >= 0) & (d_w < WINDOW))
        o_w = _dot(p_w.astype(bf16), v_w)

        gates = [jnp.concatenate([sig[:, br * NSA_HEADS + h:br * NSA_HEADS + h + 1] for h in heads], axis=0)
                 for br in range(3)]
        o = gates[0] * o_c + gates[1] * o_s + gates[2] * o_w
        for hh, h in enumerate(heads):
            piece = o[hh * tq:(hh + 1) * tq]
            if h % 2 != g:
                piece = pltpu.roll(piece, HEAD_DIM, 1)
            chunks[h // 2] = chunks[h // 2] + jnp.where(lane // HEAD_DIM == h % 2, piece, 0.0)

    o_ref[...] = jnp.concatenate(chunks, axis=1).astype(o_ref.dtype)


def _nsa_prompt(q_n, g_n, cmp, agg, nsa_bf, win_bf, n_cmp, tq, tk):
    b, l, _ = q_n.shape
    nc = cmp.shape[1]
    rows = HPG * tq
    return pl.pallas_call(
        functools.partial(_nsa_prompt_body, tq=tq, tk=tk, n_cmp=n_cmp),
        out_shape=jax.ShapeDtypeStruct((b, l, NSA_HEADS * HEAD_DIM), bf16),
        grid=(b, l // tq),
        in_specs=[pl.BlockSpec((None, tq, 512), lambda bi, i: (bi, i, 0)),
                  pl.BlockSpec((None, tq, LANES), lambda bi, i: (bi, i, 0)),
                  pl.BlockSpec((None, nc, 256), lambda bi, i: (bi, 0, 0)),
                  _const_spec(agg.shape),
                  pl.BlockSpec((None, l, 256), lambda bi, i: (bi, 0, 1)),
                  pl.BlockSpec((None, l, 256), lambda bi, i: (bi, 0, 0))],
        out_specs=pl.BlockSpec((None, tq, 512), lambda bi, i: (bi, i, 0)),
        scratch_shapes=[pltpu.VMEM((rows, 1), f32), pltpu.VMEM((rows, 1), f32), pltpu.VMEM((rows, LANES), f32)],
        compiler_params=_cparams(("parallel", "parallel")),
        name="nsa_attention_prompt",
    )(q_n, g_n, cmp, agg, nsa_bf, win_bf)


def _block_mean_body(k_ref, o_ref):
    o_ref[...] = jnp.mean(k_ref[...], axis=0, keepdims=True)


def _block_mean(rows, width, blk):
    b, l, _ = rows.shape
    return pl.pallas_call(
        _block_mean_body,
        out_shape=jax.ShapeDtypeStruct((b, l // blk, 1, width), f32),
        grid=(b, l // blk),
        in_specs=[pl.BlockSpec((None, blk, width), lambda bi, j: (bi, j, 0))],
        out_specs=pl.BlockSpec((None, None, 1, width), lambda bi, j: (bi, j, 0, 0)),
        compiler_params=_cparams(("parallel", "parallel")),
        name="moba_block_mean",
    )(rows)


def _moba_prompt_body(q_ref, kmt_ref, kv_ref, o_ref, m_scr, l_scr, acc_scr, *, tq, tk, nb_pad):
    i = pl.program_id(1)
    t0 = i * tq
    q = q_ref[...]
    qf = q.astype(f32)
    lane = lax.broadcasted_iota(jnp.int32, (tq, LANES), 1)
    t_row = t0 + lax.broadcasted_iota(jnp.int32, (2 * tq, 1), 0) % tq

    gate_t = _nt(kmt_ref[...].astype(bf16), q)
    blk_id = lax.broadcasted_iota(jnp.int32, (LANES, tq), 0) % nb_pad
    own = (t0 + lax.broadcasted_iota(jnp.int32, (LANES, tq), 1)) // MOBA_BLOCK
    past = blk_id < own
    score = jnp.where(past, gate_t, NEG)
    parts = [_topk_rows(score[nb_pad * h:nb_pad * (h + 1)], blk_id[nb_pad * h:nb_pad * (h + 1)], MOBA_TOPK, nb_pad)
             for h in range(MOBA_HEADS)]
    sel_t = jnp.where(past, jnp.concatenate(parts, axis=0), 0.0)
    sel_t = jnp.where(blk_id == own, 1.0, sel_t)
    notsel = (1.0 - sel_t).T

    out_chunks = []
    for c in range(MOBA_HEADS // 2):
        q_rows = []
        for e in range(2):
            h = 2 * c + e
            qh = jnp.where(lane // HEAD_DIM == e, qf[:, LANES * c:LANES * (c + 1)], 0.0)
            ns = notsel if h == 0 else pltpu.roll(notsel, LANES - nb_pad * h, 1)
            ns = jnp.where(lane < nb_pad, ns, 0.0)
            q_rows.append(jnp.concatenate([qh, ns], axis=1))
        q_aug = jnp.concatenate(q_rows, axis=0).astype(bf16)
        slope = _col_const([MOBA_SLOPES[2 * c], MOBA_SLOPES[2 * c + 1]], tq)
        _flash_init(m_scr, l_scr, acc_scr)

        def step(kt, masked, c=c, q_aug=q_aug, slope=slope):
            k0 = pl.multiple_of(kt * tk, tk)
            k_t = kv_ref[pl.ds(k0, tk), LANES * c:LANES * (c + 1)]
            v_t = kv_ref[pl.ds(k0, tk), 256 + LANES * c:256 + LANES * (c + 1)]
            key_blk = (k0 + lax.broadcasted_iota(jnp.int32, (tk, LANES), 0)) // MOBA_BLOCK
            onehot = jnp.where(lax.broadcasted_iota(jnp.int32, (tk, LANES), 1) == key_blk, -MASK_BIG, 0.0)
            s = _nt(q_aug, jnp.concatenate([k_t, onehot.astype(bf16)], axis=1))
            dist = t_row - (k0 + lax.broadcasted_iota(jnp.int32, (1, tk), 1))
            s = s - slope * dist.astype(f32)
            if masked:
                s = jnp.where(dist >= 0, s, -MASK_BIG)
            _flash_update(s, v_t, m_scr, l_scr, acc_scr)

        kd = t0 // tk

        def loop(kt, carry, step=step):
            step(kt, False)
            return carry

        lax.fori_loop(0, kd, loop, 0)
        step(kd, True)
        o = acc_scr[...] / l_scr[...]
        out_chunks.append(jnp.where(lane < HEAD_DIM, o[0:tq], o[tq:2 * tq]))
    o_ref[...] = jnp.concatenate(out_chunks, axis=1).astype(o_ref.dtype)


def _moba_prompt(mb_q, kmt, moba_bf, tq, tk):
    b, l, _ = mb_q.shape
    nb_pad = LANES // MOBA_HEADS
    return pl.pallas_call(
        functools.partial(_moba_prompt_body, tq=tq, tk=tk, nb_pad=nb_pad),
        out_shape=jax.ShapeDtypeStruct((b, l, MOBA_HEADS * HEAD_DIM), bf16),
        grid=(b, l // tq),
        in_specs=[pl.BlockSpec((None, tq, 256), lambda bi, i: (bi, i, 0)),
                  pl.BlockSpec((None, LANES, 256), lambda bi, i: (bi, 0, 0)),
                  pl.BlockSpec((None, l, 512), lambda bi, i: (bi, 0, 0))],
        out_specs=pl.BlockSpec((None, tq, 256), lambda bi, i: (bi, i, 0)),
        scratch_shapes=[pltpu.VMEM((2 * tq, 1), f32), pltpu.VMEM((2 * tq, 1), f32),
                        pltpu.VMEM((2 * tq, LANES), f32)],
        compiler_params=_cparams(("parallel", "parallel")),
        name="moba_attention_prompt",
    )(mb_q, kmt, moba_bf)


def _log_keep(z):
    return -(jnp.maximum(z, 0.0) + jnp.log(1.0 + jnp.exp(-jnp.abs(z))))


def _suffix_sums(lk, uu):
    hi = lk.astype(bf16)
    lo = (lk - hi.astype(f32)).astype(bf16)
    r = _dot(jnp.concatenate([hi, lo], axis=1), uu)
    return r[:, 0:LANES], r[:, LANES:2 * LANES]


def _sb_prompt_body(q_ref, uu_ref, k_ref, v_ref, o_ref, carry_scr, acc_scr, *, tq):
    i = pl.program_id(2)
    t0 = i * tq
    qf = q_ref[...].astype(f32)
    lane = lax.broadcasted_iota(jnp.int32, (tq, LANES), 1)
    q2 = jnp.concatenate([jnp.where(lane // HEAD_DIM == e, qf, 0.0) for e in range(2)], axis=0).astype(bf16)
    t_row = t0 + lax.broadcasted_iota(jnp.int32, (2 * tq, 1), 0) % tq
    uu = uu_ref[...]
    carry_scr[...] = jnp.zeros(carry_scr.shape, f32)
    acc_scr[...] = jnp.zeros(acc_scr.shape, f32)

    def step(kc, diag):
        k0 = pl.multiple_of(kc * LANES, LANES)
        z = _nt(q2, k_ref[pl.ds(k0, LANES), :])
        lk = _log_keep(z)
        if diag:
            is_past = (k0 + lax.broadcasted_iota(jnp.int32, (1, LANES), 1)) < t_row
            lk = jnp.where(is_past, lk, 0.0)
        later, total = _suffix_sums(lk, uu)
        w = jnp.exp(z + lk + later + carry_scr[...])
        if diag:
            w = jnp.where(is_past, w, 0.0)
        acc_scr[...] = acc_scr[...] + _dot(w.astype(bf16), v_ref[pl.ds(k0, LANES), :])
        carry_scr[...] = carry_scr[...] + total

    n_diag = tq // LANES
    for dgi in range(n_diag):
        step(i * n_diag + (n_diag - 1 - dgi), True)

    def loop(j, carry):
        step(i * n_diag - 1 - j, False)
        return carry

    lax.fori_loop(0, i * n_diag, loop, 0)
    o = acc_scr[...]
    o_ref[...] = jnp.where(lane < HEAD_DIM, o[0:tq], o[tq:2 * tq]).astype(o_ref.dtype)


def _sb_prompt(sb_q, uu, sb_bf, tq):
    b, l, _ = sb_q.shape
    nch = SB_HEADS // 2
    return pl.pallas_call(
        functools.partial(_sb_prompt_body, tq=tq),
        out_shape=jax.ShapeDtypeStruct((b, l, SB_HEADS * HEAD_DIM), bf16),
        grid=(b, nch, l // tq),
        in_specs=[pl.BlockSpec((None, tq, LANES), lambda bi, c, i: (bi, i, c)),
                  _const_spec(uu.shape),
                  pl.BlockSpec((None, l, LANES), lambda bi, c, i: (bi, 0, c)),
                  pl.BlockSpec((None, l, LANES), lambda bi, c, i: (bi, 0, nch + c))],
        out_specs=pl.BlockSpec((None, tq, LANES), lambda bi, c, i: (bi, i, c)),
        scratch_shapes=[pltpu.VMEM((2 * tq, LANES), f32), pltpu.VMEM((2 * tq, LANES), f32)],
        compiler_params=_cparams(("parallel", "parallel", "parallel")),
        name="stickbreak_attention_prompt",
    )(sb_q, uu, sb_bf, sb_bf)


def _merge_body(oa_ref, ob_ref, oc_ref, gm_ref, x_ref, gate_ref, wa_ref, wb_ref, wc_ref, wo_ref, o_ref):
    d = x_ref.shape[1]
    g = jax.nn.sigmoid(gm_ref[...])
    merged = (g[:, 0:d] * _dot(oa_ref[...], wa_ref[...]) + g[:, d:2 * d] * _dot(ob_ref[...], wb_ref[...])
              + g[:, 2 * d:3 * d] * _dot(oc_ref[...], wc_ref[...]))
    o_ref[...] = x_ref[...] + gate_ref[...] * _dot(merged.astype(bf16), wo_ref[...])


def _merge_out(o_a, o_b, o_c, g_m, x, gate, wa, wb, wc, wo, tm):
    r, d = x.shape
    nb, rb, _ = gate.shape
    tiles_per_b = (r // nb) // tm
    row = lambda w: pl.BlockSpec((tm, w), lambda i: (i, 0))
    return pl.pallas_call(
        _merge_body,
        out_shape=jax.ShapeDtypeStruct((r, d), f32),
        grid=(r // tm,),
        in_specs=[row(o_a.shape[1]), row(o_b.shape[1]), row(o_c.shape[1]), row(3 * d), row(d),
                  pl.BlockSpec((None, rb, d), lambda i: (i // tiles_per_b, 0, 0)),
                  _const_spec(wa.shape), _const_spec(wb.shape), _const_spec(wc.shape), _const_spec(wo.shape)],
        out_specs=row(d),
        compiler_params=_cparams(("parallel",)),
        name="merge_out_proj",
    )(o_a, o_b, o_c, g_m, x, gate, wa, wb, wc, wo)


def _ffn_tail(a, a_m1, a_m2, b, cw_ref, cb_ref, wd_ref, x_ref, gate_ref, o_ref):
    conv = cb_ref[...] + a_m2 * cw_ref[0:1, :]
    conv = conv + a_m1 * cw_ref[1:2, :]
    conv = conv + a * cw_ref[2:3, :]
    y = _dot((_gelu_tanh(conv) * b).astype(bf16), wd_ref[...])
    o_ref[...] = x_ref[...] + gate_ref[...] * y


def _ffn_seq_body(a_ref, halo_ref, b_ref, cw_ref, cb_ref, wd_ref, x_ref, gate_ref, o_ref, *, tiles_per_b):
    a = a_ref[...]
    tm = a.shape[0]
    first = pl.program_id(0) % tiles_per_b == 0
    halo = jnp.where(first, 0.0, halo_ref[...])
    row = lax.broadcasted_iota(jnp.int32, a.shape, 0)
    a_m1 = jnp.where(row < 1, halo[7:8, :], pltpu.roll(a, 1, 0))
    a_m2 = jnp.where(row < 1, halo[6:7, :], jnp.where(row < 2, halo[7:8, :], pltpu.roll(a, 2, 0)))
    _ffn_tail(a, a_m1, a_m2, b_ref[...], cw_ref, cb_ref, wd_ref, x_ref, gate_ref, o_ref)


def _ffn_step_body(a_ref, am1_ref, am2_ref, b_ref, cw_ref, cb_ref, wd_ref, x_ref, gate_ref, o_ref):
    _ffn_tail(a_ref[...], am1_ref[...], am2_ref[...], b_ref[...], cw_ref, cb_ref, wd_ref, x_ref, gate_ref, o_ref)


def _ffn_down_seq(u_a, u_b, cw8, cb, wd, x, gate, tm):
    r, d = x.shape
    ff = u_a.shape[1]
    nb = gate.shape[0]
    tiles_per_b = (r // nb) // tm
    row = lambda w: pl.BlockSpec((tm, w), lambda i: (i, 0))
    return pl.pallas_call(
        functools.partial(_ffn_seq_body, tiles_per_b=tiles_per_b),
        out_shape=jax.ShapeDtypeStruct((r, d), f32),
        grid=(r // tm,),
        in_specs=[row(ff), pl.BlockSpec((8, ff), lambda i: (jnp.maximum(i * (tm // 8) - 1, 0), 0)), row(ff),
                  _const_spec(cw8.shape), _const_spec(cb.shape), _const_spec(wd.shape), row(d),
                  pl.BlockSpec((None, 1, d), lambda i: (i // tiles_per_b, 0, 0))],
        out_specs=row(d),
        compiler_params=_cparams(("parallel",)),
        name="conv_ffn_down_seq",
    )(u_a, u_a, u_b, cw8, cb, wd, x, gate)


def _ffn_down_step(u_a, a_m1, a_m2, u_b, cw8, cb, wd, x, gate):
    r, d = x.shape
    full = lambda a: pl.BlockSpec(a.shape, lambda i: (0,) * a.ndim)
    return pl.pallas_call(
        _ffn_step_body,
        out_shape=jax.ShapeDtypeStruct((r, d), f32),
        grid=(1,),
        in_specs=[full(u_a), full(a_m1), full(a_m2), full(u_b), full(cw8), full(cb), full(wd), full(x),
                  pl.BlockSpec((None, r, d), lambda i: (0, 0, 0))],
        out_specs=full(x),
        compiler_params=_cparams(("arbitrary",)),
        name="conv_ffn_down_step",
    )(u_a, a_m1, a_m2, u_b, cw8, cb, wd, x, gate)


def _final_norm_body(x_ref, g_ref, o_ref):
    x = x_ref[...]
    o_ref[...] = x * lax.rsqrt(jnp.mean(x * x, axis=-1, keepdims=True) + RMS_EPS) * g_ref[...]


def _final_norm(x, g, tm):
    r, d = x.shape
    return pl.pallas_call(
        _final_norm_body,
        out_shape=jax.ShapeDtypeStruct((r, d), f32),
        grid=(r // tm,),
        in_specs=[pl.BlockSpec((tm, d), lambda i: (i, 0)), _const_spec((1, d))],
        out_specs=pl.BlockSpec((tm, d), lambda i: (i, 0)),
        compiler_params=_cparams(("parallel",)),
        name="final_rmsnorm",
    )(x, g.reshape(1, d))


def _page_specs(n_pages, width, layer_base):
    return [pl.BlockSpec((None, PAGE_SIZE, width),
                         functools.partial(lambda s, pt, j: (layer_base + pt[s * n_pages + j], 0, 0), j=j))
            for j in range(n_pages)]


def _per_seq(width):
    return pl.BlockSpec((None, 1, width), lambda s, pt: (s, 0, 0))


def _dec_const(shape):
    nd = len(shape)
    return pl.BlockSpec(shape, lambda s, pt: (0,) * nd, pipeline_mode=pl.Buffered(1))


def _head_rows(q_row, heads_per_chunk_rows=8):
    row = lax.broadcasted_iota(jnp.int32, (8, LANES), 0)
    lane = lax.broadcasted_iota(jnp.int32, (8, LANES), 1)
    q8 = jnp.broadcast_to(q_row, (8, q_row.shape[1]))
    qsel = jnp.zeros((8, LANES), f32)
    for c in range(NSA_HEADS // 2):
        qsel = qsel + jnp.where(row // 2 == c, q8[:, LANES * c:LANES * (c + 1)], 0.0)
    swap = (row % 2) != (row // HPG)
    qm = jnp.where(swap, pltpu.roll(qsel, HEAD_DIM, 1), qsel)
    return jnp.where(lane // HEAD_DIM == row // HPG, qm, 0.0), swap


def _rank_select(score_row, k):
    a = jnp.broadcast_to(score_row, (LANES, LANES))
    b = a.T
    ii = lax.broadcasted_iota(jnp.int32, (LANES, LANES), 0)
    jj = lax.broadcasted_iota(jnp.int32, (LANES, LANES), 1)
    ahead = (b > a) | ((b == a) & (ii < jj))
    rank = jnp.sum(jnp.where(ahead, 1.0, 0.0), axis=0, keepdims=True)
    return jnp.where(rank < k, 1.0, 0.0)


def _nsa_dec_body(pt_ref, q_ref, gn_ref, new_ref, wnew_ref, sw_ref, pe_ref, w1_ref, w2_ref, agg_ref, e_ref,
                  *rest, n_pages):
    pages, o_ref = rest[:n_pages], rest[n_pages]
    kc_scr, vc_scr, ks_scr, vs_scr = rest[n_pages + 1:]
    past = n_pages * PAGE_SIZE
    kp = ks_scr.shape[0]
    tail_row = lax.broadcasted_iota(jnp.int32, (kp - past, LANES), 0)
    new_row = new_ref[...]
    for k, scr in enumerate((kc_scr, vc_scr, ks_scr, vs_scr)):
        for j in range(n_pages):
            scr[PAGE_SIZE * j:PAGE_SIZE * (j + 1), :] = pages[j][:, LANES * k:LANES * (k + 1)]
        if scr.shape[0] > past:
            scr[past:kp, :] = jnp.where(tail_row == 0, new_row[:, LANES * k:LANES * (k + 1)], 0.0)

    nc = past // CMP_STRIDE
    n_cmp = (past + 1 - CMP_LEN) // CMP_STRIDE + 1
    cmpv = _cmp_core((kc_scr, vc_scr), pe_ref, w1_ref, w2_ref, nc)

    row = lax.broadcasted_iota(jnp.int32, (8, LANES), 0)
    lane = lax.broadcasted_iota(jnp.int32, (8, LANES), 1)
    qm, swap = _head_rows(q_ref[...])
    qmb = qm.astype(bf16)
    slope = jnp.zeros((8, 1), f32)
    row1 = lax.broadcasted_iota(jnp.int32, (8, 1), 0)
    for h in range(NSA_HEADS):
        slope = jnp.where(row1 == h, NSA_SLOPES[h], slope)
    grp0 = row < HPG

    d_c = past - (lane[0:1] * CMP_STRIDE + CMP_LEN - 1)
    s_c = _nt(qmb, cmpv[:, 0:LANES].astype(bf16)) - slope * d_c.astype(f32)
    p_c = _ref_softmax(s_c, (d_c >= 0) & (lane[0:1] < n_cmp))
    o_c = _dot(p_c.astype(bf16), cmpv[:, LANES:2 * LANES].astype(bf16))
    aggb = agg_ref[...].astype(bf16)
    p_hi = p_c.astype(bf16)
    p_lo = (p_c - p_hi.astype(f32)).astype(bf16)
    imp_rows = _dot(p_hi, aggb) + _dot(p_lo, aggb)

    cur = past // SEL_BLOCK
    blk = lane[0:1]
    forced = (blk == 0) | (blk == cur) | (blk == cur - 1)
    causal = blk <= cur
    notsel_g = []
    for g in range(NSA_KV):
        imp = jnp.sum(imp_rows[HPG * g:HPG * (g + 1)], axis=0, keepdims=True)
        score = jnp.where(causal, jnp.where(forced, FORCE, imp), NEG)
        sel = jnp.where(causal, _rank_select(score, min(SEL_TOPN, -(-(past + 1) // SEL_BLOCK))), 0.0)
        notsel_g.append(jnp.broadcast_to(1.0 - sel, (8, LANES)))
    notsel = jnp.where(grp0, notsel_g[0], notsel_g[1])

    pos = lax.broadcasted_iota(jnp.int32, (1, kp), 1)
    dist = past - pos
    blocked = _dot(notsel.astype(bf16), e_ref[...])
    s_s = _nt(qmb, ks_scr[...].astype(bf16)) - slope * dist.astype(f32)
    p_s = _ref_softmax(s_s, (blocked < 0.5) & (dist >= 0))
    o_s = _dot(p_s.astype(bf16), vs_scr[...].astype(bf16))

    wb = sw_ref.shape[0]
    idx = lax.broadcasted_iota(jnp.int32, (1, wb), 1)
    d_w = wb - idx
    s_w = _nt(qmb, sw_ref[:, 0:LANES].astype(bf16)) - slope * d_w.astype(f32)
    valid_w = (d_w < WINDOW) & (d_w >= 0)
    s_w = jnp.where(valid_w, s_w, NEG)
    wnew = wnew_ref[...]
    s_n = jnp.sum(qm * wnew[:, 0:LANES], axis=-1, keepdims=True)
    m_w = jnp.maximum(jnp.max(s_w, axis=-1, keepdims=True), s_n)
    e_w = jnp.where(valid_w, jnp.exp(s_w - m_w), 0.0)
    e_n = jnp.exp(s_n - m_w)
    den = jnp.maximum(jnp.sum(e_w, axis=-1, keepdims=True) + e_n, 1e-30)
    o_w = (_dot(e_w.astype(bf16), sw_ref[:, LANES:2 * LANES].astype(bf16)) + e_n * wnew[:, LANES:2 * LANES]) / den

    sig = jnp.broadcast_to(jax.nn.sigmoid(gn_ref[...]), (8, LANES))
    gates = [jnp.sum(jnp.where(lane == br * NSA_HEADS + row, sig, 0.0), axis=-1, keepdims=True) for br in range(3)]
    o = gates[0] * o_c + gates[1] * o_s + gates[2] * o_w
    o = jnp.where(swap, pltpu.roll(o, HEAD_DIM, 1), o)
    o = jnp.where(lane // HEAD_DIM == row % 2, o, 0.0)
    o_ref[...] = jnp.concatenate([o[2 * c:2 * c + 1] + o[2 * c + 1:2 * c + 2] for c in range(NSA_HEADS // 2)], axis=1)


def _nsa_decode(page_table, cache, layer_base, q, g_n, nsa_new, win_new, state_win_l, pe4, w1bd, w2bd, agg, e_sel):
    s, n_pages = page_table.shape
    kp = e_sel.shape[1]
    consts = (pe4, w1bd, w2bd, agg, e_sel)
    grid_spec = pltpu.PrefetchScalarGridSpec(
        num_scalar_prefetch=1,
        grid=(s,),
        in_specs=[_per_seq(512), _per_seq(LANES), _per_seq(512), _per_seq(256),
                  pl.BlockSpec((None,) + state_win_l.shape[1:], lambda si, pt: (si, 0, 0))]
        + [_dec_const(c.shape) for c in consts] + _page_specs(n_pages, 512, layer_base),
        out_specs=_per_seq(512),
        scratch_shapes=[pltpu.VMEM((kp - PAGE_SIZE, LANES), f32), pltpu.VMEM((kp - PAGE_SIZE, LANES), f32),
                        pltpu.VMEM((kp, LANES), f32), pltpu.VMEM((kp, LANES), f32)],
    )
    return pl.pallas_call(
        functools.partial(_nsa_dec_body, n_pages=n_pages),
        out_shape=jax.ShapeDtypeStruct((s, 1, 512), f32),
        grid_spec=grid_spec,
        compiler_params=_cparams(("parallel",)),
        name="nsa_attention_decode",
    )(page_table.reshape(-1), q, g_n, nsa_new, win_new, state_win_l, *consts, *([cache] * n_pages))


def _q_head_rows4(q_row):
    row = lax.broadcasted_iota(jnp.int32, (8, 256), 0)
    lane = lax.broadcasted_iota(jnp.int32, (8, 256), 1)
    own = lane // HEAD_DIM == row
    return jnp.where(own, jnp.broadcast_to(q_row, (8, 256)), 0.0), own


def _moba_dec_body(pt_ref, q_ref, new_ref, e_ref, *rest, n_pages):
    pages, o_ref, kv_scr = rest[:n_pages], rest[n_pages], rest[n_pages + 1]
    past = n_pages * PAGE_SIZE
    kp = kv_scr.shape[0]
    for j in range(n_pages):
        kv_scr[PAGE_SIZE * j:PAGE_SIZE * (j + 1), :] = pages[j][...]
    tail_row = lax.broadcasted_iota(jnp.int32, (kp - past, 512), 0)
    kv_scr[past:kp, :] = jnp.where(tail_row == 0, new_ref[...], 0.0)

    nb_past = past // MOBA_BLOCK
    qm, own_lanes = _q_head_rows4(q_ref[...])
    qmb = qm.astype(bf16)
    km_row = lax.broadcasted_iota(jnp.int32, (LANES, 256), 0)
    km = jnp.zeros((LANES, 256), f32)
    for j in range(nb_past):
        mean_j = jnp.mean(kv_scr[MOBA_BLOCK * j:MOBA_BLOCK * (j + 1), 0:256], axis=0, keepdims=True)
        km = jnp.where(km_row == j, mean_j, km)
    gate = _nt(qmb, km.astype(bf16))

    lane = lax.broadcasted_iota(jnp.int32, (8, LANES), 1)
    is_past = lane < nb_past
    score = jnp.where(is_past, gate, NEG)
    rank = jnp.zeros((8, LANES), f32)
    for i in range(nb_past):
        gi = score[:, i:i + 1]
        rank = rank + jnp.where((gi > score) | ((gi == score) & (i < lane)), 1.0, 0.0)
    sel = (is_past & (rank < MOBA_TOPK)) | (lane == nb_past)
    notsel = jnp.where(sel, 0.0, 1.0)

    row1 = lax.broadcasted_iota(jnp.int32, (8, 1), 0)
    slope = jnp.zeros((8, 1), f32)
    for h in range(MOBA_HEADS):
        slope = jnp.where(row1 == h, MOBA_SLOPES[h], slope)
    pos = lax.broadcasted_iota(jnp.int32, (1, kp), 1)
    dist = past - pos
    blocked = _dot(notsel.astype(bf16), e_ref[...])
    s = _nt(qmb, kv_scr[:, 0:256].astype(bf16)) - slope * dist.astype(f32)
    p = _ref_softmax(s, (blocked < 0.5) & (dist >= 0))
    o = _dot(p.astype(bf16), kv_scr[:, 256:512].astype(bf16))
    o_ref[...] = jnp.sum(jnp.where(own_lanes, o, 0.0), axis=0, keepdims=True)


def _moba_decode(page_table, cache, layer_base, q, moba_new, e_blk):
    s, n_pages = page_table.shape
    kp = e_blk.shape[1]
    grid_spec = pltpu.PrefetchScalarGridSpec(
        num_scalar_prefetch=1,
        grid=(s,),
        in_specs=[_per_seq(256), _per_seq(512), _dec_const(e_blk.shape)] + _page_specs(n_pages, 512, layer_base),
        out_specs=_per_seq(256),
        scratch_shapes=[pltpu.VMEM((kp, 512), f32)],
    )
    return pl.pallas_call(
        functools.partial(_moba_dec_body, n_pages=n_pages),
        out_shape=jax.ShapeDtypeStruct((s, 1, 256), f32),
        grid_spec=grid_spec,
        compiler_params=_cparams(("parallel",)),
        name="moba_attention_decode",
    )(page_table.reshape(-1), q, moba_new, e_blk, *([cache] * n_pages))


def _sb_dec_body(pt_ref, q_ref, uu_ref, *rest, n_pages):
    pages, o_ref = rest[:n_pages], rest[n_pages]
    qm, own_lanes = _q_head_rows4(q_ref[...])
    qmb = qm.astype(bf16)
    uu = uu_ref[...]
    carry = jnp.zeros((8, LANES), f32)
    acc = jnp.zeros((8, 256), f32)
    for j in reversed(range(n_pages)):
        z = _nt(qmb, pages[j][:, 0:256].astype(bf16))
        lk = _log_keep(z)
        later, total = _suffix_sums(lk, uu)
        w = jnp.exp(z + lk + later + carry)
        acc = acc + _dot(w.astype(bf16), pages[j][:, 256:512].astype(bf16))
        carry = carry + total
    o_ref[...] = jnp.sum(jnp.where(own_lanes, acc, 0.0), axis=0, keepdims=True)


def _sb_decode(page_table, cache, layer_base, q, uu):
    s, n_pages = page_table.shape
    grid_spec = pltpu.PrefetchScalarGridSpec(
        num_scalar_prefetch=1,
        grid=(s,),
        in_specs=[_per_seq(256), _dec_const(uu.shape)] + _page_specs(n_pages, 512, layer_base),
        out_specs=_per_seq(256),
    )
    return pl.pallas_call(
        functools.partial(_sb_dec_body, n_pages=n_pages),
        out_shape=jax.ShapeDtypeStruct((s, 1, 256), f32),
        grid_spec=grid_spec,
        compiler_params=_cparams(("parallel",)),
        name="stickbreak_attention_decode",
    )(page_table.reshape(-1), q, uu, *([cache] * n_pages))


def _agg_matrix(nc, n_cmp):
    c0 = np.arange(nc)[:, None] * CMP_STRIDE
    s0 = np.arange(LANES)[None, :] * SEL_BLOCK
    ov = np.clip(np.minimum(c0 + CMP_LEN, s0 + SEL_BLOCK) - np.maximum(c0, s0), 0, None) / CMP_LEN
    ov[n_cmp:] = 0.0
    return jnp.asarray(ov, f32)


def _expand_matrix(block, kp):
    e = (np.arange(kp)[None, :] // block) == np.arange(LANES)[:, None]
    return jnp.asarray(e, bf16)


def _suffix_matrix():
    j = np.arange(2 * LANES)[:, None] % LANES
    s = np.arange(2 * LANES)[None, :]
    return jnp.asarray((s >= LANES) | (j > s), bf16)


def _layer_weights(l, w_in, cmp_pe, cmp_w1, cmp_w2, w_br_a, w_br_b, w_br_c, w_o, w_up, conv_w, conv_b, w_down):
    d = w_in.shape[1]
    w = w_in[l]
    w_proj = jnp.concatenate([w[:, 0:1280], w[:, 1304:], w[:, 1280:1304], jnp.zeros((d, _PROJ_W - 5912), f32)],
                             axis=1).astype(bf16)
    pe4 = jnp.concatenate([cmp_pe[l], cmp_pe[l]], axis=2)
    w1 = cmp_w1[l].reshape(2, CMP_LEN, HEAD_DIM, CMP_HID)
    w1bd = jnp.zeros((2, CMP_LEN, NSA_KV * HEAD_DIM, NSA_KV * CMP_HID), f32)
    w2bd = jnp.zeros((2, NSA_KV * CMP_HID, NSA_KV * HEAD_DIM), f32)
    for g in range(NSA_KV):
        w1bd = w1bd.at[:, :, HEAD_DIM * g:HEAD_DIM * (g + 1), CMP_HID * g:CMP_HID * (g + 1)].set(w1)
        w2bd = w2bd.at[:, CMP_HID * g:CMP_HID * (g + 1), HEAD_DIM * g:HEAD_DIM * (g + 1)].set(cmp_w2[l])
    cw8 = jnp.concatenate([conv_w[l], jnp.zeros((8 - CONV_W, conv_w.shape[2]), f32)], axis=0)
    return dict(w_proj=w_proj, pe4=pe4, w1bd=w1bd.astype(bf16), w2bd=w2bd.astype(bf16),
                wa=w_br_a[l].astype(bf16), wb=w_br_b[l].astype(bf16), wc=w_br_c[l].astype(bf16),
                wo=w_o[l].astype(bf16), w_up=w_up[l].astype(bf16), cw8=cw8, cb=conv_b[l].reshape(1, -1),
                wd=w_down[l].astype(bf16))


def _mod_parts(mod_rows, per_row):
    r = mod_rows.shape[0]
    parts = mod_rows.reshape(r, 6, D_MODEL)
    return [parts[:, k].reshape((1, r, D_MODEL) if per_row else (r, 1, D_MODEL)) for k in range(6)]


def kernel(x_prompt, x_sample, cache_nsa, cache_moba, cache_sb, state_win, state_conv, page_table, c_prompt,
           c_sample, norm1_g, norm2_g, w_ada, b_ada, w_in, cmp_pe, cmp_w1, cmp_w2, w_br_a, w_br_b, w_br_c, w_o,
           w_up, conv_w, conv_b, w_down, final_g):
    b, t, d = x_prompt.shape
    s = x_sample.shape[0]
    depth = w_in.shape[0]
    n_phys = cache_nsa.shape[1]
    n_pages = page_table.shape[1]
    past = n_pages * PAGE_SIZE
    kp = past + PAGE_SIZE
    tm = 256
    tq = 128
    tk = 512

    n_c = b + s
    c_all = jnp.concatenate([c_prompt, c_sample, jnp.zeros((-n_c % 8, d), f32)], axis=0)
    mod = _ada_mod(c_all, w_ada, b_ada)

    nc_p = t // CMP_STRIDE
    n_cmp_p = (t - CMP_LEN) // CMP_STRIDE + 1
    agg_p = _agg_matrix(nc_p, n_cmp_p)
    nc_s = past // CMP_STRIDE
    agg_s = _agg_matrix(nc_s, (past + 1 - CMP_LEN) // CMP_STRIDE + 1)
    e_sel = _expand_matrix(SEL_BLOCK, kp)
    e_blk = _expand_matrix(MOBA_BLOCK, kp)
    uu = _suffix_matrix()
    nb_pad = LANES // MOBA_HEADS

    caches = [c.reshape(depth * n_phys, PAGE_SIZE, 512) for c in (cache_nsa, cache_moba, cache_sb)]

    xp = x_prompt.reshape(b * t, d)
    xs = x_sample.reshape(s, d)
    outs_p = [[] for _ in range(5)]
    outs_s = [[] for _ in range(5)]
    for l in range(depth):
        w = _layer_weights(l, w_in, cmp_pe, cmp_w1, cmp_w2, w_br_a, w_br_b, w_br_c, w_o, w_up, conv_w, conv_b,
                           w_down)
        mp = _mod_parts(mod[l, 0:b], per_row=False)
        ms = _mod_parts(mod[l, b:b + s], per_row=True)

        (q_n, nsa_rows, nsa_bf, win_rows, win_bf, mb_q, moba_rows, moba_bf, sb_q, sb_rows, sb_bf, g_m, g_n) = \
            _norm_mod_matmul(xp, norm1_g[l], mp[0], mp[1], w["w_proj"], _IN_SEGS, _IN_DTYPES, tm, "in_proj_prompt")
        r3 = lambda a: a.reshape(b, t, a.shape[1])
        cmp = _cmp_prompt(r3(nsa_rows), w["pe4"], w["w1bd"], w["w2bd"])
        o_a = _nsa_prompt(r3(q_n), r3(g_n), cmp, agg_p, r3(nsa_bf), r3(win_bf), n_cmp_p, tq, tk)
        kmean = _block_mean(r3(moba_rows), 256, MOBA_BLOCK).reshape(b, t // MOBA_BLOCK, MOBA_HEADS, HEAD_DIM)
        kmt = jnp.zeros((b, MOBA_HEADS, nb_pad, MOBA_HEADS, HEAD_DIM), f32)
        for h in range(MOBA_HEADS):
            kmt = kmt.at[:, h, 0:t // MOBA_BLOCK, h].set(kmean[:, :, h])
        o_b = _moba_prompt(r3(mb_q), kmt.reshape(b, LANES, 256), r3(moba_bf), tq, tk)
        o_c = _sb_prompt(r3(sb_q), uu, r3(sb_bf), tq)
        xp = _merge_out(o_a.reshape(b * t, -1), o_b.reshape(b * t, -1), o_c.reshape(b * t, -1), g_m, xp, mp[2],
                        w["wa"], w["wb"], w["wc"], w["wo"], tm)
        u_a, u_b = _norm_mod_matmul(xp, norm2_g[l], mp[3], mp[4], w["w_up"], _UP_SEGS, _UP_DTYPES, tm,
                                    "ffn_up_prompt")
        xp = _ffn_down_seq(u_a, u_b, w["cw8"], w["cb"], w["wd"], xp, mp[5], tm)
        keep = min(WINDOW, t)
        outs_p[0].append(nsa_rows.reshape(b, t // PAGE_SIZE, PAGE_SIZE, 4, NSA_KV, HEAD_DIM))
        outs_p[1].append(moba_rows.reshape(b, t // PAGE_SIZE, PAGE_SIZE, 2, MOBA_HEADS, HEAD_DIM))
        outs_p[2].append(sb_rows.reshape(b, t // PAGE_SIZE, PAGE_SIZE, 2, SB_HEADS, HEAD_DIM))
        outs_p[3].append(r3(win_rows)[:, t - keep:].reshape(b, keep, 2, NSA_KV, HEAD_DIM))
        outs_p[4].append(r3(u_a)[:, t - (CONV_W - 1):])

        (q_n, nsa_rows, _, win_rows, _, mb_q, moba_rows, _, sb_q, sb_rows, _, g_m, g_n) = \
            _norm_mod_matmul(xs, norm1_g[l], ms[0], ms[1], w["w_proj"], _IN_SEGS, _IN_DTYPES, s, "in_proj_sample")
        s3 = lambda a: a.astype(f32).reshape(s, 1, a.shape[1])
        base = l * n_phys
        o_a = _nsa_decode(page_table, caches[0], base, s3(q_n), s3(g_n), s3(nsa_rows), s3(win_rows),
                          state_win[l].reshape(s, -1, 256), w["pe4"], w["w1bd"], w["w2bd"], agg_s, e_sel)
        o_b = _moba_decode(page_table, caches[1], base, s3(mb_q), s3(moba_rows), e_blk)
        o_c = _sb_decode(page_table, caches[2], base, s3(sb_q), uu)
        xs = _merge_out(o_a.reshape(s, -1).astype(bf16), o_b.reshape(s, -1).astype(bf16),
                        o_c.reshape(s, -1).astype(bf16), g_m, xs, ms[2], w["wa"], w["wb"], w["wc"], w["wo"], s)
        u_a, u_b = _norm_mod_matmul(xs, norm2_g[l], ms[3], ms[4], w["w_up"], _UP_SEGS, _UP_DTYPES, s,
                                    "ffn_up_sample")
        xs = _ffn_down_step(u_a, state_conv[l, :, 1], state_conv[l, :, 0], u_b, w["cw8"], w["cb"], w["wd"], xs,
                            ms[5])
        win_full = jnp.concatenate([state_win[l], win_rows.reshape(s, 1, 2, NSA_KV, HEAD_DIM)], axis=1)
        conv_full = jnp.concatenate([state_conv[l], u_a.reshape(s, 1, -1)], axis=1)
        outs_s[0].append(nsa_rows.reshape(s, 1, 4, NSA_KV, HEAD_DIM))
        outs_s[1].append(moba_rows.reshape(s, 1, 2, MOBA_HEADS, HEAD_DIM))
        outs_s[2].append(sb_rows.reshape(s, 1, 2, SB_HEADS, HEAD_DIM))
        outs_s[3].append(win_full[:, win_full.shape[1] - state_win.shape[2]:])
        outs_s[4].append(conv_full[:, conv_full.shape[1] - (CONV_W - 1):])

    y_prompt = _final_norm(xp, final_g, tm).reshape(b, t, d)
    y_sample = _final_norm(xs, final_g, s).reshape(s, 1, d)
    st = lambda lst: jnp.stack(lst)
    return (y_prompt, y_sample, st(outs_p[0]), st(outs_s[0]), st(outs_p[1]), st(outs_s[1]), st(outs_p[2]),
            st(outs_s[2]), st(outs_p[3]), st(outs_s[3]), st(outs_p[4]), st(outs_s[4]))
```

```python
import functools

import numpy as np
import jax
import jax.numpy as jnp
from jax import lax
from jax.experimental import pallas as pl
from jax.experimental.pallas import tpu as pltpu

f32 = jnp.float32
bf16 = jnp.bfloat16

D_MODEL = 1024
HEAD_DIM = 64
NSA_HEADS = 8
NSA_KV = 2
HPG = NSA_HEADS // NSA_KV
CMP_LEN = 32
CMP_STRIDE = 16
CMP_HID = 128
SEL_BLOCK = 64
SEL_TOPN = 16
WINDOW = 512
MOBA_HEADS = 4
MOBA_BLOCK = 256
MOBA_TOPK = 3
SB_HEADS = 4
D_FF = 2816
CONV_W = 3
PAGE_SIZE = 128
RMS_EPS = 1e-6
NEG = -1e30
FORCE = 1e9

LANES = 128
MASK_BIG = 2.0 ** 100
M_INIT = -1e29
REMOVED = -3e38
SB_CUTOFF = -110.0
VMEM_LIMIT_MB = 56

NSA_SLOPES = [2.0 ** (-8.0 * (h + 1) / NSA_HEADS) for h in range(NSA_HEADS)]
MOBA_SLOPES = [2.0 ** (-8.0 * (h + 1) / MOBA_HEADS) for h in range(MOBA_HEADS)]

_Q_N, _NSA, _WIN, _MB_Q, _MOBA, _SB_Q, _SB, _G_M, _G_N, _PROJ_W = 0, 512, 1024, 1280, 1536, 2048, 2304, 2816, 5888, 6016


def _cparams(sem, vmem_mb=VMEM_LIMIT_MB):
    return pltpu.CompilerParams(dimension_semantics=sem, vmem_limit_bytes=vmem_mb * 2 ** 20)


def _const_spec(shape):
    nd = len(shape)
    return pl.BlockSpec(shape, lambda *_: (0,) * nd, pipeline_mode=pl.Buffered(1))


def _nt(a, b):
    return lax.dot_general(a, b, (((1,), (1,)), ((), ())), preferred_element_type=f32)


def _dot(a, b):
    return jnp.dot(a, b, preferred_element_type=f32)


def _gelu_tanh(x):
    return x * (0.5 * (1.0 + jnp.tanh(np.sqrt(2.0 / np.pi) * (x + 0.044715 * (x * x * x)))))


def _ref_softmax(s, valid):
    s = jnp.where(valid, s, NEG)
    e = jnp.where(valid, jnp.exp(s - jnp.max(s, axis=-1, keepdims=True)), 0.0)
    return e / jnp.maximum(jnp.sum(e, axis=-1, keepdims=True), 1e-30)


def _col_const(vals, reps):
    return jnp.concatenate([jnp.full((reps, 1), v, f32) for v in vals], axis=0)


def _ada_body(c_ref, w_ref, b_ref, o_ref):
    c = c_ref[...]
    s = c * jax.nn.sigmoid(c)
    o_ref[...] = _dot(s.astype(bf16), w_ref[...].astype(bf16)) + b_ref[...]


def _ada_mod(c_all, w_ada, b_ada):
    depth, d, n = w_ada.shape
    r = c_all.shape[0]
    tn = 1536
    return pl.pallas_call(
        _ada_body,
        out_shape=jax.ShapeDtypeStruct((depth, r, n), f32),
        grid=(depth, n // tn),
        in_specs=[pl.BlockSpec((r, d), lambda l, j: (0, 0)),
                  pl.BlockSpec((None, d, tn), lambda l, j: (l, 0, j)),
                  pl.BlockSpec((None, 1, tn), lambda l, j: (l, 0, j))],
        out_specs=pl.BlockSpec((None, r, tn), lambda l, j: (l, 0, j)),
        compiler_params=_cparams(("parallel", "parallel")),
        name="ada_mod",
    )(c_all, w_ada, b_ada.reshape(depth, 1, n))


def _nmm_body(x_ref, g_ref, sh_ref, sc_ref, w_ref, *o_refs, segs):
    x = x_ref[...]
    y = x * lax.rsqrt(jnp.mean(x * x, axis=-1, keepdims=True) + RMS_EPS)
    h = (y * g_ref[...]) * (1.0 + sc_ref[...]) + sh_ref[...]
    hb = h.astype(bf16)
    k = 0
    for off, width, scales in segs:
        outs = o_refs[k:k + len(scales)]
        k += len(scales)
        for c0 in range(0, width, 512):
            cw = min(512, width - c0)
            acc = _dot(hb, w_ref[:, off + c0:off + c0 + cw])
            for o, scale in zip(outs, scales):
                o[:, c0:c0 + cw] = (acc if scale == 1.0 else acc * scale).astype(o.dtype)


def _norm_mod_matmul(x, g, shift, scale, w_bf, segs, out_dtypes, tm, name):
    r, d = x.shape
    nb, rb, _ = shift.shape
    tiles_per_b = (r // nb) // tm
    out_shape, out_specs = [], []
    k = 0
    for off, width, scales in segs:
        for _ in scales:
            out_shape.append(jax.ShapeDtypeStruct((r, width), out_dtypes[k]))
            out_specs.append(pl.BlockSpec((tm, width), lambda i: (i, 0)))
            k += 1
    mod_spec = pl.BlockSpec((None, rb, d), lambda i: (i // tiles_per_b, 0, 0))
    return pl.pallas_call(
        functools.partial(_nmm_body, segs=segs),
        out_shape=out_shape,
        grid=(r // tm,),
        in_specs=[pl.BlockSpec((tm, d), lambda i: (i, 0)), _const_spec((1, d)), mod_spec, mod_spec,
                  _const_spec(w_bf.shape)],
        out_specs=out_specs,
        compiler_params=_cparams(("parallel",)),
        name=name,
    )(x, g.reshape(1, d), shift, scale, w_bf)


_IN_SEGS = ((_Q_N, 512, (0.125,)), (_NSA, 512, (1.0, 1.0)), (_WIN, 256, (1.0, 1.0)), (_MB_Q, 256, (0.125,)),
            (_MOBA, 512, (1.0, 1.0)), (_SB_Q, 256, (0.125,)), (_SB, 512, (1.0, 1.0)), (_G_M, 3072, (1.0,)),
            (_G_N, 128, (1.0,)))
_IN_DTYPES = (bf16, f32, bf16, f32, bf16, bf16, f32, bf16, bf16, f32, bf16, f32, f32)
_UP_SEGS = ((0, D_FF, (1.0,)), (D_FF, D_FF, (1.0,)))
_UP_DTYPES = (f32, f32)


def _cmp_core(rows_refs, pe_ref, w1_ref, w2_ref, nc):
    outs = []
    for kv in range(2):
        acc_a = jnp.zeros((nc, NSA_KV * CMP_HID), f32)
        acc_b = jnp.zeros((nc, NSA_KV * CMP_HID), f32)
        for r in range(CMP_STRIDE):
            y = rows_refs[kv][pl.ds(r, nc, stride=CMP_STRIDE), :]
            acc_a = acc_a + _dot((y + pe_ref[kv, r:r + 1, :]).astype(bf16), w1_ref[kv, r])
            acc_b = acc_b + _dot((y + pe_ref[kv, CMP_STRIDE + r:CMP_STRIDE + r + 1, :]).astype(bf16),
                                 w1_ref[kv, CMP_STRIDE + r])
        pre = acc_a + pltpu.roll(acc_b, nc - 1, 0)
        outs.append(_dot(_gelu_tanh(pre).astype(bf16), w2_ref[kv]))
    return jnp.concatenate(outs, axis=1)


def _cmp_prompt_body(k_ref, v_ref, pe_ref, w1_ref, w2_ref, o_ref, *, nc):
    o_ref[...] = _cmp_core((k_ref, v_ref), pe_ref, w1_ref, w2_ref, nc)


def _cmp_prompt(nsa_rows, pe2, w1bd, w2bd):
    b, l, _ = nsa_rows.shape
    nc = l // CMP_STRIDE
    return pl.pallas_call(
        functools.partial(_cmp_prompt_body, nc=nc),
        out_shape=jax.ShapeDtypeStruct((b, nc, 256), f32),
        grid=(b,),
        in_specs=[pl.BlockSpec((None, l, LANES), lambda i: (i, 0, 0)),
                  pl.BlockSpec((None, l, LANES), lambda i: (i, 0, 1)), _const_spec(pe2.shape),
                  _const_spec(w1bd.shape), _const_spec(w2bd.shape)],
        out_specs=pl.BlockSpec((None, nc, 256), lambda i: (i, 0, 0)),
        compiler_params=_cparams(("parallel",)),
        name="nsa_compress_prompt",
    )(nsa_rows, nsa_rows, pe2, w1bd, w2bd)


SLAB = 64
FLASH_SCRATCH = 7


class _Flash:
    def __init__(self, scratch, slopes, tq, t0):
        self.s, self.p, self.m, self.l, self.alpha, self.acc, self.bias = scratch
        self.slope8 = jnp.concatenate(
            [jnp.full((1, 1), v, f32) for v in slopes] + [jnp.zeros((8 - len(slopes), 1), f32)], axis=0)
        self.tq, self.t0 = tq, t0
        self.m[...] = jnp.full(self.m.shape, M_INIT, f32)
        self.l[...] = jnp.zeros(self.l.shape, f32)
        self.acc[...] = jnp.zeros(self.acc.shape, f32)

    def slab(self, j, rel, masked):
        r0 = j * SLAB
        rows = slice(r0, r0 + SLAB)
        head = r0 // self.tq
        s = self.s[rows, :] + self.bias[head:head + 1, :]
        if masked:
            t_rel = (r0 + lax.broadcasted_iota(jnp.int32, (SLAB, 1), 0)) % self.tq
            s = jnp.where(rel <= t_rel, s, -MASK_BIG)
        m_old = self.m[rows, :]
        m_new = jnp.maximum(m_old, jnp.max(s, axis=-1, keepdims=True))
        alpha = jnp.exp(m_old - m_new)
        p = jnp.exp(s - m_new)
        self.l[rows, :] = alpha * self.l[rows, :] + jnp.sum(p, axis=-1, keepdims=True)
        self.m[rows, :] = m_new
        self.alpha[rows, :] = alpha
        self.p[rows, :] = p.astype(bf16)

    def result(self):
        return self.acc[...] / self.l[...]


def _flash_tiles(chains, k0, masked):
    first = chains[0][0]
    rows, tk = first.s.shape
    rel = k0 - first.t0 + lax.broadcasted_iota(jnp.int32, (1, tk), 1)
    for f, q_aug, kt_aug, _ in chains:
        f.s[...] = _dot(q_aug, kt_aug)
        f.bias[...] = f.slope8 * rel.astype(f32)
    for j in range(rows // SLAB):
        for f, _, _, _ in chains:
            f.slab(j, rel, masked)
    for f, _, _, v_t in chains:
        f.acc[...] = f.alpha[...] * f.acc[...] + _dot(f.p[...], v_t)


def _flash_scratch(rows, tk):
    return [pltpu.VMEM((rows, tk), f32), pltpu.VMEM((rows, tk), bf16), pltpu.VMEM((rows, 1), f32),
            pltpu.VMEM((rows, 1), f32), pltpu.VMEM((rows, 1), f32), pltpu.VMEM((rows, LANES), f32),
            pltpu.VMEM((8, tk), f32)]


def _topk_rows(score, ids, k, n_ids):
    picked = jnp.zeros(score.shape, f32)
    for _ in range(k):
        mx = jnp.max(score, axis=0, keepdims=True)
        idx = jnp.min(jnp.where(score == mx, ids, n_ids), axis=0, keepdims=True)
        pick = ids == idx
        picked = jnp.where(pick, 1.0, picked)
        score = jnp.where(pick, REMOVED, score)
    return picked


def _nsa_prompt_body(q_ref, gn_ref, kct_ref, vc_ref, agg_ref, kst_ref, vs_ref, kwt_ref, vw_ref, o_ref, sel_scr,
                     qa_scr, part_scr, gate_scr, *flash_scr, tq, tk, n_cmp):
    i = pl.program_id(1)
    t0 = i * tq
    rows = HPG * tq
    nc = vc_ref.shape[0]
    qf = q_ref[...].astype(f32)
    lane = lax.broadcasted_iota(jnp.int32, (tq, LANES), 1)
    t_row = t0 + lax.broadcasted_iota(jnp.int32, (rows, 1), 0) % tq
    sig = jax.nn.sigmoid(gn_ref[...])
    kct = kct_ref[...]
    vc = vc_ref[...].astype(bf16)
    aggb = agg_ref[...].astype(bf16)
    n_id = lax.broadcasted_iota(jnp.int32, (1, nc), 1)
    blk_id = lax.broadcasted_iota(jnp.int32, (LANES, tq), 0)
    cur = (t0 + lax.broadcasted_iota(jnp.int32, (LANES, tq), 1)) // SEL_BLOCK
    causal_blk = blk_id <= cur
    forced = (blk_id == 0) | (blk_id == cur) | (blk_id == cur - 1)
    chunks = [jnp.zeros((tq, LANES), f32) for _ in range(NSA_HEADS // 2)]

    for g in range(NSA_KV):
        heads = [HPG * g + hh for hh in range(HPG)]
        pieces = []
        for h in heads:
            blk = qf[:, LANES * (h // 2):LANES * (h // 2 + 1)]
            if h % 2 != g:
                blk = pltpu.roll(blk, HEAD_DIM, 1)
            pieces.append(jnp.where(lane // HEAD_DIM == g, blk, 0.0))
        qg = jnp.concatenate(pieces, axis=0).astype(bf16)
        slope = _col_const([NSA_SLOPES[h] for h in heads], tq)

        d_c = t_row - (n_id * CMP_STRIDE + CMP_LEN - 1)
        p_c = _ref_softmax(_dot(qg, kct) - slope * d_c.astype(f32), (d_c >= 0) & (n_id < n_cmp))
        o_c = _dot(p_c.astype(bf16), vc)
        psum = p_c[0:tq] + p_c[tq:2 * tq] + p_c[2 * tq:3 * tq] + p_c[3 * tq:4 * tq]
        p_hi = psum.astype(bf16)
        p_lo = (psum - p_hi.astype(f32)).astype(bf16)
        imp = _dot(p_hi, aggb) + _dot(p_lo, aggb)

        score = jnp.where(causal_blk, jnp.where(forced, FORCE, imp.T), NEG)
        sel_t = jnp.where(causal_blk, _topk_rows(score, blk_id, SEL_TOPN, LANES), 0.0)
        notsel = (1.0 - sel_t).T
        qa_scr[g] = jnp.concatenate([qg, jnp.concatenate([notsel] * HPG, axis=0).astype(bf16)], axis=1)
        sel_scr[g] = sel_t

        wl = WINDOW + tq
        s0 = pl.multiple_of(jnp.maximum(t0 - WINDOW, 0), tq)
        v_w = vw_ref[pl.ds(s0, wl), :]
        d_w = t_row - (s0 + lax.broadcasted_iota(jnp.int32, (1, wl), 1))
        p_w = _ref_softmax(_dot(qg, kwt_ref[:, pl.ds(s0, wl)]) - slope * d_w.astype(f32),
                           (d_w >= 0) & (d_w < WINDOW))
        o_w = _dot(p_w.astype(bf16), v_w)

        gates = [jnp.concatenate([sig[:, br * NSA_HEADS + h:br * NSA_HEADS + h + 1] for h in heads], axis=0)
                 for br in range(3)]
        part_scr[g] = gates[0] * o_c + gates[2] * o_w
        gate_scr[g] = gates[1]

    flashes = [_Flash(flash_scr[FLASH_SCRATCH * g:FLASH_SCRATCH * (g + 1)],
                      NSA_SLOPES[HPG * g:HPG * (g + 1)], tq, t0) for g in range(NSA_KV)]

    def sel_step(kt, masked):
        k0 = pl.multiple_of(kt * tk, tk)
        kt_aug = kst_ref[:, pl.ds(k0, tk)]
        v_t = vs_ref[pl.ds(k0, tk), :]
        _flash_tiles([(flashes[g], qa_scr[g], kt_aug, v_t) for g in range(NSA_KV)], k0, masked)

    def sel_loop(kt, carry):
        blk0 = pl.multiple_of(kt * (tk // SEL_BLOCK), tk // SEL_BLOCK)
        picked = jnp.maximum(sel_scr[0, pl.ds(blk0, tk // SEL_BLOCK), :], sel_scr[1, pl.ds(blk0, tk // SEL_BLOCK), :])

        @pl.when(jnp.max(picked) > 0.0)
        def _():
            sel_step(kt, False)

        return carry

    kd = t0 // tk
    lax.fori_loop(0, kd, sel_loop, 0)
    sel_step(kd, True)

    for g in range(NSA_KV):
        heads = [HPG * g + hh for hh in range(HPG)]
        o = part_scr[g] + gate_scr[g] * flashes[g].result()
        for hh, h in enumerate(heads):
            piece = o[hh * tq:(hh + 1) * tq]
            if h % 2 != g:
                piece = pltpu.roll(piece, HEAD_DIM, 1)
            chunks[h // 2] = chunks[h // 2] + jnp.where(lane // HEAD_DIM == h % 2, piece, 0.0)

    o_ref[...] = jnp.concatenate(chunks, axis=1).astype(o_ref.dtype)


def _nsa_prompt(q_n, g_n, cmp, agg, nsa_bf, win_bf, n_cmp, tq, tk):
    b, l, _ = q_n.shape
    nc = cmp.shape[1]
    rows = HPG * tq
    kct = jnp.swapaxes(cmp[:, :, 0:LANES], 1, 2).astype(bf16)
    onehot_t = jnp.broadcast_to(_block_onehot_t(SEL_BLOCK, l)[None], (b, LANES, l))
    kst = jnp.concatenate([jnp.swapaxes(nsa_bf[:, :, 2 * LANES:3 * LANES], 1, 2), onehot_t], axis=1)
    kwt = jnp.swapaxes(win_bf[:, :, 0:LANES], 1, 2)
    per_b = lambda shape, col=0: pl.BlockSpec((None,) + shape, lambda bi, i: (bi, 0, col))
    return pl.pallas_call(
        functools.partial(_nsa_prompt_body, tq=tq, tk=tk, n_cmp=n_cmp),
        out_shape=jax.ShapeDtypeStruct((b, l, NSA_HEADS * HEAD_DIM), bf16),
        grid=(b, l // tq),
        in_specs=[pl.BlockSpec((None, tq, 512), lambda bi, i: (bi, i, 0)),
                  pl.BlockSpec((None, tq, LANES), lambda bi, i: (bi, i, 0)),
                  per_b((LANES, nc)), per_b((nc, LANES), 1), _const_spec(agg.shape),
                  per_b((2 * LANES, l)), per_b((l, LANES), 3), per_b((LANES, l)), per_b((l, LANES), 1)],
        out_specs=pl.BlockSpec((None, tq, 512), lambda bi, i: (bi, i, 0)),
        scratch_shapes=[pltpu.VMEM((NSA_KV, LANES, tq), f32), pltpu.VMEM((NSA_KV, rows, 2 * LANES), bf16),
                        pltpu.VMEM((NSA_KV, rows, LANES), f32), pltpu.VMEM((NSA_KV, rows, 1), f32)]
        + _flash_scratch(rows, tk) * NSA_KV,
        compiler_params=_cparams(("parallel", "parallel")),
        name="nsa_attention_prompt",
    )(q_n, g_n, kct, cmp, agg, kst, nsa_bf, kwt, win_bf)


def _block_mean_body(k_ref, o_ref):
    o_ref[...] = jnp.mean(k_ref[...], axis=0, keepdims=True)


def _block_mean(rows, width, blk):
    b, l, _ = rows.shape
    return pl.pallas_call(
        _block_mean_body,
        out_shape=jax.ShapeDtypeStruct((b, l // blk, 1, width), f32),
        grid=(b, l // blk),
        in_specs=[pl.BlockSpec((None, blk, width), lambda bi, j: (bi, j, 0))],
        out_specs=pl.BlockSpec((None, None, 1, width), lambda bi, j: (bi, j, 0, 0)),
        compiler_params=_cparams(("parallel", "parallel")),
        name="moba_block_mean",
    )(rows)


def _moba_prompt_body(q_ref, kmt_ref, kt_ref, v_ref, o_ref, qa_scr, *flash_scr, tq, tk, nb_pad):
    i = pl.program_id(1)
    t0 = i * tq
    q = q_ref[...]
    qf = q.astype(f32)
    lane = lax.broadcasted_iota(jnp.int32, (tq, LANES), 1)

    gate_t = _nt(kmt_ref[...].astype(bf16), q)
    blk_id = lax.broadcasted_iota(jnp.int32, (LANES, tq), 0) % nb_pad
    own = (t0 + lax.broadcasted_iota(jnp.int32, (LANES, tq), 1)) // MOBA_BLOCK
    past = blk_id < own
    score = jnp.where(past, gate_t, NEG)
    parts = [_topk_rows(score[nb_pad * h:nb_pad * (h + 1)], blk_id[nb_pad * h:nb_pad * (h + 1)], MOBA_TOPK, nb_pad)
             for h in range(MOBA_HEADS)]
    sel_t = jnp.where(past, jnp.concatenate(parts, axis=0), 0.0)
    sel_t = jnp.where(blk_id == own, 1.0, sel_t)
    notsel = (1.0 - sel_t).T

    nch = MOBA_HEADS // 2
    for c in range(nch):
        q_rows = []
        for e in range(2):
            h = 2 * c + e
            qh = jnp.where(lane // HEAD_DIM == e, qf[:, LANES * c:LANES * (c + 1)], 0.0)
            ns = notsel if h == 0 else pltpu.roll(notsel, LANES - nb_pad * h, 1)
            ns = jnp.where(lane < nb_pad, ns, 0.0)
            q_rows.append(jnp.concatenate([qh, ns], axis=1))
        qa_scr[c] = jnp.concatenate(q_rows, axis=0).astype(bf16)
    flashes = [_Flash(flash_scr[FLASH_SCRATCH * c:FLASH_SCRATCH * (c + 1)],
                      MOBA_SLOPES[2 * c:2 * c + 2], tq, t0) for c in range(nch)]

    def step(kt, masked):
        k0 = pl.multiple_of(kt * tk, tk)
        _flash_tiles([(flashes[c], qa_scr[c], kt_ref[c, :, pl.ds(k0, tk)],
                       v_ref[pl.ds(k0, tk), LANES * c:LANES * (c + 1)]) for c in range(nch)], k0, masked)

    def loop(kt, carry):
        step(kt, False)
        return carry

    lax.fori_loop(0, t0 // tk, loop, 0)
    step(t0 // tk, True)
    out_chunks = []
    for c in range(nch):
        o = flashes[c].result()
        out_chunks.append(jnp.where(lane < HEAD_DIM, o[0:tq], o[tq:2 * tq]))
    o_ref[...] = jnp.concatenate(out_chunks, axis=1).astype(o_ref.dtype)


def _moba_prompt(mb_q, kmt, moba_bf, tq, tk):
    b, l, _ = mb_q.shape
    nb_pad = LANES // MOBA_HEADS
    nch = MOBA_HEADS // 2
    onehot_t = jnp.broadcast_to(_block_onehot_t(MOBA_BLOCK, l)[None, None], (b, nch, LANES, l))
    k_t = jnp.swapaxes(moba_bf[:, :, 0:nch * LANES].reshape(b, l, nch, LANES), 1, 3)
    kt_aug = jnp.concatenate([jnp.swapaxes(k_t, 1, 2), onehot_t], axis=2)
    return pl.pallas_call(
        functools.partial(_moba_prompt_body, tq=tq, tk=tk, nb_pad=nb_pad),
        out_shape=jax.ShapeDtypeStruct((b, l, MOBA_HEADS * HEAD_DIM), bf16),
        grid=(b, l // tq),
        in_specs=[pl.BlockSpec((None, tq, 256), lambda bi, i: (bi, i, 0)),
                  pl.BlockSpec((None, LANES, 256), lambda bi, i: (bi, 0, 0)),
                  pl.BlockSpec((None, nch, 2 * LANES, l), lambda bi, i: (bi, 0, 0, 0)),
                  pl.BlockSpec((None, l, nch * LANES), lambda bi, i: (bi, 0, 1))],
        out_specs=pl.BlockSpec((None, tq, 256), lambda bi, i: (bi, i, 0)),
        scratch_shapes=[pltpu.VMEM((nch, 2 * tq, 2 * LANES), bf16)] + _flash_scratch(2 * tq, tk) * nch,
        compiler_params=_cparams(("parallel", "parallel")),
        name="moba_attention_prompt",
    )(mb_q, kmt, kt_aug, moba_bf)


def _log_keep(z):
    return -(jnp.maximum(z, 0.0) + jnp.log(1.0 + jnp.exp(-jnp.abs(z))))


def _suffix_sums(lk, uu):
    hi = lk.astype(bf16)
    lo = (lk - hi.astype(f32)).astype(bf16)
    r = _dot(jnp.concatenate([hi, lo], axis=1), uu)
    return r[:, 0:LANES], r[:, LANES:2 * LANES]


def _sb_prompt_body(q_ref, uu_ref, k_ref, v_ref, o_ref, carry_scr, acc_scr, *, tq):
    i = pl.program_id(2)
    t0 = i * tq
    qf = q_ref[...].astype(f32)
    lane = lax.broadcasted_iota(jnp.int32, (tq, LANES), 1)
    q2 = jnp.concatenate([jnp.where(lane // HEAD_DIM == e, qf, 0.0) for e in range(2)], axis=0).astype(bf16)
    t_row = t0 + lax.broadcasted_iota(jnp.int32, (2 * tq, 1), 0) % tq
    uu = uu_ref[...]
    carry_scr[...] = jnp.zeros(carry_scr.shape, f32)
    acc_scr[...] = jnp.zeros(acc_scr.shape, f32)

    def step(kc, diag):
        k0 = pl.multiple_of(kc * LANES, LANES)
        z = _dot(q2, k_ref[:, pl.ds(k0, LANES)])
        lk = _log_keep(z)
        if diag:
            is_past = (k0 + lax.broadcasted_iota(jnp.int32, (1, LANES), 1)) < t_row
            lk = jnp.where(is_past, lk, 0.0)
        later, total = _suffix_sums(lk, uu)
        w = jnp.exp(z + lk + later + carry_scr[...])
        if diag:
            w = jnp.where(is_past, w, 0.0)
        acc_scr[...] = acc_scr[...] + _dot(w.astype(bf16), v_ref[pl.ds(k0, LANES), :])
        carry_scr[...] = carry_scr[...] + total

    n_diag = tq // LANES
    for dgi in range(n_diag):
        step(i * n_diag + (n_diag - 1 - dgi), True)

    def more(state):
        j, top = state
        return (j < i * n_diag) & (top > SB_CUTOFF)

    def walk(state):
        j, _ = state
        step(i * n_diag - 1 - j, False)
        return j + 1, jnp.max(carry_scr[...])

    lax.while_loop(more, walk, (0, jnp.max(carry_scr[...])))
    o = acc_scr[...]
    o_ref[...] = jnp.where(lane < HEAD_DIM, o[0:tq], o[tq:2 * tq]).astype(o_ref.dtype)


def _sb_prompt(sb_q, uu, sb_bf, tq):
    b, l, _ = sb_q.shape
    nch = SB_HEADS // 2
    k_t = jnp.swapaxes(jnp.swapaxes(sb_bf[:, :, 0:nch * LANES].reshape(b, l, nch, LANES), 1, 3), 1, 2)
    return pl.pallas_call(
        functools.partial(_sb_prompt_body, tq=tq),
        out_shape=jax.ShapeDtypeStruct((b, l, SB_HEADS * HEAD_DIM), bf16),
        grid=(b, nch, l // tq),
        in_specs=[pl.BlockSpec((None, tq, LANES), lambda bi, c, i: (bi, i, c)),
                  _const_spec(uu.shape),
                  pl.BlockSpec((None, None, LANES, l), lambda bi, c, i: (bi, c, 0, 0)),
                  pl.BlockSpec((None, l, LANES), lambda bi, c, i: (bi, 0, nch + c))],
        out_specs=pl.BlockSpec((None, tq, LANES), lambda bi, c, i: (bi, i, c)),
        scratch_shapes=[pltpu.VMEM((2 * tq, LANES), f32), pltpu.VMEM((2 * tq, LANES), f32)],
        compiler_params=_cparams(("parallel", "parallel", "parallel")),
        name="stickbreak_attention_prompt",
    )(sb_q, uu, k_t, sb_bf)


def _merge_body(oa_ref, ob_ref, oc_ref, gm_ref, x_ref, gate_ref, wa_ref, wb_ref, wc_ref, wo_ref, o_ref):
    d = x_ref.shape[1]
    g = jax.nn.sigmoid(gm_ref[...])
    merged = (g[:, 0:d] * _dot(oa_ref[...], wa_ref[...]) + g[:, d:2 * d] * _dot(ob_ref[...], wb_ref[...])
              + g[:, 2 * d:3 * d] * _dot(oc_ref[...], wc_ref[...]))
    o_ref[...] = x_ref[...] + gate_ref[...] * _dot(merged.astype(bf16), wo_ref[...])


def _merge_out(o_a, o_b, o_c, g_m, x, gate, wa, wb, wc, wo, tm):
    r, d = x.shape
    nb, rb, _ = gate.shape
    tiles_per_b = (r // nb) // tm
    row = lambda w: pl.BlockSpec((tm, w), lambda i: (i, 0))
    return pl.pallas_call(
        _merge_body,
        out_shape=jax.ShapeDtypeStruct((r, d), f32),
        grid=(r // tm,),
        in_specs=[row(o_a.shape[1]), row(o_b.shape[1]), row(o_c.shape[1]), row(3 * d), row(d),
                  pl.BlockSpec((None, rb, d), lambda i: (i // tiles_per_b, 0, 0)),
                  _const_spec(wa.shape), _const_spec(wb.shape), _const_spec(wc.shape), _const_spec(wo.shape)],
        out_specs=row(d),
        compiler_params=_cparams(("parallel",)),
        name="merge_out_proj",
    )(o_a, o_b, o_c, g_m, x, gate, wa, wb, wc, wo)


def _ffn_tail(a, a_m1, a_m2, b, cw_ref, cb_ref, wd_ref, x_ref, gate_ref, o_ref):
    conv = cb_ref[...] + a_m2 * cw_ref[0:1, :]
    conv = conv + a_m1 * cw_ref[1:2, :]
    conv = conv + a * cw_ref[2:3, :]
    y = _dot((_gelu_tanh(conv) * b).astype(bf16), wd_ref[...])
    o_ref[...] = x_ref[...] + gate_ref[...] * y


def _ffn_seq_body(a_ref, halo_ref, b_ref, cw_ref, cb_ref, wd_ref, x_ref, gate_ref, o_ref, *, tiles_per_b):
    a = a_ref[...]
    tm = a.shape[0]
    first = pl.program_id(0) % tiles_per_b == 0
    halo = jnp.where(first, 0.0, halo_ref[...])
    row = lax.broadcasted_iota(jnp.int32, a.shape, 0)
    a_m1 = jnp.where(row < 1, halo[7:8, :], pltpu.roll(a, 1, 0))
    a_m2 = jnp.where(row < 1, halo[6:7, :], jnp.where(row < 2, halo[7:8, :], pltpu.roll(a, 2, 0)))
    _ffn_tail(a, a_m1, a_m2, b_ref[...], cw_ref, cb_ref, wd_ref, x_ref, gate_ref, o_ref)


def _ffn_step_body(a_ref, am1_ref, am2_ref, b_ref, cw_ref, cb_ref, wd_ref, x_ref, gate_ref, o_ref):
    _ffn_tail(a_ref[...], am1_ref[...], am2_ref[...], b_ref[...], cw_ref, cb_ref, wd_ref, x_ref, gate_ref, o_ref)


def _ffn_down_seq(u_a, u_b, cw8, cb, wd, x, gate, tm):
    r, d = x.shape
    ff = u_a.shape[1]
    nb = gate.shape[0]
    tiles_per_b = (r // nb) // tm
    row = lambda w: pl.BlockSpec((tm, w), lambda i: (i, 0))
    return pl.pallas_call(
        functools.partial(_ffn_seq_body, tiles_per_b=tiles_per_b),
        out_shape=jax.ShapeDtypeStruct((r, d), f32),
        grid=(r // tm,),
        in_specs=[row(ff), pl.BlockSpec((8, ff), lambda i: (jnp.maximum(i * (tm // 8) - 1, 0), 0)), row(ff),
                  _const_spec(cw8.shape), _const_spec(cb.shape), _const_spec(wd.shape), row(d),
                  pl.BlockSpec((None, 1, d), lambda i: (i // tiles_per_b, 0, 0))],
        out_specs=row(d),
        compiler_params=_cparams(("parallel",)),
        name="conv_ffn_down_seq",
    )(u_a, u_a, u_b, cw8, cb, wd, x, gate)


def _ffn_down_step(u_a, a_m1, a_m2, u_b, cw8, cb, wd, x, gate):
    r, d = x.shape
    full = lambda a: pl.BlockSpec(a.shape, lambda i: (0,) * a.ndim)
    return pl.pallas_call(
        _ffn_step_body,
        out_shape=jax.ShapeDtypeStruct((r, d), f32),
        grid=(1,),
        in_specs=[full(u_a), full(a_m1), full(a_m2), full(u_b), full(cw8), full(cb), full(wd), full(x),
                  pl.BlockSpec((None, r, d), lambda i: (0, 0, 0))],
        out_specs=full(x),
        compiler_params=_cparams(("arbitrary",)),
        name="conv_ffn_down_step",
    )(u_a, a_m1, a_m2, u_b, cw8, cb, wd, x, gate)


def _final_norm_body(x_ref, g_ref, o_ref):
    x = x_ref[...]
    o_ref[...] = x * lax.rsqrt(jnp.mean(x * x, axis=-1, keepdims=True) + RMS_EPS) * g_ref[...]


def _final_norm(x, g, tm):
    r, d = x.shape
    return pl.pallas_call(
        _final_norm_body,
        out_shape=jax.ShapeDtypeStruct((r, d), f32),
        grid=(r // tm,),
        in_specs=[pl.BlockSpec((tm, d), lambda i: (i, 0)), _const_spec((1, d))],
        out_specs=pl.BlockSpec((tm, d), lambda i: (i, 0)),
        compiler_params=_cparams(("parallel",)),
        name="final_rmsnorm",
    )(x, g.reshape(1, d))


def _page_specs(n_pages, width, layer_base):
    return [pl.BlockSpec((None, PAGE_SIZE, width),
                         functools.partial(lambda s, pt, j: (layer_base + pt[s * n_pages + j], 0, 0), j=j))
            for j in range(n_pages)]


def _per_seq(width):
    return pl.BlockSpec((None, 1, width), lambda s, pt: (s, 0, 0))


def _dec_const(shape):
    nd = len(shape)
    return pl.BlockSpec(shape, lambda s, pt: (0,) * nd, pipeline_mode=pl.Buffered(1))


def _head_rows(q_row, heads_per_chunk_rows=8):
    row = lax.broadcasted_iota(jnp.int32, (8, LANES), 0)
    lane = lax.broadcasted_iota(jnp.int32, (8, LANES), 1)
    q8 = jnp.broadcast_to(q_row, (8, q_row.shape[1]))
    qsel = jnp.zeros((8, LANES), f32)
    for c in range(NSA_HEADS // 2):
        qsel = qsel + jnp.where(row // 2 == c, q8[:, LANES * c:LANES * (c + 1)], 0.0)
    swap = (row % 2) != (row // HPG)
    qm = jnp.where(swap, pltpu.roll(qsel, HEAD_DIM, 1), qsel)
    return jnp.where(lane // HEAD_DIM == row // HPG, qm, 0.0), swap


def _rank_select(score_row, k):
    a = jnp.broadcast_to(score_row, (LANES, LANES))
    b = a.T
    ii = lax.broadcasted_iota(jnp.int32, (LANES, LANES), 0)
    jj = lax.broadcasted_iota(jnp.int32, (LANES, LANES), 1)
    ahead = (b > a) | ((b == a) & (ii < jj))
    rank = jnp.sum(jnp.where(ahead, 1.0, 0.0), axis=0, keepdims=True)
    return jnp.where(rank < k, 1.0, 0.0)


def _nsa_dec_body(pt_ref, q_ref, gn_ref, new_ref, wnew_ref, sw_ref, pe_ref, w1_ref, w2_ref, agg_ref, e_ref,
                  *rest, n_pages):
    pages, o_ref = rest[:n_pages], rest[n_pages]
    kc_scr, vc_scr, ks_scr, vs_scr = rest[n_pages + 1:]
    past = n_pages * PAGE_SIZE
    kp = ks_scr.shape[0]
    tail_row = lax.broadcasted_iota(jnp.int32, (kp - past, LANES), 0)
    new_row = new_ref[...]
    for k, scr in enumerate((kc_scr, vc_scr, ks_scr, vs_scr)):
        for j in range(n_pages):
            scr[PAGE_SIZE * j:PAGE_SIZE * (j + 1), :] = pages[j][:, LANES * k:LANES * (k + 1)]
        if scr.shape[0] > past:
            scr[past:kp, :] = jnp.where(tail_row == 0, new_row[:, LANES * k:LANES * (k + 1)], 0.0)

    nc = past // CMP_STRIDE
    n_cmp = (past + 1 - CMP_LEN) // CMP_STRIDE + 1
    cmpv = _cmp_core((kc_scr, vc_scr), pe_ref, w1_ref, w2_ref, nc)

    row = lax.broadcasted_iota(jnp.int32, (8, LANES), 0)
    lane = lax.broadcasted_iota(jnp.int32, (8, LANES), 1)
    qm, swap = _head_rows(q_ref[...])
    qmb = qm.astype(bf16)
    slope = jnp.zeros((8, 1), f32)
    row1 = lax.broadcasted_iota(jnp.int32, (8, 1), 0)
    for h in range(NSA_HEADS):
        slope = jnp.where(row1 == h, NSA_SLOPES[h], slope)
    grp0 = row < HPG

    d_c = past - (lane[0:1] * CMP_STRIDE + CMP_LEN - 1)
    s_c = _nt(qmb, cmpv[:, 0:LANES].astype(bf16)) - slope * d_c.astype(f32)
    p_c = _ref_softmax(s_c, (d_c >= 0) & (lane[0:1] < n_cmp))
    o_c = _dot(p_c.astype(bf16), cmpv[:, LANES:2 * LANES].astype(bf16))
    aggb = agg_ref[...].astype(bf16)
    p_hi = p_c.astype(bf16)
    p_lo = (p_c - p_hi.astype(f32)).astype(bf16)
    imp_rows = _dot(p_hi, aggb) + _dot(p_lo, aggb)

    cur = past // SEL_BLOCK
    blk = lane[0:1]
    forced = (blk == 0) | (blk == cur) | (blk == cur - 1)
    causal = blk <= cur
    notsel_g = []
    for g in range(NSA_KV):
        imp = jnp.sum(imp_rows[HPG * g:HPG * (g + 1)], axis=0, keepdims=True)
        score = jnp.where(causal, jnp.where(forced, FORCE, imp), NEG)
        sel = jnp.where(causal, _rank_select(score, min(SEL_TOPN, -(-(past + 1) // SEL_BLOCK))), 0.0)
        notsel_g.append(jnp.broadcast_to(1.0 - sel, (8, LANES)))
    notsel = jnp.where(grp0, notsel_g[0], notsel_g[1])

    pos = lax.broadcasted_iota(jnp.int32, (1, kp), 1)
    dist = past - pos
    blocked = _dot(notsel.astype(bf16), e_ref[...])
    s_s = _nt(qmb, ks_scr[...].astype(bf16)) - slope * dist.astype(f32)
    p_s = _ref_softmax(s_s, (blocked < 0.5) & (dist >= 0))
    o_s = _dot(p_s.astype(bf16), vs_scr[...].astype(bf16))

    wb = sw_ref.shape[0]
    idx = lax.broadcasted_iota(jnp.int32, (1, wb), 1)
    d_w = wb - idx
    s_w = _nt(qmb, sw_ref[:, 0:LANES].astype(bf16)) - slope * d_w.astype(f32)
    valid_w = (d_w < WINDOW) & (d_w >= 0)
    s_w = jnp.where(valid_w, s_w, NEG)
    wnew = wnew_ref[...]
    s_n = jnp.sum(qm * wnew[:, 0:LANES], axis=-1, keepdims=True)
    m_w = jnp.maximum(jnp.max(s_w, axis=-1, keepdims=True), s_n)
    e_w = jnp.where(valid_w, jnp.exp(s_w - m_w), 0.0)
    e_n = jnp.exp(s_n - m_w)
    den = jnp.maximum(jnp.sum(e_w, axis=-1, keepdims=True) + e_n, 1e-30)
    o_w = (_dot(e_w.astype(bf16), sw_ref[:, LANES:2 * LANES].astype(bf16)) + e_n * wnew[:, LANES:2 * LANES]) / den

    sig = jnp.broadcast_to(jax.nn.sigmoid(gn_ref[...]), (8, LANES))
    gates = [jnp.sum(jnp.where(lane == br * NSA_HEADS + row, sig, 0.0), axis=-1, keepdims=True) for br in range(3)]
    o = gates[0] * o_c + gates[1] * o_s + gates[2] * o_w
    o = jnp.where(swap, pltpu.roll(o, HEAD_DIM, 1), o)
    o = jnp.where(lane // HEAD_DIM == row % 2, o, 0.0)
    o_ref[...] = jnp.concatenate([o[2 * c:2 * c + 1] + o[2 * c + 1:2 * c + 2] for c in range(NSA_HEADS // 2)], axis=1)


def _nsa_decode(page_table, cache, layer_base, q, g_n, nsa_new, win_new, state_win_l, pe4, w1bd, w2bd, agg, e_sel):
    s, n_pages = page_table.shape
    kp = e_sel.shape[1]
    consts = (pe4, w1bd, w2bd, agg, e_sel)
    grid_spec = pltpu.PrefetchScalarGridSpec(
        num_scalar_prefetch=1,
        grid=(s,),
        in_specs=[_per_seq(512), _per_seq(LANES), _per_seq(512), _per_seq(256),
                  pl.BlockSpec((None,) + state_win_l.shape[1:], lambda si, pt: (si, 0, 0))]
        + [_dec_const(c.shape) for c in consts] + _page_specs(n_pages, 512, layer_base),
        out_specs=_per_seq(512),
        scratch_shapes=[pltpu.VMEM((kp - PAGE_SIZE, LANES), f32), pltpu.VMEM((kp - PAGE_SIZE, LANES), f32),
                        pltpu.VMEM((kp, LANES), f32), pltpu.VMEM((kp, LANES), f32)],
    )
    return pl.pallas_call(
        functools.partial(_nsa_dec_body, n_pages=n_pages),
        out_shape=jax.ShapeDtypeStruct((s, 1, 512), f32),
        grid_spec=grid_spec,
        compiler_params=_cparams(("parallel",)),
        name="nsa_attention_decode",
    )(page_table.reshape(-1), q, g_n, nsa_new, win_new, state_win_l, *consts, *([cache] * n_pages))


def _q_head_rows4(q_row):
    row = lax.broadcasted_iota(jnp.int32, (8, 256), 0)
    lane = lax.broadcasted_iota(jnp.int32, (8, 256), 1)
    own = lane // HEAD_DIM == row
    return jnp.where(own, jnp.broadcast_to(q_row, (8, 256)), 0.0), own


def _moba_dec_body(pt_ref, q_ref, new_ref, e_ref, *rest, n_pages):
    pages, o_ref, kv_scr = rest[:n_pages], rest[n_pages], rest[n_pages + 1]
    past = n_pages * PAGE_SIZE
    kp = kv_scr.shape[0]
    for j in range(n_pages):
        kv_scr[PAGE_SIZE * j:PAGE_SIZE * (j + 1), :] = pages[j][...]
    tail_row = lax.broadcasted_iota(jnp.int32, (kp - past, 512), 0)
    kv_scr[past:kp, :] = jnp.where(tail_row == 0, new_ref[...], 0.0)

    nb_past = past // MOBA_BLOCK
    qm, own_lanes = _q_head_rows4(q_ref[...])
    qmb = qm.astype(bf16)
    km_row = lax.broadcasted_iota(jnp.int32, (LANES, 256), 0)
    km = jnp.zeros((LANES, 256), f32)
    for j in range(nb_past):
        mean_j = jnp.mean(kv_scr[MOBA_BLOCK * j:MOBA_BLOCK * (j + 1), 0:256], axis=0, keepdims=True)
        km = jnp.where(km_row == j, mean_j, km)
    gate = _nt(qmb, km.astype(bf16))

    lane = lax.broadcasted_iota(jnp.int32, (8, LANES), 1)
    is_past = lane < nb_past
    score = jnp.where(is_past, gate, NEG)
    rank = jnp.zeros((8, LANES), f32)
    for i in range(nb_past):
        gi = score[:, i:i + 1]
        rank = rank + jnp.where((gi > score) | ((gi == score) & (i < lane)), 1.0, 0.0)
    sel = (is_past & (rank < MOBA_TOPK)) | (lane == nb_past)
    notsel = jnp.where(sel, 0.0, 1.0)

    row1 = lax.broadcasted_iota(jnp.int32, (8, 1), 0)
    slope = jnp.zeros((8, 1), f32)
    for h in range(MOBA_HEADS):
        slope = jnp.where(row1 == h, MOBA_SLOPES[h], slope)
    pos = lax.broadcasted_iota(jnp.int32, (1, kp), 1)
    dist = past - pos
    blocked = _dot(notsel.astype(bf16), e_ref[...])
    s = _nt(qmb, kv_scr[:, 0:256].astype(bf16)) - slope * dist.astype(f32)
    p = _ref_softmax(s, (blocked < 0.5) & (dist >= 0))
    o = _dot(p.astype(bf16), kv_scr[:, 256:512].astype(bf16))
    o_ref[...] = jnp.sum(jnp.where(own_lanes, o, 0.0), axis=0, keepdims=True)


def _moba_decode(page_table, cache, layer_base, q, moba_new, e_blk):
    s, n_pages = page_table.shape
    kp = e_blk.shape[1]
    grid_spec = pltpu.PrefetchScalarGridSpec(
        num_scalar_prefetch=1,
        grid=(s,),
        in_specs=[_per_seq(256), _per_seq(512), _dec_const(e_blk.shape)] + _page_specs(n_pages, 512, layer_base),
        out_specs=_per_seq(256),
        scratch_shapes=[pltpu.VMEM((kp, 512), f32)],
    )
    return pl.pallas_call(
        functools.partial(_moba_dec_body, n_pages=n_pages),
        out_shape=jax.ShapeDtypeStruct((s, 1, 256), f32),
        grid_spec=grid_spec,
        compiler_params=_cparams(("parallel",)),
        name="moba_attention_decode",
    )(page_table.reshape(-1), q, moba_new, e_blk, *([cache] * n_pages))


def _sb_dec_body(pt_ref, q_ref, uu_ref, *rest, n_pages):
    pages, o_ref = rest[:n_pages], rest[n_pages]
    qm, own_lanes = _q_head_rows4(q_ref[...])
    qmb = qm.astype(bf16)
    uu = uu_ref[...]
    carry = jnp.zeros((8, LANES), f32)
    acc = jnp.zeros((8, 256), f32)
    for j in reversed(range(n_pages)):
        z = _nt(qmb, pages[j][:, 0:256].astype(bf16))
        lk = _log_keep(z)
        later, total = _suffix_sums(lk, uu)
        w = jnp.exp(z + lk + later + carry)
        acc = acc + _dot(w.astype(bf16), pages[j][:, 256:512].astype(bf16))
        carry = carry + total
    o_ref[...] = jnp.sum(jnp.where(own_lanes, acc, 0.0), axis=0, keepdims=True)


def _sb_decode(page_table, cache, layer_base, q, uu):
    s, n_pages = page_table.shape
    grid_spec = pltpu.PrefetchScalarGridSpec(
        num_scalar_prefetch=1,
        grid=(s,),
        in_specs=[_per_seq(256), _dec_const(uu.shape)] + _page_specs(n_pages, 512, layer_base),
        out_specs=_per_seq(256),
    )
    return pl.pallas_call(
        functools.partial(_sb_dec_body, n_pages=n_pages),
        out_shape=jax.ShapeDtypeStruct((s, 1, 256), f32),
        grid_spec=grid_spec,
        compiler_params=_cparams(("parallel",)),
        name="stickbreak_attention_decode",
    )(page_table.reshape(-1), q, uu, *([cache] * n_pages))


def _agg_matrix(nc, n_cmp):
    c0 = np.arange(nc)[:, None] * CMP_STRIDE
    s0 = np.arange(LANES)[None, :] * SEL_BLOCK
    ov = np.clip(np.minimum(c0 + CMP_LEN, s0 + SEL_BLOCK) - np.maximum(c0, s0), 0, None) / CMP_LEN
    ov[n_cmp:] = 0.0
    return jnp.asarray(ov, f32)


def _block_onehot_t(block, l):
    e = (np.arange(l)[None, :] // block) == np.arange(LANES)[:, None]
    return jnp.asarray(np.where(e, -MASK_BIG, 0.0), bf16)


def _expand_matrix(block, kp):
    e = (np.arange(kp)[None, :] // block) == np.arange(LANES)[:, None]
    return jnp.asarray(e, bf16)


def _suffix_matrix():
    j = np.arange(2 * LANES)[:, None] % LANES
    s = np.arange(2 * LANES)[None, :]
    return jnp.asarray((s >= LANES) | (j > s), bf16)


def _layer_weights(l, w_in, cmp_pe, cmp_w1, cmp_w2, w_br_a, w_br_b, w_br_c, w_o, w_up, conv_w, conv_b, w_down):
    d = w_in.shape[1]
    w = w_in[l]
    w_proj = jnp.concatenate([w[:, 0:1280], w[:, 1304:], w[:, 1280:1304], jnp.zeros((d, _PROJ_W - 5912), f32)],
                             axis=1).astype(bf16)
    pe4 = jnp.concatenate([cmp_pe[l], cmp_pe[l]], axis=2)
    w1 = cmp_w1[l].reshape(2, CMP_LEN, HEAD_DIM, CMP_HID)
    w1bd = jnp.zeros((2, CMP_LEN, NSA_KV * HEAD_DIM, NSA_KV * CMP_HID), f32)
    w2bd = jnp.zeros((2, NSA_KV * CMP_HID, NSA_KV * HEAD_DIM), f32)
    for g in range(NSA_KV):
        w1bd = w1bd.at[:, :, HEAD_DIM * g:HEAD_DIM * (g + 1), CMP_HID * g:CMP_HID * (g + 1)].set(w1)
        w2bd = w2bd.at[:, CMP_HID * g:CMP_HID * (g + 1), HEAD_DIM * g:HEAD_DIM * (g + 1)].set(cmp_w2[l])
    cw8 = jnp.concatenate([conv_w[l], jnp.zeros((8 - CONV_W, conv_w.shape[2]), f32)], axis=0)
    return dict(w_proj=w_proj, pe4=pe4, w1bd=w1bd.astype(bf16), w2bd=w2bd.astype(bf16),
                wa=w_br_a[l].astype(bf16), wb=w_br_b[l].astype(bf16), wc=w_br_c[l].astype(bf16),
                wo=w_o[l].astype(bf16), w_up=w_up[l].astype(bf16), cw8=cw8, cb=conv_b[l].reshape(1, -1),
                wd=w_down[l].astype(bf16))


def _mod_parts(mod_rows, per_row):
    r = mod_rows.shape[0]
    parts = mod_rows.reshape(r, 6, D_MODEL)
    return [parts[:, k].reshape((1, r, D_MODEL) if per_row else (r, 1, D_MODEL)) for k in range(6)]


def kernel(x_prompt, x_sample, cache_nsa, cache_moba, cache_sb, state_win, state_conv, page_table, c_prompt,
           c_sample, norm1_g, norm2_g, w_ada, b_ada, w_in, cmp_pe, cmp_w1, cmp_w2, w_br_a, w_br_b, w_br_c, w_o,
           w_up, conv_w, conv_b, w_down, final_g):
    b, t, d = x_prompt.shape
    s = x_sample.shape[0]
    depth = w_in.shape[0]
    n_phys = cache_nsa.shape[1]
    n_pages = page_table.shape[1]
    past = n_pages * PAGE_SIZE
    kp = past + PAGE_SIZE
    tm = 256
    tq = 128
    tk = 512

    n_c = b + s
    c_all = jnp.concatenate([c_prompt, c_sample, jnp.zeros((-n_c % 8, d), f32)], axis=0)
    mod = _ada_mod(c_all, w_ada, b_ada)

    nc_p = t // CMP_STRIDE
    n_cmp_p = (t - CMP_LEN) // CMP_STRIDE + 1
    agg_p = _agg_matrix(nc_p, n_cmp_p)
    nc_s = past // CMP_STRIDE
    agg_s = _agg_matrix(nc_s, (past + 1 - CMP_LEN) // CMP_STRIDE + 1)
    e_sel = _expand_matrix(SEL_BLOCK, kp)
    e_blk = _expand_matrix(MOBA_BLOCK, kp)
    uu = _suffix_matrix()
    nb_pad = LANES // MOBA_HEADS

    caches = [c.reshape(depth * n_phys, PAGE_SIZE, 512) for c in (cache_nsa, cache_moba, cache_sb)]

    xp = x_prompt.reshape(b * t, d)
    xs = x_sample.reshape(s, d)
    outs_p = [[] for _ in range(5)]
    outs_s = [[] for _ in range(5)]
    for l in range(depth):
        w = _layer_weights(l, w_in, cmp_pe, cmp_w1, cmp_w2, w_br_a, w_br_b, w_br_c, w_o, w_up, conv_w, conv_b,
                           w_down)
        mp = _mod_parts(mod[l, 0:b], per_row=False)
        ms = _mod_parts(mod[l, b:b + s], per_row=True)

        (q_n, nsa_rows, nsa_bf, win_rows, win_bf, mb_q, moba_rows, moba_bf, sb_q, sb_rows, sb_bf, g_m, g_n) = \
            _norm_mod_matmul(xp, norm1_g[l], mp[0], mp[1], w["w_proj"], _IN_SEGS, _IN_DTYPES, tm, "in_proj_prompt")
        r3 = lambda a: a.reshape(b, t, a.shape[1])
        cmp = _cmp_prompt(r3(nsa_rows), w["pe4"], w["w1bd"], w["w2bd"])
        o_a = _nsa_prompt(r3(q_n), r3(g_n), cmp, agg_p, r3(nsa_bf), r3(win_bf), n_cmp_p, tq, tk)
        kmean = _block_mean(r3(moba_rows), 256, MOBA_BLOCK).reshape(b, t // MOBA_BLOCK, MOBA_HEADS, HEAD_DIM)
        kmt = jnp.zeros((b, MOBA_HEADS, nb_pad, MOBA_HEADS, HEAD_DIM), f32)
        for h in range(MOBA_HEADS):
            kmt = kmt.at[:, h, 0:t // MOBA_BLOCK, h].set(kmean[:, :, h])
        o_b = _moba_prompt(r3(mb_q), kmt.reshape(b, LANES, 256), r3(moba_bf), tq, tk)
        o_c = _sb_prompt(r3(sb_q), uu, r3(sb_bf), tq)
        xp = _merge_out(o_a.reshape(b * t, -1), o_b.reshape(b * t, -1), o_c.reshape(b * t, -1), g_m, xp, mp[2],
                        w["wa"], w["wb"], w["wc"], w["wo"], tm)
        u_a, u_b = _norm_mod_matmul(xp, norm2_g[l], mp[3], mp[4], w["w_up"], _UP_SEGS, _UP_DTYPES, tm,
                                    "ffn_up_prompt")
        xp = _ffn_down_seq(u_a, u_b, w["cw8"], w["cb"], w["wd"], xp, mp[5], tm)
        keep = min(WINDOW, t)
        outs_p[0].append(nsa_rows.reshape(b, t // PAGE_SIZE, PAGE_SIZE, 4, NSA_KV, HEAD_DIM))
        outs_p[1].append(moba_rows.reshape(b, t // PAGE_SIZE, PAGE_SIZE, 2, MOBA_HEADS, HEAD_DIM))
        outs_p[2].append(sb_rows.reshape(b, t // PAGE_SIZE, PAGE_SIZE, 2, SB_HEADS, HEAD_DIM))
        outs_p[3].append(r3(win_rows)[:, t - keep:].reshape(b, keep, 2, NSA_KV, HEAD_DIM))
        outs_p[4].append(r3(u_a)[:, t - (CONV_W - 1):])

        (q_n, nsa_rows, _, win_rows, _, mb_q, moba_rows, _, sb_q, sb_rows, _, g_m, g_n) = \
            _norm_mod_matmul(xs, norm1_g[l], ms[0], ms[1], w["w_proj"], _IN_SEGS, _IN_DTYPES, s, "in_proj_sample")
        s3 = lambda a: a.astype(f32).reshape(s, 1, a.shape[1])
        base = l * n_phys
        o_a = _nsa_decode(page_table, caches[0], base, s3(q_n), s3(g_n), s3(nsa_rows), s3(win_rows),
                          state_win[l].reshape(s, -1, 256), w["pe4"], w["w1bd"], w["w2bd"], agg_s, e_sel)
        o_b = _moba_decode(page_table, caches[1], base, s3(mb_q), s3(moba_rows), e_blk)
        o_c = _sb_decode(page_table, caches[2], base, s3(sb_q), uu)
        xs = _merge_out(o_a.reshape(s, -1).astype(bf16), o_b.reshape(s, -1).astype(bf16),
                        o_c.reshape(s, -1).astype(bf16), g_m, xs, ms[2], w["wa"], w["wb"], w["wc"], w["wo"], s)
        u_a, u_b = _norm_mod_matmul(xs, norm2_g[l], ms[3], ms[4], w["w_up"], _UP_SEGS, _UP_DTYPES, s,
                                    "ffn_up_sample")
        xs = _ffn_down_step(u_a, state_conv[l, :, 1], state_conv[l, :, 0], u_b, w["cw8"], w["cb"], w["wd"], xs,
                            ms[5])
        win_full = jnp.concatenate([state_win[l], win_rows.reshape(s, 1, 2, NSA_KV, HEAD_DIM)], axis=1)
        conv_full = jnp.concatenate([state_conv[l], u_a.reshape(s, 1, -1)], axis=1)
        outs_s[0].append(nsa_rows.reshape(s, 1, 4, NSA_KV, HEAD_DIM))
        outs_s[1].append(moba_rows.reshape(s, 1, 2, MOBA_HEADS, HEAD_DIM))
        outs_s[2].append(sb_rows.reshape(s, 1, 2, SB_HEADS, HEAD_DIM))
        outs_s[3].append(win_full[:, win_full.shape[1] - state_win.shape[2]:])
        outs_s[4].append(conv_full[:, conv_full.shape[1] - (CONV_W - 1):])

    y_prompt = _final_norm(xp, final_g, tm).reshape(b, t, d)
    y_sample = _final_norm(xs, final_g, s).reshape(s, 1, d)
    st = lambda lst: jnp.stack(lst)
    return (y_prompt, y_sample, st(outs_p[0]), st(outs_s[0]), st(outs_p[1]), st(outs_s[1]), st(outs_p[2]),
            st(outs_s[2]), st(outs_p[3]), st(outs_s[3]), st(outs_p[4]), st(outs_s[4]))
```

```python
import functools

import numpy as np
import jax
import jax.numpy as jnp
from jax import lax
from jax.experimental import pallas as pl
from jax.experimental.pallas import tpu as pltpu

f32 = jnp.float32
bf16 = jnp.bfloat16

D_MODEL = 1024
HEAD_DIM = 64
NSA_HEADS = 8
NSA_KV = 2
HPG = NSA_HEADS // NSA_KV
CMP_LEN = 32
CMP_STRIDE = 16
CMP_HID = 128
SEL_BLOCK = 64
SEL_TOPN = 16
WINDOW = 512
MOBA_HEADS = 4
MOBA_BLOCK = 256
MOBA_TOPK = 3
SB_HEADS = 4
D_FF = 2816
CONV_W = 3
PAGE_SIZE = 128
RMS_EPS = 1e-6
NEG = -1e30
FORCE = 1e9

LANES = 128
MASK_BIG = 2.0 ** 100
M_INIT = -1e29
REMOVED = -3e38
SB_CUTOFF = -110.0
VMEM_LIMIT_MB = 56

NSA_SLOPES = [2.0 ** (-8.0 * (h + 1) / NSA_HEADS) for h in range(NSA_HEADS)]
MOBA_SLOPES = [2.0 ** (-8.0 * (h + 1) / MOBA_HEADS) for h in range(MOBA_HEADS)]

_Q_N, _NSA, _WIN, _MB_Q, _MOBA, _SB_Q, _SB, _G_M, _G_N, _PROJ_W = 0, 512, 1024, 1280, 1536, 2048, 2304, 2816, 5888, 6016


def _cparams(sem, vmem_mb=VMEM_LIMIT_MB):
    return pltpu.CompilerParams(dimension_semantics=sem, vmem_limit_bytes=vmem_mb * 2 ** 20)


def _const_spec(shape):
    nd = len(shape)
    return pl.BlockSpec(shape, lambda *_: (0,) * nd, pipeline_mode=pl.Buffered(1))


def _nt(a, b):
    return lax.dot_general(a, b, (((1,), (1,)), ((), ())), preferred_element_type=f32)


def _dot(a, b):
    return jnp.dot(a, b, preferred_element_type=f32)


def _gelu_tanh(x):
    return x * (0.5 * (1.0 + jnp.tanh(np.sqrt(2.0 / np.pi) * (x + 0.044715 * (x * x * x)))))


def _ref_softmax(s, valid):
    s = jnp.where(valid, s, NEG)
    e = jnp.where(valid, jnp.exp(s - jnp.max(s, axis=-1, keepdims=True)), 0.0)
    return e / jnp.maximum(jnp.sum(e, axis=-1, keepdims=True), 1e-30)


def _ada_body(c_ref, w_ref, b_ref, o_ref):
    c = c_ref[...]
    s = c * jax.nn.sigmoid(c)
    o_ref[...] = _dot(s.astype(bf16), w_ref[...].astype(bf16)) + b_ref[...]


def _ada_mod(c_all, w_ada, b_ada):
    depth, d, n = w_ada.shape
    r = c_all.shape[0]
    tn = 1536
    return pl.pallas_call(
        _ada_body,
        out_shape=jax.ShapeDtypeStruct((depth, r, n), f32),
        grid=(depth, n // tn),
        in_specs=[pl.BlockSpec((r, d), lambda l, j: (0, 0)),
                  pl.BlockSpec((None, d, tn), lambda l, j: (l, 0, j)),
                  pl.BlockSpec((None, 1, tn), lambda l, j: (l, 0, j))],
        out_specs=pl.BlockSpec((None, r, tn), lambda l, j: (l, 0, j)),
        compiler_params=_cparams(("parallel", "parallel")),
        name="ada_mod",
    )(c_all, w_ada, b_ada.reshape(depth, 1, n))


def _nmm_body(x_ref, g_ref, sh_ref, sc_ref, w_ref, *o_refs, segs):
    x = x_ref[...]
    y = x * lax.rsqrt(jnp.mean(x * x, axis=-1, keepdims=True) + RMS_EPS)
    h = (y * g_ref[...]) * (1.0 + sc_ref[...]) + sh_ref[...]
    hb = h.astype(bf16)
    k = 0
    for off, width, scales in segs:
        outs = o_refs[k:k + len(scales)]
        k += len(scales)
        for c0 in range(0, width, 512):
            cw = min(512, width - c0)
            acc = _dot(hb, w_ref[:, off + c0:off + c0 + cw])
            for o, scale in zip(outs, scales):
                o[:, c0:c0 + cw] = (acc if scale == 1.0 else acc * scale).astype(o.dtype)


def _norm_mod_matmul(x, g, shift, scale, w_bf, segs, out_dtypes, tm, name):
    r, d = x.shape
    nb, rb, _ = shift.shape
    tiles_per_b = (r // nb) // tm
    out_shape, out_specs = [], []
    k = 0
    for off, width, scales in segs:
        for _ in scales:
            out_shape.append(jax.ShapeDtypeStruct((r, width), out_dtypes[k]))
            out_specs.append(pl.BlockSpec((tm, width), lambda i: (i, 0)))
            k += 1
    mod_spec = pl.BlockSpec((None, rb, d), lambda i: (i // tiles_per_b, 0, 0))
    return pl.pallas_call(
        functools.partial(_nmm_body, segs=segs),
        out_shape=out_shape,
        grid=(r // tm,),
        in_specs=[pl.BlockSpec((tm, d), lambda i: (i, 0)), _const_spec((1, d)), mod_spec, mod_spec,
                  _const_spec(w_bf.shape)],
        out_specs=out_specs,
        compiler_params=_cparams(("parallel",)),
        name=name,
    )(x, g.reshape(1, d), shift, scale, w_bf)


_IN_SEGS = ((_Q_N, 512, (0.125,)), (_NSA, 512, (1.0, 1.0)), (_WIN, 256, (1.0, 1.0)), (_MB_Q, 256, (0.125,)),
            (_MOBA, 512, (1.0, 1.0)), (_SB_Q, 256, (0.125,)), (_SB, 512, (1.0, 1.0)), (_G_M, 3072, (1.0,)),
            (_G_N, 128, (1.0,)))
_IN_DTYPES = (bf16, f32, bf16, f32, bf16, bf16, f32, bf16, bf16, f32, bf16, f32, f32)
_UP_SEGS = ((0, D_FF, (1.0,)), (D_FF, D_FF, (1.0,)))
_UP_DTYPES = (f32, f32)


def _cmp_core(rows_refs, pe_ref, w1_ref, w2_ref, nc):
    outs = []
    for kv in range(2):
        acc_a = jnp.zeros((nc, NSA_KV * CMP_HID), f32)
        acc_b = jnp.zeros((nc, NSA_KV * CMP_HID), f32)
        for r in range(CMP_STRIDE):
            y = rows_refs[kv][pl.ds(r, nc, stride=CMP_STRIDE), :]
            acc_a = acc_a + _dot((y + pe_ref[kv, r:r + 1, :]).astype(bf16), w1_ref[kv, r])
            acc_b = acc_b + _dot((y + pe_ref[kv, CMP_STRIDE + r:CMP_STRIDE + r + 1, :]).astype(bf16),
                                 w1_ref[kv, CMP_STRIDE + r])
        pre = acc_a + pltpu.roll(acc_b, nc - 1, 0)
        outs.append(_dot(_gelu_tanh(pre).astype(bf16), w2_ref[kv]))
    return jnp.concatenate(outs, axis=1)


def _cmp_prompt_body(k_ref, v_ref, pe_ref, w1_ref, w2_ref, o_ref, *, nc):
    o_ref[...] = _cmp_core((k_ref, v_ref), pe_ref, w1_ref, w2_ref, nc)


def _cmp_prompt(nsa_rows, pe2, w1bd, w2bd):
    b, l, _ = nsa_rows.shape
    nc = l // CMP_STRIDE
    return pl.pallas_call(
        functools.partial(_cmp_prompt_body, nc=nc),
        out_shape=jax.ShapeDtypeStruct((b, nc, 256), f32),
        grid=(b,),
        in_specs=[pl.BlockSpec((None, l, LANES), lambda i: (i, 0, 0)),
                  pl.BlockSpec((None, l, LANES), lambda i: (i, 0, 1)), _const_spec(pe2.shape),
                  _const_spec(w1bd.shape), _const_spec(w2bd.shape)],
        out_specs=pl.BlockSpec((None, nc, 256), lambda i: (i, 0, 0)),
        compiler_params=_cparams(("parallel",)),
        name="nsa_compress_prompt",
    )(nsa_rows, nsa_rows, pe2, w1bd, w2bd)


SLAB = 64
FLASH_SCRATCH = 2


class _Flash:
    def __init__(self, scratch, slopes, tq, t0):
        self.m, self.acc = scratch
        self.slope8 = jnp.concatenate(
            [jnp.full((1, 1), v, f32) for v in slopes] + [jnp.zeros((8 - len(slopes), 1), f32)], axis=0)
        self.tq, self.t0 = tq, t0
        self.m[...] = jnp.full(self.m.shape, M_INIT, f32)
        self.acc[...] = jnp.zeros(self.acc.shape, f32)

    def slab(self, j, s_all, bias, rel, masked):
        r0 = j * SLAB
        rows = slice(r0, r0 + SLAB)
        head = r0 // self.tq
        s = s_all[rows, :] + bias[head:head + 1, :]
        if masked:
            t_rel = (r0 + lax.broadcasted_iota(jnp.int32, (SLAB, 1), 0)) % self.tq
            s = jnp.where(rel <= t_rel, s, -MASK_BIG)
        m_old = self.m[rows, :]
        m_new = jnp.maximum(m_old, jnp.max(s, axis=-1, keepdims=True))
        self.m[rows, :] = m_new
        p = jnp.exp(s - jnp.concatenate([m_new] * (s.shape[1] // LANES), axis=1))
        return p.astype(bf16), jnp.exp(m_old - m_new)

    def result(self):
        acc = self.acc[...]
        return acc[:, 0:LANES] / acc[:, LANES:2 * LANES]


def _flash_tiles(chains, k0, masked):
    first = chains[0][0]
    rows = first.m.shape[0]
    tk = chains[0][2].shape[1]
    rel = k0 - first.t0 + lax.broadcasted_iota(jnp.int32, (1, tk), 1)
    logits = [(_dot(q_aug, kt_aug), f.slope8 * rel.astype(f32)) for f, q_aug, kt_aug, _ in chains]
    parts = [[] for _ in chains]
    for j in range(rows // SLAB):
        for c, (f, _, _, _) in enumerate(chains):
            parts[c].append(f.slab(j, logits[c][0], logits[c][1], rel, masked))
    for c, (f, _, _, v_aug) in enumerate(chains):
        p = jnp.concatenate([x[0] for x in parts[c]], axis=0)
        alpha = jnp.concatenate([x[1] for x in parts[c]], axis=0)
        f.acc[...] = jnp.concatenate([alpha, alpha], axis=1) * f.acc[...] + _dot(p, v_aug)


def _flash_scratch(rows):
    return [pltpu.VMEM((rows, LANES), f32), pltpu.VMEM((rows, 2 * LANES), f32)]


def _with_ones(v):
    return jnp.concatenate([v, jnp.ones(v.shape, v.dtype)], axis=-1)


def _masked_exp(s_all, slopes, rel, valid_fn, tq):
    out = []
    for j in range(s_all.shape[0] // SLAB):
        r0 = j * SLAB
        t_rel = r0 % tq + lax.broadcasted_iota(jnp.int32, (SLAB, 1), 0)
        valid = valid_fn(t_rel)
        s = jnp.where(valid, s_all[r0:r0 + SLAB, :] + slopes[r0 // tq] * rel.astype(f32), NEG)
        out.append(jnp.exp(s - jnp.maximum(jnp.max(s, axis=-1, keepdims=True), M_INIT)))
    return jnp.concatenate(out, axis=0)


def _topk_rows(score, ids, k, n_ids):
    picked = jnp.zeros(score.shape, f32)
    for _ in range(k):
        mx = jnp.max(score, axis=0, keepdims=True)
        idx = jnp.min(jnp.where(score == mx, ids, n_ids), axis=0, keepdims=True)
        pick = ids == idx
        picked = jnp.where(pick, 1.0, picked)
        score = jnp.where(pick, REMOVED, score)
    return picked


def _nsa_prompt_body(q_ref, gn_ref, kct_ref, vc_ref, agg_ref, kst_ref, vs_ref, kwt_ref, vw_ref, o_ref, sel_scr,
                     qa_scr, part_scr, gate_scr, *flash_scr, tq, tk, n_cmp):
    i = pl.program_id(1)
    t0 = i * tq
    nc = vc_ref.shape[0]
    qf = q_ref[...].astype(f32)
    lane = lax.broadcasted_iota(jnp.int32, (tq, LANES), 1)
    sig = jax.nn.sigmoid(gn_ref[...])
    kct = kct_ref[...]
    vc = vc_ref[...].astype(bf16)
    aggb = agg_ref[...].astype(bf16)
    n_id = lax.broadcasted_iota(jnp.int32, (1, nc), 1)
    cend_rel = n_id * CMP_STRIDE + (CMP_LEN - 1) - t0
    ones_c = jnp.ones((nc, LANES), bf16)
    blk_id = lax.broadcasted_iota(jnp.int32, (LANES, tq), 0)
    cur = (t0 + lax.broadcasted_iota(jnp.int32, (LANES, tq), 1)) // SEL_BLOCK
    causal_blk = blk_id <= cur
    forced = (blk_id == 0) | (blk_id == cur) | (blk_id == cur - 1)
    chunks = [jnp.zeros((tq, LANES), f32) for _ in range(NSA_HEADS // 2)]

    for g in range(NSA_KV):
        heads = [HPG * g + hh for hh in range(HPG)]
        pieces = []
        for h in heads:
            blk = qf[:, LANES * (h // 2):LANES * (h // 2 + 1)]
            if h % 2 != g:
                blk = pltpu.roll(blk, HEAD_DIM, 1)
            pieces.append(jnp.where(lane // HEAD_DIM == g, blk, 0.0))
        qg = jnp.concatenate(pieces, axis=0).astype(bf16)
        slopes = [NSA_SLOPES[h] for h in heads]

        valid_c = lambda t_rel: (cend_rel <= t_rel) & (n_id < n_cmp)
        e_c = _masked_exp(_dot(qg, kct), slopes, cend_rel, valid_c, tq)
        e_hi = e_c.astype(bf16)
        e_lo = (e_c - e_hi.astype(f32)).astype(bf16)
        r_hi = _dot(e_hi, jnp.concatenate([vc, ones_c, aggb], axis=1))
        r_lo = _dot(e_lo, jnp.concatenate([aggb, ones_c], axis=1))
        o_c = r_hi[:, 0:LANES] / jnp.maximum(r_hi[:, LANES:2 * LANES], 1e-30)
        imp_rows = (r_hi[:, 2 * LANES:3 * LANES] + r_lo[:, 0:LANES]) / jnp.maximum(
            r_hi[:, LANES:2 * LANES] + r_lo[:, LANES:2 * LANES], 1e-30)
        imp = imp_rows[0:tq] + imp_rows[tq:2 * tq] + imp_rows[2 * tq:3 * tq] + imp_rows[3 * tq:4 * tq]

        score = jnp.where(causal_blk, jnp.where(forced, FORCE, imp.T), NEG)
        sel_t = jnp.where(causal_blk, _topk_rows(score, blk_id, SEL_TOPN, LANES), 0.0)
        notsel = (1.0 - sel_t).T
        qa_scr[g] = jnp.concatenate([qg, jnp.concatenate([notsel] * HPG, axis=0).astype(bf16)], axis=1)
        sel_scr[g] = sel_t

        wl = WINDOW + tq
        s0 = pl.multiple_of(jnp.maximum(t0 - WINDOW, 0), tq)
        w_rel = s0 - t0 + lax.broadcasted_iota(jnp.int32, (1, wl), 1)
        valid_w = lambda t_rel: (w_rel <= t_rel) & (w_rel > t_rel - WINDOW)
        e_w = _masked_exp(_dot(qg, kwt_ref[:, pl.ds(s0, wl)]), slopes, w_rel, valid_w, tq)
        r_w = _dot(e_w.astype(bf16), vw_ref[pl.ds(s0, wl), :])
        o_w = r_w[:, 0:LANES] / r_w[:, LANES:2 * LANES]

        gates = [jnp.concatenate([sig[:, br * NSA_HEADS + h:br * NSA_HEADS + h + 1] for h in heads], axis=0)
                 for br in range(3)]
        part_scr[g] = gates[0] * o_c + gates[2] * o_w
        gate_scr[g] = gates[1]

    flashes = [_Flash(flash_scr[FLASH_SCRATCH * g:FLASH_SCRATCH * (g + 1)],
                      NSA_SLOPES[HPG * g:HPG * (g + 1)], tq, t0) for g in range(NSA_KV)]

    def sel_step(kt, masked):
        k0 = pl.multiple_of(kt * tk, tk)
        kt_aug = kst_ref[:, pl.ds(k0, tk)]
        v_aug = vs_ref[pl.ds(k0, tk), :]
        _flash_tiles([(flashes[g], qa_scr[g], kt_aug, v_aug) for g in range(NSA_KV)], k0, masked)

    def sel_loop(kt, carry):
        blk0 = pl.multiple_of(kt * (tk // SEL_BLOCK), tk // SEL_BLOCK)
        picked = jnp.maximum(sel_scr[0, pl.ds(blk0, tk // SEL_BLOCK), :], sel_scr[1, pl.ds(blk0, tk // SEL_BLOCK), :])

        @pl.when(jnp.max(picked) > 0.0)
        def _():
            sel_step(kt, False)

        return carry

    kd = t0 // tk
    lax.fori_loop(0, kd, sel_loop, 0)
    sel_step(kd, True)

    for g in range(NSA_KV):
        heads = [HPG * g + hh for hh in range(HPG)]
        o = part_scr[g] + gate_scr[g] * flashes[g].result()
        for hh, h in enumerate(heads):
            piece = o[hh * tq:(hh + 1) * tq]
            if h % 2 != g:
                piece = pltpu.roll(piece, HEAD_DIM, 1)
            chunks[h // 2] = chunks[h // 2] + jnp.where(lane // HEAD_DIM == h % 2, piece, 0.0)

    o_ref[...] = jnp.concatenate(chunks, axis=1).astype(o_ref.dtype)


def _nsa_prompt(q_n, g_n, cmp, agg, nsa_bf, win_bf, n_cmp, tq, tk):
    b, l, _ = q_n.shape
    nc = cmp.shape[1]
    rows = HPG * tq
    kct = jnp.swapaxes(cmp[:, :, 0:LANES], 1, 2).astype(bf16)
    onehot_t = jnp.broadcast_to(_block_onehot_t(SEL_BLOCK, l)[None], (b, LANES, l))
    kst = jnp.concatenate([jnp.swapaxes(nsa_bf[:, :, 2 * LANES:3 * LANES], 1, 2), onehot_t], axis=1)
    kwt = jnp.swapaxes(win_bf[:, :, 0:LANES], 1, 2)
    per_b = lambda shape, col=0: pl.BlockSpec((None,) + shape, lambda bi, i: (bi, 0, col))
    return pl.pallas_call(
        functools.partial(_nsa_prompt_body, tq=tq, tk=tk, n_cmp=n_cmp),
        out_shape=jax.ShapeDtypeStruct((b, l, NSA_HEADS * HEAD_DIM), bf16),
        grid=(b, l // tq),
        in_specs=[pl.BlockSpec((None, tq, 512), lambda bi, i: (bi, i, 0)),
                  pl.BlockSpec((None, tq, LANES), lambda bi, i: (bi, i, 0)),
                  per_b((LANES, nc)), per_b((nc, LANES), 1), _const_spec(agg.shape),
                  per_b((2 * LANES, l)), per_b((l, 2 * LANES)), per_b((LANES, l)), per_b((l, 2 * LANES))],
        out_specs=pl.BlockSpec((None, tq, 512), lambda bi, i: (bi, i, 0)),
        scratch_shapes=[pltpu.VMEM((NSA_KV, LANES, tq), f32), pltpu.VMEM((NSA_KV, rows, 2 * LANES), bf16),
                        pltpu.VMEM((NSA_KV, rows, LANES), f32), pltpu.VMEM((NSA_KV, rows, 1), f32)]
        + _flash_scratch(rows) * NSA_KV,
        compiler_params=_cparams(("parallel", "parallel")),
        name="nsa_attention_prompt",
    )(q_n, g_n, kct, cmp, agg, kst, _with_ones(nsa_bf[:, :, 3 * LANES:4 * LANES]), kwt,
      _with_ones(win_bf[:, :, LANES:2 * LANES]))


def _block_mean_body(k_ref, o_ref):
    o_ref[...] = jnp.mean(k_ref[...], axis=0, keepdims=True)


def _block_mean(rows, width, blk):
    b, l, _ = rows.shape
    return pl.pallas_call(
        _block_mean_body,
        out_shape=jax.ShapeDtypeStruct((b, l // blk, 1, width), f32),
        grid=(b, l // blk),
        in_specs=[pl.BlockSpec((None, blk, width), lambda bi, j: (bi, j, 0))],
        out_specs=pl.BlockSpec((None, None, 1, width), lambda bi, j: (bi, j, 0, 0)),
        compiler_params=_cparams(("parallel", "parallel")),
        name="moba_block_mean",
    )(rows)


def _moba_prompt_body(q_ref, kmt_ref, kt_ref, v_ref, o_ref, qa_scr, *flash_scr, tq, tk, nb_pad):
    i = pl.program_id(1)
    t0 = i * tq
    q = q_ref[...]
    qf = q.astype(f32)
    lane = lax.broadcasted_iota(jnp.int32, (tq, LANES), 1)

    gate_t = _nt(kmt_ref[...].astype(bf16), q)
    blk_id = lax.broadcasted_iota(jnp.int32, (LANES, tq), 0) % nb_pad
    own = (t0 + lax.broadcasted_iota(jnp.int32, (LANES, tq), 1)) // MOBA_BLOCK
    past = blk_id < own
    score = jnp.where(past, gate_t, NEG)
    parts = [_topk_rows(score[nb_pad * h:nb_pad * (h + 1)], blk_id[nb_pad * h:nb_pad * (h + 1)], MOBA_TOPK, nb_pad)
             for h in range(MOBA_HEADS)]
    sel_t = jnp.where(past, jnp.concatenate(parts, axis=0), 0.0)
    sel_t = jnp.where(blk_id == own, 1.0, sel_t)
    notsel = (1.0 - sel_t).T

    nch = MOBA_HEADS // 2
    for c in range(nch):
        q_rows = []
        for e in range(2):
            h = 2 * c + e
            qh = jnp.where(lane // HEAD_DIM == e, qf[:, LANES * c:LANES * (c + 1)], 0.0)
            ns = notsel if h == 0 else pltpu.roll(notsel, LANES - nb_pad * h, 1)
            ns = jnp.where(lane < nb_pad, ns, 0.0)
            q_rows.append(jnp.concatenate([qh, ns], axis=1))
        qa_scr[c] = jnp.concatenate(q_rows, axis=0).astype(bf16)
    flashes = [_Flash(flash_scr[FLASH_SCRATCH * c:FLASH_SCRATCH * (c + 1)],
                      MOBA_SLOPES[2 * c:2 * c + 2], tq, t0) for c in range(nch)]

    def step(kt, masked):
        k0 = pl.multiple_of(kt * tk, tk)
        _flash_tiles([(flashes[c], qa_scr[c], kt_ref[c, :, pl.ds(k0, tk)],
                       v_ref[c, pl.ds(k0, tk), :]) for c in range(nch)], k0, masked)

    def loop(kt, carry):
        step(kt, False)
        return carry

    lax.fori_loop(0, t0 // tk, loop, 0)
    step(t0 // tk, True)
    out_chunks = []
    for c in range(nch):
        o = flashes[c].result()
        out_chunks.append(jnp.where(lane < HEAD_DIM, o[0:tq], o[tq:2 * tq]))
    o_ref[...] = jnp.concatenate(out_chunks, axis=1).astype(o_ref.dtype)


def _moba_prompt(mb_q, kmt, moba_bf, tq, tk):
    b, l, _ = mb_q.shape
    nb_pad = LANES // MOBA_HEADS
    nch = MOBA_HEADS // 2
    onehot_t = jnp.broadcast_to(_block_onehot_t(MOBA_BLOCK, l)[None, None], (b, nch, LANES, l))
    k_t = jnp.swapaxes(moba_bf[:, :, 0:nch * LANES].reshape(b, l, nch, LANES), 1, 3)
    kt_aug = jnp.concatenate([jnp.swapaxes(k_t, 1, 2), onehot_t], axis=2)
    v_aug = _with_ones(jnp.swapaxes(moba_bf[:, :, nch * LANES:].reshape(b, l, nch, LANES), 1, 2))
    return pl.pallas_call(
        functools.partial(_moba_prompt_body, tq=tq, tk=tk, nb_pad=nb_pad),
        out_shape=jax.ShapeDtypeStruct((b, l, MOBA_HEADS * HEAD_DIM), bf16),
        grid=(b, l // tq),
        in_specs=[pl.BlockSpec((None, tq, 256), lambda bi, i: (bi, i, 0)),
                  pl.BlockSpec((None, LANES, 256), lambda bi, i: (bi, 0, 0)),
                  pl.BlockSpec((None, nch, 2 * LANES, l), lambda bi, i: (bi, 0, 0, 0)),
                  pl.BlockSpec((None, nch, l, 2 * LANES), lambda bi, i: (bi, 0, 0, 0))],
        out_specs=pl.BlockSpec((None, tq, 256), lambda bi, i: (bi, i, 0)),
        scratch_shapes=[pltpu.VMEM((nch, 2 * tq, 2 * LANES), bf16)] + _flash_scratch(2 * tq) * nch,
        compiler_params=_cparams(("parallel", "parallel")),
        name="moba_attention_prompt",
    )(mb_q, kmt, kt_aug, v_aug)


def _log_keep(z):
    return -(jnp.maximum(z, 0.0) + jnp.log(1.0 + jnp.exp(-jnp.abs(z))))


def _suffix_sums(lk, uu):
    hi = lk.astype(bf16)
    lo = (lk - hi.astype(f32)).astype(bf16)
    r = _dot(jnp.concatenate([hi, lo], axis=1), uu)
    return r[:, 0:LANES], r[:, LANES:2 * LANES]


def _sb_prompt_body(q_ref, uu_ref, k_ref, v_ref, o_ref, carry_scr, acc_scr, *, tq):
    i = pl.program_id(2)
    t0 = i * tq
    qf = q_ref[...].astype(f32)
    lane = lax.broadcasted_iota(jnp.int32, (tq, LANES), 1)
    q2 = jnp.concatenate([jnp.where(lane // HEAD_DIM == e, qf, 0.0) for e in range(2)], axis=0).astype(bf16)
    t_row = t0 + lax.broadcasted_iota(jnp.int32, (2 * tq, 1), 0) % tq
    uu = uu_ref[...]
    carry_scr[...] = jnp.zeros(carry_scr.shape, f32)
    acc_scr[...] = jnp.zeros(acc_scr.shape, f32)

    def step(kc, diag):
        k0 = pl.multiple_of(kc * LANES, LANES)
        z = _dot(q2, k_ref[:, pl.ds(k0, LANES)])
        lk = _log_keep(z)
        if diag:
            is_past = (k0 + lax.broadcasted_iota(jnp.int32, (1, LANES), 1)) < t_row
            lk = jnp.where(is_past, lk, 0.0)
        later, total = _suffix_sums(lk, uu)
        w = jnp.exp(z + lk + later + carry_scr[...])
        if diag:
            w = jnp.where(is_past, w, 0.0)
        acc_scr[...] = acc_scr[...] + _dot(w.astype(bf16), v_ref[pl.ds(k0, LANES), :])
        carry_scr[...] = carry_scr[...] + total

    n_diag = tq // LANES
    for dgi in range(n_diag):
        step(i * n_diag + (n_diag - 1 - dgi), True)

    def more(state):
        j, top = state
        return (j < i * n_diag) & (top > SB_CUTOFF)

    def walk(state):
        j, _ = state
        step(i * n_diag - 1 - j, False)
        return j + 1, jnp.max(carry_scr[...])

    lax.while_loop(more, walk, (0, jnp.max(carry_scr[...])))
    o = acc_scr[...]
    o_ref[...] = jnp.where(lane < HEAD_DIM, o[0:tq], o[tq:2 * tq]).astype(o_ref.dtype)


def _sb_prompt(sb_q, uu, sb_bf, tq):
    b, l, _ = sb_q.shape
    nch = SB_HEADS // 2
    k_t = jnp.swapaxes(jnp.swapaxes(sb_bf[:, :, 0:nch * LANES].reshape(b, l, nch, LANES), 1, 3), 1, 2)
    return pl.pallas_call(
        functools.partial(_sb_prompt_body, tq=tq),
        out_shape=jax.ShapeDtypeStruct((b, l, SB_HEADS * HEAD_DIM), bf16),
        grid=(b, nch, l // tq),
        in_specs=[pl.BlockSpec((None, tq, LANES), lambda bi, c, i: (bi, i, c)),
                  _const_spec(uu.shape),
                  pl.BlockSpec((None, None, LANES, l), lambda bi, c, i: (bi, c, 0, 0)),
                  pl.BlockSpec((None, l, LANES), lambda bi, c, i: (bi, 0, nch + c))],
        out_specs=pl.BlockSpec((None, tq, LANES), lambda bi, c, i: (bi, i, c)),
        scratch_shapes=[pltpu.VMEM((2 * tq, LANES), f32), pltpu.VMEM((2 * tq, LANES), f32)],
        compiler_params=_cparams(("parallel", "parallel", "parallel")),
        name="stickbreak_attention_prompt",
    )(sb_q, uu, k_t, sb_bf)


def _merge_body(oa_ref, ob_ref, oc_ref, gm_ref, x_ref, gate_ref, wa_ref, wb_ref, wc_ref, wo_ref, o_ref):
    d = x_ref.shape[1]
    g = jax.nn.sigmoid(gm_ref[...])
    merged = (g[:, 0:d] * _dot(oa_ref[...], wa_ref[...]) + g[:, d:2 * d] * _dot(ob_ref[...], wb_ref[...])
              + g[:, 2 * d:3 * d] * _dot(oc_ref[...], wc_ref[...]))
    o_ref[...] = x_ref[...] + gate_ref[...] * _dot(merged.astype(bf16), wo_ref[...])


def _merge_out(o_a, o_b, o_c, g_m, x, gate, wa, wb, wc, wo, tm):
    r, d = x.shape
    nb, rb, _ = gate.shape
    tiles_per_b = (r // nb) // tm
    row = lambda w: pl.BlockSpec((tm, w), lambda i: (i, 0))
    return pl.pallas_call(
        _merge_body,
        out_shape=jax.ShapeDtypeStruct((r, d), f32),
        grid=(r // tm,),
        in_specs=[row(o_a.shape[1]), row(o_b.shape[1]), row(o_c.shape[1]), row(3 * d), row(d),
                  pl.BlockSpec((None, rb, d), lambda i: (i // tiles_per_b, 0, 0)),
                  _const_spec(wa.shape), _const_spec(wb.shape), _const_spec(wc.shape), _const_spec(wo.shape)],
        out_specs=row(d),
        compiler_params=_cparams(("parallel",)),
        name="merge_out_proj",
    )(o_a, o_b, o_c, g_m, x, gate, wa, wb, wc, wo)


def _ffn_tail(a, a_m1, a_m2, b, cw_ref, cb_ref, wd_ref, x_ref, gate_ref, o_ref):
    conv = cb_ref[...] + a_m2 * cw_ref[0:1, :]
    conv = conv + a_m1 * cw_ref[1:2, :]
    conv = conv + a * cw_ref[2:3, :]
    y = _dot((_gelu_tanh(conv) * b).astype(bf16), wd_ref[...])
    o_ref[...] = x_ref[...] + gate_ref[...] * y


def _ffn_seq_body(a_ref, halo_ref, b_ref, cw_ref, cb_ref, wd_ref, x_ref, gate_ref, o_ref, *, tiles_per_b):
    a = a_ref[...]
    first = pl.program_id(0) % tiles_per_b == 0
    halo = jnp.where(first, 0.0, halo_ref[...])
    row = lax.broadcasted_iota(jnp.int32, a.shape, 0)
    a_m1 = jnp.where(row < 1, halo[7:8, :], pltpu.roll(a, 1, 0))
    a_m2 = jnp.where(row < 1, halo[6:7, :], jnp.where(row < 2, halo[7:8, :], pltpu.roll(a, 2, 0)))
    _ffn_tail(a, a_m1, a_m2, b_ref[...], cw_ref, cb_ref, wd_ref, x_ref, gate_ref, o_ref)


def _ffn_step_body(a_ref, am1_ref, am2_ref, b_ref, cw_ref, cb_ref, wd_ref, x_ref, gate_ref, o_ref):
    _ffn_tail(a_ref[...], am1_ref[...], am2_ref[...], b_ref[...], cw_ref, cb_ref, wd_ref, x_ref, gate_ref, o_ref)


def _ffn_down_seq(u_a, u_b, cw8, cb, wd, x, gate, tm):
    r, d = x.shape
    ff = u_a.shape[1]
    nb = gate.shape[0]
    tiles_per_b = (r // nb) // tm
    row = lambda w: pl.BlockSpec((tm, w), lambda i: (i, 0))
    return pl.pallas_call(
        functools.partial(_ffn_seq_body, tiles_per_b=tiles_per_b),
        out_shape=jax.ShapeDtypeStruct((r, d), f32),
        grid=(r // tm,),
        in_specs=[row(ff), pl.BlockSpec((8, ff), lambda i: (jnp.maximum(i * (tm // 8) - 1, 0), 0)), row(ff),
                  _const_spec(cw8.shape), _const_spec(cb.shape), _const_spec(wd.shape), row(d),
                  pl.BlockSpec((None, 1, d), lambda i: (i // tiles_per_b, 0, 0))],
        out_specs=row(d),
        compiler_params=_cparams(("parallel",)),
        name="conv_ffn_down_seq",
    )(u_a, u_a, u_b, cw8, cb, wd, x, gate)


def _ffn_down_step(u_a, a_m1, a_m2, u_b, cw8, cb, wd, x, gate):
    r, d = x.shape
    full = lambda a: pl.BlockSpec(a.shape, lambda i: (0,) * a.ndim)
    return pl.pallas_call(
        _ffn_step_body,
        out_shape=jax.ShapeDtypeStruct((r, d), f32),
        grid=(1,),
        in_specs=[full(u_a), full(a_m1), full(a_m2), full(u_b), full(cw8), full(cb), full(wd), full(x),
                  pl.BlockSpec((None, r, d), lambda i: (0, 0, 0))],
        out_specs=full(x),
        compiler_params=_cparams(("arbitrary",)),
        name="conv_ffn_down_step",
    )(u_a, a_m1, a_m2, u_b, cw8, cb, wd, x, gate)


def _final_norm_body(x_ref, g_ref, o_ref):
    x = x_ref[...]
    o_ref[...] = x * lax.rsqrt(jnp.mean(x * x, axis=-1, keepdims=True) + RMS_EPS) * g_ref[...]


def _final_norm(x, g, tm):
    r, d = x.shape
    return pl.pallas_call(
        _final_norm_body,
        out_shape=jax.ShapeDtypeStruct((r, d), f32),
        grid=(r // tm,),
        in_specs=[pl.BlockSpec((tm, d), lambda i: (i, 0)), _const_spec((1, d))],
        out_specs=pl.BlockSpec((tm, d), lambda i: (i, 0)),
        compiler_params=_cparams(("parallel",)),
        name="final_rmsnorm",
    )(x, g.reshape(1, d))


def _page_specs(n_pages, layer_base):
    return [pl.BlockSpec((None, 512, PAGE_SIZE),
                         functools.partial(lambda s, pt, j: (layer_base + pt[s * n_pages + j], 0, 0), j=j))
            for j in range(n_pages)]


def _per_seq(width):
    return pl.BlockSpec((None, 1, width), lambda s, pt: (s, 0, 0))


def _dec_const(shape):
    nd = len(shape)
    return pl.BlockSpec(shape, lambda s, pt: (0,) * nd, pipeline_mode=pl.Buffered(1))


def _head_rows(q_row):
    row = lax.broadcasted_iota(jnp.int32, (8, LANES), 0)
    lane = lax.broadcasted_iota(jnp.int32, (8, LANES), 1)
    q8 = jnp.broadcast_to(q_row, (8, q_row.shape[1]))
    qsel = jnp.zeros((8, LANES), f32)
    for c in range(NSA_HEADS // 2):
        qsel = qsel + jnp.where(row // 2 == c, q8[:, LANES * c:LANES * (c + 1)], 0.0)
    swap = (row % 2) != (row // HPG)
    qm = jnp.where(swap, pltpu.roll(qsel, HEAD_DIM, 1), qsel)
    return jnp.where(lane // HEAD_DIM == row // HPG, qm, 0.0), swap


def _rank_select(score_row, k):
    a = jnp.broadcast_to(score_row, (LANES, LANES))
    b = a.T
    ii = lax.broadcasted_iota(jnp.int32, (LANES, LANES), 0)
    jj = lax.broadcasted_iota(jnp.int32, (LANES, LANES), 1)
    ahead = (b > a) | ((b == a) & (ii < jj))
    rank = jnp.sum(jnp.where(ahead, 1.0, 0.0), axis=0, keepdims=True)
    return jnp.where(rank < k, 1.0, 0.0)


def _nsa_dec_body(pt_ref, q_ref, gn_ref, new_ref, wnew_ref, sw_ref, pe_ref, w1_ref, w2_ref, agg_ref, e_ref,
                  *rest, n_pages):
    pages, o_ref = rest[:n_pages], rest[n_pages]
    kc_scr, vc_scr = rest[n_pages + 1:]
    past = n_pages * PAGE_SIZE
    for j in range(n_pages):
        kc_scr[PAGE_SIZE * j:PAGE_SIZE * (j + 1), :] = pages[j][0:LANES, :].T
        vc_scr[PAGE_SIZE * j:PAGE_SIZE * (j + 1), :] = pages[j][LANES:2 * LANES, :].T

    nc = past // CMP_STRIDE
    n_cmp = (past + 1 - CMP_LEN) // CMP_STRIDE + 1
    cmpv = _cmp_core((kc_scr, vc_scr), pe_ref, w1_ref, w2_ref, nc)

    row = lax.broadcasted_iota(jnp.int32, (8, LANES), 0)
    lane = lax.broadcasted_iota(jnp.int32, (8, LANES), 1)
    qm, swap = _head_rows(q_ref[...])
    qmb = qm.astype(bf16)
    slope = jnp.zeros((8, 1), f32)
    row1 = lax.broadcasted_iota(jnp.int32, (8, 1), 0)
    for h in range(NSA_HEADS):
        slope = jnp.where(row1 == h, NSA_SLOPES[h], slope)
    grp0 = row < HPG

    d_c = past - (lane[0:1] * CMP_STRIDE + CMP_LEN - 1)
    s_c = _nt(qmb, cmpv[:, 0:LANES].astype(bf16)) - slope * d_c.astype(f32)
    p_c = _ref_softmax(s_c, (d_c >= 0) & (lane[0:1] < n_cmp))
    o_c = _dot(p_c.astype(bf16), cmpv[:, LANES:2 * LANES].astype(bf16))
    aggb = agg_ref[...].astype(bf16)
    p_hi = p_c.astype(bf16)
    p_lo = (p_c - p_hi.astype(f32)).astype(bf16)
    imp_rows = _dot(p_hi, aggb) + _dot(p_lo, aggb)

    cur = past // SEL_BLOCK
    blk = lane[0:1]
    forced = (blk == 0) | (blk == cur) | (blk == cur - 1)
    causal = blk <= cur
    notsel_g = []
    for g in range(NSA_KV):
        imp = jnp.sum(imp_rows[HPG * g:HPG * (g + 1)], axis=0, keepdims=True)
        score = jnp.where(causal, jnp.where(forced, FORCE, imp), NEG)
        sel = jnp.where(causal, _rank_select(score, min(SEL_TOPN, -(-(past + 1) // SEL_BLOCK))), 0.0)
        notsel_g.append(jnp.broadcast_to(1.0 - sel, (8, LANES)))
    notsel = jnp.where(grp0, notsel_g[0], notsel_g[1])

    new_row = new_ref[...]
    dist = past - lax.broadcasted_iota(jnp.int32, (1, past), 1)
    picked = _dot(notsel.astype(bf16), e_ref[...]) < 0.5
    ks_t = jnp.concatenate([pages[j][2 * LANES:3 * LANES, :].astype(bf16) for j in range(n_pages)], axis=1)
    vs_t = jnp.concatenate([pages[j][3 * LANES:4 * LANES, :].astype(bf16) for j in range(n_pages)], axis=1)
    s_s = jnp.where(picked, _dot(qmb, ks_t) - slope * dist.astype(f32), NEG)
    s_n = jnp.sum(qm * new_row[:, 2 * LANES:3 * LANES], axis=-1, keepdims=True)
    m_s = jnp.maximum(jnp.max(s_s, axis=-1, keepdims=True), s_n)
    e_s = jnp.where(picked, jnp.exp(s_s - m_s), 0.0)
    e_n = jnp.exp(s_n - m_s)
    den = jnp.maximum(jnp.sum(e_s, axis=-1, keepdims=True) + e_n, 1e-30)
    o_s = (e_n * new_row[:, 3 * LANES:4 * LANES] + _nt(e_s.astype(bf16), vs_t)) / den

    wb = sw_ref.shape[1]
    d_w = wb - lax.broadcasted_iota(jnp.int32, (1, wb), 1)
    s_w = _dot(qmb, sw_ref[0:LANES, :].astype(bf16)) - slope * d_w.astype(f32)
    valid_w = (d_w < WINDOW) & (d_w >= 0)
    s_w = jnp.where(valid_w, s_w, NEG)
    wnew = wnew_ref[...]
    s_n = jnp.sum(qm * wnew[:, 0:LANES], axis=-1, keepdims=True)
    m_w = jnp.maximum(jnp.max(s_w, axis=-1, keepdims=True), s_n)
    e_w = jnp.where(valid_w, jnp.exp(s_w - m_w), 0.0)
    e_n = jnp.exp(s_n - m_w)
    den = jnp.maximum(jnp.sum(e_w, axis=-1, keepdims=True) + e_n, 1e-30)
    o_w = (_nt(e_w.astype(bf16), sw_ref[LANES:2 * LANES, :].astype(bf16)) + e_n * wnew[:, LANES:2 * LANES]) / den

    sig = jnp.broadcast_to(jax.nn.sigmoid(gn_ref[...]), (8, LANES))
    gates = [jnp.sum(jnp.where(lane == br * NSA_HEADS + row, sig, 0.0), axis=-1, keepdims=True) for br in range(3)]
    o = gates[0] * o_c + gates[1] * o_s + gates[2] * o_w
    o = jnp.where(swap, pltpu.roll(o, HEAD_DIM, 1), o)
    o = jnp.where(lane // HEAD_DIM == row % 2, o, 0.0)
    o_ref[...] = jnp.concatenate([o[2 * c:2 * c + 1] + o[2 * c + 1:2 * c + 2] for c in range(NSA_HEADS // 2)], axis=1)


def _nsa_decode(page_table, cache_t, layer_base, q, g_n, nsa_new, win_new, state_win_t, win_base, pe4, w1bd, w2bd,
                agg, e_sel):
    s, n_pages = page_table.shape
    past = n_pages * PAGE_SIZE
    consts = (pe4, w1bd, w2bd, agg, e_sel)
    grid_spec = pltpu.PrefetchScalarGridSpec(
        num_scalar_prefetch=1,
        grid=(s,),
        in_specs=[_per_seq(512), _per_seq(LANES), _per_seq(512), _per_seq(256),
                  pl.BlockSpec((None,) + state_win_t.shape[1:], lambda si, pt: (win_base + si, 0, 0))]
        + [_dec_const(c.shape) for c in consts] + _page_specs(n_pages, layer_base),
        out_specs=_per_seq(512),
        scratch_shapes=[pltpu.VMEM((past, LANES), f32), pltpu.VMEM((past, LANES), f32)],
    )
    return pl.pallas_call(
        functools.partial(_nsa_dec_body, n_pages=n_pages),
        out_shape=jax.ShapeDtypeStruct((s, 1, 512), f32),
        grid_spec=grid_spec,
        compiler_params=_cparams(("parallel",)),
        name="nsa_attention_decode",
    )(page_table.reshape(-1), q, g_n, nsa_new, win_new, state_win_t, *consts, *([cache_t] * n_pages))


def _q_head_rows4(q_row):
    row = lax.broadcasted_iota(jnp.int32, (8, 256), 0)
    lane = lax.broadcasted_iota(jnp.int32, (8, 256), 1)
    own = lane // HEAD_DIM == row
    return jnp.where(own, jnp.broadcast_to(q_row, (8, 256)), 0.0), own


def _moba_dec_body(pt_ref, q_ref, new_ref, seg_ref, e_ref, *rest, n_pages):
    pages, o_ref = rest[:n_pages], rest[n_pages]
    past = n_pages * PAGE_SIZE
    nb_past = past // MOBA_BLOCK
    kw = MOBA_HEADS * HEAD_DIM
    qm, own_lanes = _q_head_rows4(q_ref[...])
    qmb = qm.astype(bf16)
    k_t = jnp.concatenate([pages[j][0:kw, :].astype(bf16) for j in range(n_pages)], axis=1)
    v_t = jnp.concatenate([pages[j][kw:2 * kw, :].astype(bf16) for j in range(n_pages)], axis=1)
    raw = _dot(qmb, k_t)
    raw_hi = raw.astype(bf16)
    raw_lo = (raw - raw_hi.astype(f32)).astype(bf16)
    gate = _dot(raw_hi, seg_ref[...]) + _dot(raw_lo, seg_ref[...])

    lane = lax.broadcasted_iota(jnp.int32, (8, LANES), 1)
    is_past = lane < nb_past
    score = jnp.where(is_past, gate, NEG)
    rank = jnp.zeros((8, LANES), f32)
    for i in range(nb_past):
        gi = score[:, i:i + 1]
        rank = rank + jnp.where((gi > score) | ((gi == score) & (i < lane)), 1.0, 0.0)
    sel = jnp.where(is_past & (rank < min(MOBA_TOPK, nb_past)), 1.0, 0.0)
    picked = _dot(sel.astype(bf16), e_ref[...]) > 0.5

    row1 = lax.broadcasted_iota(jnp.int32, (8, 1), 0)
    slope = jnp.zeros((8, 1), f32)
    for h in range(MOBA_HEADS):
        slope = jnp.where(row1 == h, MOBA_SLOPES[h], slope)
    dist = past - lax.broadcasted_iota(jnp.int32, (1, past), 1)
    new_row = new_ref[...]
    s = jnp.where(picked, raw - slope * dist.astype(f32), NEG)
    s_n = jnp.sum(qm * new_row[:, 0:kw], axis=-1, keepdims=True)
    m = jnp.maximum(jnp.max(s, axis=-1, keepdims=True), s_n)
    e = jnp.where(picked, jnp.exp(s - m), 0.0)
    e_n = jnp.exp(s_n - m)
    den = jnp.maximum(jnp.sum(e, axis=-1, keepdims=True) + e_n, 1e-30)
    o = (e_n * new_row[:, kw:2 * kw] + _nt(e.astype(bf16), v_t)) / den
    o_ref[...] = jnp.sum(jnp.where(own_lanes, o, 0.0), axis=0, keepdims=True)


def _moba_decode(page_table, cache_t, layer_base, q, moba_new, seg_mean, e_blk):
    s, n_pages = page_table.shape
    grid_spec = pltpu.PrefetchScalarGridSpec(
        num_scalar_prefetch=1,
        grid=(s,),
        in_specs=[_per_seq(256), _per_seq(512), _dec_const(seg_mean.shape), _dec_const(e_blk.shape)]
        + _page_specs(n_pages, layer_base),
        out_specs=_per_seq(256),
    )
    return pl.pallas_call(
        functools.partial(_moba_dec_body, n_pages=n_pages),
        out_shape=jax.ShapeDtypeStruct((s, 1, 256), f32),
        grid_spec=grid_spec,
        compiler_params=_cparams(("parallel",)),
        name="moba_attention_decode",
    )(page_table.reshape(-1), q, moba_new, seg_mean, e_blk, *([cache_t] * n_pages))


def _sb_dec_body(pt_ref, q_ref, uu_ref, *rest, n_pages):
    pages, o_ref = rest[:n_pages], rest[n_pages]
    kw = SB_HEADS * HEAD_DIM
    qm, own_lanes = _q_head_rows4(q_ref[...])
    qmb = qm.astype(bf16)
    k_t = jnp.concatenate([pages[j][0:kw, :].astype(bf16) for j in range(n_pages)], axis=1)
    v_t = jnp.concatenate([pages[j][kw:2 * kw, :].astype(bf16) for j in range(n_pages)], axis=1)
    z = _dot(qmb, k_t)
    lk = _log_keep(z)
    stacked = jnp.concatenate([lk[:, PAGE_SIZE * j:PAGE_SIZE * (j + 1)] for j in range(n_pages)], axis=0)
    later, total = _suffix_sums(stacked, uu_ref[...])
    carry = jnp.zeros((8, LANES), f32)
    between = [None] * n_pages
    for j in reversed(range(n_pages)):
        between[j] = later[8 * j:8 * (j + 1)] + carry
        carry = carry + total[8 * j:8 * (j + 1)]
    w = jnp.exp(z + lk + jnp.concatenate(between, axis=1))
    acc = _nt(w.astype(bf16), v_t)
    o_ref[...] = jnp.sum(jnp.where(own_lanes, acc, 0.0), axis=0, keepdims=True)


def _sb_decode(page_table, cache_t, layer_base, q, uu):
    s, n_pages = page_table.shape
    grid_spec = pltpu.PrefetchScalarGridSpec(
        num_scalar_prefetch=1,
        grid=(s,),
        in_specs=[_per_seq(256), _dec_const(uu.shape)] + _page_specs(n_pages, layer_base),
        out_specs=_per_seq(256),
    )
    return pl.pallas_call(
        functools.partial(_sb_dec_body, n_pages=n_pages),
        out_shape=jax.ShapeDtypeStruct((s, 1, 256), f32),
        grid_spec=grid_spec,
        compiler_params=_cparams(("parallel",)),
        name="stickbreak_attention_decode",
    )(page_table.reshape(-1), q, uu, *([cache_t] * n_pages))


def _agg_matrix(nc, n_cmp):
    c0 = np.arange(nc)[:, None] * CMP_STRIDE
    s0 = np.arange(LANES)[None, :] * SEL_BLOCK
    ov = np.clip(np.minimum(c0 + CMP_LEN, s0 + SEL_BLOCK) - np.maximum(c0, s0), 0, None) / CMP_LEN
    ov[n_cmp:] = 0.0
    return jnp.asarray(ov, f32)


def _block_onehot_t(block, l):
    e = (np.arange(l)[None, :] // block) == np.arange(LANES)[:, None]
    return jnp.asarray(np.where(e, -MASK_BIG, 0.0), bf16)


def _expand_matrix(block, kp):
    e = (np.arange(kp)[None, :] // block) == np.arange(LANES)[:, None]
    return jnp.asarray(e, bf16)


def _suffix_matrix():
    j = np.arange(2 * LANES)[:, None] % LANES
    s = np.arange(2 * LANES)[None, :]
    return jnp.asarray((s >= LANES) | (j > s), bf16)


def _layer_weights(l, w_in, cmp_pe, cmp_w1, cmp_w2, w_br_a, w_br_b, w_br_c, w_o, w_up, conv_w, conv_b, w_down):
    d = w_in.shape[1]
    w = w_in[l]
    w_proj = jnp.concatenate([w[:, 0:1280], w[:, 1304:], w[:, 1280:1304], jnp.zeros((d, _PROJ_W - 5912), f32)],
                             axis=1).astype(bf16)
    pe4 = jnp.concatenate([cmp_pe[l], cmp_pe[l]], axis=2)
    w1 = cmp_w1[l].reshape(2, CMP_LEN, HEAD_DIM, CMP_HID)
    w1bd = jnp.zeros((2, CMP_LEN, NSA_KV * HEAD_DIM, NSA_KV * CMP_HID), f32)
    w2bd = jnp.zeros((2, NSA_KV * CMP_HID, NSA_KV * HEAD_DIM), f32)
    for g in range(NSA_KV):
        w1bd = w1bd.at[:, :, HEAD_DIM * g:HEAD_DIM * (g + 1), CMP_HID * g:CMP_HID * (g + 1)].set(w1)
        w2bd = w2bd.at[:, CMP_HID * g:CMP_HID * (g + 1), HEAD_DIM * g:HEAD_DIM * (g + 1)].set(cmp_w2[l])
    cw8 = jnp.concatenate([conv_w[l], jnp.zeros((8 - CONV_W, conv_w.shape[2]), f32)], axis=0)
    return dict(w_proj=w_proj, pe4=pe4, w1bd=w1bd.astype(bf16), w2bd=w2bd.astype(bf16),
                wa=w_br_a[l].astype(bf16), wb=w_br_b[l].astype(bf16), wc=w_br_c[l].astype(bf16),
                wo=w_o[l].astype(bf16), w_up=w_up[l].astype(bf16), cw8=cw8, cb=conv_b[l].reshape(1, -1),
                wd=w_down[l].astype(bf16))


def _mod_parts(mod_rows, per_row):
    r = mod_rows.shape[0]
    parts = mod_rows.reshape(r, 6, D_MODEL)
    return [parts[:, k].reshape((1, r, D_MODEL) if per_row else (r, 1, D_MODEL)) for k in range(6)]


def kernel(x_prompt, x_sample, cache_nsa, cache_moba, cache_sb, state_win, state_conv, page_table, c_prompt,
           c_sample, norm1_g, norm2_g, w_ada, b_ada, w_in, cmp_pe, cmp_w1, cmp_w2, w_br_a, w_br_b, w_br_c, w_o,
           w_up, conv_w, conv_b, w_down, final_g):
    b, t, d = x_prompt.shape
    s = x_sample.shape[0]
    depth = w_in.shape[0]
    n_phys = cache_nsa.shape[1]
    n_pages = page_table.shape[1]
    past = n_pages * PAGE_SIZE
    tm = 256
    tq = 128
    tk = 512

    n_c = b + s
    c_all = jnp.concatenate([c_prompt, c_sample, jnp.zeros((-n_c % 8, d), f32)], axis=0)
    mod = _ada_mod(c_all, w_ada, b_ada)

    nc_p = t // CMP_STRIDE
    n_cmp_p = (t - CMP_LEN) // CMP_STRIDE + 1
    agg_p = _agg_matrix(nc_p, n_cmp_p)
    nc_s = past // CMP_STRIDE
    agg_s = _agg_matrix(nc_s, (past + 1 - CMP_LEN) // CMP_STRIDE + 1)
    e_sel = _expand_matrix(SEL_BLOCK, past)
    e_blk = _expand_matrix(MOBA_BLOCK, past)
    seg_mean = (jnp.swapaxes(e_blk, 0, 1).astype(f32) * (1.0 / MOBA_BLOCK)).astype(bf16)
    uu = _suffix_matrix()
    nb_pad = LANES // MOBA_HEADS

    feat_major = lambda a: jnp.transpose(a, (0, 1, 3, 4, 5, 2))
    caches = [feat_major(c).reshape(depth * n_phys, 512, PAGE_SIZE) for c in (cache_nsa, cache_moba, cache_sb)]
    state_win_t = feat_major(state_win).reshape(depth * s, 256, state_win.shape[2])

    xp = x_prompt.reshape(b * t, d)
    xs = x_sample.reshape(s, d)
    outs_p = [[] for _ in range(5)]
    outs_s = [[] for _ in range(5)]
    for l in range(depth):
        w = _layer_weights(l, w_in, cmp_pe, cmp_w1, cmp_w2, w_br_a, w_br_b, w_br_c, w_o, w_up, conv_w, conv_b,
                           w_down)
        mp = _mod_parts(mod[l, 0:b], per_row=False)
        ms = _mod_parts(mod[l, b:b + s], per_row=True)

        (q_n, nsa_rows, nsa_bf, win_rows, win_bf, mb_q, moba_rows, moba_bf, sb_q, sb_rows, sb_bf, g_m, g_n) = \
            _norm_mod_matmul(xp, norm1_g[l], mp[0], mp[1], w["w_proj"], _IN_SEGS, _IN_DTYPES, tm, "in_proj_prompt")
        r3 = lambda a: a.reshape(b, t, a.shape[1])
        cmp = _cmp_prompt(r3(nsa_rows), w["pe4"], w["w1bd"], w["w2bd"])
        o_a = _nsa_prompt(r3(q_n), r3(g_n), cmp, agg_p, r3(nsa_bf), r3(win_bf), n_cmp_p, tq, tk)
        kmean = _block_mean(r3(moba_rows), 256, MOBA_BLOCK).reshape(b, t // MOBA_BLOCK, MOBA_HEADS, HEAD_DIM)
        kmt = jnp.zeros((b, MOBA_HEADS, nb_pad, MOBA_HEADS, HEAD_DIM), f32)
        for h in range(MOBA_HEADS):
            kmt = kmt.at[:, h, 0:t // MOBA_BLOCK, h].set(kmean[:, :, h])
        o_b = _moba_prompt(r3(mb_q), kmt.reshape(b, LANES, 256), r3(moba_bf), tq, tk)
        o_c = _sb_prompt(r3(sb_q), uu, r3(sb_bf), tq)
        xp = _merge_out(o_a.reshape(b * t, -1), o_b.reshape(b * t, -1), o_c.reshape(b * t, -1), g_m, xp, mp[2],
                        w["wa"], w["wb"], w["wc"], w["wo"], tm)
        u_a, u_b = _norm_mod_matmul(xp, norm2_g[l], mp[3], mp[4], w["w_up"], _UP_SEGS, _UP_DTYPES, tm,
                                    "ffn_up_prompt")
        xp = _ffn_down_seq(u_a, u_b, w["cw8"], w["cb"], w["wd"], xp, mp[5], tm)
        keep = min(WINDOW, t)
        outs_p[0].append(nsa_rows.reshape(b, t // PAGE_SIZE, PAGE_SIZE, 4, NSA_KV, HEAD_DIM))
        outs_p[1].append(moba_rows.reshape(b, t // PAGE_SIZE, PAGE_SIZE, 2, MOBA_HEADS, HEAD_DIM))
        outs_p[2].append(sb_rows.reshape(b, t // PAGE_SIZE, PAGE_SIZE, 2, SB_HEADS, HEAD_DIM))
        outs_p[3].append(r3(win_rows)[:, t - keep:].reshape(b, keep, 2, NSA_KV, HEAD_DIM))
        outs_p[4].append(r3(u_a)[:, t - (CONV_W - 1):])

        (q_n, nsa_rows, _, win_rows, _, mb_q, moba_rows, _, sb_q, sb_rows, _, g_m, g_n) = \
            _norm_mod_matmul(xs, norm1_g[l], ms[0], ms[1], w["w_proj"], _IN_SEGS, _IN_DTYPES, s, "in_proj_sample")
        s3 = lambda a: a.astype(f32).reshape(s, 1, a.shape[1])
        base = l * n_phys
        o_a = _nsa_decode(page_table, caches[0], base, s3(q_n), s3(g_n), s3(nsa_rows), s3(win_rows), state_win_t,
                          l * s, w["pe4"], w["w1bd"], w["w2bd"], agg_s, e_sel)
        o_b = _moba_decode(page_table, caches[1], base, s3(mb_q), s3(moba_rows), seg_mean, e_blk)
        o_c = _sb_decode(page_table, caches[2], base, s3(sb_q), uu)
        xs = _merge_out(o_a.reshape(s, -1).astype(bf16), o_b.reshape(s, -1).astype(bf16),
                        o_c.reshape(s, -1).astype(bf16), g_m, xs, ms[2], w["wa"], w["wb"], w["wc"], w["wo"], s)
        u_a, u_b = _norm_mod_matmul(xs, norm2_g[l], ms[3], ms[4], w["w_up"], _UP_SEGS, _UP_DTYPES, s,
                                    "ffn_up_sample")
        xs = _ffn_down_step(u_a, state_conv[l, :, 1], state_conv[l, :, 0], u_b, w["cw8"], w["cb"], w["wd"], xs,
                            ms[5])
        win_full = jnp.concatenate([state_win[l], win_rows.reshape(s, 1, 2, NSA_KV, HEAD_DIM)], axis=1)
        conv_full = jnp.concatenate([state_conv[l], u_a.reshape(s, 1, -1)], axis=1)
        outs_s[0].append(nsa_rows.reshape(s, 1, 4, NSA_KV, HEAD_DIM))
        outs_s[1].append(moba_rows.reshape(s, 1, 2, MOBA_HEADS, HEAD_DIM))
        outs_s[2].append(sb_rows.reshape(s, 1, 2, SB_HEADS, HEAD_DIM))
        outs_s[3].append(win_full[:, win_full.shape[1] - state_win.shape[2]:])
        outs_s[4].append(conv_full[:, conv_full.shape[1] - (CONV_W - 1):])

    y_prompt = _final_norm(xp, final_g, tm).reshape(b, t, d)
    y_sample = _final_norm(xs, final_g, s).reshape(s, 1, d)
    st = lambda lst: jnp.stack(lst)
    return (y_prompt, y_sample, st(outs_p[0]), st(outs_s[0]), st(outs_p[1]), st(outs_s[1]), st(outs_p[2]),
            st(outs_s[2]), st(outs_p[3]), st(outs_s[3]), st(outs_p[4]), st(outs_s[4]))
```

```python
import functools

import numpy as np
import jax
import jax.numpy as jnp
from jax import lax
from jax.experimental import pallas as pl
from jax.experimental.pallas import tpu as pltpu

f32 = jnp.float32
bf16 = jnp.bfloat16

D_MODEL = 1024
HEAD_DIM = 64
NSA_HEADS = 8
NSA_KV = 2
HPG = NSA_HEADS // NSA_KV
CMP_LEN = 32
CMP_STRIDE = 16
CMP_HID = 128
SEL_BLOCK = 64
SEL_TOPN = 16
WINDOW = 512
MOBA_HEADS = 4
MOBA_BLOCK = 256
MOBA_TOPK = 3
SB_HEADS = 4
D_FF = 2816
CONV_W = 3
PAGE_SIZE = 128
RMS_EPS = 1e-6
NEG = -1e30
FORCE = 1e9

LANES = 128
MASK_BIG = 2.0 ** 100
M_INIT = -1e29
REMOVED = -3e38
SB_CUTOFF = -110.0
VMEM_LIMIT_MB = 56

NSA_SLOPES = [2.0 ** (-8.0 * (h + 1) / NSA_HEADS) for h in range(NSA_HEADS)]
MOBA_SLOPES = [2.0 ** (-8.0 * (h + 1) / MOBA_HEADS) for h in range(MOBA_HEADS)]

_Q_N, _NSA, _WIN, _MB_Q, _MOBA, _SB_Q, _SB, _G_M, _G_N, _PROJ_W = 0, 512, 1024, 1280, 1536, 2048, 2304, 2816, 5888, 6016


def _cparams(sem, vmem_mb=VMEM_LIMIT_MB):
    return pltpu.CompilerParams(dimension_semantics=sem, vmem_limit_bytes=vmem_mb * 2 ** 20)


def _const_spec(shape):
    nd = len(shape)
    return pl.BlockSpec(shape, lambda *_: (0,) * nd, pipeline_mode=pl.Buffered(1))


def _nt(a, b):
    return lax.dot_general(a, b, (((1,), (1,)), ((), ())), preferred_element_type=f32)


def _dot(a, b):
    return jnp.dot(a, b, preferred_element_type=f32)


def _gelu_tanh(x):
    return x * (0.5 * (1.0 + jnp.tanh(np.sqrt(2.0 / np.pi) * (x + 0.044715 * (x * x * x)))))


def _ref_softmax(s, valid):
    s = jnp.where(valid, s, NEG)
    e = jnp.where(valid, jnp.exp(s - jnp.max(s, axis=-1, keepdims=True)), 0.0)
    return e / jnp.maximum(jnp.sum(e, axis=-1, keepdims=True), 1e-30)


def _ada_body(c_ref, w_ref, b_ref, o_ref):
    c = c_ref[...]
    s = c * jax.nn.sigmoid(c)
    o_ref[...] = _dot(s.astype(bf16), w_ref[...].astype(bf16)) + b_ref[...]


def _ada_mod(c_all, w_ada, b_ada):
    depth, d, n = w_ada.shape
    r = c_all.shape[0]
    tn = 1536
    return pl.pallas_call(
        _ada_body,
        out_shape=jax.ShapeDtypeStruct((depth, r, n), f32),
        grid=(depth, n // tn),
        in_specs=[pl.BlockSpec((r, d), lambda l, j: (0, 0)),
                  pl.BlockSpec((None, d, tn), lambda l, j: (l, 0, j)),
                  pl.BlockSpec((None, 1, tn), lambda l, j: (l, 0, j))],
        out_specs=pl.BlockSpec((None, r, tn), lambda l, j: (l, 0, j)),
        compiler_params=_cparams(("parallel", "parallel")),
        name="ada_mod",
    )(c_all, w_ada, b_ada.reshape(depth, 1, n))


def _nmm_body(x_ref, g_ref, sh_ref, sc_ref, w_ref, *o_refs, segs):
    x = x_ref[...]
    y = x * lax.rsqrt(jnp.mean(x * x, axis=-1, keepdims=True) + RMS_EPS)
    h = (y * g_ref[...]) * (1.0 + sc_ref[...]) + sh_ref[...]
    hb = h.astype(bf16)
    k = 0
    for off, width, scales in segs:
        outs = o_refs[k:k + len(scales)]
        k += len(scales)
        for c0 in range(0, width, 512):
            cw = min(512, width - c0)
            acc = _dot(hb, w_ref[:, off + c0:off + c0 + cw])
            for o, scale in zip(outs, scales):
                o[:, c0:c0 + cw] = (acc if scale == 1.0 else acc * scale).astype(o.dtype)


def _norm_mod_matmul(x, g, shift, scale, w_bf, segs, out_dtypes, tm, name):
    r, d = x.shape
    nb, rb, _ = shift.shape
    tiles_per_b = (r // nb) // tm
    out_shape, out_specs = [], []
    k = 0
    for off, width, scales in segs:
        for _ in scales:
            out_shape.append(jax.ShapeDtypeStruct((r, width), out_dtypes[k]))
            out_specs.append(pl.BlockSpec((tm, width), lambda i: (i, 0)))
            k += 1
    mod_spec = pl.BlockSpec((None, rb, d), lambda i: (i // tiles_per_b, 0, 0))
    return pl.pallas_call(
        functools.partial(_nmm_body, segs=segs),
        out_shape=out_shape,
        grid=(r // tm,),
        in_specs=[pl.BlockSpec((tm, d), lambda i: (i, 0)), _const_spec((1, d)), mod_spec, mod_spec,
                  _const_spec(w_bf.shape)],
        out_specs=out_specs,
        compiler_params=_cparams(("parallel",)),
        name=name,
    )(x, g.reshape(1, d), shift, scale, w_bf)


_IN_SEGS = ((_Q_N, 512, (0.125,)), (_NSA, 512, (1.0, 1.0)), (_WIN, 256, (1.0, 1.0)), (_MB_Q, 256, (0.125,)),
            (_MOBA, 512, (1.0, 1.0)), (_SB_Q, 256, (0.125,)), (_SB, 512, (1.0, 1.0)), (_G_M, 3072, (1.0,)),
            (_G_N, 128, (1.0,)))
_IN_DTYPES = (bf16, f32, bf16, f32, bf16, bf16, f32, bf16, bf16, f32, bf16, f32, f32)
_UP_SEGS = ((0, D_FF, (1.0,)), (D_FF, D_FF, (1.0,)))
_UP_DTYPES = (f32, f32)


def _cmp_core(rows_refs, pe_ref, w1_ref, w2_ref, nc):
    outs = []
    for kv in range(2):
        acc_a = jnp.zeros((nc, NSA_KV * CMP_HID), f32)
        acc_b = jnp.zeros((nc, NSA_KV * CMP_HID), f32)
        for r in range(CMP_STRIDE):
            y = rows_refs[kv][pl.ds(r, nc, stride=CMP_STRIDE), :]
            acc_a = acc_a + _dot((y + pe_ref[kv, r:r + 1, :]).astype(bf16), w1_ref[kv, r])
            acc_b = acc_b + _dot((y + pe_ref[kv, CMP_STRIDE + r:CMP_STRIDE + r + 1, :]).astype(bf16),
                                 w1_ref[kv, CMP_STRIDE + r])
        pre = acc_a + pltpu.roll(acc_b, nc - 1, 0)
        outs.append(_dot(_gelu_tanh(pre).astype(bf16), w2_ref[kv]))
    return jnp.concatenate(outs, axis=1)


def _cmp_prompt_body(k_ref, v_ref, pe_ref, w1_ref, w2_ref, o_ref, *, nc):
    o_ref[...] = _cmp_core((k_ref, v_ref), pe_ref, w1_ref, w2_ref, nc)


def _cmp_prompt(nsa_rows, pe2, w1bd, w2bd):
    b, l, _ = nsa_rows.shape
    nc = l // CMP_STRIDE
    return pl.pallas_call(
        functools.partial(_cmp_prompt_body, nc=nc),
        out_shape=jax.ShapeDtypeStruct((b, nc, 256), f32),
        grid=(b,),
        in_specs=[pl.BlockSpec((None, l, LANES), lambda i: (i, 0, 0)),
                  pl.BlockSpec((None, l, LANES), lambda i: (i, 0, 1)), _const_spec(pe2.shape),
                  _const_spec(w1bd.shape), _const_spec(w2bd.shape)],
        out_specs=pl.BlockSpec((None, nc, 256), lambda i: (i, 0, 0)),
        compiler_params=_cparams(("parallel",)),
        name="nsa_compress_prompt",
    )(nsa_rows, nsa_rows, pe2, w1bd, w2bd)


SLAB = 64
FLASH_SCRATCH = 2


class _Flash:
    def __init__(self, scratch, slopes, tq, t0):
        self.m, self.acc = scratch
        self.slope8 = jnp.concatenate(
            [jnp.full((1, 1), v, f32) for v in slopes] + [jnp.zeros((8 - len(slopes), 1), f32)], axis=0)
        self.tq, self.t0 = tq, t0
        self.m[...] = jnp.full(self.m.shape, M_INIT, f32)
        self.acc[...] = jnp.zeros(self.acc.shape, f32)

    def slab(self, j, s_all, bias, rel, masked):
        r0 = j * SLAB
        rows = slice(r0, r0 + SLAB)
        head = r0 // self.tq
        s = s_all[rows, :] + bias[head:head + 1, :]
        if masked:
            t_rel = (r0 + lax.broadcasted_iota(jnp.int32, (SLAB, 1), 0)) % self.tq
            s = jnp.where(rel <= t_rel, s, -MASK_BIG)
        m_old = self.m[rows, :]
        m_new = jnp.maximum(m_old, jnp.max(s, axis=-1, keepdims=True))
        self.m[rows, :] = m_new
        p = jnp.exp(s - jnp.concatenate([m_new] * (s.shape[1] // LANES), axis=1))
        return p.astype(bf16), jnp.exp(m_old - m_new)

    def result(self):
        acc = self.acc[...]
        return acc[:, 0:LANES] / acc[:, LANES:2 * LANES]


def _flash_tiles(chains, k0, masked):
    first = chains[0][0]
    rows = first.m.shape[0]
    tk = chains[0][2].shape[1]
    rel = k0 - first.t0 + lax.broadcasted_iota(jnp.int32, (1, tk), 1)
    logits = [(_dot(q_aug, kt_aug), f.slope8 * rel.astype(f32)) for f, q_aug, kt_aug, _ in chains]
    parts = [[] for _ in chains]
    for j in range(rows // SLAB):
        for c, (f, _, _, _) in enumerate(chains):
            parts[c].append(f.slab(j, logits[c][0], logits[c][1], rel, masked))
    for c, (f, _, _, v_aug) in enumerate(chains):
        p = jnp.concatenate([x[0] for x in parts[c]], axis=0)
        alpha = jnp.concatenate([x[1] for x in parts[c]], axis=0)
        f.acc[...] = jnp.concatenate([alpha, alpha], axis=1) * f.acc[...] + _dot(p, v_aug)


def _flash_scratch(rows):
    return [pltpu.VMEM((rows, LANES), f32), pltpu.VMEM((rows, 2 * LANES), f32)]


def _with_ones(v):
    return jnp.concatenate([v, jnp.ones(v.shape, v.dtype)], axis=-1)


def _masked_exp(s_all, slopes, rel, valid_fn, tq):
    out = []
    for j in range(s_all.shape[0] // SLAB):
        r0 = j * SLAB
        t_rel = r0 % tq + lax.broadcasted_iota(jnp.int32, (SLAB, 1), 0)
        valid = valid_fn(t_rel)
        s = jnp.where(valid, s_all[r0:r0 + SLAB, :] + slopes[r0 // tq] * rel.astype(f32), NEG)
        out.append(jnp.exp(s - jnp.maximum(jnp.max(s, axis=-1, keepdims=True), M_INIT)))
    return jnp.concatenate(out, axis=0)


def _topk_rows(score, ids, k, n_ids):
    picked = jnp.zeros(score.shape, f32)
    for _ in range(k):
        mx = jnp.max(score, axis=0, keepdims=True)
        idx = jnp.min(jnp.where(score == mx, ids, n_ids), axis=0, keepdims=True)
        pick = ids == idx
        picked = jnp.where(pick, 1.0, picked)
        score = jnp.where(pick, REMOVED, score)
    return picked


def _nsa_prompt_body(q_ref, gn_ref, kct_ref, vc_ref, agg_ref, kst_ref, vs_ref, kwt_ref, vw_ref, o_ref, sel_scr,
                     qa_scr, part_scr, gate_scr, *flash_scr, tq, tk, n_cmp):
    i = pl.program_id(1)
    t0 = i * tq
    nc = vc_ref.shape[0]
    qf = q_ref[...].astype(f32)
    lane = lax.broadcasted_iota(jnp.int32, (tq, LANES), 1)
    sig = jax.nn.sigmoid(gn_ref[...])
    kct = kct_ref[...]
    vc = vc_ref[...].astype(bf16)
    aggb = agg_ref[...].astype(bf16)
    n_id = lax.broadcasted_iota(jnp.int32, (1, nc), 1)
    cend_rel = n_id * CMP_STRIDE + (CMP_LEN - 1) - t0
    ones_c = jnp.ones((nc, LANES), bf16)
    blk_id = lax.broadcasted_iota(jnp.int32, (LANES, tq), 0)
    cur = (t0 + lax.broadcasted_iota(jnp.int32, (LANES, tq), 1)) // SEL_BLOCK
    causal_blk = blk_id <= cur
    forced = (blk_id == 0) | (blk_id == cur) | (blk_id == cur - 1)
    chunks = [jnp.zeros((tq, LANES), f32) for _ in range(NSA_HEADS // 2)]

    for g in range(NSA_KV):
        heads = [HPG * g + hh for hh in range(HPG)]
        pieces = []
        for h in heads:
            blk = qf[:, LANES * (h // 2):LANES * (h // 2 + 1)]
            if h % 2 != g:
                blk = pltpu.roll(blk, HEAD_DIM, 1)
            pieces.append(jnp.where(lane // HEAD_DIM == g, blk, 0.0))
        qg = jnp.concatenate(pieces, axis=0).astype(bf16)
        slopes = [NSA_SLOPES[h] for h in heads]

        valid_c = lambda t_rel: (cend_rel <= t_rel) & (n_id < n_cmp)
        e_c = _masked_exp(_dot(qg, kct), slopes, cend_rel, valid_c, tq)
        e_hi = e_c.astype(bf16)
        e_lo = (e_c - e_hi.astype(f32)).astype(bf16)
        r_hi = _dot(e_hi, jnp.concatenate([vc, ones_c, aggb], axis=1))
        r_lo = _dot(e_lo, jnp.concatenate([aggb, ones_c], axis=1))
        o_c = r_hi[:, 0:LANES] / jnp.maximum(r_hi[:, LANES:2 * LANES], 1e-30)
        imp_rows = (r_hi[:, 2 * LANES:3 * LANES] + r_lo[:, 0:LANES]) / jnp.maximum(
            r_hi[:, LANES:2 * LANES] + r_lo[:, LANES:2 * LANES], 1e-30)
        imp = imp_rows[0:tq] + imp_rows[tq:2 * tq] + imp_rows[2 * tq:3 * tq] + imp_rows[3 * tq:4 * tq]

        score = jnp.where(causal_blk, jnp.where(forced, FORCE, imp.T), NEG)
        sel_t = jnp.where(causal_blk, _topk_rows(score, blk_id, SEL_TOPN, LANES), 0.0)
        notsel = (1.0 - sel_t).T
        qa_scr[g] = jnp.concatenate([qg, jnp.concatenate([notsel] * HPG, axis=0).astype(bf16)], axis=1)
        sel_scr[g] = sel_t

        wl = WINDOW + tq
        s0 = pl.multiple_of(jnp.maximum(t0 - WINDOW, 0), tq)
        w_rel = s0 - t0 + lax.broadcasted_iota(jnp.int32, (1, wl), 1)
        valid_w = lambda t_rel: (w_rel <= t_rel) & (w_rel > t_rel - WINDOW)
        e_w = _masked_exp(_dot(qg, kwt_ref[:, pl.ds(s0, wl)]), slopes, w_rel, valid_w, tq)
        r_w = _dot(e_w.astype(bf16), vw_ref[pl.ds(s0, wl), :])
        o_w = r_w[:, 0:LANES] / r_w[:, LANES:2 * LANES]

        gates = [jnp.concatenate([sig[:, br * NSA_HEADS + h:br * NSA_HEADS + h + 1] for h in heads], axis=0)
                 for br in range(3)]
        part_scr[g] = gates[0] * o_c + gates[2] * o_w
        gate_scr[g] = gates[1]

    flashes = [_Flash(flash_scr[FLASH_SCRATCH * g:FLASH_SCRATCH * (g + 1)],
                      NSA_SLOPES[HPG * g:HPG * (g + 1)], tq, t0) for g in range(NSA_KV)]

    def sel_step(kt, masked):
        k0 = pl.multiple_of(kt * tk, tk)
        kt_aug = kst_ref[:, pl.ds(k0, tk)]
        v_aug = vs_ref[pl.ds(k0, tk), :]
        _flash_tiles([(flashes[g], qa_scr[g], kt_aug, v_aug) for g in range(NSA_KV)], k0, masked)

    def sel_loop(kt, carry):
        blk0 = pl.multiple_of(kt * (tk // SEL_BLOCK), tk // SEL_BLOCK)
        picked = jnp.maximum(sel_scr[0, pl.ds(blk0, tk // SEL_BLOCK), :], sel_scr[1, pl.ds(blk0, tk // SEL_BLOCK), :])

        @pl.when(jnp.max(picked) > 0.0)
        def _():
            sel_step(kt, False)

        return carry

    kd = t0 // tk
    lax.fori_loop(0, kd, sel_loop, 0)
    sel_step(kd, True)

    for g in range(NSA_KV):
        heads = [HPG * g + hh for hh in range(HPG)]
        o = part_scr[g] + gate_scr[g] * flashes[g].result()
        for hh, h in enumerate(heads):
            piece = o[hh * tq:(hh + 1) * tq]
            if h % 2 != g:
                piece = pltpu.roll(piece, HEAD_DIM, 1)
            chunks[h // 2] = chunks[h // 2] + jnp.where(lane // HEAD_DIM == h % 2, piece, 0.0)

    o_ref[...] = jnp.concatenate(chunks, axis=1).astype(o_ref.dtype)


def _nsa_prompt(q_n, g_n, cmp, agg, nsa_bf, win_bf, n_cmp, tq, tk):
    b, l, _ = q_n.shape
    nc = cmp.shape[1]
    rows = HPG * tq
    kct = jnp.swapaxes(cmp[:, :, 0:LANES], 1, 2).astype(bf16)
    onehot_t = jnp.broadcast_to(_block_onehot_t(SEL_BLOCK, l)[None], (b, LANES, l))
    kst = jnp.concatenate([jnp.swapaxes(nsa_bf[:, :, 2 * LANES:3 * LANES], 1, 2), onehot_t], axis=1)
    kwt = jnp.swapaxes(win_bf[:, :, 0:LANES], 1, 2)
    per_b = lambda shape, col=0: pl.BlockSpec((None,) + shape, lambda bi, i: (bi, 0, col))
    return pl.pallas_call(
        functools.partial(_nsa_prompt_body, tq=tq, tk=tk, n_cmp=n_cmp),
        out_shape=jax.ShapeDtypeStruct((b, l, NSA_HEADS * HEAD_DIM), bf16),
        grid=(b, l // tq),
        in_specs=[pl.BlockSpec((None, tq, 512), lambda bi, i: (bi, i, 0)),
                  pl.BlockSpec((None, tq, LANES), lambda bi, i: (bi, i, 0)),
                  per_b((LANES, nc)), per_b((nc, LANES), 1), _const_spec(agg.shape),
                  per_b((2 * LANES, l)), per_b((l, 2 * LANES)), per_b((LANES, l)), per_b((l, 2 * LANES))],
        out_specs=pl.BlockSpec((None, tq, 512), lambda bi, i: (bi, i, 0)),
        scratch_shapes=[pltpu.VMEM((NSA_KV, LANES, tq), f32), pltpu.VMEM((NSA_KV, rows, 2 * LANES), bf16),
                        pltpu.VMEM((NSA_KV, rows, LANES), f32), pltpu.VMEM((NSA_KV, rows, 1), f32)]
        + _flash_scratch(rows) * NSA_KV,
        compiler_params=_cparams(("parallel", "parallel")),
        name="nsa_attention_prompt",
    )(q_n, g_n, kct, cmp, agg, kst, _with_ones(nsa_bf[:, :, 3 * LANES:4 * LANES]), kwt,
      _with_ones(win_bf[:, :, LANES:2 * LANES]))


def _block_mean_body(k_ref, o_ref):
    o_ref[...] = jnp.mean(k_ref[...], axis=0, keepdims=True)


def _block_mean(rows, width, blk):
    b, l, _ = rows.shape
    return pl.pallas_call(
        _block_mean_body,
        out_shape=jax.ShapeDtypeStruct((b, l // blk, 1, width), f32),
        grid=(b, l // blk),
        in_specs=[pl.BlockSpec((None, blk, width), lambda bi, j: (bi, j, 0))],
        out_specs=pl.BlockSpec((None, None, 1, width), lambda bi, j: (bi, j, 0, 0)),
        compiler_params=_cparams(("parallel", "parallel")),
        name="moba_block_mean",
    )(rows)


def _moba_prompt_body(q_ref, kmt_ref, kt_ref, v_ref, o_ref, qa_scr, *flash_scr, tq, tk, nb_pad):
    i = pl.program_id(1)
    t0 = i * tq
    q = q_ref[...]
    qf = q.astype(f32)
    lane = lax.broadcasted_iota(jnp.int32, (tq, LANES), 1)

    gate_t = _nt(kmt_ref[...].astype(bf16), q)
    blk_id = lax.broadcasted_iota(jnp.int32, (LANES, tq), 0) % nb_pad
    own = (t0 + lax.broadcasted_iota(jnp.int32, (LANES, tq), 1)) // MOBA_BLOCK
    past = blk_id < own
    score = jnp.where(past, gate_t, NEG)
    parts = [_topk_rows(score[nb_pad * h:nb_pad * (h + 1)], blk_id[nb_pad * h:nb_pad * (h + 1)], MOBA_TOPK, nb_pad)
             for h in range(MOBA_HEADS)]
    sel_t = jnp.where(past, jnp.concatenate(parts, axis=0), 0.0)
    sel_t = jnp.where(blk_id == own, 1.0, sel_t)
    notsel = (1.0 - sel_t).T

    nch = MOBA_HEADS // 2
    for c in range(nch):
        q_rows = []
        for e in range(2):
            h = 2 * c + e
            qh = jnp.where(lane // HEAD_DIM == e, qf[:, LANES * c:LANES * (c + 1)], 0.0)
            ns = notsel if h == 0 else pltpu.roll(notsel, LANES - nb_pad * h, 1)
            ns = jnp.where(lane < nb_pad, ns, 0.0)
            q_rows.append(jnp.concatenate([qh, ns], axis=1))
        qa_scr[c] = jnp.concatenate(q_rows, axis=0).astype(bf16)
    flashes = [_Flash(flash_scr[FLASH_SCRATCH * c:FLASH_SCRATCH * (c + 1)],
                      MOBA_SLOPES[2 * c:2 * c + 2], tq, t0) for c in range(nch)]

    def step(kt, masked):
        k0 = pl.multiple_of(kt * tk, tk)
        _flash_tiles([(flashes[c], qa_scr[c], kt_ref[c, :, pl.ds(k0, tk)],
                       v_ref[c, pl.ds(k0, tk), :]) for c in range(nch)], k0, masked)

    def loop(kt, carry):
        step(kt, False)
        return carry

    lax.fori_loop(0, t0 // tk, loop, 0)
    step(t0 // tk, True)
    out_chunks = []
    for c in range(nch):
        o = flashes[c].result()
        out_chunks.append(jnp.where(lane < HEAD_DIM, o[0:tq], o[tq:2 * tq]))
    o_ref[...] = jnp.concatenate(out_chunks, axis=1).astype(o_ref.dtype)


def _moba_prompt(mb_q, kmt, moba_bf, tq, tk):
    b, l, _ = mb_q.shape
    nb_pad = LANES // MOBA_HEADS
    nch = MOBA_HEADS // 2
    onehot_t = jnp.broadcast_to(_block_onehot_t(MOBA_BLOCK, l)[None, None], (b, nch, LANES, l))
    k_t = jnp.swapaxes(moba_bf[:, :, 0:nch * LANES].reshape(b, l, nch, LANES), 1, 3)
    kt_aug = jnp.concatenate([jnp.swapaxes(k_t, 1, 2), onehot_t], axis=2)
    v_aug = _with_ones(jnp.swapaxes(moba_bf[:, :, nch * LANES:].reshape(b, l, nch, LANES), 1, 2))
    return pl.pallas_call(
        functools.partial(_moba_prompt_body, tq=tq, tk=tk, nb_pad=nb_pad),
        out_shape=jax.ShapeDtypeStruct((b, l, MOBA_HEADS * HEAD_DIM), bf16),
        grid=(b, l // tq),
        in_specs=[pl.BlockSpec((None, tq, 256), lambda bi, i: (bi, i, 0)),
                  pl.BlockSpec((None, LANES, 256), lambda bi, i: (bi, 0, 0)),
                  pl.BlockSpec((None, nch, 2 * LANES, l), lambda bi, i: (bi, 0, 0, 0)),
                  pl.BlockSpec((None, nch, l, 2 * LANES), lambda bi, i: (bi, 0, 0, 0))],
        out_specs=pl.BlockSpec((None, tq, 256), lambda bi, i: (bi, i, 0)),
        scratch_shapes=[pltpu.VMEM((nch, 2 * tq, 2 * LANES), bf16)] + _flash_scratch(2 * tq) * nch,
        compiler_params=_cparams(("parallel", "parallel")),
        name="moba_attention_prompt",
    )(mb_q, kmt, kt_aug, v_aug)


def _log_keep(z):
    return -(jnp.maximum(z, 0.0) + jnp.log(1.0 + jnp.exp(-jnp.abs(z))))


def _suffix_sums(lk, uu):
    hi = lk.astype(bf16)
    lo = (lk - hi.astype(f32)).astype(bf16)
    r = _dot(jnp.concatenate([hi, lo], axis=1), uu)
    return r[:, 0:LANES], r[:, LANES:2 * LANES]


def _sb_prompt_body(q_ref, uu_ref, k_ref, v_ref, o_ref, carry_scr, acc_scr, *, tq, tk):
    i = pl.program_id(2)
    t0 = i * tq
    qf = q_ref[...].astype(f32)
    lane = lax.broadcasted_iota(jnp.int32, (tq, LANES), 1)
    q2 = jnp.concatenate([jnp.where(lane // HEAD_DIM == e, qf, 0.0) for e in range(2)], axis=0).astype(bf16)
    t_row = t0 + lax.broadcasted_iota(jnp.int32, (2 * tq, 1), 0) % tq
    uu = uu_ref[...]
    carry_scr[...] = jnp.zeros(carry_scr.shape, f32)
    acc_scr[...] = jnp.zeros(acc_scr.shape, f32)

    def step(kt, diag):
        k0 = pl.multiple_of(kt * tk, tk)
        z = _dot(q2, k_ref[:, pl.ds(k0, tk)])
        lk = _log_keep(z)
        if diag:
            is_past = (k0 + lax.broadcasted_iota(jnp.int32, (1, tk), 1)) < t_row
            lk = jnp.where(is_past, lk, 0.0)
        carry = carry_scr[...]
        between = [None] * (tk // LANES)
        for c in reversed(range(tk // LANES)):
            later, total = _suffix_sums(lk[:, LANES * c:LANES * (c + 1)], uu)
            between[c] = later + carry
            carry = carry + total
        w = jnp.exp(z + lk + jnp.concatenate(between, axis=1))
        if diag:
            w = jnp.where(is_past, w, 0.0)
        acc_scr[...] = acc_scr[...] + _dot(w.astype(bf16), v_ref[pl.ds(k0, tk), :])
        carry_scr[...] = carry

    kd = t0 // tk
    step(kd, True)

    def more(state):
        j, top = state
        return (j < kd) & (top > SB_CUTOFF)

    def walk(state):
        j, _ = state
        step(kd - 1 - j, False)
        return j + 1, jnp.max(carry_scr[...])

    lax.while_loop(more, walk, (0, jnp.max(carry_scr[...])))
    o = acc_scr[...]
    o_ref[...] = jnp.where(lane < HEAD_DIM, o[0:tq], o[tq:2 * tq]).astype(o_ref.dtype)


def _sb_prompt(sb_q, uu, sb_bf, tq, tk):
    b, l, _ = sb_q.shape
    nch = SB_HEADS // 2
    k_t = jnp.swapaxes(jnp.swapaxes(sb_bf[:, :, 0:nch * LANES].reshape(b, l, nch, LANES), 1, 3), 1, 2)
    return pl.pallas_call(
        functools.partial(_sb_prompt_body, tq=tq, tk=tk),
        out_shape=jax.ShapeDtypeStruct((b, l, SB_HEADS * HEAD_DIM), bf16),
        grid=(b, nch, l // tq),
        in_specs=[pl.BlockSpec((None, tq, LANES), lambda bi, c, i: (bi, i, c)),
                  _const_spec(uu.shape),
                  pl.BlockSpec((None, None, LANES, l), lambda bi, c, i: (bi, c, 0, 0)),
                  pl.BlockSpec((None, l, LANES), lambda bi, c, i: (bi, 0, nch + c))],
        out_specs=pl.BlockSpec((None, tq, LANES), lambda bi, c, i: (bi, i, c)),
        scratch_shapes=[pltpu.VMEM((2 * tq, LANES), f32), pltpu.VMEM((2 * tq, LANES), f32)],
        compiler_params=_cparams(("parallel", "parallel", "parallel")),
        name="stickbreak_attention_prompt",
    )(sb_q, uu, k_t, sb_bf)


def _merge_body(oa_ref, ob_ref, oc_ref, gm_ref, x_ref, gate_ref, wa_ref, wb_ref, wc_ref, wo_ref, o_ref):
    d = x_ref.shape[1]
    g = jax.nn.sigmoid(gm_ref[...])
    merged = (g[:, 0:d] * _dot(oa_ref[...], wa_ref[...]) + g[:, d:2 * d] * _dot(ob_ref[...], wb_ref[...])
              + g[:, 2 * d:3 * d] * _dot(oc_ref[...], wc_ref[...]))
    o_ref[...] = x_ref[...] + gate_ref[...] * _dot(merged.astype(bf16), wo_ref[...])


def _merge_out(o_a, o_b, o_c, g_m, x, gate, wa, wb, wc, wo, tm):
    r, d = x.shape
    nb, rb, _ = gate.shape
    tiles_per_b = (r // nb) // tm
    row = lambda w: pl.BlockSpec((tm, w), lambda i: (i, 0))
    return pl.pallas_call(
        _merge_body,
        out_shape=jax.ShapeDtypeStruct((r, d), f32),
        grid=(r // tm,),
        in_specs=[row(o_a.shape[1]), row(o_b.shape[1]), row(o_c.shape[1]), row(3 * d), row(d),
                  pl.BlockSpec((None, rb, d), lambda i: (i // tiles_per_b, 0, 0)),
                  _const_spec(wa.shape), _const_spec(wb.shape), _const_spec(wc.shape), _const_spec(wo.shape)],
        out_specs=row(d),
        compiler_params=_cparams(("parallel",)),
        name="merge_out_proj",
    )(o_a, o_b, o_c, g_m, x, gate, wa, wb, wc, wo)


def _ffn_tail(a, a_m1, a_m2, b, cw_ref, cb_ref, wd_ref, x_ref, gate_ref, o_ref):
    conv = cb_ref[...] + a_m2 * cw_ref[0:1, :]
    conv = conv + a_m1 * cw_ref[1:2, :]
    conv = conv + a * cw_ref[2:3, :]
    y = _dot((_gelu_tanh(conv) * b).astype(bf16), wd_ref[...])
    o_ref[...] = x_ref[...] + gate_ref[...] * y


def _ffn_seq_body(a_ref, halo_ref, b_ref, cw_ref, cb_ref, wd_ref, x_ref, gate_ref, o_ref, *, tiles_per_b):
    a = a_ref[...]
    first = pl.program_id(0) % tiles_per_b == 0
    halo = jnp.where(first, 0.0, halo_ref[...])
    row = lax.broadcasted_iota(jnp.int32, a.shape, 0)
    a_m1 = jnp.where(row < 1, halo[7:8, :], pltpu.roll(a, 1, 0))
    a_m2 = jnp.where(row < 1, halo[6:7, :], jnp.where(row < 2, halo[7:8, :], pltpu.roll(a, 2, 0)))
    _ffn_tail(a, a_m1, a_m2, b_ref[...], cw_ref, cb_ref, wd_ref, x_ref, gate_ref, o_ref)


def _ffn_step_body(a_ref, am1_ref, am2_ref, b_ref, cw_ref, cb_ref, wd_ref, x_ref, gate_ref, o_ref):
    _ffn_tail(a_ref[...], am1_ref[...], am2_ref[...], b_ref[...], cw_ref, cb_ref, wd_ref, x_ref, gate_ref, o_ref)


def _ffn_down_seq(u_a, u_b, cw8, cb, wd, x, gate, tm):
    r, d = x.shape
    ff = u_a.shape[1]
    nb = gate.shape[0]
    tiles_per_b = (r // nb) // tm
    row = lambda w: pl.BlockSpec((tm, w), lambda i: (i, 0))
    return pl.pallas_call(
        functools.partial(_ffn_seq_body, tiles_per_b=tiles_per_b),
        out_shape=jax.ShapeDtypeStruct((r, d), f32),
        grid=(r // tm,),
        in_specs=[row(ff), pl.BlockSpec((8, ff), lambda i: (jnp.maximum(i * (tm // 8) - 1, 0), 0)), row(ff),
                  _const_spec(cw8.shape), _const_spec(cb.shape), _const_spec(wd.shape), row(d),
                  pl.BlockSpec((None, 1, d), lambda i: (i // tiles_per_b, 0, 0))],
        out_specs=row(d),
        compiler_params=_cparams(("parallel",)),
        name="conv_ffn_down_seq",
    )(u_a, u_a, u_b, cw8, cb, wd, x, gate)


def _ffn_down_step(u_a, a_m1, a_m2, u_b, cw8, cb, wd, x, gate):
    r, d = x.shape
    full = lambda a: pl.BlockSpec(a.shape, lambda i: (0,) * a.ndim)
    return pl.pallas_call(
        _ffn_step_body,
        out_shape=jax.ShapeDtypeStruct((r, d), f32),
        grid=(1,),
        in_specs=[full(u_a), full(a_m1), full(a_m2), full(u_b), full(cw8), full(cb), full(wd), full(x),
                  pl.BlockSpec((None, r, d), lambda i: (0, 0, 0))],
        out_specs=full(x),
        compiler_params=_cparams(("arbitrary",)),
        name="conv_ffn_down_step",
    )(u_a, a_m1, a_m2, u_b, cw8, cb, wd, x, gate)


def _final_norm_body(x_ref, g_ref, o_ref):
    x = x_ref[...]
    o_ref[...] = x * lax.rsqrt(jnp.mean(x * x, axis=-1, keepdims=True) + RMS_EPS) * g_ref[...]


def _final_norm(x, g, tm):
    r, d = x.shape
    return pl.pallas_call(
        _final_norm_body,
        out_shape=jax.ShapeDtypeStruct((r, d), f32),
        grid=(r // tm,),
        in_specs=[pl.BlockSpec((tm, d), lambda i: (i, 0)), _const_spec((1, d))],
        out_specs=pl.BlockSpec((tm, d), lambda i: (i, 0)),
        compiler_params=_cparams(("parallel",)),
        name="final_rmsnorm",
    )(x, g.reshape(1, d))


def _page_specs(n_pages, layer_base):
    return [pl.BlockSpec((None, 512, PAGE_SIZE),
                         functools.partial(lambda s, pt, j: (layer_base + pt[s * n_pages + j], 0, 0), j=j))
            for j in range(n_pages)]


def _per_seq(width):
    return pl.BlockSpec((None, 1, width), lambda s, pt: (s, 0, 0))


def _dec_const(shape):
    nd = len(shape)
    return pl.BlockSpec(shape, lambda s, pt: (0,) * nd, pipeline_mode=pl.Buffered(1))


def _head_rows(q_row):
    row = lax.broadcasted_iota(jnp.int32, (8, LANES), 0)
    lane = lax.broadcasted_iota(jnp.int32, (8, LANES), 1)
    q8 = jnp.broadcast_to(q_row, (8, q_row.shape[1]))
    qsel = jnp.zeros((8, LANES), f32)
    for c in range(NSA_HEADS // 2):
        qsel = qsel + jnp.where(row // 2 == c, q8[:, LANES * c:LANES * (c + 1)], 0.0)
    swap = (row % 2) != (row // HPG)
    qm = jnp.where(swap, pltpu.roll(qsel, HEAD_DIM, 1), qsel)
    return jnp.where(lane // HEAD_DIM == row // HPG, qm, 0.0), swap


def _rank_select(score_row, k):
    a = jnp.broadcast_to(score_row, (LANES, LANES))
    b = a.T
    ii = lax.broadcasted_iota(jnp.int32, (LANES, LANES), 0)
    jj = lax.broadcasted_iota(jnp.int32, (LANES, LANES), 1)
    ahead = (b > a) | ((b == a) & (ii < jj))
    rank = jnp.sum(jnp.where(ahead, 1.0, 0.0), axis=0, keepdims=True)
    return jnp.where(rank < k, 1.0, 0.0)


def _nsa_dec_body(pt_ref, q_ref, gn_ref, new_ref, wnew_ref, sw_ref, pe_ref, w1_ref, w2_ref, agg_ref, e_ref,
                  *rest, n_pages):
    pages, o_ref = rest[:n_pages], rest[n_pages]
    kc_scr, vc_scr = rest[n_pages + 1:]
    past = n_pages * PAGE_SIZE
    for j in range(n_pages):
        kc_scr[PAGE_SIZE * j:PAGE_SIZE * (j + 1), :] = pages[j][0:LANES, :].T
        vc_scr[PAGE_SIZE * j:PAGE_SIZE * (j + 1), :] = pages[j][LANES:2 * LANES, :].T

    nc = past // CMP_STRIDE
    n_cmp = (past + 1 - CMP_LEN) // CMP_STRIDE + 1
    cmpv = _cmp_core((kc_scr, vc_scr), pe_ref, w1_ref, w2_ref, nc)

    row = lax.broadcasted_iota(jnp.int32, (8, LANES), 0)
    lane = lax.broadcasted_iota(jnp.int32, (8, LANES), 1)
    qm, swap = _head_rows(q_ref[...])
    qmb = qm.astype(bf16)
    slope = jnp.zeros((8, 1), f32)
    row1 = lax.broadcasted_iota(jnp.int32, (8, 1), 0)
    for h in range(NSA_HEADS):
        slope = jnp.where(row1 == h, NSA_SLOPES[h], slope)
    grp0 = row < HPG

    d_c = past - (lane[0:1] * CMP_STRIDE + CMP_LEN - 1)
    s_c = _nt(qmb, cmpv[:, 0:LANES].astype(bf16)) - slope * d_c.astype(f32)
    p_c = _ref_softmax(s_c, (d_c >= 0) & (lane[0:1] < n_cmp))
    o_c = _dot(p_c.astype(bf16), cmpv[:, LANES:2 * LANES].astype(bf16))
    aggb = agg_ref[...].astype(bf16)
    p_hi = p_c.astype(bf16)
    p_lo = (p_c - p_hi.astype(f32)).astype(bf16)
    imp_rows = _dot(p_hi, aggb) + _dot(p_lo, aggb)

    cur = past // SEL_BLOCK
    blk = lane[0:1]
    forced = (blk == 0) | (blk == cur) | (blk == cur - 1)
    causal = blk <= cur
    notsel_g = []
    for g in range(NSA_KV):
        imp = jnp.sum(imp_rows[HPG * g:HPG * (g + 1)], axis=0, keepdims=True)
        score = jnp.where(causal, jnp.where(forced, FORCE, imp), NEG)
        sel = jnp.where(causal, _rank_select(score, min(SEL_TOPN, -(-(past + 1) // SEL_BLOCK))), 0.0)
        notsel_g.append(jnp.broadcast_to(1.0 - sel, (8, LANES)))
    notsel = jnp.where(grp0, notsel_g[0], notsel_g[1])

    new_row = new_ref[...]
    dist = past - lax.broadcasted_iota(jnp.int32, (1, past), 1)
    picked = _dot(notsel.astype(bf16), e_ref[...]) < 0.5
    ks_t = jnp.concatenate([pages[j][2 * LANES:3 * LANES, :].astype(bf16) for j in range(n_pages)], axis=1)
    vs_t = jnp.concatenate([pages[j][3 * LANES:4 * LANES, :].astype(bf16) for j in range(n_pages)], axis=1)
    s_s = jnp.where(picked, _dot(qmb, ks_t) - slope * dist.astype(f32), NEG)
    s_n = jnp.sum(qm * new_row[:, 2 * LANES:3 * LANES], axis=-1, keepdims=True)
    m_s = jnp.maximum(jnp.max(s_s, axis=-1, keepdims=True), s_n)
    e_s = jnp.where(picked, jnp.exp(s_s - m_s), 0.0)
    e_n = jnp.exp(s_n - m_s)
    den = jnp.maximum(jnp.sum(e_s, axis=-1, keepdims=True) + e_n, 1e-30)
    o_s = (e_n * new_row[:, 3 * LANES:4 * LANES] + _nt(e_s.astype(bf16), vs_t)) / den

    wb = sw_ref.shape[1]
    d_w = wb - lax.broadcasted_iota(jnp.int32, (1, wb), 1)
    s_w = _dot(qmb, sw_ref[0:LANES, :].astype(bf16)) - slope * d_w.astype(f32)
    valid_w = (d_w < WINDOW) & (d_w >= 0)
    s_w = jnp.where(valid_w, s_w, NEG)
    wnew = wnew_ref[...]
    s_n = jnp.sum(qm * wnew[:, 0:LANES], axis=-1, keepdims=True)
    m_w = jnp.maximum(jnp.max(s_w, axis=-1, keepdims=True), s_n)
    e_w = jnp.where(valid_w, jnp.exp(s_w - m_w), 0.0)
    e_n = jnp.exp(s_n - m_w)
    den = jnp.maximum(jnp.sum(e_w, axis=-1, keepdims=True) + e_n, 1e-30)
    o_w = (_nt(e_w.astype(bf16), sw_ref[LANES:2 * LANES, :].astype(bf16)) + e_n * wnew[:, LANES:2 * LANES]) / den

    sig = jnp.broadcast_to(jax.nn.sigmoid(gn_ref[...]), (8, LANES))
    gates = [jnp.sum(jnp.where(lane == br * NSA_HEADS + row, sig, 0.0), axis=-1, keepdims=True) for br in range(3)]
    o = gates[0] * o_c + gates[1] * o_s + gates[2] * o_w
    o = jnp.where(swap, pltpu.roll(o, HEAD_DIM, 1), o)
    o = jnp.where(lane // HEAD_DIM == row % 2, o, 0.0)
    o_ref[...] = jnp.concatenate([o[2 * c:2 * c + 1] + o[2 * c + 1:2 * c + 2] for c in range(NSA_HEADS // 2)], axis=1)


def _nsa_decode(page_table, cache_t, layer_base, q, g_n, nsa_new, win_new, state_win_t, win_base, pe4, w1bd, w2bd,
                agg, e_sel):
    s, n_pages = page_table.shape
    past = n_pages * PAGE_SIZE
    consts = (pe4, w1bd, w2bd, agg, e_sel)
    grid_spec = pltpu.PrefetchScalarGridSpec(
        num_scalar_prefetch=1,
        grid=(s,),
        in_specs=[_per_seq(512), _per_seq(LANES), _per_seq(512), _per_seq(256),
                  pl.BlockSpec((None,) + state_win_t.shape[1:], lambda si, pt: (win_base + si, 0, 0))]
        + [_dec_const(c.shape) for c in consts] + _page_specs(n_pages, layer_base),
        out_specs=_per_seq(512),
        scratch_shapes=[pltpu.VMEM((past, LANES), f32), pltpu.VMEM((past, LANES), f32)],
    )
    return pl.pallas_call(
        functools.partial(_nsa_dec_body, n_pages=n_pages),
        out_shape=jax.ShapeDtypeStruct((s, 1, 512), f32),
        grid_spec=grid_spec,
        compiler_params=_cparams(("parallel",)),
        name="nsa_attention_decode",
    )(page_table.reshape(-1), q, g_n, nsa_new, win_new, state_win_t, *consts, *([cache_t] * n_pages))


def _win_shift_body(sw_ref, new_ref, o_ref):
    n, feat, wb = sw_ref.shape
    lane = lax.broadcasted_iota(jnp.int32, (feat, wb), 1)
    for k in range(n):
        col = jnp.broadcast_to(new_ref[k], (LANES, feat)).T
        col = jnp.concatenate([col] * (wb // LANES), axis=1)
        o_ref[k] = jnp.where(lane == wb - 1, col, pltpu.roll(sw_ref[k], wb - 1, 1))


def _win_shift(state_win_t, new_rows, per_step):
    n, feat, wb = state_win_t.shape
    return pl.pallas_call(
        _win_shift_body,
        out_shape=jax.ShapeDtypeStruct((n, feat, wb), f32),
        grid=(n // per_step,),
        in_specs=[pl.BlockSpec((per_step, feat, wb), lambda i: (i, 0, 0)),
                  pl.BlockSpec((per_step, 1, feat), lambda i: (i, 0, 0))],
        out_specs=pl.BlockSpec((per_step, feat, wb), lambda i: (i, 0, 0)),
        compiler_params=_cparams(("parallel",)),
        name="window_state_shift",
    )(state_win_t, new_rows)


def _q_head_rows4(q_row):
    row = lax.broadcasted_iota(jnp.int32, (8, 256), 0)
    lane = lax.broadcasted_iota(jnp.int32, (8, 256), 1)
    own = lane // HEAD_DIM == row
    return jnp.where(own, jnp.broadcast_to(q_row, (8, 256)), 0.0), own


def _moba_dec_body(pt_ref, q_ref, new_ref, seg_ref, e_ref, *rest, n_pages):
    pages, o_ref = rest[:n_pages], rest[n_pages]
    past = n_pages * PAGE_SIZE
    nb_past = past // MOBA_BLOCK
    kw = MOBA_HEADS * HEAD_DIM
    qm, own_lanes = _q_head_rows4(q_ref[...])
    qmb = qm.astype(bf16)
    k_t = jnp.concatenate([pages[j][0:kw, :].astype(bf16) for j in range(n_pages)], axis=1)
    v_t = jnp.concatenate([pages[j][kw:2 * kw, :].astype(bf16) for j in range(n_pages)], axis=1)
    raw = _dot(qmb, k_t)
    raw_hi = raw.astype(bf16)
    raw_lo = (raw - raw_hi.astype(f32)).astype(bf16)
    gate = _dot(raw_hi, seg_ref[...]) + _dot(raw_lo, seg_ref[...])

    lane = lax.broadcasted_iota(jnp.int32, (8, LANES), 1)
    is_past = lane < nb_past
    score = jnp.where(is_past, gate, NEG)
    rank = jnp.zeros((8, LANES), f32)
    for i in range(nb_past):
        gi = score[:, i:i + 1]
        rank = rank + jnp.where((gi > score) | ((gi == score) & (i < lane)), 1.0, 0.0)
    sel = jnp.where(is_past & (rank < min(MOBA_TOPK, nb_past)), 1.0, 0.0)
    picked = _dot(sel.astype(bf16), e_ref[...]) > 0.5

    row1 = lax.broadcasted_iota(jnp.int32, (8, 1), 0)
    slope = jnp.zeros((8, 1), f32)
    for h in range(MOBA_HEADS):
        slope = jnp.where(row1 == h, MOBA_SLOPES[h], slope)
    dist = past - lax.broadcasted_iota(jnp.int32, (1, past), 1)
    new_row = new_ref[...]
    s = jnp.where(picked, raw - slope * dist.astype(f32), NEG)
    s_n = jnp.sum(qm * new_row[:, 0:kw], axis=-1, keepdims=True)
    m = jnp.maximum(jnp.max(s, axis=-1, keepdims=True), s_n)
    e = jnp.where(picked, jnp.exp(s - m), 0.0)
    e_n = jnp.exp(s_n - m)
    den = jnp.maximum(jnp.sum(e, axis=-1, keepdims=True) + e_n, 1e-30)
    o = (e_n * new_row[:, kw:2 * kw] + _nt(e.astype(bf16), v_t)) / den
    o_ref[...] = jnp.sum(jnp.where(own_lanes, o, 0.0), axis=0, keepdims=True)


def _moba_decode(page_table, cache_t, layer_base, q, moba_new, seg_mean, e_blk):
    s, n_pages = page_table.shape
    grid_spec = pltpu.PrefetchScalarGridSpec(
        num_scalar_prefetch=1,
        grid=(s,),
        in_specs=[_per_seq(256), _per_seq(512), _dec_const(seg_mean.shape), _dec_const(e_blk.shape)]
        + _page_specs(n_pages, layer_base),
        out_specs=_per_seq(256),
    )
    return pl.pallas_call(
        functools.partial(_moba_dec_body, n_pages=n_pages),
        out_shape=jax.ShapeDtypeStruct((s, 1, 256), f32),
        grid_spec=grid_spec,
        compiler_params=_cparams(("parallel",)),
        name="moba_attention_decode",
    )(page_table.reshape(-1), q, moba_new, seg_mean, e_blk, *([cache_t] * n_pages))


def _sb_dec_body(pt_ref, q_ref, uu_ref, *rest, n_pages):
    pages, o_ref = rest[:n_pages], rest[n_pages]
    kw = SB_HEADS * HEAD_DIM
    qm, own_lanes = _q_head_rows4(q_ref[...])
    qmb = qm.astype(bf16)
    k_t = jnp.concatenate([pages[j][0:kw, :].astype(bf16) for j in range(n_pages)], axis=1)
    v_t = jnp.concatenate([pages[j][kw:2 * kw, :].astype(bf16) for j in range(n_pages)], axis=1)
    z = _dot(qmb, k_t)
    lk = _log_keep(z)
    stacked = jnp.concatenate([lk[:, PAGE_SIZE * j:PAGE_SIZE * (j + 1)] for j in range(n_pages)], axis=0)
    later, total = _suffix_sums(stacked, uu_ref[...])
    carry = jnp.zeros((8, LANES), f32)
    between = [None] * n_pages
    for j in reversed(range(n_pages)):
        between[j] = later[8 * j:8 * (j + 1)] + carry
        carry = carry + total[8 * j:8 * (j + 1)]
    w = jnp.exp(z + lk + jnp.concatenate(between, axis=1))
    acc = _nt(w.astype(bf16), v_t)
    o_ref[...] = jnp.sum(jnp.where(own_lanes, acc, 0.0), axis=0, keepdims=True)


def _sb_decode(page_table, cache_t, layer_base, q, uu):
    s, n_pages = page_table.shape
    grid_spec = pltpu.PrefetchScalarGridSpec(
        num_scalar_prefetch=1,
        grid=(s,),
        in_specs=[_per_seq(256), _dec_const(uu.shape)] + _page_specs(n_pages, layer_base),
        out_specs=_per_seq(256),
    )
    return pl.pallas_call(
        functools.partial(_sb_dec_body, n_pages=n_pages),
        out_shape=jax.ShapeDtypeStruct((s, 1, 256), f32),
        grid_spec=grid_spec,
        compiler_params=_cparams(("parallel",)),
        name="stickbreak_attention_decode",
    )(page_table.reshape(-1), q, uu, *([cache_t] * n_pages))


def _agg_matrix(nc, n_cmp):
    c0 = np.arange(nc)[:, None] * CMP_STRIDE
    s0 = np.arange(LANES)[None, :] * SEL_BLOCK
    ov = np.clip(np.minimum(c0 + CMP_LEN, s0 + SEL_BLOCK) - np.maximum(c0, s0), 0, None) / CMP_LEN
    ov[n_cmp:] = 0.0
    return jnp.asarray(ov, f32)


def _block_onehot_t(block, l):
    e = (np.arange(l)[None, :] // block) == np.arange(LANES)[:, None]
    return jnp.asarray(np.where(e, -MASK_BIG, 0.0), bf16)


def _expand_matrix(block, kp):
    e = (np.arange(kp)[None, :] // block) == np.arange(LANES)[:, None]
    return jnp.asarray(e, bf16)


def _suffix_matrix():
    j = np.arange(2 * LANES)[:, None] % LANES
    s = np.arange(2 * LANES)[None, :]
    return jnp.asarray((s >= LANES) | (j > s), bf16)


def _layer_weights(l, w_in, cmp_pe, cmp_w1, cmp_w2, w_br_a, w_br_b, w_br_c, w_o, w_up, conv_w, conv_b, w_down):
    d = w_in.shape[1]
    w = w_in[l]
    w_proj = jnp.concatenate([w[:, 0:1280], w[:, 1304:], w[:, 1280:1304], jnp.zeros((d, _PROJ_W - 5912), f32)],
                             axis=1).astype(bf16)
    pe4 = jnp.concatenate([cmp_pe[l], cmp_pe[l]], axis=2)
    w1 = cmp_w1[l].reshape(2, CMP_LEN, HEAD_DIM, CMP_HID)
    w1bd = jnp.zeros((2, CMP_LEN, NSA_KV * HEAD_DIM, NSA_KV * CMP_HID), f32)
    w2bd = jnp.zeros((2, NSA_KV * CMP_HID, NSA_KV * HEAD_DIM), f32)
    for g in range(NSA_KV):
        w1bd = w1bd.at[:, :, HEAD_DIM * g:HEAD_DIM * (g + 1), CMP_HID * g:CMP_HID * (g + 1)].set(w1)
        w2bd = w2bd.at[:, CMP_HID * g:CMP_HID * (g + 1), HEAD_DIM * g:HEAD_DIM * (g + 1)].set(cmp_w2[l])
    cw8 = jnp.concatenate([conv_w[l], jnp.zeros((8 - CONV_W, conv_w.shape[2]), f32)], axis=0)
    return dict(w_proj=w_proj, pe4=pe4, w1bd=w1bd.astype(bf16), w2bd=w2bd.astype(bf16),
                wa=w_br_a[l].astype(bf16), wb=w_br_b[l].astype(bf16), wc=w_br_c[l].astype(bf16),
                wo=w_o[l].astype(bf16), w_up=w_up[l].astype(bf16), cw8=cw8, cb=conv_b[l].reshape(1, -1),
                wd=w_down[l].astype(bf16))


def _mod_parts(mod_rows, per_row):
    r = mod_rows.shape[0]
    parts = mod_rows.reshape(r, 6, D_MODEL)
    return [parts[:, k].reshape((1, r, D_MODEL) if per_row else (r, 1, D_MODEL)) for k in range(6)]


def kernel(x_prompt, x_sample, cache_nsa, cache_moba, cache_sb, state_win, state_conv, page_table, c_prompt,
           c_sample, norm1_g, norm2_g, w_ada, b_ada, w_in, cmp_pe, cmp_w1, cmp_w2, w_br_a, w_br_b, w_br_c, w_o,
           w_up, conv_w, conv_b, w_down, final_g):
    b, t, d = x_prompt.shape
    s = x_sample.shape[0]
    depth = w_in.shape[0]
    n_phys = cache_nsa.shape[1]
    n_pages = page_table.shape[1]
    past = n_pages * PAGE_SIZE
    tm = 256
    tq = 128
    tk = 512

    n_c = b + s
    c_all = jnp.concatenate([c_prompt, c_sample, jnp.zeros((-n_c % 8, d), f32)], axis=0)
    mod = _ada_mod(c_all, w_ada, b_ada)

    nc_p = t // CMP_STRIDE
    n_cmp_p = (t - CMP_LEN) // CMP_STRIDE + 1
    agg_p = _agg_matrix(nc_p, n_cmp_p)
    nc_s = past // CMP_STRIDE
    agg_s = _agg_matrix(nc_s, (past + 1 - CMP_LEN) // CMP_STRIDE + 1)
    e_sel = _expand_matrix(SEL_BLOCK, past)
    e_blk = _expand_matrix(MOBA_BLOCK, past)
    seg_mean = (jnp.swapaxes(e_blk, 0, 1).astype(f32) * (1.0 / MOBA_BLOCK)).astype(bf16)
    uu = _suffix_matrix()
    nb_pad = LANES // MOBA_HEADS

    feat_major = lambda a: jnp.transpose(a, (0, 1, 3, 4, 5, 2))
    caches = [feat_major(c).reshape(depth * n_phys, 512, PAGE_SIZE) for c in (cache_nsa, cache_moba, cache_sb)]
    state_win_t = feat_major(state_win).reshape(depth * s, 256, state_win.shape[2])

    xp = x_prompt.reshape(b * t, d)
    xs = x_sample.reshape(s, d)
    outs_p = [[] for _ in range(5)]
    outs_s = [[] for _ in range(5)]
    for l in range(depth):
        w = _layer_weights(l, w_in, cmp_pe, cmp_w1, cmp_w2, w_br_a, w_br_b, w_br_c, w_o, w_up, conv_w, conv_b,
                           w_down)
        mp = _mod_parts(mod[l, 0:b], per_row=False)
        ms = _mod_parts(mod[l, b:b + s], per_row=True)

        (q_n, nsa_rows, nsa_bf, win_rows, win_bf, mb_q, moba_rows, moba_bf, sb_q, sb_rows, sb_bf, g_m, g_n) = \
            _norm_mod_matmul(xp, norm1_g[l], mp[0], mp[1], w["w_proj"], _IN_SEGS, _IN_DTYPES, tm, "in_proj_prompt")
        r3 = lambda a: a.reshape(b, t, a.shape[1])
        cmp = _cmp_prompt(r3(nsa_rows), w["pe4"], w["w1bd"], w["w2bd"])
        o_a = _nsa_prompt(r3(q_n), r3(g_n), cmp, agg_p, r3(nsa_bf), r3(win_bf), n_cmp_p, tq, tk)
        kmean = _block_mean(r3(moba_rows), 256, MOBA_BLOCK).reshape(b, t // MOBA_BLOCK, MOBA_HEADS, HEAD_DIM)
        kmt = jnp.zeros((b, MOBA_HEADS, nb_pad, MOBA_HEADS, HEAD_DIM), f32)
        for h in range(MOBA_HEADS):
            kmt = kmt.at[:, h, 0:t // MOBA_BLOCK, h].set(kmean[:, :, h])
        o_b = _moba_prompt(r3(mb_q), kmt.reshape(b, LANES, 256), r3(moba_bf), tq, tk)
        o_c = _sb_prompt(r3(sb_q), uu, r3(sb_bf), tq, tk)
        xp = _merge_out(o_a.reshape(b * t, -1), o_b.reshape(b * t, -1), o_c.reshape(b * t, -1), g_m, xp, mp[2],
                        w["wa"], w["wb"], w["wc"], w["wo"], tm)
        u_a, u_b = _norm_mod_matmul(xp, norm2_g[l], mp[3], mp[4], w["w_up"], _UP_SEGS, _UP_DTYPES, tm,
                                    "ffn_up_prompt")
        xp = _ffn_down_seq(u_a, u_b, w["cw8"], w["cb"], w["wd"], xp, mp[5], tm)
        keep = min(WINDOW, t)
        outs_p[0].append(nsa_rows.reshape(b, t // PAGE_SIZE, PAGE_SIZE, 4, NSA_KV, HEAD_DIM))
        outs_p[1].append(moba_rows.reshape(b, t // PAGE_SIZE, PAGE_SIZE, 2, MOBA_HEADS, HEAD_DIM))
        outs_p[2].append(sb_rows.reshape(b, t // PAGE_SIZE, PAGE_SIZE, 2, SB_HEADS, HEAD_DIM))
        outs_p[3].append(r3(win_rows)[:, t - keep:].reshape(b, keep, 2, NSA_KV, HEAD_DIM))
        outs_p[4].append(r3(u_a)[:, t - (CONV_W - 1):])

        (q_n, nsa_rows, _, win_rows, _, mb_q, moba_rows, _, sb_q, sb_rows, _, g_m, g_n) = \
            _norm_mod_matmul(xs, norm1_g[l], ms[0], ms[1], w["w_proj"], _IN_SEGS, _IN_DTYPES, s, "in_proj_sample")
        s3 = lambda a: a.astype(f32).reshape(s, 1, a.shape[1])
        base = l * n_phys
        o_a = _nsa_decode(page_table, caches[0], base, s3(q_n), s3(g_n), s3(nsa_rows), s3(win_rows), state_win_t,
                          l * s, w["pe4"], w["w1bd"], w["w2bd"], agg_s, e_sel)
        o_b = _moba_decode(page_table, caches[1], base, s3(mb_q), s3(moba_rows), seg_mean, e_blk)
        o_c = _sb_decode(page_table, caches[2], base, s3(sb_q), uu)
        xs = _merge_out(o_a.reshape(s, -1).astype(bf16), o_b.reshape(s, -1).astype(bf16),
                        o_c.reshape(s, -1).astype(bf16), g_m, xs, ms[2], w["wa"], w["wb"], w["wc"], w["wo"], s)
        u_a, u_b = _norm_mod_matmul(xs, norm2_g[l], ms[3], ms[4], w["w_up"], _UP_SEGS, _UP_DTYPES, s,
                                    "ffn_up_sample")
        xs = _ffn_down_step(u_a, state_conv[l, :, 1], state_conv[l, :, 0], u_b, w["cw8"], w["cb"], w["wd"], xs,
                            ms[5])
        conv_full = jnp.concatenate([state_conv[l], u_a.reshape(s, 1, -1)], axis=1)
        outs_s[0].append(nsa_rows.reshape(s, 1, 4, NSA_KV, HEAD_DIM))
        outs_s[1].append(moba_rows.reshape(s, 1, 2, MOBA_HEADS, HEAD_DIM))
        outs_s[2].append(sb_rows.reshape(s, 1, 2, SB_HEADS, HEAD_DIM))
        outs_s[3].append(win_rows.reshape(s, 1, 256))
        outs_s[4].append(conv_full[:, conv_full.shape[1] - (CONV_W - 1):])

    y_prompt = _final_norm(xp, final_g, tm).reshape(b, t, d)
    y_sample = _final_norm(xs, final_g, s).reshape(s, 1, d)
    wb = state_win.shape[2]
    win_s = _win_shift(state_win_t, jnp.concatenate(outs_s[3], axis=0), 4)
    win_s = jnp.transpose(win_s.reshape(depth, s, 2, NSA_KV, HEAD_DIM, wb), (0, 1, 5, 2, 3, 4))
    st = lambda lst: jnp.stack(lst)
    return (y_prompt, y_sample, st(outs_p[0]), st(outs_s[0]), st(outs_p[1]), st(outs_s[1]), st(outs_p[2]),
            st(outs_s[2]), st(outs_p[3]), win_s, st(outs_p[4]), st(outs_s[4]))
```

```python
import functools

import numpy as np
import jax
import jax.numpy as jnp
from jax import lax
from jax.experimental import pallas as pl
from jax.experimental.pallas import tpu as pltpu

f32 = jnp.float32
bf16 = jnp.bfloat16

D_MODEL = 1024
HEAD_DIM = 64
NSA_HEADS = 8
NSA_KV = 2
HPG = NSA_HEADS // NSA_KV
CMP_LEN = 32
CMP_STRIDE = 16
CMP_HID = 128
SEL_BLOCK = 64
SEL_TOPN = 16
WINDOW = 512
MOBA_HEADS = 4
MOBA_BLOCK = 256
MOBA_TOPK = 3
SB_HEADS = 4
D_FF = 2816
CONV_W = 3
PAGE_SIZE = 128
RMS_EPS = 1e-6
NEG = -1e30
FORCE = 1e9

LANES = 128
MASK_BIG = 2.0 ** 100
M_INIT = -1e29
REMOVED = -3e38
SB_CUTOFF = -110.0
VMEM_LIMIT_MB = 56

NSA_SLOPES = [2.0 ** (-8.0 * (h + 1) / NSA_HEADS) for h in range(NSA_HEADS)]
MOBA_SLOPES = [2.0 ** (-8.0 * (h + 1) / MOBA_HEADS) for h in range(MOBA_HEADS)]

_Q_N, _NSA, _WIN, _MB_Q, _MOBA, _SB_Q, _SB, _G_M, _G_N, _PROJ_W = 0, 512, 1024, 1280, 1536, 2048, 2304, 2816, 5888, 6016


def _cparams(sem, vmem_mb=VMEM_LIMIT_MB):
    return pltpu.CompilerParams(dimension_semantics=sem, vmem_limit_bytes=vmem_mb * 2 ** 20)


def _const_spec(shape):
    nd = len(shape)
    return pl.BlockSpec(shape, lambda *_: (0,) * nd, pipeline_mode=pl.Buffered(1))


def _nt(a, b):
    return lax.dot_general(a, b, (((1,), (1,)), ((), ())), preferred_element_type=f32)


def _dot(a, b):
    return jnp.dot(a, b, preferred_element_type=f32)


def _gelu_tanh(x):
    return x * (0.5 * (1.0 + jnp.tanh(np.sqrt(2.0 / np.pi) * (x + 0.044715 * (x * x * x)))))


def _ref_softmax(s, valid):
    s = jnp.where(valid, s, NEG)
    e = jnp.where(valid, jnp.exp(s - jnp.max(s, axis=-1, keepdims=True)), 0.0)
    return e / jnp.maximum(jnp.sum(e, axis=-1, keepdims=True), 1e-30)


def _ada_body(c_ref, w_ref, b_ref, o_ref):
    c = c_ref[...]
    s = c * jax.nn.sigmoid(c)
    o_ref[...] = _dot(s.astype(bf16), w_ref[...].astype(bf16)) + b_ref[...]


def _ada_mod(c_all, w_ada, b_ada):
    depth, d, n = w_ada.shape
    r = c_all.shape[0]
    tn = 1536
    return pl.pallas_call(
        _ada_body,
        out_shape=jax.ShapeDtypeStruct((depth, r, n), f32),
        grid=(depth, n // tn),
        in_specs=[pl.BlockSpec((r, d), lambda l, j: (0, 0)),
                  pl.BlockSpec((None, d, tn), lambda l, j: (l, 0, j)),
                  pl.BlockSpec((None, 1, tn), lambda l, j: (l, 0, j))],
        out_specs=pl.BlockSpec((None, r, tn), lambda l, j: (l, 0, j)),
        compiler_params=_cparams(("parallel", "parallel")),
        name="ada_mod",
    )(c_all, w_ada, b_ada.reshape(depth, 1, n))


def _nmm_body(x_ref, g_ref, sh_ref, sc_ref, w_ref, *o_refs, segs):
    x = x_ref[...]
    y = x * lax.rsqrt(jnp.mean(x * x, axis=-1, keepdims=True) + RMS_EPS)
    h = (y * g_ref[...]) * (1.0 + sc_ref[...]) + sh_ref[...]
    hb = h.astype(bf16)
    k = 0
    for off, width, scales in segs:
        outs = o_refs[k:k + len(scales)]
        k += len(scales)
        for c0 in range(0, width, 512):
            cw = min(512, width - c0)
            acc = _dot(hb, w_ref[:, off + c0:off + c0 + cw])
            for o, scale in zip(outs, scales):
                o[:, c0:c0 + cw] = (acc if scale == 1.0 else acc * scale).astype(o.dtype)


def _norm_mod_matmul(x, g, shift, scale, w_bf, segs, out_dtypes, tm, name):
    r, d = x.shape
    nb, rb, _ = shift.shape
    tiles_per_b = (r // nb) // tm
    out_shape, out_specs = [], []
    k = 0
    for off, width, scales in segs:
        for _ in scales:
            out_shape.append(jax.ShapeDtypeStruct((r, width), out_dtypes[k]))
            out_specs.append(pl.BlockSpec((tm, width), lambda i: (i, 0)))
            k += 1
    mod_spec = pl.BlockSpec((None, rb, d), lambda i: (i // tiles_per_b, 0, 0))
    return pl.pallas_call(
        functools.partial(_nmm_body, segs=segs),
        out_shape=out_shape,
        grid=(r // tm,),
        in_specs=[pl.BlockSpec((tm, d), lambda i: (i, 0)), _const_spec((1, d)), mod_spec, mod_spec,
                  _const_spec(w_bf.shape)],
        out_specs=out_specs,
        compiler_params=_cparams(("parallel",)),
        name=name,
    )(x, g.reshape(1, d), shift, scale, w_bf)


_IN_SEGS = ((_Q_N, 512, (0.125,)), (_NSA, 512, (1.0, 1.0)), (_WIN, 256, (1.0, 1.0)), (_MB_Q, 256, (0.125,)),
            (_MOBA, 512, (1.0, 1.0)), (_SB_Q, 256, (0.125,)), (_SB, 512, (1.0, 1.0)), (_G_M, 3072, (1.0,)),
            (_G_N, 128, (1.0,)))
_IN_DTYPES = (bf16, f32, bf16, f32, bf16, bf16, f32, bf16, bf16, f32, bf16, f32, f32)
_UP_SEGS = ((0, D_FF, (1.0,)), (D_FF, D_FF, (1.0,)))
_UP_DTYPES = (f32, f32)


def _cmp_core(rows_refs, pe_ref, w1_ref, w2_ref, nc):
    outs = []
    for kv in range(2):
        acc_a = jnp.zeros((nc, NSA_KV * CMP_HID), f32)
        acc_b = jnp.zeros((nc, NSA_KV * CMP_HID), f32)
        for r in range(CMP_STRIDE):
            y = rows_refs[kv][pl.ds(r, nc, stride=CMP_STRIDE), :]
            acc_a = acc_a + _dot((y + pe_ref[kv, r:r + 1, :]).astype(bf16), w1_ref[kv, r])
            acc_b = acc_b + _dot((y + pe_ref[kv, CMP_STRIDE + r:CMP_STRIDE + r + 1, :]).astype(bf16),
                                 w1_ref[kv, CMP_STRIDE + r])
        pre = acc_a + pltpu.roll(acc_b, nc - 1, 0)
        outs.append(_dot(_gelu_tanh(pre).astype(bf16), w2_ref[kv]))
    return jnp.concatenate(outs, axis=1)


def _cmp_prompt_body(k_ref, v_ref, pe_ref, w1_ref, w2_ref, o_ref, *, nc):
    o_ref[...] = _cmp_core((k_ref, v_ref), pe_ref, w1_ref, w2_ref, nc)


def _cmp_prompt(nsa_rows, pe2, w1bd, w2bd):
    b, l, _ = nsa_rows.shape
    nc = l // CMP_STRIDE
    return pl.pallas_call(
        functools.partial(_cmp_prompt_body, nc=nc),
        out_shape=jax.ShapeDtypeStruct((b, nc, 256), f32),
        grid=(b,),
        in_specs=[pl.BlockSpec((None, l, LANES), lambda i: (i, 0, 0)),
                  pl.BlockSpec((None, l, LANES), lambda i: (i, 0, 1)), _const_spec(pe2.shape),
                  _const_spec(w1bd.shape), _const_spec(w2bd.shape)],
        out_specs=pl.BlockSpec((None, nc, 256), lambda i: (i, 0, 0)),
        compiler_params=_cparams(("parallel",)),
        name="nsa_compress_prompt",
    )(nsa_rows, nsa_rows, pe2, w1bd, w2bd)


SLAB = 64
FLASH_SCRATCH = 2


class _Flash:
    def __init__(self, scratch, slopes, tq, t0):
        self.m, self.acc = scratch
        self.slope8 = jnp.concatenate(
            [jnp.full((1, 1), v, f32) for v in slopes] + [jnp.zeros((8 - len(slopes), 1), f32)], axis=0)
        self.tq, self.t0 = tq, t0
        self.m[...] = jnp.full(self.m.shape, M_INIT, f32)
        self.acc[...] = jnp.zeros(self.acc.shape, f32)

    def slab(self, j, s_all, bias, rel, masked):
        r0 = j * SLAB
        rows = slice(r0, r0 + SLAB)
        head = r0 // self.tq
        s = s_all[rows, :] + bias[head:head + 1, :]
        if masked:
            t_rel = (r0 + lax.broadcasted_iota(jnp.int32, (SLAB, 1), 0)) % self.tq
            s = jnp.where(rel <= t_rel, s, -MASK_BIG)
        m_old = self.m[rows, :]
        m_new = jnp.maximum(m_old, jnp.max(s, axis=-1, keepdims=True))
        self.m[rows, :] = m_new
        p = jnp.exp(s - jnp.concatenate([m_new] * (s.shape[1] // LANES), axis=1))
        return p.astype(bf16), jnp.exp(m_old - m_new)

    def result(self):
        acc = self.acc[...]
        return acc[:, 0:LANES] / acc[:, LANES:2 * LANES]


def _flash_tiles(chains, k0, masked):
    first = chains[0][0]
    rows = first.m.shape[0]
    tk = chains[0][2].shape[1]
    rel = k0 - first.t0 + lax.broadcasted_iota(jnp.int32, (1, tk), 1)
    logits = [(_dot(q_aug, kt_aug), f.slope8 * rel.astype(f32)) for f, q_aug, kt_aug, _ in chains]
    parts = [[] for _ in chains]
    for j in range(rows // SLAB):
        for c, (f, _, _, _) in enumerate(chains):
            parts[c].append(f.slab(j, logits[c][0], logits[c][1], rel, masked))
    for c, (f, _, _, v_aug) in enumerate(chains):
        p = jnp.concatenate([x[0] for x in parts[c]], axis=0)
        alpha = jnp.concatenate([x[1] for x in parts[c]], axis=0)
        f.acc[...] = jnp.concatenate([alpha, alpha], axis=1) * f.acc[...] + _dot(p, v_aug)


def _flash_scratch(rows):
    return [pltpu.VMEM((rows, LANES), f32), pltpu.VMEM((rows, 2 * LANES), f32)]


def _with_ones(v):
    return jnp.concatenate([v, jnp.ones(v.shape, v.dtype)], axis=-1)


def _masked_exp(s_all, slopes, rel, valid_fn, tq):
    out = []
    for j in range(s_all.shape[0] // SLAB):
        r0 = j * SLAB
        t_rel = r0 % tq + lax.broadcasted_iota(jnp.int32, (SLAB, 1), 0)
        valid = valid_fn(t_rel)
        s = jnp.where(valid, s_all[r0:r0 + SLAB, :] + slopes[r0 // tq] * rel.astype(f32), NEG)
        out.append(jnp.exp(s - jnp.maximum(jnp.max(s, axis=-1, keepdims=True), M_INIT)))
    return jnp.concatenate(out, axis=0)


def _topk_rows(score, ids, k, n_ids):
    picked = jnp.zeros(score.shape, f32)
    for _ in range(k):
        mx = jnp.max(score, axis=0, keepdims=True)
        idx = jnp.min(jnp.where(score == mx, ids, n_ids), axis=0, keepdims=True)
        pick = ids == idx
        picked = jnp.where(pick, 1.0, picked)
        score = jnp.where(pick, REMOVED, score)
    return picked


def _nsa_prompt_body(q_ref, gn_ref, kct_ref, vc_ref, agg_ref, kst_ref, vs_ref, kwt_ref, vw_ref, o_ref, sel_scr,
                     qa_scr, part_scr, gate_scr, *flash_scr, tq, tk, n_cmp):
    i = pl.program_id(1)
    t0 = i * tq
    nc = vc_ref.shape[0]
    qf = q_ref[...].astype(f32)
    lane = lax.broadcasted_iota(jnp.int32, (tq, LANES), 1)
    sig = jax.nn.sigmoid(gn_ref[...])
    kct = kct_ref[...]
    vc = vc_ref[...].astype(bf16)
    aggb = agg_ref[...].astype(bf16)
    n_id = lax.broadcasted_iota(jnp.int32, (1, nc), 1)
    cend_rel = n_id * CMP_STRIDE + (CMP_LEN - 1) - t0
    ones_c = jnp.ones((nc, LANES), bf16)
    blk_id = lax.broadcasted_iota(jnp.int32, (LANES, tq), 0)
    cur = (t0 + lax.broadcasted_iota(jnp.int32, (LANES, tq), 1)) // SEL_BLOCK
    causal_blk = blk_id <= cur
    forced = (blk_id == 0) | (blk_id == cur) | (blk_id == cur - 1)
    chunks = [jnp.zeros((tq, LANES), f32) for _ in range(NSA_HEADS // 2)]

    for g in range(NSA_KV):
        heads = [HPG * g + hh for hh in range(HPG)]
        pieces = []
        for h in heads:
            blk = qf[:, LANES * (h // 2):LANES * (h // 2 + 1)]
            if h % 2 != g:
                blk = pltpu.roll(blk, HEAD_DIM, 1)
            pieces.append(jnp.where(lane // HEAD_DIM == g, blk, 0.0))
        qg = jnp.concatenate(pieces, axis=0).astype(bf16)
        slopes = [NSA_SLOPES[h] for h in heads]

        valid_c = lambda t_rel: (cend_rel <= t_rel) & (n_id < n_cmp)
        e_c = _masked_exp(_dot(qg, kct), slopes, cend_rel, valid_c, tq)
        e_hi = e_c.astype(bf16)
        e_lo = (e_c - e_hi.astype(f32)).astype(bf16)
        r_hi = _dot(e_hi, jnp.concatenate([vc, ones_c, aggb], axis=1))
        r_lo = _dot(e_lo, jnp.concatenate([aggb, ones_c], axis=1))
        o_c = r_hi[:, 0:LANES] / jnp.maximum(r_hi[:, LANES:2 * LANES], 1e-30)
        imp_rows = (r_hi[:, 2 * LANES:3 * LANES] + r_lo[:, 0:LANES]) / jnp.maximum(
            r_hi[:, LANES:2 * LANES] + r_lo[:, LANES:2 * LANES], 1e-30)
        imp = imp_rows[0:tq] + imp_rows[tq:2 * tq] + imp_rows[2 * tq:3 * tq] + imp_rows[3 * tq:4 * tq]

        score = jnp.where(causal_blk, jnp.where(forced, FORCE, imp.T), NEG)
        sel_t = jnp.where(causal_blk, _topk_rows(score, blk_id, SEL_TOPN, LANES), 0.0)
        notsel = (1.0 - sel_t).T
        qa_scr[g] = jnp.concatenate([qg, jnp.concatenate([notsel] * HPG, axis=0).astype(bf16)], axis=1)
        sel_scr[g] = sel_t

        wl = WINDOW + tq
        s0 = pl.multiple_of(jnp.maximum(t0 - WINDOW, 0), tq)
        w_rel = s0 - t0 + lax.broadcasted_iota(jnp.int32, (1, wl), 1)
        valid_w = lambda t_rel: (w_rel <= t_rel) & (w_rel > t_rel - WINDOW)
        e_w = _masked_exp(_dot(qg, kwt_ref[:, pl.ds(s0, wl)]), slopes, w_rel, valid_w, tq)
        r_w = _dot(e_w.astype(bf16), vw_ref[pl.ds(s0, wl), :])
        o_w = r_w[:, 0:LANES] / r_w[:, LANES:2 * LANES]

        gates = [jnp.concatenate([sig[:, br * NSA_HEADS + h:br * NSA_HEADS + h + 1] for h in heads], axis=0)
                 for br in range(3)]
        part_scr[g] = gates[0] * o_c + gates[2] * o_w
        gate_scr[g] = gates[1]

    flashes = [_Flash(flash_scr[FLASH_SCRATCH * g:FLASH_SCRATCH * (g + 1)],
                      NSA_SLOPES[HPG * g:HPG * (g + 1)], tq, t0) for g in range(NSA_KV)]

    def sel_step(kt, masked):
        k0 = pl.multiple_of(kt * tk, tk)
        kt_aug = kst_ref[:, pl.ds(k0, tk)]
        v_aug = vs_ref[pl.ds(k0, tk), :]
        _flash_tiles([(flashes[g], qa_scr[g], kt_aug, v_aug) for g in range(NSA_KV)], k0, masked)

    def sel_loop(kt, carry):
        blk0 = pl.multiple_of(kt * (tk // SEL_BLOCK), tk // SEL_BLOCK)
        picked = jnp.maximum(sel_scr[0, pl.ds(blk0, tk // SEL_BLOCK), :], sel_scr[1, pl.ds(blk0, tk // SEL_BLOCK), :])

        @pl.when(jnp.max(picked) > 0.0)
        def _():
            sel_step(kt, False)

        return carry

    kd = t0 // tk
    lax.fori_loop(0, kd, sel_loop, 0)
    sel_step(kd, True)

    for g in range(NSA_KV):
        heads = [HPG * g + hh for hh in range(HPG)]
        o = part_scr[g] + gate_scr[g] * flashes[g].result()
        for hh, h in enumerate(heads):
            piece = o[hh * tq:(hh + 1) * tq]
            if h % 2 != g:
                piece = pltpu.roll(piece, HEAD_DIM, 1)
            chunks[h // 2] = chunks[h // 2] + jnp.where(lane // HEAD_DIM == h % 2, piece, 0.0)

    o_ref[...] = jnp.concatenate(chunks, axis=1).astype(o_ref.dtype)


def _nsa_prompt(q_n, g_n, cmp, agg, nsa_bf, win_bf, n_cmp, tq, tk):
    b, l, _ = q_n.shape
    nc = cmp.shape[1]
    rows = HPG * tq
    kct = jnp.swapaxes(cmp[:, :, 0:LANES], 1, 2).astype(bf16)
    onehot_t = jnp.broadcast_to(_block_onehot_t(SEL_BLOCK, l)[None], (b, LANES, l))
    kst = jnp.concatenate([jnp.swapaxes(nsa_bf[:, :, 2 * LANES:3 * LANES], 1, 2), onehot_t], axis=1)
    kwt = jnp.swapaxes(win_bf[:, :, 0:LANES], 1, 2)
    per_b = lambda shape, col=0: pl.BlockSpec((None,) + shape, lambda bi, i: (bi, 0, col))
    return pl.pallas_call(
        functools.partial(_nsa_prompt_body, tq=tq, tk=tk, n_cmp=n_cmp),
        out_shape=jax.ShapeDtypeStruct((b, l, NSA_HEADS * HEAD_DIM), bf16),
        grid=(b, l // tq),
        in_specs=[pl.BlockSpec((None, tq, 512), lambda bi, i: (bi, i, 0)),
                  pl.BlockSpec((None, tq, LANES), lambda bi, i: (bi, i, 0)),
                  per_b((LANES, nc)), per_b((nc, LANES), 1), _const_spec(agg.shape),
                  per_b((2 * LANES, l)), per_b((l, 2 * LANES)), per_b((LANES, l)), per_b((l, 2 * LANES))],
        out_specs=pl.BlockSpec((None, tq, 512), lambda bi, i: (bi, i, 0)),
        scratch_shapes=[pltpu.VMEM((NSA_KV, LANES, tq), f32), pltpu.VMEM((NSA_KV, rows, 2 * LANES), bf16),
                        pltpu.VMEM((NSA_KV, rows, LANES), f32), pltpu.VMEM((NSA_KV, rows, 1), f32)]
        + _flash_scratch(rows) * NSA_KV,
        compiler_params=_cparams(("parallel", "parallel")),
        name="nsa_attention_prompt",
    )(q_n, g_n, kct, cmp, agg, kst, _with_ones(nsa_bf[:, :, 3 * LANES:4 * LANES]), kwt,
      _with_ones(win_bf[:, :, LANES:2 * LANES]))


def _block_mean_body(k_ref, o_ref):
    o_ref[...] = jnp.mean(k_ref[...], axis=0, keepdims=True)


def _block_mean(rows, width, blk):
    b, l, _ = rows.shape
    return pl.pallas_call(
        _block_mean_body,
        out_shape=jax.ShapeDtypeStruct((b, l // blk, 1, width), f32),
        grid=(b, l // blk),
        in_specs=[pl.BlockSpec((None, blk, width), lambda bi, j: (bi, j, 0))],
        out_specs=pl.BlockSpec((None, None, 1, width), lambda bi, j: (bi, j, 0, 0)),
        compiler_params=_cparams(("parallel", "parallel")),
        name="moba_block_mean",
    )(rows)


def _moba_prompt_body(q_ref, kmt_ref, kt_ref, v_ref, o_ref, qa_scr, *flash_scr, tq, tk, nb_pad):
    i = pl.program_id(1)
    t0 = i * tq
    q = q_ref[...]
    qf = q.astype(f32)
    lane = lax.broadcasted_iota(jnp.int32, (tq, LANES), 1)

    gate_t = _nt(kmt_ref[...].astype(bf16), q)
    blk_id = lax.broadcasted_iota(jnp.int32, (LANES, tq), 0) % nb_pad
    own = (t0 + lax.broadcasted_iota(jnp.int32, (LANES, tq), 1)) // MOBA_BLOCK
    past = blk_id < own
    score = jnp.where(past, gate_t, NEG)
    parts = [_topk_rows(score[nb_pad * h:nb_pad * (h + 1)], blk_id[nb_pad * h:nb_pad * (h + 1)], MOBA_TOPK, nb_pad)
             for h in range(MOBA_HEADS)]
    sel_t = jnp.where(past, jnp.concatenate(parts, axis=0), 0.0)
    sel_t = jnp.where(blk_id == own, 1.0, sel_t)
    notsel = (1.0 - sel_t).T

    nch = MOBA_HEADS // 2
    for c in range(nch):
        q_rows = []
        for e in range(2):
            h = 2 * c + e
            qh = jnp.where(lane // HEAD_DIM == e, qf[:, LANES * c:LANES * (c + 1)], 0.0)
            ns = notsel if h == 0 else pltpu.roll(notsel, LANES - nb_pad * h, 1)
            ns = jnp.where(lane < nb_pad, ns, 0.0)
            q_rows.append(jnp.concatenate([qh, ns], axis=1))
        qa_scr[c] = jnp.concatenate(q_rows, axis=0).astype(bf16)
    flashes = [_Flash(flash_scr[FLASH_SCRATCH * c:FLASH_SCRATCH * (c + 1)],
                      MOBA_SLOPES[2 * c:2 * c + 2], tq, t0) for c in range(nch)]

    def step(kt, masked):
        k0 = pl.multiple_of(kt * tk, tk)
        _flash_tiles([(flashes[c], qa_scr[c], kt_ref[c, :, pl.ds(k0, tk)],
                       v_ref[c, pl.ds(k0, tk), :]) for c in range(nch)], k0, masked)

    def loop(kt, carry):
        step(kt, False)
        return carry

    lax.fori_loop(0, t0 // tk, loop, 0)
    step(t0 // tk, True)
    out_chunks = []
    for c in range(nch):
        o = flashes[c].result()
        out_chunks.append(jnp.where(lane < HEAD_DIM, o[0:tq], o[tq:2 * tq]))
    o_ref[...] = jnp.concatenate(out_chunks, axis=1).astype(o_ref.dtype)


def _moba_prompt(mb_q, kmt, moba_bf, tq, tk):
    b, l, _ = mb_q.shape
    nb_pad = LANES // MOBA_HEADS
    nch = MOBA_HEADS // 2
    onehot_t = jnp.broadcast_to(_block_onehot_t(MOBA_BLOCK, l)[None, None], (b, nch, LANES, l))
    k_t = jnp.swapaxes(moba_bf[:, :, 0:nch * LANES].reshape(b, l, nch, LANES), 1, 3)
    kt_aug = jnp.concatenate([jnp.swapaxes(k_t, 1, 2), onehot_t], axis=2)
    v_aug = _with_ones(jnp.swapaxes(moba_bf[:, :, nch * LANES:].reshape(b, l, nch, LANES), 1, 2))
    return pl.pallas_call(
        functools.partial(_moba_prompt_body, tq=tq, tk=tk, nb_pad=nb_pad),
        out_shape=jax.ShapeDtypeStruct((b, l, MOBA_HEADS * HEAD_DIM), bf16),
        grid=(b, l // tq),
        in_specs=[pl.BlockSpec((None, tq, 256), lambda bi, i: (bi, i, 0)),
                  pl.BlockSpec((None, LANES, 256), lambda bi, i: (bi, 0, 0)),
                  pl.BlockSpec((None, nch, 2 * LANES, l), lambda bi, i: (bi, 0, 0, 0)),
                  pl.BlockSpec((None, nch, l, 2 * LANES), lambda bi, i: (bi, 0, 0, 0))],
        out_specs=pl.BlockSpec((None, tq, 256), lambda bi, i: (bi, i, 0)),
        scratch_shapes=[pltpu.VMEM((nch, 2 * tq, 2 * LANES), bf16)] + _flash_scratch(2 * tq) * nch,
        compiler_params=_cparams(("parallel", "parallel")),
        name="moba_attention_prompt",
    )(mb_q, kmt, kt_aug, v_aug)


def _log_keep(z):
    return -(jnp.maximum(z, 0.0) + jnp.log(1.0 + jnp.exp(-jnp.abs(z))))


def _suffix_sums(lk, uu):
    hi = lk.astype(bf16)
    lo = (lk - hi.astype(f32)).astype(bf16)
    r = _dot(jnp.concatenate([hi, lo], axis=1), uu)
    return r[:, 0:LANES], r[:, LANES:2 * LANES]


def _sb_prompt_body(q_ref, uu_ref, k_ref, v_ref, o_ref, carry_scr, acc_scr, *, tq, tk):
    i = pl.program_id(2)
    t0 = i * tq
    qf = q_ref[...].astype(f32)
    lane = lax.broadcasted_iota(jnp.int32, (tq, LANES), 1)
    q2 = jnp.concatenate([jnp.where(lane // HEAD_DIM == e, qf, 0.0) for e in range(2)], axis=0).astype(bf16)
    t_row = t0 + lax.broadcasted_iota(jnp.int32, (2 * tq, 1), 0) % tq
    uu = uu_ref[...]
    carry_scr[...] = jnp.zeros(carry_scr.shape, f32)
    acc_scr[...] = jnp.zeros(acc_scr.shape, f32)

    def step(kt, diag):
        k0 = pl.multiple_of(kt * tk, tk)
        z = _dot(q2, k_ref[:, pl.ds(k0, tk)])
        lk = _log_keep(z)
        if diag:
            is_past = (k0 + lax.broadcasted_iota(jnp.int32, (1, tk), 1)) < t_row
            lk = jnp.where(is_past, lk, 0.0)
        carry = carry_scr[...]
        between = [None] * (tk // LANES)
        for c in reversed(range(tk // LANES)):
            later, total = _suffix_sums(lk[:, LANES * c:LANES * (c + 1)], uu)
            between[c] = later + carry
            carry = carry + total
        w = jnp.exp(z + lk + jnp.concatenate(between, axis=1))
        if diag:
            w = jnp.where(is_past, w, 0.0)
        acc_scr[...] = acc_scr[...] + _dot(w.astype(bf16), v_ref[pl.ds(k0, tk), :])
        carry_scr[...] = carry

    kd = t0 // tk
    step(kd, True)

    def more(state):
        j, top = state
        return (j < kd) & (top > SB_CUTOFF)

    def walk(state):
        j, _ = state
        step(kd - 1 - j, False)
        return j + 1, jnp.max(carry_scr[...])

    lax.while_loop(more, walk, (0, jnp.max(carry_scr[...])))
    o = acc_scr[...]
    o_ref[...] = jnp.where(lane < HEAD_DIM, o[0:tq], o[tq:2 * tq]).astype(o_ref.dtype)


def _sb_prompt(sb_q, uu, sb_bf, tq, tk):
    b, l, _ = sb_q.shape
    nch = SB_HEADS // 2
    k_t = jnp.swapaxes(jnp.swapaxes(sb_bf[:, :, 0:nch * LANES].reshape(b, l, nch, LANES), 1, 3), 1, 2)
    return pl.pallas_call(
        functools.partial(_sb_prompt_body, tq=tq, tk=tk),
        out_shape=jax.ShapeDtypeStruct((b, l, SB_HEADS * HEAD_DIM), bf16),
        grid=(b, nch, l // tq),
        in_specs=[pl.BlockSpec((None, tq, LANES), lambda bi, c, i: (bi, i, c)),
                  _const_spec(uu.shape),
                  pl.BlockSpec((None, None, LANES, l), lambda bi, c, i: (bi, c, 0, 0)),
                  pl.BlockSpec((None, l, LANES), lambda bi, c, i: (bi, 0, nch + c))],
        out_specs=pl.BlockSpec((None, tq, LANES), lambda bi, c, i: (bi, i, c)),
        scratch_shapes=[pltpu.VMEM((2 * tq, LANES), f32), pltpu.VMEM((2 * tq, LANES), f32)],
        compiler_params=_cparams(("parallel", "parallel", "parallel")),
        name="stickbreak_attention_prompt",
    )(sb_q, uu, k_t, sb_bf)


def _merge_body(oa_ref, ob_ref, oc_ref, gm_ref, x_ref, gate_ref, wa_ref, wb_ref, wc_ref, wo_ref, o_ref):
    d = x_ref.shape[1]
    g = jax.nn.sigmoid(gm_ref[...])
    merged = (g[:, 0:d] * _dot(oa_ref[...], wa_ref[...]) + g[:, d:2 * d] * _dot(ob_ref[...], wb_ref[...])
              + g[:, 2 * d:3 * d] * _dot(oc_ref[...], wc_ref[...]))
    o_ref[...] = x_ref[...] + gate_ref[...] * _dot(merged.astype(bf16), wo_ref[...])


def _merge_out(o_a, o_b, o_c, g_m, x, gate, wa, wb, wc, wo, tm):
    r, d = x.shape
    nb, rb, _ = gate.shape
    tiles_per_b = (r // nb) // tm
    row = lambda w: pl.BlockSpec((tm, w), lambda i: (i, 0))
    return pl.pallas_call(
        _merge_body,
        out_shape=jax.ShapeDtypeStruct((r, d), f32),
        grid=(r // tm,),
        in_specs=[row(o_a.shape[1]), row(o_b.shape[1]), row(o_c.shape[1]), row(3 * d), row(d),
                  pl.BlockSpec((None, rb, d), lambda i: (i // tiles_per_b, 0, 0)),
                  _const_spec(wa.shape), _const_spec(wb.shape), _const_spec(wc.shape), _const_spec(wo.shape)],
        out_specs=row(d),
        compiler_params=_cparams(("parallel",)),
        name="merge_out_proj",
    )(o_a, o_b, o_c, g_m, x, gate, wa, wb, wc, wo)


def _ffn_tail(a, a_m1, a_m2, b, cw_ref, cb_ref, wd_ref, x_ref, gate_ref, o_ref):
    conv = cb_ref[...] + a_m2 * cw_ref[0:1, :]
    conv = conv + a_m1 * cw_ref[1:2, :]
    conv = conv + a * cw_ref[2:3, :]
    y = _dot((_gelu_tanh(conv) * b).astype(bf16), wd_ref[...])
    o_ref[...] = x_ref[...] + gate_ref[...] * y


def _ffn_seq_body(a_ref, halo_ref, b_ref, cw_ref, cb_ref, wd_ref, x_ref, gate_ref, o_ref, *, tiles_per_b):
    a = a_ref[...]
    first = pl.program_id(0) % tiles_per_b == 0
    halo = jnp.where(first, 0.0, halo_ref[...])
    row = lax.broadcasted_iota(jnp.int32, a.shape, 0)
    a_m1 = jnp.where(row < 1, halo[7:8, :], pltpu.roll(a, 1, 0))
    a_m2 = jnp.where(row < 1, halo[6:7, :], jnp.where(row < 2, halo[7:8, :], pltpu.roll(a, 2, 0)))
    _ffn_tail(a, a_m1, a_m2, b_ref[...], cw_ref, cb_ref, wd_ref, x_ref, gate_ref, o_ref)


def _ffn_step_body(a_ref, am1_ref, am2_ref, b_ref, cw_ref, cb_ref, wd_ref, x_ref, gate_ref, o_ref):
    _ffn_tail(a_ref[...], am1_ref[...], am2_ref[...], b_ref[...], cw_ref, cb_ref, wd_ref, x_ref, gate_ref, o_ref)


def _ffn_down_seq(u_a, u_b, cw8, cb, wd, x, gate, tm):
    r, d = x.shape
    ff = u_a.shape[1]
    nb = gate.shape[0]
    tiles_per_b = (r // nb) // tm
    row = lambda w: pl.BlockSpec((tm, w), lambda i: (i, 0))
    return pl.pallas_call(
        functools.partial(_ffn_seq_body, tiles_per_b=tiles_per_b),
        out_shape=jax.ShapeDtypeStruct((r, d), f32),
        grid=(r // tm,),
        in_specs=[row(ff), pl.BlockSpec((8, ff), lambda i: (jnp.maximum(i * (tm // 8) - 1, 0), 0)), row(ff),
                  _const_spec(cw8.shape), _const_spec(cb.shape), _const_spec(wd.shape), row(d),
                  pl.BlockSpec((None, 1, d), lambda i: (i // tiles_per_b, 0, 0))],
        out_specs=row(d),
        compiler_params=_cparams(("parallel",)),
        name="conv_ffn_down_seq",
    )(u_a, u_a, u_b, cw8, cb, wd, x, gate)


def _ffn_down_step(u_a, a_m1, a_m2, u_b, cw8, cb, wd, x, gate):
    r, d = x.shape
    full = lambda a: pl.BlockSpec(a.shape, lambda i: (0,) * a.ndim)
    return pl.pallas_call(
        _ffn_step_body,
        out_shape=jax.ShapeDtypeStruct((r, d), f32),
        grid=(1,),
        in_specs=[full(u_a), full(a_m1), full(a_m2), full(u_b), full(cw8), full(cb), full(wd), full(x),
                  pl.BlockSpec((None, r, d), lambda i: (0, 0, 0))],
        out_specs=full(x),
        compiler_params=_cparams(("arbitrary",)),
        name="conv_ffn_down_step",
    )(u_a, a_m1, a_m2, u_b, cw8, cb, wd, x, gate)


def _final_norm_body(x_ref, g_ref, o_ref):
    x = x_ref[...]
    o_ref[...] = x * lax.rsqrt(jnp.mean(x * x, axis=-1, keepdims=True) + RMS_EPS) * g_ref[...]


def _final_norm(x, g, tm):
    r, d = x.shape
    return pl.pallas_call(
        _final_norm_body,
        out_shape=jax.ShapeDtypeStruct((r, d), f32),
        grid=(r // tm,),
        in_specs=[pl.BlockSpec((tm, d), lambda i: (i, 0)), _const_spec((1, d))],
        out_specs=pl.BlockSpec((tm, d), lambda i: (i, 0)),
        compiler_params=_cparams(("parallel",)),
        name="final_rmsnorm",
    )(x, g.reshape(1, d))


def _page_specs(n_pages, layer_base, per_step=1, k=0):
    return [pl.BlockSpec((None, 512, PAGE_SIZE),
                         functools.partial(lambda s, pt, j: (layer_base + pt[(s * per_step + k) * n_pages + j], 0, 0),
                                           j=j))
            for j in range(n_pages)]


def _per_seq(width):
    return pl.BlockSpec((None, 1, width), lambda s, pt: (s, 0, 0))


def _dec_const(shape):
    nd = len(shape)
    return pl.BlockSpec(shape, lambda s, pt: (0,) * nd, pipeline_mode=pl.Buffered(1))


def _head_rows(q_row):
    row = lax.broadcasted_iota(jnp.int32, (8, LANES), 0)
    lane = lax.broadcasted_iota(jnp.int32, (8, LANES), 1)
    q8 = jnp.broadcast_to(q_row, (8, q_row.shape[1]))
    qsel = jnp.zeros((8, LANES), f32)
    for c in range(NSA_HEADS // 2):
        qsel = qsel + jnp.where(row // 2 == c, q8[:, LANES * c:LANES * (c + 1)], 0.0)
    swap = (row % 2) != (row // HPG)
    qm = jnp.where(swap, pltpu.roll(qsel, HEAD_DIM, 1), qsel)
    return jnp.where(lane // HEAD_DIM == row // HPG, qm, 0.0), swap


def _rank_select(score_row, k):
    a = jnp.broadcast_to(score_row, (LANES, LANES))
    b = a.T
    ii = lax.broadcasted_iota(jnp.int32, (LANES, LANES), 0)
    jj = lax.broadcasted_iota(jnp.int32, (LANES, LANES), 1)
    ahead = (b > a) | ((b == a) & (ii < jj))
    rank = jnp.sum(jnp.where(ahead, 1.0, 0.0), axis=0, keepdims=True)
    return jnp.where(rank < k, 1.0, 0.0)


def _nsa_dec_body(pt_ref, q_ref, gn_ref, new_ref, wnew_ref, sw_ref, pe_ref, w1_ref, w2_ref, agg_ref, e_ref,
                  *rest, n_pages, per_step):
    pages, o_ref = rest[:n_pages * per_step], rest[n_pages * per_step]
    kc_scr, vc_scr = rest[n_pages * per_step + 1:]
    past = n_pages * PAGE_SIZE
    for j in range(n_pages * per_step):
        kc_scr[PAGE_SIZE * j:PAGE_SIZE * (j + 1), :] = pages[j][0:LANES, :].T
        vc_scr[PAGE_SIZE * j:PAGE_SIZE * (j + 1), :] = pages[j][LANES:2 * LANES, :].T

    nc = past // CMP_STRIDE
    cmp_all = _cmp_core((kc_scr, vc_scr), pe_ref, w1_ref, w2_ref, nc * per_step)
    for k in range(per_step):
        o_ref[k] = _nsa_dec_one(q_ref[k], gn_ref[k], new_ref[k], wnew_ref[k], sw_ref.at[k],
                                cmp_all[nc * k:nc * (k + 1)], pages[n_pages * k:n_pages * (k + 1)], agg_ref, e_ref)


def _nsa_dec_one(q_row, gn_row, new_row, wnew, sw_ref, cmpv, pages, agg_ref, e_ref):
    n_pages = len(pages)
    past = n_pages * PAGE_SIZE
    n_cmp = (past + 1 - CMP_LEN) // CMP_STRIDE + 1
    row = lax.broadcasted_iota(jnp.int32, (8, LANES), 0)
    lane = lax.broadcasted_iota(jnp.int32, (8, LANES), 1)
    qm, swap = _head_rows(q_row)
    qmb = qm.astype(bf16)
    slope = jnp.zeros((8, 1), f32)
    row1 = lax.broadcasted_iota(jnp.int32, (8, 1), 0)
    for h in range(NSA_HEADS):
        slope = jnp.where(row1 == h, NSA_SLOPES[h], slope)
    grp0 = row < HPG

    d_c = past - (lane[0:1] * CMP_STRIDE + CMP_LEN - 1)
    s_c = _nt(qmb, cmpv[:, 0:LANES].astype(bf16)) - slope * d_c.astype(f32)
    p_c = _ref_softmax(s_c, (d_c >= 0) & (lane[0:1] < n_cmp))
    o_c = _dot(p_c.astype(bf16), cmpv[:, LANES:2 * LANES].astype(bf16))
    aggb = agg_ref[...].astype(bf16)
    p_hi = p_c.astype(bf16)
    p_lo = (p_c - p_hi.astype(f32)).astype(bf16)
    imp_rows = _dot(p_hi, aggb) + _dot(p_lo, aggb)

    cur = past // SEL_BLOCK
    blk = lane[0:1]
    forced = (blk == 0) | (blk == cur) | (blk == cur - 1)
    causal = blk <= cur
    notsel_g = []
    for g in range(NSA_KV):
        imp = jnp.sum(imp_rows[HPG * g:HPG * (g + 1)], axis=0, keepdims=True)
        score = jnp.where(causal, jnp.where(forced, FORCE, imp), NEG)
        sel = jnp.where(causal, _rank_select(score, min(SEL_TOPN, -(-(past + 1) // SEL_BLOCK))), 0.0)
        notsel_g.append(jnp.broadcast_to(1.0 - sel, (8, LANES)))
    notsel = jnp.where(grp0, notsel_g[0], notsel_g[1])

    dist = past - lax.broadcasted_iota(jnp.int32, (1, past), 1)
    picked = _dot(notsel.astype(bf16), e_ref[...]) < 0.5
    ks_t = jnp.concatenate([pages[j][2 * LANES:3 * LANES, :].astype(bf16) for j in range(n_pages)], axis=1)
    vs_t = jnp.concatenate([pages[j][3 * LANES:4 * LANES, :].astype(bf16) for j in range(n_pages)], axis=1)
    s_s = jnp.where(picked, _dot(qmb, ks_t) - slope * dist.astype(f32), NEG)
    s_n = jnp.sum(qm * new_row[:, 2 * LANES:3 * LANES], axis=-1, keepdims=True)
    m_s = jnp.maximum(jnp.max(s_s, axis=-1, keepdims=True), s_n)
    e_s = jnp.where(picked, jnp.exp(s_s - m_s), 0.0)
    e_n = jnp.exp(s_n - m_s)
    den = jnp.maximum(jnp.sum(e_s, axis=-1, keepdims=True) + e_n, 1e-30)
    o_s = (e_n * new_row[:, 3 * LANES:4 * LANES] + _nt(e_s.astype(bf16), vs_t)) / den

    wb = sw_ref.shape[1]
    d_w = wb - lax.broadcasted_iota(jnp.int32, (1, wb), 1)
    s_w = _dot(qmb, sw_ref[0:LANES, :].astype(bf16)) - slope * d_w.astype(f32)
    valid_w = (d_w < WINDOW) & (d_w >= 0)
    s_w = jnp.where(valid_w, s_w, NEG)
    s_n = jnp.sum(qm * wnew[:, 0:LANES], axis=-1, keepdims=True)
    m_w = jnp.maximum(jnp.max(s_w, axis=-1, keepdims=True), s_n)
    e_w = jnp.where(valid_w, jnp.exp(s_w - m_w), 0.0)
    e_n = jnp.exp(s_n - m_w)
    den = jnp.maximum(jnp.sum(e_w, axis=-1, keepdims=True) + e_n, 1e-30)
    o_w = (_nt(e_w.astype(bf16), sw_ref[LANES:2 * LANES, :].astype(bf16)) + e_n * wnew[:, LANES:2 * LANES]) / den

    sig = jnp.broadcast_to(jax.nn.sigmoid(gn_row), (8, LANES))
    gates = [jnp.sum(jnp.where(lane == br * NSA_HEADS + row, sig, 0.0), axis=-1, keepdims=True) for br in range(3)]
    o = gates[0] * o_c + gates[1] * o_s + gates[2] * o_w
    o = jnp.where(swap, pltpu.roll(o, HEAD_DIM, 1), o)
    o = jnp.where(lane // HEAD_DIM == row % 2, o, 0.0)
    return jnp.concatenate([o[2 * c:2 * c + 1] + o[2 * c + 1:2 * c + 2] for c in range(NSA_HEADS // 2)], axis=1)


def _nsa_decode(page_table, cache_t, layer_base, q, g_n, nsa_new, win_new, state_win_t, win_base, pe4, w1bd, w2bd,
                agg, e_sel, per_step):
    s, n_pages = page_table.shape
    past = n_pages * PAGE_SIZE
    consts = (pe4, w1bd, w2bd, agg, e_sel)
    seqs = lambda width: pl.BlockSpec((per_step, 1, width), lambda si, pt: (si, 0, 0))
    page_specs = [spec for k in range(per_step) for spec in _page_specs(n_pages, layer_base, per_step, k)]
    grid_spec = pltpu.PrefetchScalarGridSpec(
        num_scalar_prefetch=1,
        grid=(s // per_step,),
        in_specs=[seqs(512), seqs(LANES), seqs(512), seqs(256),
                  pl.BlockSpec((per_step,) + state_win_t.shape[1:],
                               lambda si, pt: (win_base // per_step + si, 0, 0))]
        + [_dec_const(c.shape) for c in consts] + page_specs,
        out_specs=seqs(512),
        scratch_shapes=[pltpu.VMEM((per_step * past, LANES), f32), pltpu.VMEM((per_step * past, LANES), f32)],
    )
    return pl.pallas_call(
        functools.partial(_nsa_dec_body, n_pages=n_pages, per_step=per_step),
        out_shape=jax.ShapeDtypeStruct((s, 1, 512), f32),
        grid_spec=grid_spec,
        compiler_params=_cparams(("parallel",)),
        name="nsa_attention_decode",
    )(page_table.reshape(-1), q, g_n, nsa_new, win_new, state_win_t, *consts,
      *([cache_t] * (n_pages * per_step)))


def _win_shift_body(sw_ref, new_ref, o_ref):
    n, feat, wb = sw_ref.shape
    lane = lax.broadcasted_iota(jnp.int32, (feat, wb), 1)
    for k in range(n):
        col = jnp.broadcast_to(new_ref[k], (LANES, feat)).T
        col = jnp.concatenate([col] * (wb // LANES), axis=1)
        o_ref[k] = jnp.where(lane == wb - 1, col, pltpu.roll(sw_ref[k], wb - 1, 1))


def _win_shift(state_win_t, new_rows, per_step):
    n, feat, wb = state_win_t.shape
    return pl.pallas_call(
        _win_shift_body,
        out_shape=jax.ShapeDtypeStruct((n, feat, wb), f32),
        grid=(n // per_step,),
        in_specs=[pl.BlockSpec((per_step, feat, wb), lambda i: (i, 0, 0)),
                  pl.BlockSpec((per_step, 1, feat), lambda i: (i, 0, 0))],
        out_specs=pl.BlockSpec((per_step, feat, wb), lambda i: (i, 0, 0)),
        compiler_params=_cparams(("parallel",)),
        name="window_state_shift",
    )(state_win_t, new_rows)


def _q_head_rows4(q_row):
    row = lax.broadcasted_iota(jnp.int32, (8, 256), 0)
    lane = lax.broadcasted_iota(jnp.int32, (8, 256), 1)
    own = lane // HEAD_DIM == row
    return jnp.where(own, jnp.broadcast_to(q_row, (8, 256)), 0.0), own


def _moba_dec_body(pt_ref, q_ref, new_ref, seg_ref, e_ref, *rest, n_pages):
    pages, o_ref = rest[:n_pages], rest[n_pages]
    past = n_pages * PAGE_SIZE
    nb_past = past // MOBA_BLOCK
    kw = MOBA_HEADS * HEAD_DIM
    qm, own_lanes = _q_head_rows4(q_ref[...])
    qmb = qm.astype(bf16)
    k_t = jnp.concatenate([pages[j][0:kw, :].astype(bf16) for j in range(n_pages)], axis=1)
    v_t = jnp.concatenate([pages[j][kw:2 * kw, :].astype(bf16) for j in range(n_pages)], axis=1)
    raw = _dot(qmb, k_t)
    raw_hi = raw.astype(bf16)
    raw_lo = (raw - raw_hi.astype(f32)).astype(bf16)
    gate = _dot(raw_hi, seg_ref[...]) + _dot(raw_lo, seg_ref[...])

    lane = lax.broadcasted_iota(jnp.int32, (8, LANES), 1)
    is_past = lane < nb_past
    score = jnp.where(is_past, gate, NEG)
    rank = jnp.zeros((8, LANES), f32)
    for i in range(nb_past):
        gi = score[:, i:i + 1]
        rank = rank + jnp.where((gi > score) | ((gi == score) & (i < lane)), 1.0, 0.0)
    sel = jnp.where(is_past & (rank < min(MOBA_TOPK, nb_past)), 1.0, 0.0)
    picked = _dot(sel.astype(bf16), e_ref[...]) > 0.5

    row1 = lax.broadcasted_iota(jnp.int32, (8, 1), 0)
    slope = jnp.zeros((8, 1), f32)
    for h in range(MOBA_HEADS):
        slope = jnp.where(row1 == h, MOBA_SLOPES[h], slope)
    dist = past - lax.broadcasted_iota(jnp.int32, (1, past), 1)
    new_row = new_ref[...]
    s = jnp.where(picked, raw - slope * dist.astype(f32), NEG)
    s_n = jnp.sum(qm * new_row[:, 0:kw], axis=-1, keepdims=True)
    m = jnp.maximum(jnp.max(s, axis=-1, keepdims=True), s_n)
    e = jnp.where(picked, jnp.exp(s - m), 0.0)
    e_n = jnp.exp(s_n - m)
    den = jnp.maximum(jnp.sum(e, axis=-1, keepdims=True) + e_n, 1e-30)
    o = (e_n * new_row[:, kw:2 * kw] + _nt(e.astype(bf16), v_t)) / den
    o_ref[...] = jnp.sum(jnp.where(own_lanes, o, 0.0), axis=0, keepdims=True)


def _moba_decode(page_table, cache_t, layer_base, q, moba_new, seg_mean, e_blk):
    s, n_pages = page_table.shape
    grid_spec = pltpu.PrefetchScalarGridSpec(
        num_scalar_prefetch=1,
        grid=(s,),
        in_specs=[_per_seq(256), _per_seq(512), _dec_const(seg_mean.shape), _dec_const(e_blk.shape)]
        + _page_specs(n_pages, layer_base),
        out_specs=_per_seq(256),
    )
    return pl.pallas_call(
        functools.partial(_moba_dec_body, n_pages=n_pages),
        out_shape=jax.ShapeDtypeStruct((s, 1, 256), f32),
        grid_spec=grid_spec,
        compiler_params=_cparams(("parallel",)),
        name="moba_attention_decode",
    )(page_table.reshape(-1), q, moba_new, seg_mean, e_blk, *([cache_t] * n_pages))


def _sb_dec_body(pt_ref, q_ref, uu_ref, *rest, n_pages):
    pages, o_ref = rest[:n_pages], rest[n_pages]
    kw = SB_HEADS * HEAD_DIM
    qm, own_lanes = _q_head_rows4(q_ref[...])
    qmb = qm.astype(bf16)
    k_t = jnp.concatenate([pages[j][0:kw, :].astype(bf16) for j in range(n_pages)], axis=1)
    v_t = jnp.concatenate([pages[j][kw:2 * kw, :].astype(bf16) for j in range(n_pages)], axis=1)
    z = _dot(qmb, k_t)
    lk = _log_keep(z)
    stacked = jnp.concatenate([lk[:, PAGE_SIZE * j:PAGE_SIZE * (j + 1)] for j in range(n_pages)], axis=0)
    later, total = _suffix_sums(stacked, uu_ref[...])
    carry = jnp.zeros((8, LANES), f32)
    between = [None] * n_pages
    for j in reversed(range(n_pages)):
        between[j] = later[8 * j:8 * (j + 1)] + carry
        carry = carry + total[8 * j:8 * (j + 1)]
    w = jnp.exp(z + lk + jnp.concatenate(between, axis=1))
    acc = _nt(w.astype(bf16), v_t)
    o_ref[...] = jnp.sum(jnp.where(own_lanes, acc, 0.0), axis=0, keepdims=True)


def _sb_decode(page_table, cache_t, layer_base, q, uu):
    s, n_pages = page_table.shape
    grid_spec = pltpu.PrefetchScalarGridSpec(
        num_scalar_prefetch=1,
        grid=(s,),
        in_specs=[_per_seq(256), _dec_const(uu.shape)] + _page_specs(n_pages, layer_base),
        out_specs=_per_seq(256),
    )
    return pl.pallas_call(
        functools.partial(_sb_dec_body, n_pages=n_pages),
        out_shape=jax.ShapeDtypeStruct((s, 1, 256), f32),
        grid_spec=grid_spec,
        compiler_params=_cparams(("parallel",)),
        name="stickbreak_attention_decode",
    )(page_table.reshape(-1), q, uu, *([cache_t] * n_pages))


def _agg_matrix(nc, n_cmp):
    c0 = np.arange(nc)[:, None] * CMP_STRIDE
    s0 = np.arange(LANES)[None, :] * SEL_BLOCK
    ov = np.clip(np.minimum(c0 + CMP_LEN, s0 + SEL_BLOCK) - np.maximum(c0, s0), 0, None) / CMP_LEN
    ov[n_cmp:] = 0.0
    return jnp.asarray(ov, f32)


def _block_onehot_t(block, l):
    e = (np.arange(l)[None, :] // block) == np.arange(LANES)[:, None]
    return jnp.asarray(np.where(e, -MASK_BIG, 0.0), bf16)


def _expand_matrix(block, kp):
    e = (np.arange(kp)[None, :] // block) == np.arange(LANES)[:, None]
    return jnp.asarray(e, bf16)


def _suffix_matrix():
    j = np.arange(2 * LANES)[:, None] % LANES
    s = np.arange(2 * LANES)[None, :]
    return jnp.asarray((s >= LANES) | (j > s), bf16)


def _layer_weights(l, w_in, cmp_pe, cmp_w1, cmp_w2, w_br_a, w_br_b, w_br_c, w_o, w_up, conv_w, conv_b, w_down):
    d = w_in.shape[1]
    w = w_in[l]
    w_proj = jnp.concatenate([w[:, 0:1280], w[:, 1304:], w[:, 1280:1304], jnp.zeros((d, _PROJ_W - 5912), f32)],
                             axis=1).astype(bf16)
    pe4 = jnp.concatenate([cmp_pe[l], cmp_pe[l]], axis=2)
    w1 = cmp_w1[l].reshape(2, CMP_LEN, HEAD_DIM, CMP_HID)
    w1bd = jnp.zeros((2, CMP_LEN, NSA_KV * HEAD_DIM, NSA_KV * CMP_HID), f32)
    w2bd = jnp.zeros((2, NSA_KV * CMP_HID, NSA_KV * HEAD_DIM), f32)
    for g in range(NSA_KV):
        w1bd = w1bd.at[:, :, HEAD_DIM * g:HEAD_DIM * (g + 1), CMP_HID * g:CMP_HID * (g + 1)].set(w1)
        w2bd = w2bd.at[:, CMP_HID * g:CMP_HID * (g + 1), HEAD_DIM * g:HEAD_DIM * (g + 1)].set(cmp_w2[l])
    cw8 = jnp.concatenate([conv_w[l], jnp.zeros((8 - CONV_W, conv_w.shape[2]), f32)], axis=0)
    return dict(w_proj=w_proj, pe4=pe4, w1bd=w1bd.astype(bf16), w2bd=w2bd.astype(bf16),
                wa=w_br_a[l].astype(bf16), wb=w_br_b[l].astype(bf16), wc=w_br_c[l].astype(bf16),
                wo=w_o[l].astype(bf16), w_up=w_up[l].astype(bf16), cw8=cw8, cb=conv_b[l].reshape(1, -1),
                wd=w_down[l].astype(bf16))


def _mod_parts(mod_rows, per_row):
    r = mod_rows.shape[0]
    parts = mod_rows.reshape(r, 6, D_MODEL)
    return [parts[:, k].reshape((1, r, D_MODEL) if per_row else (r, 1, D_MODEL)) for k in range(6)]


def kernel(x_prompt, x_sample, cache_nsa, cache_moba, cache_sb, state_win, state_conv, page_table, c_prompt,
           c_sample, norm1_g, norm2_g, w_ada, b_ada, w_in, cmp_pe, cmp_w1, cmp_w2, w_br_a, w_br_b, w_br_c, w_o,
           w_up, conv_w, conv_b, w_down, final_g):
    b, t, d = x_prompt.shape
    s = x_sample.shape[0]
    depth = w_in.shape[0]
    n_phys = cache_nsa.shape[1]
    n_pages = page_table.shape[1]
    past = n_pages * PAGE_SIZE
    tm = 512
    tq = 128
    tk = 512

    n_c = b + s
    c_all = jnp.concatenate([c_prompt, c_sample, jnp.zeros((-n_c % 8, d), f32)], axis=0)
    mod = _ada_mod(c_all, w_ada, b_ada)

    nc_p = t // CMP_STRIDE
    n_cmp_p = (t - CMP_LEN) // CMP_STRIDE + 1
    agg_p = _agg_matrix(nc_p, n_cmp_p)
    nc_s = past // CMP_STRIDE
    agg_s = _agg_matrix(nc_s, (past + 1 - CMP_LEN) // CMP_STRIDE + 1)
    e_sel = _expand_matrix(SEL_BLOCK, past)
    e_blk = _expand_matrix(MOBA_BLOCK, past)
    seg_mean = (jnp.swapaxes(e_blk, 0, 1).astype(f32) * (1.0 / MOBA_BLOCK)).astype(bf16)
    uu = _suffix_matrix()
    nb_pad = LANES // MOBA_HEADS

    feat_major = lambda a: jnp.transpose(a, (0, 1, 3, 4, 5, 2))
    caches = [feat_major(c).reshape(depth * n_phys, 512, PAGE_SIZE) for c in (cache_nsa, cache_moba, cache_sb)]
    state_win_t = feat_major(state_win).reshape(depth * s, 256, state_win.shape[2])

    xp = x_prompt.reshape(b * t, d)
    xs = x_sample.reshape(s, d)
    outs_p = [[] for _ in range(5)]
    outs_s = [[] for _ in range(5)]
    for l in range(depth):
        w = _layer_weights(l, w_in, cmp_pe, cmp_w1, cmp_w2, w_br_a, w_br_b, w_br_c, w_o, w_up, conv_w, conv_b,
                           w_down)
        mp = _mod_parts(mod[l, 0:b], per_row=False)
        ms = _mod_parts(mod[l, b:b + s], per_row=True)

        (q_n, nsa_rows, nsa_bf, win_rows, win_bf, mb_q, moba_rows, moba_bf, sb_q, sb_rows, sb_bf, g_m, g_n) = \
            _norm_mod_matmul(xp, norm1_g[l], mp[0], mp[1], w["w_proj"], _IN_SEGS, _IN_DTYPES, tm, "in_proj_prompt")
        r3 = lambda a: a.reshape(b, t, a.shape[1])
        cmp = _cmp_prompt(r3(nsa_rows), w["pe4"], w["w1bd"], w["w2bd"])
        o_a = _nsa_prompt(r3(q_n), r3(g_n), cmp, agg_p, r3(nsa_bf), r3(win_bf), n_cmp_p, tq, tk)
        kmean = _block_mean(r3(moba_rows), 256, MOBA_BLOCK).reshape(b, t // MOBA_BLOCK, MOBA_HEADS, HEAD_DIM)
        kmt = jnp.zeros((b, MOBA_HEADS, nb_pad, MOBA_HEADS, HEAD_DIM), f32)
        for h in range(MOBA_HEADS):
            kmt = kmt.at[:, h, 0:t // MOBA_BLOCK, h].set(kmean[:, :, h])
        o_b = _moba_prompt(r3(mb_q), kmt.reshape(b, LANES, 256), r3(moba_bf), tq, tk)
        o_c = _sb_prompt(r3(sb_q), uu, r3(sb_bf), 2 * tq, tk)
        xp = _merge_out(o_a.reshape(b * t, -1), o_b.reshape(b * t, -1), o_c.reshape(b * t, -1), g_m, xp, mp[2],
                        w["wa"], w["wb"], w["wc"], w["wo"], tm)
        u_a, u_b = _norm_mod_matmul(xp, norm2_g[l], mp[3], mp[4], w["w_up"], _UP_SEGS, _UP_DTYPES, tm,
                                    "ffn_up_prompt")
        xp = _ffn_down_seq(u_a, u_b, w["cw8"], w["cb"], w["wd"], xp, mp[5], tm)
        keep = min(WINDOW, t)
        outs_p[0].append(nsa_rows.reshape(b, t // PAGE_SIZE, PAGE_SIZE, 4, NSA_KV, HEAD_DIM))
        outs_p[1].append(moba_rows.reshape(b, t // PAGE_SIZE, PAGE_SIZE, 2, MOBA_HEADS, HEAD_DIM))
        outs_p[2].append(sb_rows.reshape(b, t // PAGE_SIZE, PAGE_SIZE, 2, SB_HEADS, HEAD_DIM))
        outs_p[3].append(r3(win_rows)[:, t - keep:].reshape(b, keep, 2, NSA_KV, HEAD_DIM))
        outs_p[4].append(r3(u_a)[:, t - (CONV_W - 1):])

        (q_n, nsa_rows, _, win_rows, _, mb_q, moba_rows, _, sb_q, sb_rows, _, g_m, g_n) = \
            _norm_mod_matmul(xs, norm1_g[l], ms[0], ms[1], w["w_proj"], _IN_SEGS, _IN_DTYPES, s, "in_proj_sample")
        s3 = lambda a: a.astype(f32).reshape(s, 1, a.shape[1])
        base = l * n_phys
        o_a = _nsa_decode(page_table, caches[0], base, s3(q_n), s3(g_n), s3(nsa_rows), s3(win_rows), state_win_t,
                          l * s, w["pe4"], w["w1bd"], w["w2bd"], agg_s, e_sel, 2)
        o_b = _moba_decode(page_table, caches[1], base, s3(mb_q), s3(moba_rows), seg_mean, e_blk)
        o_c = _sb_decode(page_table, caches[2], base, s3(sb_q), uu)
        xs = _merge_out(o_a.reshape(s, -1).astype(bf16), o_b.reshape(s, -1).astype(bf16),
                        o_c.reshape(s, -1).astype(bf16), g_m, xs, ms[2], w["wa"], w["wb"], w["wc"], w["wo"], s)
        u_a, u_b = _norm_mod_matmul(xs, norm2_g[l], ms[3], ms[4], w["w_up"], _UP_SEGS, _UP_DTYPES, s,
                                    "ffn_up_sample")
        xs = _ffn_down_step(u_a, state_conv[l, :, 1], state_conv[l, :, 0], u_b, w["cw8"], w["cb"], w["wd"], xs,
                            ms[5])
        conv_full = jnp.concatenate([state_conv[l], u_a.reshape(s, 1, -1)], axis=1)
        outs_s[0].append(nsa_rows.reshape(s, 1, 4, NSA_KV, HEAD_DIM))
        outs_s[1].append(moba_rows.reshape(s, 1, 2, MOBA_HEADS, HEAD_DIM))
        outs_s[2].append(sb_rows.reshape(s, 1, 2, SB_HEADS, HEAD_DIM))
        outs_s[3].append(win_rows.reshape(s, 1, 256))
        outs_s[4].append(conv_full[:, conv_full.shape[1] - (CONV_W - 1):])

    y_prompt = _final_norm(xp, final_g, tm).reshape(b, t, d)
    y_sample = _final_norm(xs, final_g, s).reshape(s, 1, d)
    wb = state_win.shape[2]
    win_s = _win_shift(state_win_t, jnp.concatenate(outs_s[3], axis=0), 4)
    win_s = jnp.transpose(win_s.reshape(depth, s, 2, NSA_KV, HEAD_DIM, wb), (0, 1, 5, 2, 3, 4))
    st = lambda lst: jnp.stack(lst)
    return (y_prompt, y_sample, st(outs_p[0]), st(outs_s[0]), st(outs_p[1]), st(outs_s[1]), st(outs_p[2]),
            st(outs_s[2]), st(outs_p[3]), win_s, st(outs_p[4]), st(outs_s[4]))
```

```python
import functools

import numpy as np
import jax
import jax.numpy as jnp
from jax import lax
from jax.experimental import pallas as pl
from jax.experimental.pallas import tpu as pltpu

f32 = jnp.float32
bf16 = jnp.bfloat16

D_MODEL = 1024
HEAD_DIM = 64
NSA_HEADS = 8
NSA_KV = 2
HPG = NSA_HEADS // NSA_KV
CMP_LEN = 32
CMP_STRIDE = 16
CMP_HID = 128
SEL_BLOCK = 64
SEL_TOPN = 16
WINDOW = 512
MOBA_HEADS = 4
MOBA_BLOCK = 256
MOBA_TOPK = 3
SB_HEADS = 4
D_FF = 2816
CONV_W = 3
PAGE_SIZE = 128
RMS_EPS = 1e-6
NEG = -1e30
FORCE = 1e9

LANES = 128
MASK_BIG = 2.0 ** 100
M_INIT = -1e29
REMOVED = -3e38
SB_CUTOFF = -110.0
VMEM_LIMIT_MB = 56

NSA_SLOPES = [2.0 ** (-8.0 * (h + 1) / NSA_HEADS) for h in range(NSA_HEADS)]
MOBA_SLOPES = [2.0 ** (-8.0 * (h + 1) / MOBA_HEADS) for h in range(MOBA_HEADS)]

_Q_N, _NSA, _WIN, _MB_Q, _MOBA, _SB_Q, _SB, _G_M, _G_N, _PROJ_W = 0, 512, 1024, 1280, 1536, 2048, 2304, 2816, 5888, 6016


def _cparams(sem, vmem_mb=VMEM_LIMIT_MB):
    return pltpu.CompilerParams(dimension_semantics=sem, vmem_limit_bytes=vmem_mb * 2 ** 20)


def _const_spec(shape):
    nd = len(shape)
    return pl.BlockSpec(shape, lambda *_: (0,) * nd, pipeline_mode=pl.Buffered(1))


def _nt(a, b):
    return lax.dot_general(a, b, (((1,), (1,)), ((), ())), preferred_element_type=f32)


def _dot(a, b):
    return jnp.dot(a, b, preferred_element_type=f32)


def _gelu_tanh(x):
    return x * (0.5 * (1.0 + jnp.tanh(np.sqrt(2.0 / np.pi) * (x + 0.044715 * (x * x * x)))))


def _ref_softmax(s, valid):
    s = jnp.where(valid, s, NEG)
    e = jnp.where(valid, jnp.exp(s - jnp.max(s, axis=-1, keepdims=True)), 0.0)
    return e / jnp.maximum(jnp.sum(e, axis=-1, keepdims=True), 1e-30)


def _ada_body(c_ref, w_ref, b_ref, o_ref):
    c = c_ref[...]
    s = c * jax.nn.sigmoid(c)
    o_ref[...] = _dot(s.astype(bf16), w_ref[...].astype(bf16)) + b_ref[...]


def _ada_mod(c_all, w_ada, b_ada):
    depth, d, n = w_ada.shape
    r = c_all.shape[0]
    tn = 1536
    return pl.pallas_call(
        _ada_body,
        out_shape=jax.ShapeDtypeStruct((depth, r, n), f32),
        grid=(depth, n // tn),
        in_specs=[pl.BlockSpec((r, d), lambda l, j: (0, 0)),
                  pl.BlockSpec((None, d, tn), lambda l, j: (l, 0, j)),
                  pl.BlockSpec((None, 1, tn), lambda l, j: (l, 0, j))],
        out_specs=pl.BlockSpec((None, r, tn), lambda l, j: (l, 0, j)),
        compiler_params=_cparams(("parallel", "parallel")),
        name="ada_mod",
    )(c_all, w_ada, b_ada.reshape(depth, 1, n))


def _nmm_body(x_ref, g_ref, sh_ref, sc_ref, w_ref, *o_refs, segs):
    x = x_ref[...]
    y = x * lax.rsqrt(jnp.mean(x * x, axis=-1, keepdims=True) + RMS_EPS)
    h = (y * g_ref[...]) * (1.0 + sc_ref[...]) + sh_ref[...]
    hb = h.astype(bf16)
    k = 0
    for off, width, scales in segs:
        outs = o_refs[k:k + len(scales)]
        k += len(scales)
        for c0 in range(0, width, 512):
            cw = min(512, width - c0)
            acc = _dot(hb, w_ref[:, off + c0:off + c0 + cw])
            for o, scale in zip(outs, scales):
                o[:, c0:c0 + cw] = (acc if scale == 1.0 else acc * scale).astype(o.dtype)


def _norm_mod_matmul(x, g, shift, scale, w_bf, segs, out_dtypes, tm, name):
    r, d = x.shape
    nb, rb, _ = shift.shape
    tiles_per_b = (r // nb) // tm
    out_shape, out_specs = [], []
    k = 0
    for off, width, scales in segs:
        for _ in scales:
            out_shape.append(jax.ShapeDtypeStruct((r, width), out_dtypes[k]))
            out_specs.append(pl.BlockSpec((tm, width), lambda i: (i, 0)))
            k += 1
    mod_spec = pl.BlockSpec((None, rb, d), lambda i: (i // tiles_per_b, 0, 0))
    return pl.pallas_call(
        functools.partial(_nmm_body, segs=segs),
        out_shape=out_shape,
        grid=(r // tm,),
        in_specs=[pl.BlockSpec((tm, d), lambda i: (i, 0)), _const_spec((1, d)), mod_spec, mod_spec,
                  _const_spec(w_bf.shape)],
        out_specs=out_specs,
        compiler_params=_cparams(("parallel",)),
        name=name,
    )(x, g.reshape(1, d), shift, scale, w_bf)


_IN_SEGS = ((_Q_N, 512, (0.125,)), (_NSA, 512, (1.0, 1.0)), (_WIN, 256, (1.0, 1.0)), (_MB_Q, 256, (0.125,)),
            (_MOBA, 512, (1.0, 1.0)), (_SB_Q, 256, (0.125,)), (_SB, 512, (1.0, 1.0)), (_G_M, 3072, (1.0,)),
            (_G_N, 128, (1.0,)))
_IN_DTYPES = (bf16, f32, bf16, f32, bf16, bf16, f32, bf16, bf16, f32, bf16, f32, f32)
_UP_SEGS = ((0, D_FF, (1.0,)), (D_FF, D_FF, (1.0,)))
_UP_DTYPES = (f32, f32)


def _cmp_core(rows_refs, pe_ref, w1_ref, w2_ref, nc):
    outs = []
    for kv in range(2):
        acc_a = jnp.zeros((nc, NSA_KV * CMP_HID), f32)
        acc_b = jnp.zeros((nc, NSA_KV * CMP_HID), f32)
        for r in range(CMP_STRIDE):
            y = rows_refs[kv][pl.ds(r, nc, stride=CMP_STRIDE), :]
            acc_a = acc_a + _dot((y + pe_ref[kv, r:r + 1, :]).astype(bf16), w1_ref[kv, r])
            acc_b = acc_b + _dot((y + pe_ref[kv, CMP_STRIDE + r:CMP_STRIDE + r + 1, :]).astype(bf16),
                                 w1_ref[kv, CMP_STRIDE + r])
        pre = acc_a + pltpu.roll(acc_b, nc - 1, 0)
        outs.append(_dot(_gelu_tanh(pre).astype(bf16), w2_ref[kv]))
    return jnp.concatenate(outs, axis=1)


def _cmp_prompt_body(k_ref, v_ref, pe_ref, w1_ref, w2_ref, o_ref, *, nc):
    o_ref[...] = _cmp_core((k_ref, v_ref), pe_ref, w1_ref, w2_ref, nc)


def _cmp_prompt(nsa_rows, pe2, w1bd, w2bd):
    b, l, _ = nsa_rows.shape
    nc = l // CMP_STRIDE
    return pl.pallas_call(
        functools.partial(_cmp_prompt_body, nc=nc),
        out_shape=jax.ShapeDtypeStruct((b, nc, 256), f32),
        grid=(b,),
        in_specs=[pl.BlockSpec((None, l, LANES), lambda i: (i, 0, 0)),
                  pl.BlockSpec((None, l, LANES), lambda i: (i, 0, 1)), _const_spec(pe2.shape),
                  _const_spec(w1bd.shape), _const_spec(w2bd.shape)],
        out_specs=pl.BlockSpec((None, nc, 256), lambda i: (i, 0, 0)),
        compiler_params=_cparams(("parallel",)),
        name="nsa_compress_prompt",
    )(nsa_rows, nsa_rows, pe2, w1bd, w2bd)


SLAB = 64
FLASH_SCRATCH = 2


class _Flash:
    def __init__(self, scratch, slopes, tq, t0):
        self.m, self.acc = scratch
        self.slope8 = jnp.concatenate(
            [jnp.full((1, 1), v, f32) for v in slopes] + [jnp.zeros((8 - len(slopes), 1), f32)], axis=0)
        self.tq, self.t0 = tq, t0
        self.m[...] = jnp.full(self.m.shape, M_INIT, f32)
        self.acc[...] = jnp.zeros(self.acc.shape, f32)

    def slab(self, j, s_all, bias, rel, masked):
        r0 = j * SLAB
        rows = slice(r0, r0 + SLAB)
        head = r0 // self.tq
        s = s_all[rows, :] + bias[head:head + 1, :]
        if masked:
            t_rel = (r0 + lax.broadcasted_iota(jnp.int32, (SLAB, 1), 0)) % self.tq
            s = jnp.where(rel <= t_rel, s, -MASK_BIG)
        m_old = self.m[rows, :]
        m_new = jnp.maximum(m_old, jnp.max(s, axis=-1, keepdims=True))
        self.m[rows, :] = m_new
        p = jnp.exp(s - jnp.concatenate([m_new] * (s.shape[1] // LANES), axis=1))
        return p.astype(bf16), jnp.exp(m_old - m_new)

    def result(self):
        acc = self.acc[...]
        return acc[:, 0:LANES] / acc[:, LANES:2 * LANES]


def _flash_tiles(chains, k0, masked):
    first = chains[0][0]
    rows = first.m.shape[0]
    tk = chains[0][2].shape[1]
    rel = k0 - first.t0 + lax.broadcasted_iota(jnp.int32, (1, tk), 1)
    logits = [(_dot(q_aug, kt_aug), f.slope8 * rel.astype(f32)) for f, q_aug, kt_aug, _ in chains]
    parts = [[] for _ in chains]
    for j in range(rows // SLAB):
        for c, (f, _, _, _) in enumerate(chains):
            parts[c].append(f.slab(j, logits[c][0], logits[c][1], rel, masked))
    for c, (f, _, _, v_aug) in enumerate(chains):
        p = jnp.concatenate([x[0] for x in parts[c]], axis=0)
        alpha = jnp.concatenate([x[1] for x in parts[c]], axis=0)
        f.acc[...] = jnp.concatenate([alpha, alpha], axis=1) * f.acc[...] + _dot(p, v_aug)


def _flash_scratch(rows):
    return [pltpu.VMEM((rows, LANES), f32), pltpu.VMEM((rows, 2 * LANES), f32)]


def _with_ones(v):
    return jnp.concatenate([v, jnp.ones(v.shape, v.dtype)], axis=-1)


def _masked_exp(s_all, slopes, rel, valid_fn, tq):
    out = []
    for j in range(s_all.shape[0] // SLAB):
        r0 = j * SLAB
        t_rel = r0 % tq + lax.broadcasted_iota(jnp.int32, (SLAB, 1), 0)
        valid = valid_fn(t_rel)
        s = jnp.where(valid, s_all[r0:r0 + SLAB, :] + slopes[r0 // tq] * rel.astype(f32), NEG)
        out.append(jnp.exp(s - jnp.maximum(jnp.max(s, axis=-1, keepdims=True), M_INIT)))
    return jnp.concatenate(out, axis=0)


def _topk_rows(score, ids, k, n_ids):
    picked = jnp.zeros(score.shape, f32)
    for _ in range(k):
        mx = jnp.max(score, axis=0, keepdims=True)
        idx = jnp.min(jnp.where(score == mx, ids, n_ids), axis=0, keepdims=True)
        pick = ids == idx
        picked = jnp.where(pick, 1.0, picked)
        score = jnp.where(pick, REMOVED, score)
    return picked


def _nsa_prompt_body(q_ref, gn_ref, kct_ref, vc_ref, agg_ref, kst_ref, oh_ref, vs_ref, kwt_ref, vw_ref, o_ref,
                     sel_scr, qa_scr, part_scr, gate_scr, *flash_scr, tq, tk, n_cmp):
    i = pl.program_id(1)
    t0 = i * tq
    nc = vc_ref.shape[0]
    qf = q_ref[...].astype(f32)
    lane = lax.broadcasted_iota(jnp.int32, (tq, LANES), 1)
    sig = jax.nn.sigmoid(gn_ref[...])
    kct = kct_ref[...]
    vc = vc_ref[...].astype(bf16)
    aggb = agg_ref[...].astype(bf16)
    n_id = lax.broadcasted_iota(jnp.int32, (1, nc), 1)
    cend_rel = n_id * CMP_STRIDE + (CMP_LEN - 1) - t0
    ones_c = jnp.ones((nc, LANES), bf16)
    blk_id = lax.broadcasted_iota(jnp.int32, (LANES, tq), 0)
    cur = (t0 + lax.broadcasted_iota(jnp.int32, (LANES, tq), 1)) // SEL_BLOCK
    causal_blk = blk_id <= cur
    forced = (blk_id == 0) | (blk_id == cur) | (blk_id == cur - 1)
    chunks = [jnp.zeros((tq, LANES), f32) for _ in range(NSA_HEADS // 2)]

    for g in range(NSA_KV):
        heads = [HPG * g + hh for hh in range(HPG)]
        pieces = []
        for h in heads:
            blk = qf[:, LANES * (h // 2):LANES * (h // 2 + 1)]
            if h % 2 != g:
                blk = pltpu.roll(blk, HEAD_DIM, 1)
            pieces.append(jnp.where(lane // HEAD_DIM == g, blk, 0.0))
        qg = jnp.concatenate(pieces, axis=0).astype(bf16)
        slopes = [NSA_SLOPES[h] for h in heads]

        valid_c = lambda t_rel: (cend_rel <= t_rel) & (n_id < n_cmp)
        e_c = _masked_exp(_dot(qg, kct), slopes, cend_rel, valid_c, tq)
        e_hi = e_c.astype(bf16)
        e_lo = (e_c - e_hi.astype(f32)).astype(bf16)
        r_hi = _dot(e_hi, jnp.concatenate([vc, ones_c, aggb], axis=1))
        r_lo = _dot(e_lo, jnp.concatenate([aggb, ones_c], axis=1))
        o_c = r_hi[:, 0:LANES] / jnp.maximum(r_hi[:, LANES:2 * LANES], 1e-30)
        imp_rows = (r_hi[:, 2 * LANES:3 * LANES] + r_lo[:, 0:LANES]) / jnp.maximum(
            r_hi[:, LANES:2 * LANES] + r_lo[:, LANES:2 * LANES], 1e-30)
        imp = imp_rows[0:tq] + imp_rows[tq:2 * tq] + imp_rows[2 * tq:3 * tq] + imp_rows[3 * tq:4 * tq]

        score = jnp.where(causal_blk, jnp.where(forced, FORCE, imp.T), NEG)
        sel_t = jnp.where(causal_blk, _topk_rows(score, blk_id, SEL_TOPN, LANES), 0.0)
        notsel = (1.0 - sel_t).T
        qa_scr[g] = jnp.concatenate([qg, jnp.concatenate([notsel] * HPG, axis=0).astype(bf16)], axis=1)
        sel_scr[g] = sel_t

        wl = WINDOW + tq
        s0 = pl.multiple_of(jnp.maximum(t0 - WINDOW, 0), tq)
        w_rel = s0 - t0 + lax.broadcasted_iota(jnp.int32, (1, wl), 1)
        valid_w = lambda t_rel: (w_rel <= t_rel) & (w_rel > t_rel - WINDOW)
        e_w = _masked_exp(_dot(qg, kwt_ref[:, pl.ds(s0, wl)]), slopes, w_rel, valid_w, tq)
        r_w = _dot(e_w.astype(bf16), _with_ones(vw_ref[pl.ds(s0, wl), :]))
        o_w = r_w[:, 0:LANES] / r_w[:, LANES:2 * LANES]

        gates = [jnp.concatenate([sig[:, br * NSA_HEADS + h:br * NSA_HEADS + h + 1] for h in heads], axis=0)
                 for br in range(3)]
        part_scr[g] = gates[0] * o_c + gates[2] * o_w
        gate_scr[g] = gates[1]

    flashes = [_Flash(flash_scr[FLASH_SCRATCH * g:FLASH_SCRATCH * (g + 1)],
                      NSA_SLOPES[HPG * g:HPG * (g + 1)], tq, t0) for g in range(NSA_KV)]

    def sel_step(kt, masked):
        k0 = pl.multiple_of(kt * tk, tk)
        kt_aug = jnp.concatenate([kst_ref[:, pl.ds(k0, tk)], oh_ref[:, pl.ds(k0, tk)]], axis=0)
        v_aug = _with_ones(vs_ref[pl.ds(k0, tk), :])
        _flash_tiles([(flashes[g], qa_scr[g], kt_aug, v_aug) for g in range(NSA_KV)], k0, masked)

    def sel_loop(kt, carry):
        blk0 = pl.multiple_of(kt * (tk // SEL_BLOCK), tk // SEL_BLOCK)
        picked = jnp.maximum(sel_scr[0, pl.ds(blk0, tk // SEL_BLOCK), :], sel_scr[1, pl.ds(blk0, tk // SEL_BLOCK), :])

        @pl.when(jnp.max(picked) > 0.0)
        def _():
            sel_step(kt, False)

        return carry

    kd = t0 // tk
    lax.fori_loop(0, kd, sel_loop, 0)
    sel_step(kd, True)

    for g in range(NSA_KV):
        heads = [HPG * g + hh for hh in range(HPG)]
        o = part_scr[g] + gate_scr[g] * flashes[g].result()
        for hh, h in enumerate(heads):
            piece = o[hh * tq:(hh + 1) * tq]
            if h % 2 != g:
                piece = pltpu.roll(piece, HEAD_DIM, 1)
            chunks[h // 2] = chunks[h // 2] + jnp.where(lane // HEAD_DIM == h % 2, piece, 0.0)

    o_ref[...] = jnp.concatenate(chunks, axis=1).astype(o_ref.dtype)


def _nsa_prompt(q_n, g_n, cmp, agg, nsa_bf, win_bf, n_cmp, tq, tk):
    b, l, _ = q_n.shape
    nc = cmp.shape[1]
    rows = HPG * tq
    kct = jnp.swapaxes(cmp[:, :, 0:LANES], 1, 2).astype(bf16)
    onehot_t = _block_onehot_t(SEL_BLOCK, l)
    kst = jnp.swapaxes(nsa_bf[:, :, 2 * LANES:3 * LANES], 1, 2)
    kwt = jnp.swapaxes(win_bf[:, :, 0:LANES], 1, 2)
    per_b = lambda shape, col=0: pl.BlockSpec((None,) + shape, lambda bi, i: (bi, 0, col))
    return pl.pallas_call(
        functools.partial(_nsa_prompt_body, tq=tq, tk=tk, n_cmp=n_cmp),
        out_shape=jax.ShapeDtypeStruct((b, l, NSA_HEADS * HEAD_DIM), bf16),
        grid=(b, l // tq),
        in_specs=[pl.BlockSpec((None, tq, 512), lambda bi, i: (bi, i, 0)),
                  pl.BlockSpec((None, tq, LANES), lambda bi, i: (bi, i, 0)),
                  per_b((LANES, nc)), per_b((nc, LANES), 1), _const_spec(agg.shape),
                  per_b((LANES, l)), _const_spec(onehot_t.shape), per_b((l, LANES), 3), per_b((LANES, l)),
                  per_b((l, LANES), 1)],
        out_specs=pl.BlockSpec((None, tq, 512), lambda bi, i: (bi, i, 0)),
        scratch_shapes=[pltpu.VMEM((NSA_KV, LANES, tq), f32), pltpu.VMEM((NSA_KV, rows, 2 * LANES), bf16),
                        pltpu.VMEM((NSA_KV, rows, LANES), f32), pltpu.VMEM((NSA_KV, rows, 1), f32)]
        + _flash_scratch(rows) * NSA_KV,
        compiler_params=_cparams(("parallel", "parallel")),
        name="nsa_attention_prompt",
    )(q_n, g_n, kct, cmp, agg, kst, onehot_t, nsa_bf, kwt, win_bf)


def _block_mean_body(k_ref, o_ref):
    o_ref[...] = jnp.mean(k_ref[...], axis=0, keepdims=True)


def _block_mean(rows, width, blk):
    b, l, _ = rows.shape
    return pl.pallas_call(
        _block_mean_body,
        out_shape=jax.ShapeDtypeStruct((b, l // blk, 1, width), f32),
        grid=(b, l // blk),
        in_specs=[pl.BlockSpec((None, blk, width), lambda bi, j: (bi, j, 0))],
        out_specs=pl.BlockSpec((None, None, 1, width), lambda bi, j: (bi, j, 0, 0)),
        compiler_params=_cparams(("parallel", "parallel")),
        name="moba_block_mean",
    )(rows)


def _moba_prompt_body(q_ref, kmt_ref, kt_ref, oh_ref, v_ref, o_ref, qa_scr, *flash_scr, tq, tk, nb_pad):
    i = pl.program_id(1)
    t0 = i * tq
    q = q_ref[...]
    qf = q.astype(f32)
    lane = lax.broadcasted_iota(jnp.int32, (tq, LANES), 1)

    gate_t = _nt(kmt_ref[...].astype(bf16), q)
    blk_id = lax.broadcasted_iota(jnp.int32, (LANES, tq), 0) % nb_pad
    own = (t0 + lax.broadcasted_iota(jnp.int32, (LANES, tq), 1)) // MOBA_BLOCK
    past = blk_id < own
    score = jnp.where(past, gate_t, NEG)
    parts = [_topk_rows(score[nb_pad * h:nb_pad * (h + 1)], blk_id[nb_pad * h:nb_pad * (h + 1)], MOBA_TOPK, nb_pad)
             for h in range(MOBA_HEADS)]
    sel_t = jnp.where(past, jnp.concatenate(parts, axis=0), 0.0)
    sel_t = jnp.where(blk_id == own, 1.0, sel_t)
    notsel = (1.0 - sel_t).T

    nch = MOBA_HEADS // 2
    for c in range(nch):
        q_rows = []
        for e in range(2):
            h = 2 * c + e
            qh = jnp.where(lane // HEAD_DIM == e, qf[:, LANES * c:LANES * (c + 1)], 0.0)
            ns = notsel if h == 0 else pltpu.roll(notsel, LANES - nb_pad * h, 1)
            ns = jnp.where(lane < nb_pad, ns, 0.0)
            q_rows.append(jnp.concatenate([qh, ns], axis=1))
        qa_scr[c] = jnp.concatenate(q_rows, axis=0).astype(bf16)
    flashes = [_Flash(flash_scr[FLASH_SCRATCH * c:FLASH_SCRATCH * (c + 1)],
                      MOBA_SLOPES[2 * c:2 * c + 2], tq, t0) for c in range(nch)]

    def step(kt, masked):
        k0 = pl.multiple_of(kt * tk, tk)
        onehot = oh_ref[:, pl.ds(k0, tk)]
        _flash_tiles([(flashes[c], qa_scr[c], jnp.concatenate([kt_ref[c, :, pl.ds(k0, tk)], onehot], axis=0),
                       _with_ones(v_ref[pl.ds(k0, tk), LANES * c:LANES * (c + 1)])) for c in range(nch)], k0, masked)

    def loop(kt, carry):
        step(kt, False)
        return carry

    lax.fori_loop(0, t0 // tk, loop, 0)
    step(t0 // tk, True)
    out_chunks = []
    for c in range(nch):
        o = flashes[c].result()
        out_chunks.append(jnp.where(lane < HEAD_DIM, o[0:tq], o[tq:2 * tq]))
    o_ref[...] = jnp.concatenate(out_chunks, axis=1).astype(o_ref.dtype)


def _moba_prompt(mb_q, kmt, moba_bf, tq, tk):
    b, l, _ = mb_q.shape
    nb_pad = LANES // MOBA_HEADS
    nch = MOBA_HEADS // 2
    onehot_t = _block_onehot_t(MOBA_BLOCK, l)
    k_t = jnp.swapaxes(jnp.swapaxes(moba_bf[:, :, 0:nch * LANES].reshape(b, l, nch, LANES), 1, 3), 1, 2)
    return pl.pallas_call(
        functools.partial(_moba_prompt_body, tq=tq, tk=tk, nb_pad=nb_pad),
        out_shape=jax.ShapeDtypeStruct((b, l, MOBA_HEADS * HEAD_DIM), bf16),
        grid=(b, l // tq),
        in_specs=[pl.BlockSpec((None, tq, 256), lambda bi, i: (bi, i, 0)),
                  pl.BlockSpec((None, LANES, 256), lambda bi, i: (bi, 0, 0)),
                  pl.BlockSpec((None, nch, LANES, l), lambda bi, i: (bi, 0, 0, 0)),
                  _const_spec(onehot_t.shape),
                  pl.BlockSpec((None, l, nch * LANES), lambda bi, i: (bi, 0, 1))],
        out_specs=pl.BlockSpec((None, tq, 256), lambda bi, i: (bi, i, 0)),
        scratch_shapes=[pltpu.VMEM((nch, 2 * tq, 2 * LANES), bf16)] + _flash_scratch(2 * tq) * nch,
        compiler_params=_cparams(("parallel", "parallel")),
        name="moba_attention_prompt",
    )(mb_q, kmt, k_t, onehot_t, moba_bf)


def _log_keep(z):
    return -(jnp.maximum(z, 0.0) + jnp.log(1.0 + jnp.exp(-jnp.abs(z))))


def _suffix_sums(lk, uu):
    hi = lk.astype(bf16)
    lo = (lk - hi.astype(f32)).astype(bf16)
    r = _dot(jnp.concatenate([hi, lo], axis=1), uu)
    return r[:, 0:LANES], r[:, LANES:2 * LANES]


def _sb_prompt_body(q_ref, uu_ref, k_ref, v_ref, o_ref, carry_scr, acc_scr, *, tq, tk):
    i = pl.program_id(2)
    t0 = i * tq
    qf = q_ref[...].astype(f32)
    lane = lax.broadcasted_iota(jnp.int32, (tq, LANES), 1)
    q2 = jnp.concatenate([jnp.where(lane // HEAD_DIM == e, qf, 0.0) for e in range(2)], axis=0).astype(bf16)
    t_row = t0 + lax.broadcasted_iota(jnp.int32, (2 * tq, 1), 0) % tq
    uu = uu_ref[...]
    carry_scr[...] = jnp.zeros(carry_scr.shape, f32)
    acc_scr[...] = jnp.zeros(acc_scr.shape, f32)

    def step(kt, diag):
        k0 = pl.multiple_of(kt * tk, tk)
        z = _dot(q2, k_ref[:, pl.ds(k0, tk)])
        lk = _log_keep(z)
        if diag:
            is_past = (k0 + lax.broadcasted_iota(jnp.int32, (1, tk), 1)) < t_row
            lk = jnp.where(is_past, lk, 0.0)
        carry = carry_scr[...]
        between = [None] * (tk // LANES)
        for c in reversed(range(tk // LANES)):
            later, total = _suffix_sums(lk[:, LANES * c:LANES * (c + 1)], uu)
            between[c] = later + carry
            carry = carry + total
        w = jnp.exp(z + lk + jnp.concatenate(between, axis=1))
        if diag:
            w = jnp.where(is_past, w, 0.0)
        acc_scr[...] = acc_scr[...] + _dot(w.astype(bf16), v_ref[pl.ds(k0, tk), :])
        carry_scr[...] = carry

    kd = t0 // tk
    step(kd, True)

    def more(state):
        j, top = state
        return (j < kd) & (top > SB_CUTOFF)

    def walk(state):
        j, _ = state
        step(kd - 1 - j, False)
        return j + 1, jnp.max(carry_scr[...])

    lax.while_loop(more, walk, (0, jnp.max(carry_scr[...])))
    o = acc_scr[...]
    o_ref[...] = jnp.where(lane < HEAD_DIM, o[0:tq], o[tq:2 * tq]).astype(o_ref.dtype)


def _sb_prompt(sb_q, uu, sb_bf, tq, tk):
    b, l, _ = sb_q.shape
    nch = SB_HEADS // 2
    k_t = jnp.swapaxes(jnp.swapaxes(sb_bf[:, :, 0:nch * LANES].reshape(b, l, nch, LANES), 1, 3), 1, 2)
    return pl.pallas_call(
        functools.partial(_sb_prompt_body, tq=tq, tk=tk),
        out_shape=jax.ShapeDtypeStruct((b, l, SB_HEADS * HEAD_DIM), bf16),
        grid=(b, nch, l // tq),
        in_specs=[pl.BlockSpec((None, tq, LANES), lambda bi, c, i: (bi, i, c)),
                  _const_spec(uu.shape),
                  pl.BlockSpec((None, None, LANES, l), lambda bi, c, i: (bi, c, 0, 0)),
                  pl.BlockSpec((None, l, LANES), lambda bi, c, i: (bi, 0, nch + c))],
        out_specs=pl.BlockSpec((None, tq, LANES), lambda bi, c, i: (bi, i, c)),
        scratch_shapes=[pltpu.VMEM((2 * tq, LANES), f32), pltpu.VMEM((2 * tq, LANES), f32)],
        compiler_params=_cparams(("parallel", "parallel", "parallel")),
        name="stickbreak_attention_prompt",
    )(sb_q, uu, k_t, sb_bf)


def _merge_body(oa_ref, ob_ref, oc_ref, gm_ref, x_ref, gate_ref, wa_ref, wb_ref, wc_ref, wo_ref, o_ref):
    d = x_ref.shape[1]
    g = jax.nn.sigmoid(gm_ref[...])
    merged = (g[:, 0:d] * _dot(oa_ref[...], wa_ref[...]) + g[:, d:2 * d] * _dot(ob_ref[...], wb_ref[...])
              + g[:, 2 * d:3 * d] * _dot(oc_ref[...], wc_ref[...]))
    o_ref[...] = x_ref[...] + gate_ref[...] * _dot(merged.astype(bf16), wo_ref[...])


def _merge_out(o_a, o_b, o_c, g_m, x, gate, wa, wb, wc, wo, tm):
    r, d = x.shape
    nb, rb, _ = gate.shape
    tiles_per_b = (r // nb) // tm
    row = lambda w: pl.BlockSpec((tm, w), lambda i: (i, 0))
    return pl.pallas_call(
        _merge_body,
        out_shape=jax.ShapeDtypeStruct((r, d), f32),
        grid=(r // tm,),
        in_specs=[row(o_a.shape[1]), row(o_b.shape[1]), row(o_c.shape[1]), row(3 * d), row(d),
                  pl.BlockSpec((None, rb, d), lambda i: (i // tiles_per_b, 0, 0)),
                  _const_spec(wa.shape), _const_spec(wb.shape), _const_spec(wc.shape), _const_spec(wo.shape)],
        out_specs=row(d),
        compiler_params=_cparams(("parallel",)),
        name="merge_out_proj",
    )(o_a, o_b, o_c, g_m, x, gate, wa, wb, wc, wo)


def _ffn_tail(a, a_m1, a_m2, b, cw_ref, cb_ref, wd_ref, x_ref, gate_ref, o_ref):
    conv = cb_ref[...] + a_m2 * cw_ref[0:1, :]
    conv = conv + a_m1 * cw_ref[1:2, :]
    conv = conv + a * cw_ref[2:3, :]
    y = _dot((_gelu_tanh(conv) * b).astype(bf16), wd_ref[...])
    o_ref[...] = x_ref[...] + gate_ref[...] * y


def _ffn_seq_body(a_ref, halo_ref, b_ref, cw_ref, cb_ref, wd_ref, x_ref, gate_ref, o_ref, *, tiles_per_b):
    a = a_ref[...]
    first = pl.program_id(0) % tiles_per_b == 0
    halo = jnp.where(first, 0.0, halo_ref[...])
    row = lax.broadcasted_iota(jnp.int32, a.shape, 0)
    a_m1 = jnp.where(row < 1, halo[7:8, :], pltpu.roll(a, 1, 0))
    a_m2 = jnp.where(row < 1, halo[6:7, :], jnp.where(row < 2, halo[7:8, :], pltpu.roll(a, 2, 0)))
    _ffn_tail(a, a_m1, a_m2, b_ref[...], cw_ref, cb_ref, wd_ref, x_ref, gate_ref, o_ref)


def _ffn_step_body(a_ref, am1_ref, am2_ref, b_ref, cw_ref, cb_ref, wd_ref, x_ref, gate_ref, o_ref):
    _ffn_tail(a_ref[...], am1_ref[...], am2_ref[...], b_ref[...], cw_ref, cb_ref, wd_ref, x_ref, gate_ref, o_ref)


def _ffn_down_seq(u_a, u_b, cw8, cb, wd, x, gate, tm):
    r, d = x.shape
    ff = u_a.shape[1]
    nb = gate.shape[0]
    tiles_per_b = (r // nb) // tm
    row = lambda w: pl.BlockSpec((tm, w), lambda i: (i, 0))
    return pl.pallas_call(
        functools.partial(_ffn_seq_body, tiles_per_b=tiles_per_b),
        out_shape=jax.ShapeDtypeStruct((r, d), f32),
        grid=(r // tm,),
        in_specs=[row(ff), pl.BlockSpec((8, ff), lambda i: (jnp.maximum(i * (tm // 8) - 1, 0), 0)), row(ff),
                  _const_spec(cw8.shape), _const_spec(cb.shape), _const_spec(wd.shape), row(d),
                  pl.BlockSpec((None, 1, d), lambda i: (i // tiles_per_b, 0, 0))],
        out_specs=row(d),
        compiler_params=_cparams(("parallel",)),
        name="conv_ffn_down_seq",
    )(u_a, u_a, u_b, cw8, cb, wd, x, gate)


def _ffn_down_step(u_a, a_m1, a_m2, u_b, cw8, cb, wd, x, gate):
    r, d = x.shape
    full = lambda a: pl.BlockSpec(a.shape, lambda i: (0,) * a.ndim)
    return pl.pallas_call(
        _ffn_step_body,
        out_shape=jax.ShapeDtypeStruct((r, d), f32),
        grid=(1,),
        in_specs=[full(u_a), full(a_m1), full(a_m2), full(u_b), full(cw8), full(cb), full(wd), full(x),
                  pl.BlockSpec((None, r, d), lambda i: (0, 0, 0))],
        out_specs=full(x),
        compiler_params=_cparams(("arbitrary",)),
        name="conv_ffn_down_step",
    )(u_a, a_m1, a_m2, u_b, cw8, cb, wd, x, gate)


def _final_norm_body(x_ref, g_ref, o_ref):
    x = x_ref[...]
    o_ref[...] = x * lax.rsqrt(jnp.mean(x * x, axis=-1, keepdims=True) + RMS_EPS) * g_ref[...]


def _final_norm(x, g, tm):
    r, d = x.shape
    return pl.pallas_call(
        _final_norm_body,
        out_shape=jax.ShapeDtypeStruct((r, d), f32),
        grid=(r // tm,),
        in_specs=[pl.BlockSpec((tm, d), lambda i: (i, 0)), _const_spec((1, d))],
        out_specs=pl.BlockSpec((tm, d), lambda i: (i, 0)),
        compiler_params=_cparams(("parallel",)),
        name="final_rmsnorm",
    )(x, g.reshape(1, d))


def _page_specs(n_pages, layer_base, per_step=1, k=0):
    return [pl.BlockSpec((None, 512, PAGE_SIZE),
                         functools.partial(lambda s, pt, j: (layer_base + pt[(s * per_step + k) * n_pages + j], 0, 0),
                                           j=j))
            for j in range(n_pages)]


def _per_seq(width):
    return pl.BlockSpec((None, 1, width), lambda s, pt: (s, 0, 0))


def _dec_const(shape):
    nd = len(shape)
    return pl.BlockSpec(shape, lambda s, pt: (0,) * nd, pipeline_mode=pl.Buffered(1))


def _head_rows(q_row):
    row = lax.broadcasted_iota(jnp.int32, (8, LANES), 0)
    lane = lax.broadcasted_iota(jnp.int32, (8, LANES), 1)
    q8 = jnp.broadcast_to(q_row, (8, q_row.shape[1]))
    qsel = jnp.zeros((8, LANES), f32)
    for c in range(NSA_HEADS // 2):
        qsel = qsel + jnp.where(row // 2 == c, q8[:, LANES * c:LANES * (c + 1)], 0.0)
    swap = (row % 2) != (row // HPG)
    qm = jnp.where(swap, pltpu.roll(qsel, HEAD_DIM, 1), qsel)
    return jnp.where(lane // HEAD_DIM == row // HPG, qm, 0.0), swap


def _rank_select(score_row, k):
    a = jnp.broadcast_to(score_row, (LANES, LANES))
    b = a.T
    ii = lax.broadcasted_iota(jnp.int32, (LANES, LANES), 0)
    jj = lax.broadcasted_iota(jnp.int32, (LANES, LANES), 1)
    ahead = (b > a) | ((b == a) & (ii < jj))
    rank = jnp.sum(jnp.where(ahead, 1.0, 0.0), axis=0, keepdims=True)
    return jnp.where(rank < k, 1.0, 0.0)


def _nsa_dec_body(pt_ref, q_ref, gn_ref, new_ref, wnew_ref, sw_ref, pe_ref, w1_ref, w2_ref, agg_ref, e_ref,
                  *rest, n_pages, per_step):
    pages, o_ref = rest[:n_pages * per_step], rest[n_pages * per_step]
    kc_scr, vc_scr = rest[n_pages * per_step + 1:]
    past = n_pages * PAGE_SIZE
    for j in range(n_pages * per_step):
        kc_scr[PAGE_SIZE * j:PAGE_SIZE * (j + 1), :] = pages[j][0:LANES, :].T
        vc_scr[PAGE_SIZE * j:PAGE_SIZE * (j + 1), :] = pages[j][LANES:2 * LANES, :].T

    nc = past // CMP_STRIDE
    cmp_all = _cmp_core((kc_scr, vc_scr), pe_ref, w1_ref, w2_ref, nc * per_step)
    for k in range(per_step):
        o_ref[k] = _nsa_dec_one(q_ref[k], gn_ref[k], new_ref[k], wnew_ref[k], sw_ref.at[k],
                                cmp_all[nc * k:nc * (k + 1)], pages[n_pages * k:n_pages * (k + 1)], agg_ref, e_ref)


def _nsa_dec_one(q_row, gn_row, new_row, wnew, sw_ref, cmpv, pages, agg_ref, e_ref):
    n_pages = len(pages)
    past = n_pages * PAGE_SIZE
    n_cmp = (past + 1 - CMP_LEN) // CMP_STRIDE + 1
    row = lax.broadcasted_iota(jnp.int32, (8, LANES), 0)
    lane = lax.broadcasted_iota(jnp.int32, (8, LANES), 1)
    qm, swap = _head_rows(q_row)
    qmb = qm.astype(bf16)
    slope = jnp.zeros((8, 1), f32)
    row1 = lax.broadcasted_iota(jnp.int32, (8, 1), 0)
    for h in range(NSA_HEADS):
        slope = jnp.where(row1 == h, NSA_SLOPES[h], slope)
    grp0 = row < HPG

    d_c = past - (lane[0:1] * CMP_STRIDE + CMP_LEN - 1)
    s_c = _nt(qmb, cmpv[:, 0:LANES].astype(bf16)) - slope * d_c.astype(f32)
    p_c = _ref_softmax(s_c, (d_c >= 0) & (lane[0:1] < n_cmp))
    o_c = _dot(p_c.astype(bf16), cmpv[:, LANES:2 * LANES].astype(bf16))
    aggb = agg_ref[...].astype(bf16)
    p_hi = p_c.astype(bf16)
    p_lo = (p_c - p_hi.astype(f32)).astype(bf16)
    imp_rows = _dot(p_hi, aggb) + _dot(p_lo, aggb)

    cur = past // SEL_BLOCK
    blk = lane[0:1]
    forced = (blk == 0) | (blk == cur) | (blk == cur - 1)
    causal = blk <= cur
    notsel_g = []
    for g in range(NSA_KV):
        imp = jnp.sum(imp_rows[HPG * g:HPG * (g + 1)], axis=0, keepdims=True)
        score = jnp.where(causal, jnp.where(forced, FORCE, imp), NEG)
        sel = jnp.where(causal, _rank_select(score, min(SEL_TOPN, -(-(past + 1) // SEL_BLOCK))), 0.0)
        notsel_g.append(jnp.broadcast_to(1.0 - sel, (8, LANES)))
    notsel = jnp.where(grp0, notsel_g[0], notsel_g[1])

    dist = past - lax.broadcasted_iota(jnp.int32, (1, past), 1)
    picked = _dot(notsel.astype(bf16), e_ref[...]) < 0.5
    ks_t = jnp.concatenate([pages[j][2 * LANES:3 * LANES, :].astype(bf16) for j in range(n_pages)], axis=1)
    vs_t = jnp.concatenate([pages[j][3 * LANES:4 * LANES, :].astype(bf16) for j in range(n_pages)], axis=1)
    s_s = jnp.where(picked, _dot(qmb, ks_t) - slope * dist.astype(f32), NEG)
    s_n = jnp.sum(qm * new_row[:, 2 * LANES:3 * LANES], axis=-1, keepdims=True)
    m_s = jnp.maximum(jnp.max(s_s, axis=-1, keepdims=True), s_n)
    e_s = jnp.where(picked, jnp.exp(s_s - m_s), 0.0)
    e_n = jnp.exp(s_n - m_s)
    den = jnp.maximum(jnp.sum(e_s, axis=-1, keepdims=True) + e_n, 1e-30)
    o_s = (e_n * new_row[:, 3 * LANES:4 * LANES] + _nt(e_s.astype(bf16), vs_t)) / den

    wb = sw_ref.shape[1]
    d_w = wb - lax.broadcasted_iota(jnp.int32, (1, wb), 1)
    s_w = _dot(qmb, sw_ref[0:LANES, :].astype(bf16)) - slope * d_w.astype(f32)
    valid_w = (d_w < WINDOW) & (d_w >= 0)
    s_w = jnp.where(valid_w, s_w, NEG)
    s_n = jnp.sum(qm * wnew[:, 0:LANES], axis=-1, keepdims=True)
    m_w = jnp.maximum(jnp.max(s_w, axis=-1, keepdims=True), s_n)
    e_w = jnp.where(valid_w, jnp.exp(s_w - m_w), 0.0)
    e_n = jnp.exp(s_n - m_w)
    den = jnp.maximum(jnp.sum(e_w, axis=-1, keepdims=True) + e_n, 1e-30)
    o_w = (_nt(e_w.astype(bf16), sw_ref[LANES:2 * LANES, :].astype(bf16)) + e_n * wnew[:, LANES:2 * LANES]) / den

    sig = jnp.broadcast_to(jax.nn.sigmoid(gn_row), (8, LANES))
    gates = [jnp.sum(jnp.where(lane == br * NSA_HEADS + row, sig, 0.0), axis=-1, keepdims=True) for br in range(3)]
    o = gates[0] * o_c + gates[1] * o_s + gates[2] * o_w
    o = jnp.where(swap, pltpu.roll(o, HEAD_DIM, 1), o)
    o = jnp.where(lane // HEAD_DIM == row % 2, o, 0.0)
    return jnp.concatenate([o[2 * c:2 * c + 1] + o[2 * c + 1:2 * c + 2] for c in range(NSA_HEADS // 2)], axis=1)


def _nsa_decode(page_table, cache_t, layer_base, q, g_n, nsa_new, win_new, state_win_t, win_base, pe4, w1bd, w2bd,
                agg, e_sel, per_step):
    s, n_pages = page_table.shape
    past = n_pages * PAGE_SIZE
    consts = (pe4, w1bd, w2bd, agg, e_sel)
    seqs = lambda width: pl.BlockSpec((per_step, 1, width), lambda si, pt: (si, 0, 0))
    page_specs = [spec for k in range(per_step) for spec in _page_specs(n_pages, layer_base, per_step, k)]
    grid_spec = pltpu.PrefetchScalarGridSpec(
        num_scalar_prefetch=1,
        grid=(s // per_step,),
        in_specs=[seqs(512), seqs(LANES), seqs(512), seqs(256),
                  pl.BlockSpec((per_step,) + state_win_t.shape[1:],
                               lambda si, pt: (win_base // per_step + si, 0, 0))]
        + [_dec_const(c.shape) for c in consts] + page_specs,
        out_specs=seqs(512),
        scratch_shapes=[pltpu.VMEM((per_step * past, LANES), f32), pltpu.VMEM((per_step * past, LANES), f32)],
    )
    return pl.pallas_call(
        functools.partial(_nsa_dec_body, n_pages=n_pages, per_step=per_step),
        out_shape=jax.ShapeDtypeStruct((s, 1, 512), f32),
        grid_spec=grid_spec,
        compiler_params=_cparams(("parallel",)),
        name="nsa_attention_decode",
    )(page_table.reshape(-1), q, g_n, nsa_new, win_new, state_win_t, *consts,
      *([cache_t] * (n_pages * per_step)))


def _win_shift_body(sw_ref, new_ref, o_ref):
    n, feat, wb = sw_ref.shape
    lane = lax.broadcasted_iota(jnp.int32, (feat, wb), 1)
    for k in range(n):
        col = jnp.broadcast_to(new_ref[k], (LANES, feat)).T
        col = jnp.concatenate([col] * (wb // LANES), axis=1)
        o_ref[k] = jnp.where(lane == wb - 1, col, pltpu.roll(sw_ref[k], wb - 1, 1))


def _win_shift(state_win_t, new_rows, per_step):
    n, feat, wb = state_win_t.shape
    return pl.pallas_call(
        _win_shift_body,
        out_shape=jax.ShapeDtypeStruct((n, feat, wb), f32),
        grid=(n // per_step,),
        in_specs=[pl.BlockSpec((per_step, feat, wb), lambda i: (i, 0, 0)),
                  pl.BlockSpec((per_step, 1, feat), lambda i: (i, 0, 0))],
        out_specs=pl.BlockSpec((per_step, feat, wb), lambda i: (i, 0, 0)),
        compiler_params=_cparams(("parallel",)),
        name="window_state_shift",
    )(state_win_t, new_rows)


def _q_head_rows4(q_row):
    row = lax.broadcasted_iota(jnp.int32, (8, 256), 0)
    lane = lax.broadcasted_iota(jnp.int32, (8, 256), 1)
    own = lane // HEAD_DIM == row
    return jnp.where(own, jnp.broadcast_to(q_row, (8, 256)), 0.0), own


def _moba_dec_body(pt_ref, q_ref, new_ref, seg_ref, e_ref, *rest, n_pages):
    pages, o_ref = rest[:n_pages], rest[n_pages]
    past = n_pages * PAGE_SIZE
    nb_past = past // MOBA_BLOCK
    kw = MOBA_HEADS * HEAD_DIM
    qm, own_lanes = _q_head_rows4(q_ref[...])
    qmb = qm.astype(bf16)
    k_hi, k_lo = [], []
    for j in range(n_pages):
        k = pages[j][0:kw, :]
        k_hi.append(k.astype(bf16))
        k_lo.append((k - k_hi[j].astype(f32)).astype(bf16))
    k_t = jnp.concatenate(k_hi, axis=1)
    v_t = jnp.concatenate([pages[j][kw:2 * kw, :].astype(bf16) for j in range(n_pages)], axis=1)
    raw = _dot(qmb, k_t)
    km_t = _dot(k_t, seg_ref[...]) + _dot(jnp.concatenate(k_lo, axis=1), seg_ref[...])
    gate = _dot(qmb, km_t.astype(bf16))

    lane = lax.broadcasted_iota(jnp.int32, (8, LANES), 1)
    is_past = lane < nb_past
    score = jnp.where(is_past, gate, NEG)
    rank = jnp.zeros((8, LANES), f32)
    for i in range(nb_past):
        gi = score[:, i:i + 1]
        rank = rank + jnp.where((gi > score) | ((gi == score) & (i < lane)), 1.0, 0.0)
    sel = jnp.where(is_past & (rank < min(MOBA_TOPK, nb_past)), 1.0, 0.0)
    picked = _dot(sel.astype(bf16), e_ref[...]) > 0.5

    row1 = lax.broadcasted_iota(jnp.int32, (8, 1), 0)
    slope = jnp.zeros((8, 1), f32)
    for h in range(MOBA_HEADS):
        slope = jnp.where(row1 == h, MOBA_SLOPES[h], slope)
    dist = past - lax.broadcasted_iota(jnp.int32, (1, past), 1)
    new_row = new_ref[...]
    s = jnp.where(picked, raw - slope * dist.astype(f32), NEG)
    s_n = jnp.sum(qm * new_row[:, 0:kw], axis=-1, keepdims=True)
    m = jnp.maximum(jnp.max(s, axis=-1, keepdims=True), s_n)
    e = jnp.where(picked, jnp.exp(s - m), 0.0)
    e_n = jnp.exp(s_n - m)
    den = jnp.maximum(jnp.sum(e, axis=-1, keepdims=True) + e_n, 1e-30)
    o = (e_n * new_row[:, kw:2 * kw] + _nt(e.astype(bf16), v_t)) / den
    o_ref[...] = jnp.sum(jnp.where(own_lanes, o, 0.0), axis=0, keepdims=True)


def _moba_decode(page_table, cache_t, layer_base, q, moba_new, seg_mean, e_blk):
    s, n_pages = page_table.shape
    grid_spec = pltpu.PrefetchScalarGridSpec(
        num_scalar_prefetch=1,
        grid=(s,),
        in_specs=[_per_seq(256), _per_seq(512), _dec_const(seg_mean.shape), _dec_const(e_blk.shape)]
        + _page_specs(n_pages, layer_base),
        out_specs=_per_seq(256),
    )
    return pl.pallas_call(
        functools.partial(_moba_dec_body, n_pages=n_pages),
        out_shape=jax.ShapeDtypeStruct((s, 1, 256), f32),
        grid_spec=grid_spec,
        compiler_params=_cparams(("parallel",)),
        name="moba_attention_decode",
    )(page_table.reshape(-1), q, moba_new, seg_mean, e_blk, *([cache_t] * n_pages))


def _sb_dec_body(pt_ref, q_ref, uu_ref, *rest, n_pages):
    pages, o_ref = rest[:n_pages], rest[n_pages]
    kw = SB_HEADS * HEAD_DIM
    qm, own_lanes = _q_head_rows4(q_ref[...])
    qmb = qm.astype(bf16)
    k_t = jnp.concatenate([pages[j][0:kw, :].astype(bf16) for j in range(n_pages)], axis=1)
    v_t = jnp.concatenate([pages[j][kw:2 * kw, :].astype(bf16) for j in range(n_pages)], axis=1)
    z = _dot(qmb, k_t)
    lk = _log_keep(z)
    stacked = jnp.concatenate([lk[:, PAGE_SIZE * j:PAGE_SIZE * (j + 1)] for j in range(n_pages)], axis=0)
    later, total = _suffix_sums(stacked, uu_ref[...])
    carry = jnp.zeros((8, LANES), f32)
    between = [None] * n_pages
    for j in reversed(range(n_pages)):
        between[j] = later[8 * j:8 * (j + 1)] + carry
        carry = carry + total[8 * j:8 * (j + 1)]
    w = jnp.exp(z + lk + jnp.concatenate(between, axis=1))
    acc = _nt(w.astype(bf16), v_t)
    o_ref[...] = jnp.sum(jnp.where(own_lanes, acc, 0.0), axis=0, keepdims=True)


def _sb_decode(page_table, cache_t, layer_base, q, uu):
    s, n_pages = page_table.shape
    grid_spec = pltpu.PrefetchScalarGridSpec(
        num_scalar_prefetch=1,
        grid=(s,),
        in_specs=[_per_seq(256), _dec_const(uu.shape)] + _page_specs(n_pages, layer_base),
        out_specs=_per_seq(256),
    )
    return pl.pallas_call(
        functools.partial(_sb_dec_body, n_pages=n_pages),
        out_shape=jax.ShapeDtypeStruct((s, 1, 256), f32),
        grid_spec=grid_spec,
        compiler_params=_cparams(("parallel",)),
        name="stickbreak_attention_decode",
    )(page_table.reshape(-1), q, uu, *([cache_t] * n_pages))


def _agg_matrix(nc, n_cmp):
    c0 = np.arange(nc)[:, None] * CMP_STRIDE
    s0 = np.arange(LANES)[None, :] * SEL_BLOCK
    ov = np.clip(np.minimum(c0 + CMP_LEN, s0 + SEL_BLOCK) - np.maximum(c0, s0), 0, None) / CMP_LEN
    ov[n_cmp:] = 0.0
    return jnp.asarray(ov, f32)


def _block_onehot_t(block, l):
    e = (np.arange(l)[None, :] // block) == np.arange(LANES)[:, None]
    return jnp.asarray(np.where(e, -MASK_BIG, 0.0), bf16)


def _expand_matrix(block, kp):
    e = (np.arange(kp)[None, :] // block) == np.arange(LANES)[:, None]
    return jnp.asarray(e, bf16)


def _suffix_matrix():
    j = np.arange(2 * LANES)[:, None] % LANES
    s = np.arange(2 * LANES)[None, :]
    return jnp.asarray((s >= LANES) | (j > s), bf16)


def _layer_weights(l, w_in, cmp_pe, cmp_w1, cmp_w2, w_br_a, w_br_b, w_br_c, w_o, w_up, conv_w, conv_b, w_down):
    d = w_in.shape[1]
    w = w_in[l]
    w_proj = jnp.concatenate([w[:, 0:1280], w[:, 1304:], w[:, 1280:1304], jnp.zeros((d, _PROJ_W - 5912), f32)],
                             axis=1).astype(bf16)
    pe4 = jnp.concatenate([cmp_pe[l], cmp_pe[l]], axis=2)
    w1 = cmp_w1[l].reshape(2, CMP_LEN, HEAD_DIM, CMP_HID)
    w1bd = jnp.zeros((2, CMP_LEN, NSA_KV * HEAD_DIM, NSA_KV * CMP_HID), f32)
    w2bd = jnp.zeros((2, NSA_KV * CMP_HID, NSA_KV * HEAD_DIM), f32)
    for g in range(NSA_KV):
        w1bd = w1bd.at[:, :, HEAD_DIM * g:HEAD_DIM * (g + 1), CMP_HID * g:CMP_HID * (g + 1)].set(w1)
        w2bd = w2bd.at[:, CMP_HID * g:CMP_HID * (g + 1), HEAD_DIM * g:HEAD_DIM * (g + 1)].set(cmp_w2[l])
    cw8 = jnp.concatenate([conv_w[l], jnp.zeros((8 - CONV_W, conv_w.shape[2]), f32)], axis=0)
    return dict(w_proj=w_proj, pe4=pe4, w1bd=w1bd.astype(bf16), w2bd=w2bd.astype(bf16),
                wa=w_br_a[l].astype(bf16), wb=w_br_b[l].astype(bf16), wc=w_br_c[l].astype(bf16),
                wo=w_o[l].astype(bf16), w_up=w_up[l].astype(bf16), cw8=cw8, cb=conv_b[l].reshape(1, -1),
                wd=w_down[l].astype(bf16))


def _mod_parts(mod_rows, per_row):
    r = mod_rows.shape[0]
    parts = mod_rows.reshape(r, 6, D_MODEL)
    return [parts[:, k].reshape((1, r, D_MODEL) if per_row else (r, 1, D_MODEL)) for k in range(6)]


def kernel(x_prompt, x_sample, cache_nsa, cache_moba, cache_sb, state_win, state_conv, page_table, c_prompt,
           c_sample, norm1_g, norm2_g, w_ada, b_ada, w_in, cmp_pe, cmp_w1, cmp_w2, w_br_a, w_br_b, w_br_c, w_o,
           w_up, conv_w, conv_b, w_down, final_g):
    b, t, d = x_prompt.shape
    s = x_sample.shape[0]
    depth = w_in.shape[0]
    n_phys = cache_nsa.shape[1]
    n_pages = page_table.shape[1]
    past = n_pages * PAGE_SIZE
    tm = 512
    tq = 128
    tk = 512

    n_c = b + s
    c_all = jnp.concatenate([c_prompt, c_sample, jnp.zeros((-n_c % 8, d), f32)], axis=0)
    mod = _ada_mod(c_all, w_ada, b_ada)

    nc_p = t // CMP_STRIDE
    n_cmp_p = (t - CMP_LEN) // CMP_STRIDE + 1
    agg_p = _agg_matrix(nc_p, n_cmp_p)
    nc_s = past // CMP_STRIDE
    agg_s = _agg_matrix(nc_s, (past + 1 - CMP_LEN) // CMP_STRIDE + 1)
    e_sel = _expand_matrix(SEL_BLOCK, past)
    e_blk = _expand_matrix(MOBA_BLOCK, past)
    seg_mean = (jnp.swapaxes(e_blk, 0, 1).astype(f32) * (1.0 / MOBA_BLOCK)).astype(bf16)
    uu = _suffix_matrix()
    nb_pad = LANES // MOBA_HEADS

    feat_major = lambda a: jnp.transpose(a, (0, 1, 3, 4, 5, 2))
    caches = [feat_major(c).reshape(depth * n_phys, 512, PAGE_SIZE) for c in (cache_nsa, cache_moba, cache_sb)]
    state_win_t = feat_major(state_win).reshape(depth * s, 256, state_win.shape[2])

    xp = x_prompt.reshape(b * t, d)
    xs = x_sample.reshape(s, d)
    outs_p = [[] for _ in range(5)]
    outs_s = [[] for _ in range(5)]
    for l in range(depth):
        w = _layer_weights(l, w_in, cmp_pe, cmp_w1, cmp_w2, w_br_a, w_br_b, w_br_c, w_o, w_up, conv_w, conv_b,
                           w_down)
        mp = _mod_parts(mod[l, 0:b], per_row=False)
        ms = _mod_parts(mod[l, b:b + s], per_row=True)

        (q_n, nsa_rows, nsa_bf, win_rows, win_bf, mb_q, moba_rows, moba_bf, sb_q, sb_rows, sb_bf, g_m, g_n) = \
            _norm_mod_matmul(xp, norm1_g[l], mp[0], mp[1], w["w_proj"], _IN_SEGS, _IN_DTYPES, tm, "in_proj_prompt")
        r3 = lambda a: a.reshape(b, t, a.shape[1])
        cmp = _cmp_prompt(r3(nsa_rows), w["pe4"], w["w1bd"], w["w2bd"])
        o_a = _nsa_prompt(r3(q_n), r3(g_n), cmp, agg_p, r3(nsa_bf), r3(win_bf), n_cmp_p, tq, tk)
        kmean = _block_mean(r3(moba_rows), 256, MOBA_BLOCK).reshape(b, t // MOBA_BLOCK, MOBA_HEADS, HEAD_DIM)
        kmt = jnp.zeros((b, MOBA_HEADS, nb_pad, MOBA_HEADS, HEAD_DIM), f32)
        for h in range(MOBA_HEADS):
            kmt = kmt.at[:, h, 0:t // MOBA_BLOCK, h].set(kmean[:, :, h])
        o_b = _moba_prompt(r3(mb_q), kmt.reshape(b, LANES, 256), r3(moba_bf), tq, tk)
        o_c = _sb_prompt(r3(sb_q), uu, r3(sb_bf), 2 * tq, tk)
        xp = _merge_out(o_a.reshape(b * t, -1), o_b.reshape(b * t, -1), o_c.reshape(b * t, -1), g_m, xp, mp[2],
                        w["wa"], w["wb"], w["wc"], w["wo"], tm)
        u_a, u_b = _norm_mod_matmul(xp, norm2_g[l], mp[3], mp[4], w["w_up"], _UP_SEGS, _UP_DTYPES, tm,
                                    "ffn_up_prompt")
        xp = _ffn_down_seq(u_a, u_b, w["cw8"], w["cb"], w["wd"], xp, mp[5], tm)
        keep = min(WINDOW, t)
        outs_p[0].append(nsa_rows.reshape(b, t // PAGE_SIZE, PAGE_SIZE, 4, NSA_KV, HEAD_DIM))
        outs_p[1].append(moba_rows.reshape(b, t // PAGE_SIZE, PAGE_SIZE, 2, MOBA_HEADS, HEAD_DIM))
        outs_p[2].append(sb_rows.reshape(b, t // PAGE_SIZE, PAGE_SIZE, 2, SB_HEADS, HEAD_DIM))
        outs_p[3].append(r3(win_rows)[:, t - keep:].reshape(b, keep, 2, NSA_KV, HEAD_DIM))
        outs_p[4].append(r3(u_a)[:, t - (CONV_W - 1):])

        (q_n, nsa_rows, _, win_rows, _, mb_q, moba_rows, _, sb_q, sb_rows, _, g_m, g_n) = \
            _norm_mod_matmul(xs, norm1_g[l], ms[0], ms[1], w["w_proj"], _IN_SEGS, _IN_DTYPES, s, "in_proj_sample")
        s3 = lambda a: a.astype(f32).reshape(s, 1, a.shape[1])
        base = l * n_phys
        o_a = _nsa_decode(page_table, caches[0], base, s3(q_n), s3(g_n), s3(nsa_rows), s3(win_rows), state_win_t,
                          l * s, w["pe4"], w["w1bd"], w["w2bd"], agg_s, e_sel, 2)
        o_b = _moba_decode(page_table, caches[1], base, s3(mb_q), s3(moba_rows), seg_mean, e_blk)
        o_c = _sb_decode(page_table, caches[2], base, s3(sb_q), uu)
        xs = _merge_out(o_a.reshape(s, -1).astype(bf16), o_b.reshape(s, -1).astype(bf16),
                        o_c.reshape(s, -1).astype(bf16), g_m, xs, ms[2], w["wa"], w["wb"], w["wc"], w["wo"], s)
        u_a, u_b = _norm_mod_matmul(xs, norm2_g[l], ms[3], ms[4], w["w_up"], _UP_SEGS, _UP_DTYPES, s,
                                    "ffn_up_sample")
        xs = _ffn_down_step(u_a, state_conv[l, :, 1], state_conv[l, :, 0], u_b, w["cw8"], w["cb"], w["wd"], xs,
                            ms[5])
        conv_full = jnp.concatenate([state_conv[l], u_a.reshape(s, 1, -1)], axis=1)
        outs_s[0].append(nsa_rows.reshape(s, 1, 4, NSA_KV, HEAD_DIM))
        outs_s[1].append(moba_rows.reshape(s, 1, 2, MOBA_HEADS, HEAD_DIM))
        outs_s[2].append(sb_rows.reshape(s, 1, 2, SB_HEADS, HEAD_DIM))
        outs_s[3].append(win_rows.reshape(s, 1, 256))
        outs_s[4].append(conv_full[:, conv_full.shape[1] - (CONV_W - 1):])

    y_prompt = _final_norm(xp, final_g, tm).reshape(b, t, d)
    y_sample = _final_norm(xs, final_g, s).reshape(s, 1, d)
    wb = state_win.shape[2]
    win_s = _win_shift(state_win_t, jnp.concatenate(outs_s[3], axis=0), 4)
    win_s = jnp.transpose(win_s.reshape(depth, s, 2, NSA_KV, HEAD_DIM, wb), (0, 1, 5, 2, 3, 4))
    st = lambda lst: jnp.stack(lst)
    return (y_prompt, y_sample, st(outs_p[0]), st(outs_s[0]), st(outs_p[1]), st(outs_s[1]), st(outs_p[2]),
            st(outs_s[2]), st(outs_p[3]), win_s, st(outs_p[4]), st(outs_s[4]))
```

```python
import functools

import numpy as np
import jax
import jax.numpy as jnp
from jax import lax
from jax.experimental import pallas as pl
from jax.experimental.pallas import tpu as pltpu

f32 = jnp.float32
bf16 = jnp.bfloat16

D_MODEL = 1024
HEAD_DIM = 64
NSA_HEADS = 8
NSA_KV = 2
HPG = NSA_HEADS // NSA_KV
CMP_LEN = 32
CMP_STRIDE = 16
CMP_HID = 128
SEL_BLOCK = 64
SEL_TOPN = 16
WINDOW = 512
MOBA_HEADS = 4
MOBA_BLOCK = 256
MOBA_TOPK = 3
SB_HEADS = 4
D_FF = 2816
CONV_W = 3
PAGE_SIZE = 128
RMS_EPS = 1e-6
NEG = -1e30
FORCE = 1e9

LANES = 128
MASK_BIG = 2.0 ** 100
M_INIT = -1e29
REMOVED = -3e38
SB_CUTOFF = -110.0
VMEM_LIMIT_MB = 56

NSA_SLOPES = [2.0 ** (-8.0 * (h + 1) / NSA_HEADS) for h in range(NSA_HEADS)]
MOBA_SLOPES = [2.0 ** (-8.0 * (h + 1) / MOBA_HEADS) for h in range(MOBA_HEADS)]

_Q_N, _NSA, _WIN, _MB_Q, _MOBA, _SB_Q, _SB, _G_M, _G_N, _PROJ_W = 0, 512, 1024, 1280, 1536, 2048, 2304, 2816, 5888, 6016


def _cparams(sem, vmem_mb=VMEM_LIMIT_MB):
    return pltpu.CompilerParams(dimension_semantics=sem, vmem_limit_bytes=vmem_mb * 2 ** 20)


def _const_spec(shape):
    nd = len(shape)
    return pl.BlockSpec(shape, lambda *_: (0,) * nd, pipeline_mode=pl.Buffered(1))


def _nt(a, b):
    return lax.dot_general(a, b, (((1,), (1,)), ((), ())), preferred_element_type=f32)


def _dot(a, b):
    return jnp.dot(a, b, preferred_element_type=f32)


def _gelu_tanh(x):
    return x * (0.5 * (1.0 + jnp.tanh(np.sqrt(2.0 / np.pi) * (x + 0.044715 * (x * x * x)))))


def _ref_softmax(s, valid):
    s = jnp.where(valid, s, NEG)
    e = jnp.where(valid, jnp.exp(s - jnp.max(s, axis=-1, keepdims=True)), 0.0)
    return e / jnp.maximum(jnp.sum(e, axis=-1, keepdims=True), 1e-30)


def _ada_body(c_ref, w_ref, b_ref, o_ref):
    c = c_ref[...]
    s = c * jax.nn.sigmoid(c)
    o_ref[...] = _dot(s.astype(bf16), w_ref[...].astype(bf16)) + b_ref[...]


def _ada_mod(c_all, w_ada, b_ada):
    depth, d, n = w_ada.shape
    r = c_all.shape[0]
    tn = 1536
    return pl.pallas_call(
        _ada_body,
        out_shape=jax.ShapeDtypeStruct((depth, r, n), f32),
        grid=(depth, n // tn),
        in_specs=[pl.BlockSpec((r, d), lambda l, j: (0, 0)),
                  pl.BlockSpec((None, d, tn), lambda l, j: (l, 0, j)),
                  pl.BlockSpec((None, 1, tn), lambda l, j: (l, 0, j))],
        out_specs=pl.BlockSpec((None, r, tn), lambda l, j: (l, 0, j)),
        compiler_params=_cparams(("parallel", "parallel")),
        name="ada_mod",
    )(c_all, w_ada, b_ada.reshape(depth, 1, n))


def _nmm_body(x_ref, g_ref, sh_ref, sc_ref, w_ref, *o_refs, segs):
    x = x_ref[...]
    y = x * lax.rsqrt(jnp.mean(x * x, axis=-1, keepdims=True) + RMS_EPS)
    h = (y * g_ref[...]) * (1.0 + sc_ref[...]) + sh_ref[...]
    hb = h.astype(bf16)
    k = 0
    for off, width, scales in segs:
        outs = o_refs[k:k + len(scales)]
        k += len(scales)
        for c0 in range(0, width, 512):
            cw = min(512, width - c0)
            acc = _dot(hb, w_ref[:, off + c0:off + c0 + cw])
            for o, scale in zip(outs, scales):
                o[:, c0:c0 + cw] = (acc if scale == 1.0 else acc * scale).astype(o.dtype)


def _norm_mod_matmul(x, g, shift, scale, w_bf, segs, out_dtypes, tm, name):
    r, d = x.shape
    nb, rb, _ = shift.shape
    tiles_per_b = (r // nb) // tm
    out_shape, out_specs = [], []
    k = 0
    for off, width, scales in segs:
        for _ in scales:
            out_shape.append(jax.ShapeDtypeStruct((r, width), out_dtypes[k]))
            out_specs.append(pl.BlockSpec((tm, width), lambda i: (i, 0)))
            k += 1
    mod_spec = pl.BlockSpec((None, rb, d), lambda i: (i // tiles_per_b, 0, 0))
    return pl.pallas_call(
        functools.partial(_nmm_body, segs=segs),
        out_shape=out_shape,
        grid=(r // tm,),
        in_specs=[pl.BlockSpec((tm, d), lambda i: (i, 0)), _const_spec((1, d)), mod_spec, mod_spec,
                  _const_spec(w_bf.shape)],
        out_specs=out_specs,
        compiler_params=_cparams(("parallel",)),
        name=name,
    )(x, g.reshape(1, d), shift, scale, w_bf)


_IN_SEGS = ((_Q_N, 512, (0.125,)), (_NSA, 512, (1.0, 1.0)), (_WIN, 256, (1.0, 1.0)), (_MB_Q, 256, (0.125,)),
            (_MOBA, 512, (1.0, 1.0)), (_SB_Q, 256, (0.125,)), (_SB, 512, (1.0, 1.0)), (_G_M, 3072, (1.0,)),
            (_G_N, 128, (1.0,)))
_IN_DTYPES = (bf16, f32, bf16, f32, bf16, bf16, f32, bf16, bf16, f32, bf16, f32, f32)
_UP_SEGS = ((0, D_FF, (1.0,)), (D_FF, D_FF, (1.0,)))
_UP_DTYPES = (f32, f32)


def _cmp_core(rows_refs, pe_ref, w1_ref, w2_ref, nc):
    outs = []
    for kv in range(2):
        acc_a = jnp.zeros((nc, NSA_KV * CMP_HID), f32)
        acc_b = jnp.zeros((nc, NSA_KV * CMP_HID), f32)
        for r in range(CMP_STRIDE):
            y = rows_refs[kv][pl.ds(r, nc, stride=CMP_STRIDE), :]
            acc_a = acc_a + _dot((y + pe_ref[kv, r:r + 1, :]).astype(bf16), w1_ref[kv, r])
            acc_b = acc_b + _dot((y + pe_ref[kv, CMP_STRIDE + r:CMP_STRIDE + r + 1, :]).astype(bf16),
                                 w1_ref[kv, CMP_STRIDE + r])
        pre = acc_a + pltpu.roll(acc_b, nc - 1, 0)
        outs.append(_dot(_gelu_tanh(pre).astype(bf16), w2_ref[kv]))
    return jnp.concatenate(outs, axis=1)


def _cmp_prompt_body(k_ref, v_ref, pe_ref, w1_ref, w2_ref, o_ref, *, nc):
    o_ref[...] = _cmp_core((k_ref, v_ref), pe_ref, w1_ref, w2_ref, nc)


def _cmp_prompt(nsa_rows, pe2, w1bd, w2bd):
    b, l, _ = nsa_rows.shape
    nc = l // CMP_STRIDE
    return pl.pallas_call(
        functools.partial(_cmp_prompt_body, nc=nc),
        out_shape=jax.ShapeDtypeStruct((b, nc, 256), f32),
        grid=(b,),
        in_specs=[pl.BlockSpec((None, l, LANES), lambda i: (i, 0, 0)),
                  pl.BlockSpec((None, l, LANES), lambda i: (i, 0, 1)), _const_spec(pe2.shape),
                  _const_spec(w1bd.shape), _const_spec(w2bd.shape)],
        out_specs=pl.BlockSpec((None, nc, 256), lambda i: (i, 0, 0)),
        compiler_params=_cparams(("parallel",)),
        name="nsa_compress_prompt",
    )(nsa_rows, nsa_rows, pe2, w1bd, w2bd)


SLAB = 64
FLASH_SCRATCH = 2


class _Flash:
    def __init__(self, scratch, slopes, tq, t0):
        self.m, self.acc = scratch
        self.slope8 = jnp.concatenate(
            [jnp.full((1, 1), v, f32) for v in slopes] + [jnp.zeros((8 - len(slopes), 1), f32)], axis=0)
        self.tq, self.t0 = tq, t0
        self.m[...] = jnp.full(self.m.shape, M_INIT, f32)
        self.acc[...] = jnp.zeros(self.acc.shape, f32)

    def slab(self, j, s_all, bias, rel, masked):
        r0 = j * SLAB
        rows = slice(r0, r0 + SLAB)
        head = r0 // self.tq
        s = s_all[rows, :] + bias[head:head + 1, :]
        if masked:
            t_rel = (r0 + lax.broadcasted_iota(jnp.int32, (SLAB, 1), 0)) % self.tq
            s = jnp.where(rel <= t_rel, s, -MASK_BIG)
        m_old = self.m[rows, :]
        m_new = jnp.maximum(m_old, jnp.max(s, axis=-1, keepdims=True))
        self.m[rows, :] = m_new
        p = jnp.exp(s - jnp.concatenate([m_new] * (s.shape[1] // LANES), axis=1))
        return p.astype(bf16), jnp.exp(m_old - m_new)

    def result(self):
        acc = self.acc[...]
        return acc[:, 0:LANES] / acc[:, LANES:2 * LANES]


def _flash_tiles(chains, k0, masked):
    first = chains[0][0]
    rows = first.m.shape[0]
    tk = chains[0][2].shape[1]
    rel = k0 - first.t0 + lax.broadcasted_iota(jnp.int32, (1, tk), 1)
    logits = [(_dot(q_aug, kt_aug), f.slope8 * rel.astype(f32)) for f, q_aug, kt_aug, _ in chains]
    parts = [[] for _ in chains]
    for j in range(rows // SLAB):
        for c, (f, _, _, _) in enumerate(chains):
            parts[c].append(f.slab(j, logits[c][0], logits[c][1], rel, masked))
    for c, (f, _, _, v_aug) in enumerate(chains):
        p = jnp.concatenate([x[0] for x in parts[c]], axis=0)
        alpha = jnp.concatenate([x[1] for x in parts[c]], axis=0)
        f.acc[...] = jnp.concatenate([alpha, alpha], axis=1) * f.acc[...] + _dot(p, v_aug)


def _flash_scratch(rows):
    return [pltpu.VMEM((rows, LANES), f32), pltpu.VMEM((rows, 2 * LANES), f32)]


def _with_ones(v):
    return jnp.concatenate([v, jnp.ones(v.shape, v.dtype)], axis=-1)


def _masked_exp(s_all, slopes, rel, valid_fn, tq):
    out = []
    for j in range(s_all.shape[0] // SLAB):
        r0 = j * SLAB
        t_rel = r0 % tq + lax.broadcasted_iota(jnp.int32, (SLAB, 1), 0)
        valid = valid_fn(t_rel)
        s = jnp.where(valid, s_all[r0:r0 + SLAB, :] + slopes[r0 // tq] * rel.astype(f32), NEG)
        out.append(jnp.exp(s - jnp.maximum(jnp.max(s, axis=-1, keepdims=True), M_INIT)))
    return jnp.concatenate(out, axis=0)


def _topk_rows(score, ids, k, n_ids):
    picked = jnp.zeros(score.shape, f32)
    for _ in range(k):
        mx = jnp.max(score, axis=0, keepdims=True)
        idx = jnp.min(jnp.where(score == mx, ids, n_ids), axis=0, keepdims=True)
        pick = ids == idx
        picked = jnp.where(pick, 1.0, picked)
        score = jnp.where(pick, REMOVED, score)
    return picked


def _nsa_prompt_body(q_ref, gn_ref, kct_ref, vc_ref, agg_ref, kst_ref, oh_ref, vs_ref, kwt_ref, vw_ref, o_ref,
                     sel_scr, qa_scr, part_scr, gate_scr, oc_scr, imp_scr, *flash_scr, tq, tk, n_cmp):
    i = pl.program_id(1)
    t0 = i * tq
    nc = vc_ref.shape[0]
    qf = q_ref[...].astype(f32)
    lane = lax.broadcasted_iota(jnp.int32, (tq, LANES), 1)
    sig = jax.nn.sigmoid(gn_ref[...])
    n_id = lax.broadcasted_iota(jnp.int32, (1, nc), 1)
    cend_rel = n_id * CMP_STRIDE + (CMP_LEN - 1) - t0
    cend_masked = jnp.where(n_id < n_cmp, cend_rel, 2 ** 30)
    blk_id = lax.broadcasted_iota(jnp.int32, (LANES, tq), 0)
    cur = (t0 + lax.broadcasted_iota(jnp.int32, (LANES, tq), 1)) // SEL_BLOCK
    causal_blk = blk_id <= cur
    forced = (blk_id == 0) | (blk_id == cur) | (blk_id == cur - 1)
    chunks = [jnp.zeros((tq, LANES), f32) for _ in range(NSA_HEADS // 2)]

    for g in range(NSA_KV):
        heads = [HPG * g + hh for hh in range(HPG)]
        pieces = []
        for h in heads:
            blk = qf[:, LANES * (h // 2):LANES * (h // 2 + 1)]
            if h % 2 != g:
                blk = pltpu.roll(blk, HEAD_DIM, 1)
            pieces.append(jnp.where(lane // HEAD_DIM == g, blk, 0.0))
        qg = jnp.concatenate(pieces, axis=0).astype(bf16)
        slopes = [NSA_SLOPES[h] for h in heads]

        def cmp_branch(ncols, qg=qg, slopes=slopes):
            valid_c = lambda t_rel: cend_masked[:, 0:ncols] <= t_rel
            e_c = _masked_exp(_dot(qg, kct_ref[:, 0:ncols]), slopes, cend_rel[:, 0:ncols], valid_c, tq)
            e_hi = e_c.astype(bf16)
            e_lo = (e_c - e_hi.astype(f32)).astype(bf16)
            agg_c = agg_ref[0:ncols, :].astype(bf16)
            ones_c = jnp.ones((ncols, LANES), bf16)
            r_hi = _dot(e_hi, jnp.concatenate([vc_ref[0:ncols, :].astype(bf16), ones_c, agg_c], axis=1))
            r_lo = _dot(e_lo, jnp.concatenate([agg_c, ones_c], axis=1))
            oc_scr[...] = r_hi[:, 0:LANES] / jnp.maximum(r_hi[:, LANES:2 * LANES], 1e-30)
            imp_rows = (r_hi[:, 2 * LANES:3 * LANES] + r_lo[:, 0:LANES]) / jnp.maximum(
                r_hi[:, LANES:2 * LANES] + r_lo[:, LANES:2 * LANES], 1e-30)
            imp_scr[...] = imp_rows[0:tq] + imp_rows[tq:2 * tq] + imp_rows[2 * tq:3 * tq] + imp_rows[3 * tq:4 * tq]

        n_var = -(-nc // LANES)
        variant = jnp.minimum(((t0 + tq) // CMP_STRIDE - 1) // LANES, n_var - 1)
        for v in range(n_var):
            pl.when(variant == v)(functools.partial(cmp_branch, min(LANES * (v + 1), nc)))
        o_c = oc_scr[...]
        imp = imp_scr[...]

        score = jnp.where(causal_blk, jnp.where(forced, FORCE, imp.T), NEG)
        sel_t = jnp.where(causal_blk, _topk_rows(score, blk_id, SEL_TOPN, LANES), 0.0)
        notsel = (1.0 - sel_t).T
        qa_scr[g] = jnp.concatenate([qg, jnp.concatenate([notsel] * HPG, axis=0).astype(bf16)], axis=1)
        sel_scr[g] = sel_t

        wl = WINDOW + tq
        s0 = pl.multiple_of(jnp.maximum(t0 - WINDOW, 0), tq)
        w_rel = s0 - t0 + lax.broadcasted_iota(jnp.int32, (1, wl), 1)
        valid_w = lambda t_rel: (w_rel <= t_rel) & (w_rel > t_rel - WINDOW)
        e_w = _masked_exp(_dot(qg, kwt_ref[:, pl.ds(s0, wl)]), slopes, w_rel, valid_w, tq)
        r_w = _dot(e_w.astype(bf16), _with_ones(vw_ref[pl.ds(s0, wl), :]))
        o_w = r_w[:, 0:LANES] / r_w[:, LANES:2 * LANES]

        gates = [jnp.concatenate([sig[:, br * NSA_HEADS + h:br * NSA_HEADS + h + 1] for h in heads], axis=0)
                 for br in range(3)]
        part_scr[g] = gates[0] * o_c + gates[2] * o_w
        gate_scr[g] = gates[1]

    flashes = [_Flash(flash_scr[FLASH_SCRATCH * g:FLASH_SCRATCH * (g + 1)],
                      NSA_SLOPES[HPG * g:HPG * (g + 1)], tq, t0) for g in range(NSA_KV)]

    def sel_step(kt, masked):
        k0 = pl.multiple_of(kt * tk, tk)
        kt_aug = jnp.concatenate([kst_ref[:, pl.ds(k0, tk)], oh_ref[:, pl.ds(k0, tk)]], axis=0)
        v_aug = _with_ones(vs_ref[pl.ds(k0, tk), :])
        _flash_tiles([(flashes[g], qa_scr[g], kt_aug, v_aug) for g in range(NSA_KV)], k0, masked)

    def sel_loop(kt, carry):
        blk0 = pl.multiple_of(kt * (tk // SEL_BLOCK), tk // SEL_BLOCK)
        picked = jnp.maximum(sel_scr[0, pl.ds(blk0, tk // SEL_BLOCK), :], sel_scr[1, pl.ds(blk0, tk // SEL_BLOCK), :])

        @pl.when(jnp.max(picked) > 0.0)
        def _():
            sel_step(kt, False)

        return carry

    kd = t0 // tk
    lax.fori_loop(0, kd, sel_loop, 0)
    sel_step(kd, True)

    for g in range(NSA_KV):
        heads = [HPG * g + hh for hh in range(HPG)]
        o = part_scr[g] + gate_scr[g] * flashes[g].result()
        for hh, h in enumerate(heads):
            piece = o[hh * tq:(hh + 1) * tq]
            if h % 2 != g:
                piece = pltpu.roll(piece, HEAD_DIM, 1)
            chunks[h // 2] = chunks[h // 2] + jnp.where(lane // HEAD_DIM == h % 2, piece, 0.0)

    o_ref[...] = jnp.concatenate(chunks, axis=1).astype(o_ref.dtype)


def _nsa_prompt(q_n, g_n, cmp, agg, nsa_bf, win_bf, n_cmp, tq, tk):
    b, l, _ = q_n.shape
    nc = cmp.shape[1]
    rows = HPG * tq
    kct = jnp.swapaxes(cmp[:, :, 0:LANES], 1, 2).astype(bf16)
    onehot_t = _block_onehot_t(SEL_BLOCK, l)
    kst = jnp.swapaxes(nsa_bf[:, :, 2 * LANES:3 * LANES], 1, 2)
    kwt = jnp.swapaxes(win_bf[:, :, 0:LANES], 1, 2)
    per_b = lambda shape, col=0: pl.BlockSpec((None,) + shape, lambda bi, i: (bi, 0, col))
    return pl.pallas_call(
        functools.partial(_nsa_prompt_body, tq=tq, tk=tk, n_cmp=n_cmp),
        out_shape=jax.ShapeDtypeStruct((b, l, NSA_HEADS * HEAD_DIM), bf16),
        grid=(b, l // tq),
        in_specs=[pl.BlockSpec((None, tq, 512), lambda bi, i: (bi, i, 0)),
                  pl.BlockSpec((None, tq, LANES), lambda bi, i: (bi, i, 0)),
                  per_b((LANES, nc)), per_b((nc, LANES), 1), _const_spec(agg.shape),
                  per_b((LANES, l)), _const_spec(onehot_t.shape), per_b((l, LANES), 3), per_b((LANES, l)),
                  per_b((l, LANES), 1)],
        out_specs=pl.BlockSpec((None, tq, 512), lambda bi, i: (bi, i, 0)),
        scratch_shapes=[pltpu.VMEM((NSA_KV, LANES, tq), f32), pltpu.VMEM((NSA_KV, rows, 2 * LANES), bf16),
                        pltpu.VMEM((NSA_KV, rows, LANES), f32), pltpu.VMEM((NSA_KV, rows, 1), f32),
                        pltpu.VMEM((rows, LANES), f32), pltpu.VMEM((tq, LANES), f32)]
        + _flash_scratch(rows) * NSA_KV,
        compiler_params=_cparams(("parallel", "parallel")),
        name="nsa_attention_prompt",
    )(q_n, g_n, kct, cmp, agg, kst, onehot_t, nsa_bf, kwt, win_bf)


def _block_mean_body(k_ref, o_ref):
    o_ref[...] = jnp.mean(k_ref[...], axis=0, keepdims=True)


def _block_mean(rows, width, blk):
    b, l, _ = rows.shape
    return pl.pallas_call(
        _block_mean_body,
        out_shape=jax.ShapeDtypeStruct((b, l // blk, 1, width), f32),
        grid=(b, l // blk),
        in_specs=[pl.BlockSpec((None, blk, width), lambda bi, j: (bi, j, 0))],
        out_specs=pl.BlockSpec((None, None, 1, width), lambda bi, j: (bi, j, 0, 0)),
        compiler_params=_cparams(("parallel", "parallel")),
        name="moba_block_mean",
    )(rows)


def _moba_prompt_body(q_ref, kmt_ref, kt_ref, oh_ref, v_ref, o_ref, qa_scr, *flash_scr, tq, tk, nb_pad):
    i = pl.program_id(1)
    t0 = i * tq
    q = q_ref[...]
    qf = q.astype(f32)
    lane = lax.broadcasted_iota(jnp.int32, (tq, LANES), 1)

    gate_t = _nt(kmt_ref[...].astype(bf16), q)
    blk_id = lax.broadcasted_iota(jnp.int32, (LANES, tq), 0) % nb_pad
    own = (t0 + lax.broadcasted_iota(jnp.int32, (LANES, tq), 1)) // MOBA_BLOCK
    past = blk_id < own
    score = jnp.where(past, gate_t, NEG)
    parts = [_topk_rows(score[nb_pad * h:nb_pad * (h + 1)], blk_id[nb_pad * h:nb_pad * (h + 1)], MOBA_TOPK, nb_pad)
             for h in range(MOBA_HEADS)]
    sel_t = jnp.where(past, jnp.concatenate(parts, axis=0), 0.0)
    sel_t = jnp.where(blk_id == own, 1.0, sel_t)
    notsel = (1.0 - sel_t).T

    nch = MOBA_HEADS // 2
    for c in range(nch):
        q_rows = []
        for e in range(2):
            h = 2 * c + e
            qh = jnp.where(lane // HEAD_DIM == e, qf[:, LANES * c:LANES * (c + 1)], 0.0)
            ns = notsel if h == 0 else pltpu.roll(notsel, LANES - nb_pad * h, 1)
            ns = jnp.where(lane < nb_pad, ns, 0.0)
            q_rows.append(jnp.concatenate([qh, ns], axis=1))
        qa_scr[c] = jnp.concatenate(q_rows, axis=0).astype(bf16)
    flashes = [_Flash(flash_scr[FLASH_SCRATCH * c:FLASH_SCRATCH * (c + 1)],
                      MOBA_SLOPES[2 * c:2 * c + 2], tq, t0) for c in range(nch)]

    def step(kt, masked):
        k0 = pl.multiple_of(kt * tk, tk)
        onehot = oh_ref[:, pl.ds(k0, tk)]
        _flash_tiles([(flashes[c], qa_scr[c], jnp.concatenate([kt_ref[c, :, pl.ds(k0, tk)], onehot], axis=0),
                       _with_ones(v_ref[pl.ds(k0, tk), LANES * c:LANES * (c + 1)])) for c in range(nch)], k0, masked)

    def loop(kt, carry):
        step(kt, False)
        return carry

    lax.fori_loop(0, t0 // tk, loop, 0)
    step(t0 // tk, True)
    out_chunks = []
    for c in range(nch):
        o = flashes[c].result()
        out_chunks.append(jnp.where(lane < HEAD_DIM, o[0:tq], o[tq:2 * tq]))
    o_ref[...] = jnp.concatenate(out_chunks, axis=1).astype(o_ref.dtype)


def _moba_prompt(mb_q, kmt, moba_bf, tq, tk):
    b, l, _ = mb_q.shape
    nb_pad = LANES // MOBA_HEADS
    nch = MOBA_HEADS // 2
    onehot_t = _block_onehot_t(MOBA_BLOCK, l)
    k_t = jnp.swapaxes(jnp.swapaxes(moba_bf[:, :, 0:nch * LANES].reshape(b, l, nch, LANES), 1, 3), 1, 2)
    return pl.pallas_call(
        functools.partial(_moba_prompt_body, tq=tq, tk=tk, nb_pad=nb_pad),
        out_shape=jax.ShapeDtypeStruct((b, l, MOBA_HEADS * HEAD_DIM), bf16),
        grid=(b, l // tq),
        in_specs=[pl.BlockSpec((None, tq, 256), lambda bi, i: (bi, i, 0)),
                  pl.BlockSpec((None, LANES, 256), lambda bi, i: (bi, 0, 0)),
                  pl.BlockSpec((None, nch, LANES, l), lambda bi, i: (bi, 0, 0, 0)),
                  _const_spec(onehot_t.shape),
                  pl.BlockSpec((None, l, nch * LANES), lambda bi, i: (bi, 0, 1))],
        out_specs=pl.BlockSpec((None, tq, 256), lambda bi, i: (bi, i, 0)),
        scratch_shapes=[pltpu.VMEM((nch, 2 * tq, 2 * LANES), bf16)] + _flash_scratch(2 * tq) * nch,
        compiler_params=_cparams(("parallel", "parallel")),
        name="moba_attention_prompt",
    )(mb_q, kmt, k_t, onehot_t, moba_bf)


def _log_keep(z):
    return -(jnp.maximum(z, 0.0) + jnp.log(1.0 + jnp.exp(-jnp.abs(z))))


def _suffix_sums(lk, uu):
    hi = lk.astype(bf16)
    lo = (lk - hi.astype(f32)).astype(bf16)
    r = _dot(jnp.concatenate([hi, lo], axis=1), uu)
    return r[:, 0:LANES], r[:, LANES:2 * LANES]


def _sb_prompt_body(q_ref, uu_ref, k_ref, v_ref, o_ref, carry_scr, acc_scr, *, tq, tk):
    i = pl.program_id(2)
    t0 = i * tq
    qf = q_ref[...].astype(f32)
    lane = lax.broadcasted_iota(jnp.int32, (tq, LANES), 1)
    q2 = jnp.concatenate([jnp.where(lane // HEAD_DIM == e, qf, 0.0) for e in range(2)], axis=0).astype(bf16)
    t_row = t0 + lax.broadcasted_iota(jnp.int32, (2 * tq, 1), 0) % tq
    uu = uu_ref[...]
    carry_scr[...] = jnp.zeros(carry_scr.shape, f32)
    acc_scr[...] = jnp.zeros(acc_scr.shape, f32)

    def step(kt, diag):
        k0 = pl.multiple_of(kt * tk, tk)
        z = _dot(q2, k_ref[:, pl.ds(k0, tk)])
        lk = _log_keep(z)
        if diag:
            is_past = (k0 + lax.broadcasted_iota(jnp.int32, (1, tk), 1)) < t_row
            lk = jnp.where(is_past, lk, 0.0)
        carry = carry_scr[...]
        between = [None] * (tk // LANES)
        for c in reversed(range(tk // LANES)):
            later, total = _suffix_sums(lk[:, LANES * c:LANES * (c + 1)], uu)
            between[c] = later + carry
            carry = carry + total
        w = jnp.exp(z + lk + jnp.concatenate(between, axis=1))
        if diag:
            w = jnp.where(is_past, w, 0.0)
        acc_scr[...] = acc_scr[...] + _dot(w.astype(bf16), v_ref[pl.ds(k0, tk), :])
        carry_scr[...] = carry

    kd = t0 // tk
    step(kd, True)

    def more(state):
        j, top = state
        return (j < kd) & (top > SB_CUTOFF)

    def walk(state):
        j, _ = state
        step(kd - 1 - j, False)
        return j + 1, jnp.max(carry_scr[...])

    lax.while_loop(more, walk, (0, jnp.max(carry_scr[...])))
    o = acc_scr[...]
    o_ref[...] = jnp.where(lane < HEAD_DIM, o[0:tq], o[tq:2 * tq]).astype(o_ref.dtype)


def _sb_prompt(sb_q, uu, sb_bf, tq, tk):
    b, l, _ = sb_q.shape
    nch = SB_HEADS // 2
    k_t = jnp.swapaxes(jnp.swapaxes(sb_bf[:, :, 0:nch * LANES].reshape(b, l, nch, LANES), 1, 3), 1, 2)
    return pl.pallas_call(
        functools.partial(_sb_prompt_body, tq=tq, tk=tk),
        out_shape=jax.ShapeDtypeStruct((b, l, SB_HEADS * HEAD_DIM), bf16),
        grid=(b, nch, l // tq),
        in_specs=[pl.BlockSpec((None, tq, LANES), lambda bi, c, i: (bi, i, c)),
                  _const_spec(uu.shape),
                  pl.BlockSpec((None, None, LANES, l), lambda bi, c, i: (bi, c, 0, 0)),
                  pl.BlockSpec((None, l, LANES), lambda bi, c, i: (bi, 0, nch + c))],
        out_specs=pl.BlockSpec((None, tq, LANES), lambda bi, c, i: (bi, i, c)),
        scratch_shapes=[pltpu.VMEM((2 * tq, LANES), f32), pltpu.VMEM((2 * tq, LANES), f32)],
        compiler_params=_cparams(("parallel", "parallel", "parallel")),
        name="stickbreak_attention_prompt",
    )(sb_q, uu, k_t, sb_bf)


def _merge_body(oa_ref, ob_ref, oc_ref, gm_ref, x_ref, gate_ref, wa_ref, wb_ref, wc_ref, wo_ref, o_ref):
    d = x_ref.shape[1]
    g = jax.nn.sigmoid(gm_ref[...])
    merged = (g[:, 0:d] * _dot(oa_ref[...], wa_ref[...]) + g[:, d:2 * d] * _dot(ob_ref[...], wb_ref[...])
              + g[:, 2 * d:3 * d] * _dot(oc_ref[...], wc_ref[...]))
    o_ref[...] = x_ref[...] + gate_ref[...] * _dot(merged.astype(bf16), wo_ref[...])


def _merge_out(o_a, o_b, o_c, g_m, x, gate, wa, wb, wc, wo, tm):
    r, d = x.shape
    nb, rb, _ = gate.shape
    tiles_per_b = (r // nb) // tm
    row = lambda w: pl.BlockSpec((tm, w), lambda i: (i, 0))
    return pl.pallas_call(
        _merge_body,
        out_shape=jax.ShapeDtypeStruct((r, d), f32),
        grid=(r // tm,),
        in_specs=[row(o_a.shape[1]), row(o_b.shape[1]), row(o_c.shape[1]), row(3 * d), row(d),
                  pl.BlockSpec((None, rb, d), lambda i: (i // tiles_per_b, 0, 0)),
                  _const_spec(wa.shape), _const_spec(wb.shape), _const_spec(wc.shape), _const_spec(wo.shape)],
        out_specs=row(d),
        compiler_params=_cparams(("parallel",)),
        name="merge_out_proj",
    )(o_a, o_b, o_c, g_m, x, gate, wa, wb, wc, wo)


def _ffn_tail(a, a_m1, a_m2, b, cw_ref, cb_ref, wd_ref, x_ref, gate_ref, o_ref):
    conv = cb_ref[...] + a_m2 * cw_ref[0:1, :]
    conv = conv + a_m1 * cw_ref[1:2, :]
    conv = conv + a * cw_ref[2:3, :]
    y = _dot((_gelu_tanh(conv) * b).astype(bf16), wd_ref[...])
    o_ref[...] = x_ref[...] + gate_ref[...] * y


def _ffn_seq_body(a_ref, halo_ref, b_ref, cw_ref, cb_ref, wd_ref, x_ref, gate_ref, o_ref, *, tiles_per_b):
    a = a_ref[...]
    first = pl.program_id(0) % tiles_per_b == 0
    halo = jnp.where(first, 0.0, halo_ref[...])
    row = lax.broadcasted_iota(jnp.int32, a.shape, 0)
    a_m1 = jnp.where(row < 1, halo[7:8, :], pltpu.roll(a, 1, 0))
    a_m2 = jnp.where(row < 1, halo[6:7, :], jnp.where(row < 2, halo[7:8, :], pltpu.roll(a, 2, 0)))
    _ffn_tail(a, a_m1, a_m2, b_ref[...], cw_ref, cb_ref, wd_ref, x_ref, gate_ref, o_ref)


def _ffn_step_body(a_ref, am1_ref, am2_ref, b_ref, cw_ref, cb_ref, wd_ref, x_ref, gate_ref, o_ref):
    _ffn_tail(a_ref[...], am1_ref[...], am2_ref[...], b_ref[...], cw_ref, cb_ref, wd_ref, x_ref, gate_ref, o_ref)


def _ffn_down_seq(u_a, u_b, cw8, cb, wd, x, gate, tm):
    r, d = x.shape
    ff = u_a.shape[1]
    nb = gate.shape[0]
    tiles_per_b = (r // nb) // tm
    row = lambda w: pl.BlockSpec((tm, w), lambda i: (i, 0))
    return pl.pallas_call(
        functools.partial(_ffn_seq_body, tiles_per_b=tiles_per_b),
        out_shape=jax.ShapeDtypeStruct((r, d), f32),
        grid=(r // tm,),
        in_specs=[row(ff), pl.BlockSpec((8, ff), lambda i: (jnp.maximum(i * (tm // 8) - 1, 0), 0)), row(ff),
                  _const_spec(cw8.shape), _const_spec(cb.shape), _const_spec(wd.shape), row(d),
                  pl.BlockSpec((None, 1, d), lambda i: (i // tiles_per_b, 0, 0))],
        out_specs=row(d),
        compiler_params=_cparams(("parallel",)),
        name="conv_ffn_down_seq",
    )(u_a, u_a, u_b, cw8, cb, wd, x, gate)


def _ffn_down_step(u_a, a_m1, a_m2, u_b, cw8, cb, wd, x, gate):
    r, d = x.shape
    full = lambda a: pl.BlockSpec(a.shape, lambda i: (0,) * a.ndim)
    return pl.pallas_call(
        _ffn_step_body,
        out_shape=jax.ShapeDtypeStruct((r, d), f32),
        grid=(1,),
        in_specs=[full(u_a), full(a_m1), full(a_m2), full(u_b), full(cw8), full(cb), full(wd), full(x),
                  pl.BlockSpec((None, r, d), lambda i: (0, 0, 0))],
        out_specs=full(x),
        compiler_params=_cparams(("arbitrary",)),
        name="conv_ffn_down_step",
    )(u_a, a_m1, a_m2, u_b, cw8, cb, wd, x, gate)


def _final_norm_body(x_ref, g_ref, o_ref):
    x = x_ref[...]
    o_ref[...] = x * lax.rsqrt(jnp.mean(x * x, axis=-1, keepdims=True) + RMS_EPS) * g_ref[...]


def _final_norm(x, g, tm):
    r, d = x.shape
    return pl.pallas_call(
        _final_norm_body,
        out_shape=jax.ShapeDtypeStruct((r, d), f32),
        grid=(r // tm,),
        in_specs=[pl.BlockSpec((tm, d), lambda i: (i, 0)), _const_spec((1, d))],
        out_specs=pl.BlockSpec((tm, d), lambda i: (i, 0)),
        compiler_params=_cparams(("parallel",)),
        name="final_rmsnorm",
    )(x, g.reshape(1, d))


def _page_specs(n_pages, layer_base, per_step=1, k=0):
    return [pl.BlockSpec((None, 512, PAGE_SIZE),
                         functools.partial(lambda s, pt, j: (layer_base + pt[(s * per_step + k) * n_pages + j], 0, 0),
                                           j=j))
            for j in range(n_pages)]


def _per_seq(width):
    return pl.BlockSpec((None, 1, width), lambda s, pt: (s, 0, 0))


def _dec_const(shape):
    nd = len(shape)
    return pl.BlockSpec(shape, lambda s, pt: (0,) * nd, pipeline_mode=pl.Buffered(1))


def _head_rows(q_row):
    row = lax.broadcasted_iota(jnp.int32, (8, LANES), 0)
    lane = lax.broadcasted_iota(jnp.int32, (8, LANES), 1)
    q8 = jnp.broadcast_to(q_row, (8, q_row.shape[1]))
    qsel = jnp.zeros((8, LANES), f32)
    for c in range(NSA_HEADS // 2):
        qsel = qsel + jnp.where(row // 2 == c, q8[:, LANES * c:LANES * (c + 1)], 0.0)
    swap = (row % 2) != (row // HPG)
    qm = jnp.where(swap, pltpu.roll(qsel, HEAD_DIM, 1), qsel)
    return jnp.where(lane // HEAD_DIM == row // HPG, qm, 0.0), swap


def _rank_select(score_row, k):
    a = jnp.broadcast_to(score_row, (LANES, LANES))
    b = a.T
    ii = lax.broadcasted_iota(jnp.int32, (LANES, LANES), 0)
    jj = lax.broadcasted_iota(jnp.int32, (LANES, LANES), 1)
    ahead = (b > a) | ((b == a) & (ii < jj))
    rank = jnp.sum(jnp.where(ahead, 1.0, 0.0), axis=0, keepdims=True)
    return jnp.where(rank < k, 1.0, 0.0)


def _nsa_dec_body(pt_ref, q_ref, gn_ref, new_ref, wnew_ref, sw_ref, pe_ref, w1_ref, w2_ref, agg_ref, e_ref,
                  *rest, n_pages, per_step):
    pages, o_ref = rest[:n_pages * per_step], rest[n_pages * per_step]
    kc_scr, vc_scr = rest[n_pages * per_step + 1:]
    past = n_pages * PAGE_SIZE
    for j in range(n_pages * per_step):
        kc_scr[PAGE_SIZE * j:PAGE_SIZE * (j + 1), :] = pages[j][0:LANES, :].T
        vc_scr[PAGE_SIZE * j:PAGE_SIZE * (j + 1), :] = pages[j][LANES:2 * LANES, :].T

    nc = past // CMP_STRIDE
    cmp_all = _cmp_core((kc_scr, vc_scr), pe_ref, w1_ref, w2_ref, nc * per_step)
    for k in range(per_step):
        o_ref[k] = _nsa_dec_one(q_ref[k], gn_ref[k], new_ref[k], wnew_ref[k], sw_ref.at[k],
                                cmp_all[nc * k:nc * (k + 1)], pages[n_pages * k:n_pages * (k + 1)], agg_ref, e_ref)


def _nsa_dec_one(q_row, gn_row, new_row, wnew, sw_ref, cmpv, pages, agg_ref, e_ref):
    n_pages = len(pages)
    past = n_pages * PAGE_SIZE
    n_cmp = (past + 1 - CMP_LEN) // CMP_STRIDE + 1
    row = lax.broadcasted_iota(jnp.int32, (8, LANES), 0)
    lane = lax.broadcasted_iota(jnp.int32, (8, LANES), 1)
    qm, swap = _head_rows(q_row)
    qmb = qm.astype(bf16)
    slope = jnp.zeros((8, 1), f32)
    row1 = lax.broadcasted_iota(jnp.int32, (8, 1), 0)
    for h in range(NSA_HEADS):
        slope = jnp.where(row1 == h, NSA_SLOPES[h], slope)
    grp0 = row < HPG

    d_c = past - (lane[0:1] * CMP_STRIDE + CMP_LEN - 1)
    s_c = _nt(qmb, cmpv[:, 0:LANES].astype(bf16)) - slope * d_c.astype(f32)
    p_c = _ref_softmax(s_c, (d_c >= 0) & (lane[0:1] < n_cmp))
    o_c = _dot(p_c.astype(bf16), cmpv[:, LANES:2 * LANES].astype(bf16))
    aggb = agg_ref[...].astype(bf16)
    p_hi = p_c.astype(bf16)
    p_lo = (p_c - p_hi.astype(f32)).astype(bf16)
    imp_rows = _dot(p_hi, aggb) + _dot(p_lo, aggb)

    cur = past // SEL_BLOCK
    blk = lane[0:1]
    forced = (blk == 0) | (blk == cur) | (blk == cur - 1)
    causal = blk <= cur
    notsel_g = []
    for g in range(NSA_KV):
        imp = jnp.sum(imp_rows[HPG * g:HPG * (g + 1)], axis=0, keepdims=True)
        score = jnp.where(causal, jnp.where(forced, FORCE, imp), NEG)
        sel = jnp.where(causal, _rank_select(score, min(SEL_TOPN, -(-(past + 1) // SEL_BLOCK))), 0.0)
        notsel_g.append(jnp.broadcast_to(1.0 - sel, (8, LANES)))
    notsel = jnp.where(grp0, notsel_g[0], notsel_g[1])

    dist = past - lax.broadcasted_iota(jnp.int32, (1, past), 1)
    picked = _dot(notsel.astype(bf16), e_ref[...]) < 0.5
    ks_t = jnp.concatenate([pages[j][2 * LANES:3 * LANES, :].astype(bf16) for j in range(n_pages)], axis=1)
    vs_t = jnp.concatenate([pages[j][3 * LANES:4 * LANES, :].astype(bf16) for j in range(n_pages)], axis=1)
    s_s = jnp.where(picked, _dot(qmb, ks_t) - slope * dist.astype(f32), NEG)
    s_n = jnp.sum(qm * new_row[:, 2 * LANES:3 * LANES], axis=-1, keepdims=True)
    m_s = jnp.maximum(jnp.max(s_s, axis=-1, keepdims=True), s_n)
    e_s = jnp.where(picked, jnp.exp(s_s - m_s), 0.0)
    e_n = jnp.exp(s_n - m_s)
    den = jnp.maximum(jnp.sum(e_s, axis=-1, keepdims=True) + e_n, 1e-30)
    o_s = (e_n * new_row[:, 3 * LANES:4 * LANES] + _nt(e_s.astype(bf16), vs_t)) / den

    wb = sw_ref.shape[1]
    d_w = wb - lax.broadcasted_iota(jnp.int32, (1, wb), 1)
    s_w = _dot(qmb, sw_ref[0:LANES, :].astype(bf16)) - slope * d_w.astype(f32)
    valid_w = (d_w < WINDOW) & (d_w >= 0)
    s_w = jnp.where(valid_w, s_w, NEG)
    s_n = jnp.sum(qm * wnew[:, 0:LANES], axis=-1, keepdims=True)
    m_w = jnp.maximum(jnp.max(s_w, axis=-1, keepdims=True), s_n)
    e_w = jnp.where(valid_w, jnp.exp(s_w - m_w), 0.0)
    e_n = jnp.exp(s_n - m_w)
    den = jnp.maximum(jnp.sum(e_w, axis=-1, keepdims=True) + e_n, 1e-30)
    o_w = (_nt(e_w.astype(bf16), sw_ref[LANES:2 * LANES, :].astype(bf16)) + e_n * wnew[:, LANES:2 * LANES]) / den

    sig = jnp.broadcast_to(jax.nn.sigmoid(gn_row), (8, LANES))
    gates = [jnp.sum(jnp.where(lane == br * NSA_HEADS + row, sig, 0.0), axis=-1, keepdims=True) for br in range(3)]
    o = gates[0] * o_c + gates[1] * o_s + gates[2] * o_w
    o = jnp.where(swap, pltpu.roll(o, HEAD_DIM, 1), o)
    o = jnp.where(lane // HEAD_DIM == row % 2, o, 0.0)
    return jnp.concatenate([o[2 * c:2 * c + 1] + o[2 * c + 1:2 * c + 2] for c in range(NSA_HEADS // 2)], axis=1)


def _nsa_decode(page_table, cache_t, layer_base, q, g_n, nsa_new, win_new, state_win_t, win_base, pe4, w1bd, w2bd,
                agg, e_sel, per_step):
    s, n_pages = page_table.shape
    past = n_pages * PAGE_SIZE
    consts = (pe4, w1bd, w2bd, agg, e_sel)
    seqs = lambda width: pl.BlockSpec((per_step, 1, width), lambda si, pt: (si, 0, 0))
    page_specs = [spec for k in range(per_step) for spec in _page_specs(n_pages, layer_base, per_step, k)]
    grid_spec = pltpu.PrefetchScalarGridSpec(
        num_scalar_prefetch=1,
        grid=(s // per_step,),
        in_specs=[seqs(512), seqs(LANES), seqs(512), seqs(256),
                  pl.BlockSpec((per_step,) + state_win_t.shape[1:],
                               lambda si, pt: (win_base // per_step + si, 0, 0))]
        + [_dec_const(c.shape) for c in consts] + page_specs,
        out_specs=seqs(512),
        scratch_shapes=[pltpu.VMEM((per_step * past, LANES), f32), pltpu.VMEM((per_step * past, LANES), f32)],
    )
    return pl.pallas_call(
        functools.partial(_nsa_dec_body, n_pages=n_pages, per_step=per_step),
        out_shape=jax.ShapeDtypeStruct((s, 1, 512), f32),
        grid_spec=grid_spec,
        compiler_params=_cparams(("parallel",)),
        name="nsa_attention_decode",
    )(page_table.reshape(-1), q, g_n, nsa_new, win_new, state_win_t, *consts,
      *([cache_t] * (n_pages * per_step)))


def _win_shift_body(sw_ref, new_ref, o_ref):
    n, feat, wb = sw_ref.shape
    lane = lax.broadcasted_iota(jnp.int32, (feat, wb), 1)
    for k in range(n):
        col = jnp.broadcast_to(new_ref[k], (LANES, feat)).T
        col = jnp.concatenate([col] * (wb // LANES), axis=1)
        o_ref[k] = jnp.where(lane == wb - 1, col, pltpu.roll(sw_ref[k], wb - 1, 1))


def _win_shift(state_win_t, new_rows, per_step):
    n, feat, wb = state_win_t.shape
    return pl.pallas_call(
        _win_shift_body,
        out_shape=jax.ShapeDtypeStruct((n, feat, wb), f32),
        grid=(n // per_step,),
        in_specs=[pl.BlockSpec((per_step, feat, wb), lambda i: (i, 0, 0)),
                  pl.BlockSpec((per_step, 1, feat), lambda i: (i, 0, 0))],
        out_specs=pl.BlockSpec((per_step, feat, wb), lambda i: (i, 0, 0)),
        compiler_params=_cparams(("parallel",)),
        name="window_state_shift",
    )(state_win_t, new_rows)


def _q_head_rows4(q_row):
    row = lax.broadcasted_iota(jnp.int32, (8, 256), 0)
    lane = lax.broadcasted_iota(jnp.int32, (8, 256), 1)
    own = lane // HEAD_DIM == row
    return jnp.where(own, jnp.broadcast_to(q_row, (8, 256)), 0.0), own


def _moba_dec_body(pt_ref, q_ref, new_ref, seg_ref, e_ref, *rest, n_pages):
    pages, o_ref = rest[:n_pages], rest[n_pages]
    past = n_pages * PAGE_SIZE
    nb_past = past // MOBA_BLOCK
    kw = MOBA_HEADS * HEAD_DIM
    qm, own_lanes = _q_head_rows4(q_ref[...])
    qmb = qm.astype(bf16)
    k_t = jnp.concatenate([pages[j][0:kw, :].astype(bf16) for j in range(n_pages)], axis=1)
    v_t = jnp.concatenate([pages[j][kw:2 * kw, :].astype(bf16) for j in range(n_pages)], axis=1)
    raw = _dot(qmb, k_t)
    raw_hi = raw.astype(bf16)
    raw_lo = (raw - raw_hi.astype(f32)).astype(bf16)
    gate = _dot(raw_hi, seg_ref[...]) + _dot(raw_lo, seg_ref[...])

    lane = lax.broadcasted_iota(jnp.int32, (8, LANES), 1)
    is_past = lane < nb_past
    score = jnp.where(is_past, gate, NEG)
    rank = jnp.zeros((8, LANES), f32)
    for i in range(nb_past):
        gi = score[:, i:i + 1]
        rank = rank + jnp.where((gi > score) | ((gi == score) & (i < lane)), 1.0, 0.0)
    sel = jnp.where(is_past & (rank < min(MOBA_TOPK, nb_past)), 1.0, 0.0)
    picked = _dot(sel.astype(bf16), e_ref[...]) > 0.5

    row1 = lax.broadcasted_iota(jnp.int32, (8, 1), 0)
    slope = jnp.zeros((8, 1), f32)
    for h in range(MOBA_HEADS):
        slope = jnp.where(row1 == h, MOBA_SLOPES[h], slope)
    dist = past - lax.broadcasted_iota(jnp.int32, (1, past), 1)
    new_row = new_ref[...]
    s = jnp.where(picked, raw - slope * dist.astype(f32), NEG)
    s_n = jnp.sum(qm * new_row[:, 0:kw], axis=-1, keepdims=True)
    m = jnp.maximum(jnp.max(s, axis=-1, keepdims=True), s_n)
    e = jnp.where(picked, jnp.exp(s - m), 0.0)
    e_n = jnp.exp(s_n - m)
    den = jnp.maximum(jnp.sum(e, axis=-1, keepdims=True) + e_n, 1e-30)
    o = (e_n * new_row[:, kw:2 * kw] + _nt(e.astype(bf16), v_t)) / den
    o_ref[...] = jnp.sum(jnp.where(own_lanes, o, 0.0), axis=0, keepdims=True)


def _moba_decode(page_table, cache_t, layer_base, q, moba_new, seg_mean, e_blk):
    s, n_pages = page_table.shape
    grid_spec = pltpu.PrefetchScalarGridSpec(
        num_scalar_prefetch=1,
        grid=(s,),
        in_specs=[_per_seq(256), _per_seq(512), _dec_const(seg_mean.shape), _dec_const(e_blk.shape)]
        + _page_specs(n_pages, layer_base),
        out_specs=_per_seq(256),
    )
    return pl.pallas_call(
        functools.partial(_moba_dec_body, n_pages=n_pages),
        out_shape=jax.ShapeDtypeStruct((s, 1, 256), f32),
        grid_spec=grid_spec,
        compiler_params=_cparams(("parallel",)),
        name="moba_attention_decode",
    )(page_table.reshape(-1), q, moba_new, seg_mean, e_blk, *([cache_t] * n_pages))


def _sb_dec_body(pt_ref, q_ref, uu_ref, *rest, n_pages):
    pages, o_ref = rest[:n_pages], rest[n_pages]
    kw = SB_HEADS * HEAD_DIM
    qm, own_lanes = _q_head_rows4(q_ref[...])
    qmb = qm.astype(bf16)
    k_t = jnp.concatenate([pages[j][0:kw, :].astype(bf16) for j in range(n_pages)], axis=1)
    v_t = jnp.concatenate([pages[j][kw:2 * kw, :].astype(bf16) for j in range(n_pages)], axis=1)
    z = _dot(qmb, k_t)
    lk = _log_keep(z)
    stacked = jnp.concatenate([lk[:, PAGE_SIZE * j:PAGE_SIZE * (j + 1)] for j in range(n_pages)], axis=0)
    later, total = _suffix_sums(stacked, uu_ref[...])
    carry = jnp.zeros((8, LANES), f32)
    between = [None] * n_pages
    for j in reversed(range(n_pages)):
        between[j] = later[8 * j:8 * (j + 1)] + carry
        carry = carry + total[8 * j:8 * (j + 1)]
    w = jnp.exp(z + lk + jnp.concatenate(between, axis=1))
    acc = _nt(w.astype(bf16), v_t)
    o_ref[...] = jnp.sum(jnp.where(own_lanes, acc, 0.0), axis=0, keepdims=True)


def _sb_decode(page_table, cache_t, layer_base, q, uu):
    s, n_pages = page_table.shape
    grid_spec = pltpu.PrefetchScalarGridSpec(
        num_scalar_prefetch=1,
        grid=(s,),
        in_specs=[_per_seq(256), _dec_const(uu.shape)] + _page_specs(n_pages, layer_base),
        out_specs=_per_seq(256),
    )
    return pl.pallas_call(
        functools.partial(_sb_dec_body, n_pages=n_pages),
        out_shape=jax.ShapeDtypeStruct((s, 1, 256), f32),
        grid_spec=grid_spec,
        compiler_params=_cparams(("parallel",)),
        name="stickbreak_attention_decode",
    )(page_table.reshape(-1), q, uu, *([cache_t] * n_pages))


def _agg_matrix(nc, n_cmp):
    c0 = np.arange(nc)[:, None] * CMP_STRIDE
    s0 = np.arange(LANES)[None, :] * SEL_BLOCK
    ov = np.clip(np.minimum(c0 + CMP_LEN, s0 + SEL_BLOCK) - np.maximum(c0, s0), 0, None) / CMP_LEN
    ov[n_cmp:] = 0.0
    return jnp.asarray(ov, f32)


def _block_onehot_t(block, l):
    e = (np.arange(l)[None, :] // block) == np.arange(LANES)[:, None]
    return jnp.asarray(np.where(e, -MASK_BIG, 0.0), bf16)


def _expand_matrix(block, kp):
    e = (np.arange(kp)[None, :] // block) == np.arange(LANES)[:, None]
    return jnp.asarray(e, bf16)


def _suffix_matrix():
    j = np.arange(2 * LANES)[:, None] % LANES
    s = np.arange(2 * LANES)[None, :]
    return jnp.asarray((s >= LANES) | (j > s), bf16)


def _layer_weights(l, w_in, cmp_pe, cmp_w1, cmp_w2, w_br_a, w_br_b, w_br_c, w_o, w_up, conv_w, conv_b, w_down):
    d = w_in.shape[1]
    w = w_in[l]
    w_proj = jnp.concatenate([w[:, 0:1280], w[:, 1304:], w[:, 1280:1304], jnp.zeros((d, _PROJ_W - 5912), f32)],
                             axis=1).astype(bf16)
    pe4 = jnp.concatenate([cmp_pe[l], cmp_pe[l]], axis=2)
    w1 = cmp_w1[l].reshape(2, CMP_LEN, HEAD_DIM, CMP_HID)
    w1bd = jnp.zeros((2, CMP_LEN, NSA_KV * HEAD_DIM, NSA_KV * CMP_HID), f32)
    w2bd = jnp.zeros((2, NSA_KV * CMP_HID, NSA_KV * HEAD_DIM), f32)
    for g in range(NSA_KV):
        w1bd = w1bd.at[:, :, HEAD_DIM * g:HEAD_DIM * (g + 1), CMP_HID * g:CMP_HID * (g + 1)].set(w1)
        w2bd = w2bd.at[:, CMP_HID * g:CMP_HID * (g + 1), HEAD_DIM * g:HEAD_DIM * (g + 1)].set(cmp_w2[l])
    cw8 = jnp.concatenate([conv_w[l], jnp.zeros((8 - CONV_W, conv_w.shape[2]), f32)], axis=0)
    return dict(w_proj=w_proj, pe4=pe4, w1bd=w1bd.astype(bf16), w2bd=w2bd.astype(bf16),
                wa=w_br_a[l].astype(bf16), wb=w_br_b[l].astype(bf16), wc=w_br_c[l].astype(bf16),
                wo=w_o[l].astype(bf16), w_up=w_up[l].astype(bf16), cw8=cw8, cb=conv_b[l].reshape(1, -1),
                wd=w_down[l].astype(bf16))


def _mod_parts(mod_rows, per_row):
    r = mod_rows.shape[0]
    parts = mod_rows.reshape(r, 6, D_MODEL)
    return [parts[:, k].reshape((1, r, D_MODEL) if per_row else (r, 1, D_MODEL)) for k in range(6)]


def kernel(x_prompt, x_sample, cache_nsa, cache_moba, cache_sb, state_win, state_conv, page_table, c_prompt,
           c_sample, norm1_g, norm2_g, w_ada, b_ada, w_in, cmp_pe, cmp_w1, cmp_w2, w_br_a, w_br_b, w_br_c, w_o,
           w_up, conv_w, conv_b, w_down, final_g):
    b, t, d = x_prompt.shape
    s = x_sample.shape[0]
    depth = w_in.shape[0]
    n_phys = cache_nsa.shape[1]
    n_pages = page_table.shape[1]
    past = n_pages * PAGE_SIZE
    tm = 512
    tq = 128
    tk = 512

    n_c = b + s
    c_all = jnp.concatenate([c_prompt, c_sample, jnp.zeros((-n_c % 8, d), f32)], axis=0)
    mod = _ada_mod(c_all, w_ada, b_ada)

    nc_p = t // CMP_STRIDE
    n_cmp_p = (t - CMP_LEN) // CMP_STRIDE + 1
    agg_p = _agg_matrix(nc_p, n_cmp_p)
    nc_s = past // CMP_STRIDE
    agg_s = _agg_matrix(nc_s, (past + 1 - CMP_LEN) // CMP_STRIDE + 1)
    e_sel = _expand_matrix(SEL_BLOCK, past)
    e_blk = _expand_matrix(MOBA_BLOCK, past)
    seg_mean = (jnp.swapaxes(e_blk, 0, 1).astype(f32) * (1.0 / MOBA_BLOCK)).astype(bf16)
    uu = _suffix_matrix()
    nb_pad = LANES // MOBA_HEADS

    feat_major = lambda a: jnp.transpose(a, (0, 1, 3, 4, 5, 2))
    caches = [feat_major(c).reshape(depth * n_phys, 512, PAGE_SIZE) for c in (cache_nsa, cache_moba, cache_sb)]
    state_win_t = feat_major(state_win).reshape(depth * s, 256, state_win.shape[2])

    xp = x_prompt.reshape(b * t, d)
    xs = x_sample.reshape(s, d)
    outs_p = [[] for _ in range(5)]
    outs_s = [[] for _ in range(5)]
    for l in range(depth):
        w = _layer_weights(l, w_in, cmp_pe, cmp_w1, cmp_w2, w_br_a, w_br_b, w_br_c, w_o, w_up, conv_w, conv_b,
                           w_down)
        mp = _mod_parts(mod[l, 0:b], per_row=False)
        ms = _mod_parts(mod[l, b:b + s], per_row=True)

        (q_n, nsa_rows, nsa_bf, win_rows, win_bf, mb_q, moba_rows, moba_bf, sb_q, sb_rows, sb_bf, g_m, g_n) = \
            _norm_mod_matmul(xp, norm1_g[l], mp[0], mp[1], w["w_proj"], _IN_SEGS, _IN_DTYPES, tm, "in_proj_prompt")
        r3 = lambda a: a.reshape(b, t, a.shape[1])
        cmp = _cmp_prompt(r3(nsa_rows), w["pe4"], w["w1bd"], w["w2bd"])
        o_a = _nsa_prompt(r3(q_n), r3(g_n), cmp, agg_p, r3(nsa_bf), r3(win_bf), n_cmp_p, tq, tk)
        kmean = _block_mean(r3(moba_rows), 256, MOBA_BLOCK).reshape(b, t // MOBA_BLOCK, MOBA_HEADS, HEAD_DIM)
        kmt = jnp.zeros((b, MOBA_HEADS, nb_pad, MOBA_HEADS, HEAD_DIM), f32)
        for h in range(MOBA_HEADS):
            kmt = kmt.at[:, h, 0:t // MOBA_BLOCK, h].set(kmean[:, :, h])
        o_b = _moba_prompt(r3(mb_q), kmt.reshape(b, LANES, 256), r3(moba_bf), tq, tk)
        o_c = _sb_prompt(r3(sb_q), uu, r3(sb_bf), 2 * tq, tk)
        xp = _merge_out(o_a.reshape(b * t, -1), o_b.reshape(b * t, -1), o_c.reshape(b * t, -1), g_m, xp, mp[2],
                        w["wa"], w["wb"], w["wc"], w["wo"], tm)
        u_a, u_b = _norm_mod_matmul(xp, norm2_g[l], mp[3], mp[4], w["w_up"], _UP_SEGS, _UP_DTYPES, tm,
                                    "ffn_up_prompt")
        xp = _ffn_down_seq(u_a, u_b, w["cw8"], w["cb"], w["wd"], xp, mp[5], tm)
        keep = min(WINDOW, t)
        outs_p[0].append(nsa_rows.reshape(b, t // PAGE_SIZE, PAGE_SIZE, 4, NSA_KV, HEAD_DIM))
        outs_p[1].append(moba_rows.reshape(b, t // PAGE_SIZE, PAGE_SIZE, 2, MOBA_HEADS, HEAD_DIM))
        outs_p[2].append(sb_rows.reshape(b, t // PAGE_SIZE, PAGE_SIZE, 2, SB_HEADS, HEAD_DIM))
        outs_p[3].append(r3(win_rows)[:, t - keep:].reshape(b, keep, 2, NSA_KV, HEAD_DIM))
        outs_p[4].append(r3(u_a)[:, t - (CONV_W - 1):])

        (q_n, nsa_rows, _, win_rows, _, mb_q, moba_rows, _, sb_q, sb_rows, _, g_m, g_n) = \
            _norm_mod_matmul(xs, norm1_g[l], ms[0], ms[1], w["w_proj"], _IN_SEGS, _IN_DTYPES, s, "in_proj_sample")
        s3 = lambda a: a.astype(f32).reshape(s, 1, a.shape[1])
        base = l * n_phys
        o_a = _nsa_decode(page_table, caches[0], base, s3(q_n), s3(g_n), s3(nsa_rows), s3(win_rows), state_win_t,
                          l * s, w["pe4"], w["w1bd"], w["w2bd"], agg_s, e_sel, 2)
        o_b = _moba_decode(page_table, caches[1], base, s3(mb_q), s3(moba_rows), seg_mean, e_blk)
        o_c = _sb_decode(page_table, caches[2], base, s3(sb_q), uu)
        xs = _merge_out(o_a.reshape(s, -1).astype(bf16), o_b.reshape(s, -1).astype(bf16),
                        o_c.reshape(s, -1).astype(bf16), g_m, xs, ms[2], w["wa"], w["wb"], w["wc"], w["wo"], s)
        u_a, u_b = _norm_mod_matmul(xs, norm2_g[l], ms[3], ms[4], w["w_up"], _UP_SEGS, _UP_DTYPES, s,
                                    "ffn_up_sample")
        xs = _ffn_down_step(u_a, state_conv[l, :, 1], state_conv[l, :, 0], u_b, w["cw8"], w["cb"], w["wd"], xs,
                            ms[5])
        conv_full = jnp.concatenate([state_conv[l], u_a.reshape(s, 1, -1)], axis=1)
        outs_s[0].append(nsa_rows.reshape(s, 1, 4, NSA_KV, HEAD_DIM))
        outs_s[1].append(moba_rows.reshape(s, 1, 2, MOBA_HEADS, HEAD_DIM))
        outs_s[2].append(sb_rows.reshape(s, 1, 2, SB_HEADS, HEAD_DIM))
        outs_s[3].append(win_rows.reshape(s, 1, 256))
        outs_s[4].append(conv_full[:, conv_full.shape[1] - (CONV_W - 1):])

    y_prompt = _final_norm(xp, final_g, tm).reshape(b, t, d)
    y_sample = _final_norm(xs, final_g, s).reshape(s, 1, d)
    wb = state_win.shape[2]
    win_s = _win_shift(state_win_t, jnp.concatenate(outs_s[3], axis=0), 4)
    win_s = jnp.transpose(win_s.reshape(depth, s, 2, NSA_KV, HEAD_DIM, wb), (0, 1, 5, 2, 3, 4))
    st = lambda lst: jnp.stack(lst)
    return (y_prompt, y_sample, st(outs_p[0]), st(outs_s[0]), st(outs_p[1]), st(outs_s[1]), st(outs_p[2]),
            st(outs_s[2]), st(outs_p[3]), win_s, st(outs_p[4]), st(outs_s[4]))
```

```python
import functools

import numpy as np
import jax
import jax.numpy as jnp
from jax import lax
from jax.experimental import pallas as pl
from jax.experimental.pallas import tpu as pltpu

f32 = jnp.float32
bf16 = jnp.bfloat16

D_MODEL = 1024
HEAD_DIM = 64
NSA_HEADS = 8
NSA_KV = 2
HPG = NSA_HEADS // NSA_KV
CMP_LEN = 32
CMP_STRIDE = 16
CMP_HID = 128
SEL_BLOCK = 64
SEL_TOPN = 16
WINDOW = 512
MOBA_HEADS = 4
MOBA_BLOCK = 256
MOBA_TOPK = 3
SB_HEADS = 4
D_FF = 2816
CONV_W = 3
PAGE_SIZE = 128
RMS_EPS = 1e-6
NEG = -1e30
FORCE = 1e9

LANES = 128
MASK_BIG = 2.0 ** 100
M_INIT = -1e29
REMOVED = -3e38
SB_CUTOFF = -110.0
VMEM_LIMIT_MB = 56

NSA_SLOPES = [2.0 ** (-8.0 * (h + 1) / NSA_HEADS) for h in range(NSA_HEADS)]
MOBA_SLOPES = [2.0 ** (-8.0 * (h + 1) / MOBA_HEADS) for h in range(MOBA_HEADS)]

_Q_N, _NSA, _WIN, _MB_Q, _MOBA, _SB_Q, _SB, _G_M, _G_N, _PROJ_W = 0, 512, 1024, 1280, 1536, 2048, 2304, 2816, 5888, 6016


def _cparams(sem, vmem_mb=VMEM_LIMIT_MB):
    return pltpu.CompilerParams(dimension_semantics=sem, vmem_limit_bytes=vmem_mb * 2 ** 20)


def _const_spec(shape):
    nd = len(shape)
    return pl.BlockSpec(shape, lambda *_: (0,) * nd, pipeline_mode=pl.Buffered(1))


def _nt(a, b):
    return lax.dot_general(a, b, (((1,), (1,)), ((), ())), preferred_element_type=f32)


def _dot(a, b):
    return jnp.dot(a, b, preferred_element_type=f32)


def _gelu_tanh(x):
    return x * (0.5 * (1.0 + jnp.tanh(np.sqrt(2.0 / np.pi) * (x + 0.044715 * (x * x * x)))))


def _ref_softmax(s, valid):
    s = jnp.where(valid, s, NEG)
    e = jnp.where(valid, jnp.exp(s - jnp.max(s, axis=-1, keepdims=True)), 0.0)
    return e / jnp.maximum(jnp.sum(e, axis=-1, keepdims=True), 1e-30)


def _ada_body(c_ref, w_ref, b_ref, o_ref):
    c = c_ref[...]
    s = c * jax.nn.sigmoid(c)
    o_ref[...] = _dot(s.astype(bf16), w_ref[...].astype(bf16)) + b_ref[...]


def _ada_mod(c_all, w_ada, b_ada):
    depth, d, n = w_ada.shape
    r = c_all.shape[0]
    tn = 1536
    return pl.pallas_call(
        _ada_body,
        out_shape=jax.ShapeDtypeStruct((depth, r, n), f32),
        grid=(depth, n // tn),
        in_specs=[pl.BlockSpec((r, d), lambda l, j: (0, 0)),
                  pl.BlockSpec((None, d, tn), lambda l, j: (l, 0, j)),
                  pl.BlockSpec((None, 1, tn), lambda l, j: (l, 0, j))],
        out_specs=pl.BlockSpec((None, r, tn), lambda l, j: (l, 0, j)),
        compiler_params=_cparams(("parallel", "parallel")),
        name="ada_mod",
    )(c_all, w_ada, b_ada.reshape(depth, 1, n))


def _nmm_body(x_ref, g_ref, sh_ref, sc_ref, w_ref, *o_refs, segs):
    x = x_ref[...]
    y = x * lax.rsqrt(jnp.mean(x * x, axis=-1, keepdims=True) + RMS_EPS)
    h = (y * g_ref[...]) * (1.0 + sc_ref[...]) + sh_ref[...]
    hb = h.astype(bf16)
    k = 0
    for off, width, scales in segs:
        outs = o_refs[k:k + len(scales)]
        k += len(scales)
        for c0 in range(0, width, 512):
            cw = min(512, width - c0)
            acc = _dot(hb, w_ref[:, off + c0:off + c0 + cw])
            for o, scale in zip(outs, scales):
                o[:, c0:c0 + cw] = (acc if scale == 1.0 else acc * scale).astype(o.dtype)


def _norm_mod_matmul(x, g, shift, scale, w_bf, segs, out_dtypes, tm, name):
    r, d = x.shape
    nb, rb, _ = shift.shape
    tiles_per_b = (r // nb) // tm
    out_shape, out_specs = [], []
    k = 0
    for off, width, scales in segs:
        for _ in scales:
            out_shape.append(jax.ShapeDtypeStruct((r, width), out_dtypes[k]))
            out_specs.append(pl.BlockSpec((tm, width), lambda i: (i, 0)))
            k += 1
    mod_spec = pl.BlockSpec((None, rb, d), lambda i: (i // tiles_per_b, 0, 0))
    return pl.pallas_call(
        functools.partial(_nmm_body, segs=segs),
        out_shape=out_shape,
        grid=(r // tm,),
        in_specs=[pl.BlockSpec((tm, d), lambda i: (i, 0)), _const_spec((1, d)), mod_spec, mod_spec,
                  _const_spec(w_bf.shape)],
        out_specs=out_specs,
        compiler_params=_cparams(("parallel",)),
        name=name,
    )(x, g.reshape(1, d), shift, scale, w_bf)


_IN_SEGS = ((_Q_N, 512, (0.125,)), (_NSA, 512, (1.0, 1.0)), (_WIN, 256, (1.0, 1.0)), (_MB_Q, 256, (0.125,)),
            (_MOBA, 512, (1.0, 1.0)), (_SB_Q, 256, (0.125,)), (_SB, 512, (1.0, 1.0)), (_G_M, 3072, (1.0,)),
            (_G_N, 128, (1.0,)))
_IN_DTYPES = (bf16, f32, bf16, f32, bf16, bf16, f32, bf16, bf16, f32, bf16, f32, f32)
_UP_SEGS = ((0, D_FF, (1.0,)), (D_FF, D_FF, (1.0,)))
_UP_DTYPES = (f32, f32)


def _cmp_core(rows_refs, pe_ref, w1_ref, w2_ref, nc):
    outs = []
    for kv in range(2):
        acc_a = jnp.zeros((nc, NSA_KV * CMP_HID), f32)
        acc_b = jnp.zeros((nc, NSA_KV * CMP_HID), f32)
        for r in range(CMP_STRIDE):
            y = rows_refs[kv][pl.ds(r, nc, stride=CMP_STRIDE), :]
            acc_a = acc_a + _dot((y + pe_ref[kv, r:r + 1, :]).astype(bf16), w1_ref[kv, r])
            acc_b = acc_b + _dot((y + pe_ref[kv, CMP_STRIDE + r:CMP_STRIDE + r + 1, :]).astype(bf16),
                                 w1_ref[kv, CMP_STRIDE + r])
        pre = acc_a + pltpu.roll(acc_b, nc - 1, 0)
        outs.append(_dot(_gelu_tanh(pre).astype(bf16), w2_ref[kv]))
    return jnp.concatenate(outs, axis=1)


def _cmp_prompt_body(k_ref, v_ref, pe_ref, w1_ref, w2_ref, o_ref, *, nc):
    o_ref[...] = _cmp_core((k_ref, v_ref), pe_ref, w1_ref, w2_ref, nc)


def _cmp_prompt(nsa_rows, pe2, w1bd, w2bd):
    b, l, _ = nsa_rows.shape
    nc = l // CMP_STRIDE
    return pl.pallas_call(
        functools.partial(_cmp_prompt_body, nc=nc),
        out_shape=jax.ShapeDtypeStruct((b, nc, 256), f32),
        grid=(b,),
        in_specs=[pl.BlockSpec((None, l, LANES), lambda i: (i, 0, 0)),
                  pl.BlockSpec((None, l, LANES), lambda i: (i, 0, 1)), _const_spec(pe2.shape),
                  _const_spec(w1bd.shape), _const_spec(w2bd.shape)],
        out_specs=pl.BlockSpec((None, nc, 256), lambda i: (i, 0, 0)),
        compiler_params=_cparams(("parallel",)),
        name="nsa_compress_prompt",
    )(nsa_rows, nsa_rows, pe2, w1bd, w2bd)


SLAB = 64
FLASH_SCRATCH = 2


class _Flash:
    def __init__(self, scratch, slopes, tq, t0):
        self.m, self.acc = scratch
        self.slope8 = jnp.concatenate(
            [jnp.full((1, 1), v, f32) for v in slopes] + [jnp.zeros((8 - len(slopes), 1), f32)], axis=0)
        self.tq, self.t0 = tq, t0
        self.m[...] = jnp.full(self.m.shape, M_INIT, f32)
        self.acc[...] = jnp.zeros(self.acc.shape, f32)

    def slab(self, j, s_all, bias, rel, masked):
        r0 = j * SLAB
        rows = slice(r0, r0 + SLAB)
        head = r0 // self.tq
        s = s_all[rows, :] + bias[head:head + 1, :]
        if masked:
            t_rel = (r0 + lax.broadcasted_iota(jnp.int32, (SLAB, 1), 0)) % self.tq
            s = jnp.where(rel <= t_rel, s, -MASK_BIG)
        m_old = self.m[rows, :]
        m_new = jnp.maximum(m_old, jnp.max(s, axis=-1, keepdims=True))
        self.m[rows, :] = m_new
        p = jnp.exp(s - jnp.concatenate([m_new] * (s.shape[1] // LANES), axis=1))
        return p.astype(bf16), jnp.exp(m_old - m_new)

    def result(self):
        acc = self.acc[...]
        return acc[:, 0:LANES] / acc[:, LANES:2 * LANES]


def _flash_tiles(chains, k0, masked):
    first = chains[0][0]
    rows = first.m.shape[0]
    tk = chains[0][2].shape[1]
    rel = k0 - first.t0 + lax.broadcasted_iota(jnp.int32, (1, tk), 1)
    logits = [(_dot(q_aug, kt_aug), f.slope8 * rel.astype(f32)) for f, q_aug, kt_aug, _ in chains]
    parts = [[] for _ in chains]
    for j in range(rows // SLAB):
        for c, (f, _, _, _) in enumerate(chains):
            parts[c].append(f.slab(j, logits[c][0], logits[c][1], rel, masked))
    for c, (f, _, _, v_aug) in enumerate(chains):
        p = jnp.concatenate([x[0] for x in parts[c]], axis=0)
        alpha = jnp.concatenate([x[1] for x in parts[c]], axis=0)
        f.acc[...] = jnp.concatenate([alpha, alpha], axis=1) * f.acc[...] + _dot(p, v_aug)


def _flash_scratch(rows):
    return [pltpu.VMEM((rows, LANES), f32), pltpu.VMEM((rows, 2 * LANES), f32)]


def _with_ones(v):
    return jnp.concatenate([v, jnp.ones(v.shape, v.dtype)], axis=-1)


def _masked_exp(s_all, slopes, rel, valid_fn, tq):
    out = []
    for j in range(s_all.shape[0] // SLAB):
        r0 = j * SLAB
        t_rel = r0 % tq + lax.broadcasted_iota(jnp.int32, (SLAB, 1), 0)
        valid = valid_fn(t_rel)
        s = jnp.where(valid, s_all[r0:r0 + SLAB, :] + slopes[r0 // tq] * rel.astype(f32), NEG)
        out.append(jnp.exp(s - jnp.maximum(jnp.max(s, axis=-1, keepdims=True), M_INIT)))
    return jnp.concatenate(out, axis=0)


def _topk_rows(score, ids, k, n_ids):
    picked = jnp.zeros(score.shape, f32)
    for _ in range(k):
        mx = jnp.max(score, axis=0, keepdims=True)
        idx = jnp.min(jnp.where(score == mx, ids, n_ids), axis=0, keepdims=True)
        pick = ids == idx
        picked = jnp.where(pick, 1.0, picked)
        score = jnp.where(pick, REMOVED, score)
    return picked


def _nsa_prompt_body(q_ref, gn_ref, kct_ref, vc_ref, agg_ref, kst_ref, oh_ref, vs_ref, kwt_ref, vw_ref, o_ref,
                     sel_scr, qa_scr, part_scr, gate_scr, *flash_scr, tq, tk, n_cmp):
    i = pl.program_id(1)
    t0 = i * tq
    nc = vc_ref.shape[0]
    qf = q_ref[...].astype(f32)
    lane = lax.broadcasted_iota(jnp.int32, (tq, LANES), 1)
    sig = jax.nn.sigmoid(gn_ref[...])
    kct = kct_ref[...]
    vc = vc_ref[...].astype(bf16)
    aggb = agg_ref[...].astype(bf16)
    ones_c = jnp.ones((nc, LANES), bf16)
    n_id = lax.broadcasted_iota(jnp.int32, (1, nc), 1)
    cend_rel = n_id * CMP_STRIDE + (CMP_LEN - 1) - t0
    cend_masked = jnp.where(n_id < n_cmp, cend_rel, 2 ** 30)
    blk_id = lax.broadcasted_iota(jnp.int32, (LANES, tq), 0)
    cur = (t0 + lax.broadcasted_iota(jnp.int32, (LANES, tq), 1)) // SEL_BLOCK
    causal_blk = blk_id <= cur
    forced = (blk_id == 0) | (blk_id == cur) | (blk_id == cur - 1)
    chunks = [jnp.zeros((tq, LANES), f32) for _ in range(NSA_HEADS // 2)]

    for g in range(NSA_KV):
        heads = [HPG * g + hh for hh in range(HPG)]
        pieces = []
        for h in heads:
            blk = qf[:, LANES * (h // 2):LANES * (h // 2 + 1)]
            if h % 2 != g:
                blk = pltpu.roll(blk, HEAD_DIM, 1)
            pieces.append(jnp.where(lane // HEAD_DIM == g, blk, 0.0))
        qg = jnp.concatenate(pieces, axis=0).astype(bf16)
        slopes = [NSA_SLOPES[h] for h in heads]

        valid_c = lambda t_rel: cend_masked <= t_rel
        e_c = _masked_exp(_dot(qg, kct), slopes, cend_rel, valid_c, tq)
        e_hi = e_c.astype(bf16)
        e_lo = (e_c - e_hi.astype(f32)).astype(bf16)
        r_hi = _dot(e_hi, jnp.concatenate([vc, ones_c, aggb], axis=1))
        r_lo = _dot(e_lo, jnp.concatenate([aggb, ones_c], axis=1))
        o_c = r_hi[:, 0:LANES] / jnp.maximum(r_hi[:, LANES:2 * LANES], 1e-30)
        imp_rows = (r_hi[:, 2 * LANES:3 * LANES] + r_lo[:, 0:LANES]) / jnp.maximum(
            r_hi[:, LANES:2 * LANES] + r_lo[:, LANES:2 * LANES], 1e-30)
        imp = imp_rows[0:tq] + imp_rows[tq:2 * tq] + imp_rows[2 * tq:3 * tq] + imp_rows[3 * tq:4 * tq]

        score = jnp.where(causal_blk, jnp.where(forced, FORCE, imp.T), NEG)
        sel_t = jnp.where(causal_blk, _topk_rows(score, blk_id, SEL_TOPN, LANES), 0.0)
        notsel = (1.0 - sel_t).T
        qa_scr[g] = jnp.concatenate([qg, jnp.concatenate([notsel] * HPG, axis=0).astype(bf16)], axis=1)
        sel_scr[g] = sel_t

        wl = WINDOW + tq
        s0 = pl.multiple_of(jnp.maximum(t0 - WINDOW, 0), tq)
        w_rel = s0 - t0 + lax.broadcasted_iota(jnp.int32, (1, wl), 1)
        valid_w = lambda t_rel: (w_rel <= t_rel) & (w_rel > t_rel - WINDOW)
        e_w = _masked_exp(_dot(qg, kwt_ref[:, pl.ds(s0, wl)]), slopes, w_rel, valid_w, tq)
        r_w = _dot(e_w.astype(bf16), _with_ones(vw_ref[pl.ds(s0, wl), :]))
        o_w = r_w[:, 0:LANES] / r_w[:, LANES:2 * LANES]

        gates = [jnp.concatenate([sig[:, br * NSA_HEADS + h:br * NSA_HEADS + h + 1] for h in heads], axis=0)
                 for br in range(3)]
        part_scr[g] = gates[0] * o_c + gates[2] * o_w
        gate_scr[g] = gates[1]

    flashes = [_Flash(flash_scr[FLASH_SCRATCH * g:FLASH_SCRATCH * (g + 1)],
                      NSA_SLOPES[HPG * g:HPG * (g + 1)], tq, t0) for g in range(NSA_KV)]

    def sel_step(kt, masked):
        k0 = pl.multiple_of(kt * tk, tk)
        kt_aug = jnp.concatenate([kst_ref[:, pl.ds(k0, tk)], oh_ref[:, pl.ds(k0, tk)]], axis=0)
        v_aug = _with_ones(vs_ref[pl.ds(k0, tk), :])
        _flash_tiles([(flashes[g], qa_scr[g], kt_aug, v_aug) for g in range(NSA_KV)], k0, masked)

    def sel_loop(kt, carry):
        blk0 = pl.multiple_of(kt * (tk // SEL_BLOCK), tk // SEL_BLOCK)
        picked = jnp.maximum(sel_scr[0, pl.ds(blk0, tk // SEL_BLOCK), :], sel_scr[1, pl.ds(blk0, tk // SEL_BLOCK), :])

        @pl.when(jnp.max(picked) > 0.0)
        def _():
            sel_step(kt, False)

        return carry

    kd = t0 // tk
    lax.fori_loop(0, kd, sel_loop, 0)
    sel_step(kd, True)

    for g in range(NSA_KV):
        heads = [HPG * g + hh for hh in range(HPG)]
        o = part_scr[g] + gate_scr[g] * flashes[g].result()
        for hh, h in enumerate(heads):
            piece = o[hh * tq:(hh + 1) * tq]
            if h % 2 != g:
                piece = pltpu.roll(piece, HEAD_DIM, 1)
            chunks[h // 2] = chunks[h // 2] + jnp.where(lane // HEAD_DIM == h % 2, piece, 0.0)

    o_ref[...] = jnp.concatenate(chunks, axis=1).astype(o_ref.dtype)


def _nsa_prompt(q_n, g_n, cmp, agg, nsa_bf, win_bf, n_cmp, tq, tk):
    b, l, _ = q_n.shape
    nc = cmp.shape[1]
    rows = HPG * tq
    kct = jnp.swapaxes(cmp[:, :, 0:LANES], 1, 2).astype(bf16)
    onehot_t = _block_onehot_t(SEL_BLOCK, l)
    kst = jnp.swapaxes(nsa_bf[:, :, 2 * LANES:3 * LANES], 1, 2)
    kwt = jnp.swapaxes(win_bf[:, :, 0:LANES], 1, 2)
    per_b = lambda shape, col=0: pl.BlockSpec((None,) + shape, lambda bi, i: (bi, 0, col))
    return pl.pallas_call(
        functools.partial(_nsa_prompt_body, tq=tq, tk=tk, n_cmp=n_cmp),
        out_shape=jax.ShapeDtypeStruct((b, l, NSA_HEADS * HEAD_DIM), bf16),
        grid=(b, l // tq),
        in_specs=[pl.BlockSpec((None, tq, 512), lambda bi, i: (bi, i, 0)),
                  pl.BlockSpec((None, tq, LANES), lambda bi, i: (bi, i, 0)),
                  per_b((LANES, nc)), per_b((nc, LANES), 1), _const_spec(agg.shape),
                  per_b((LANES, l)), _const_spec(onehot_t.shape), per_b((l, LANES), 3), per_b((LANES, l)),
                  per_b((l, LANES), 1)],
        out_specs=pl.BlockSpec((None, tq, 512), lambda bi, i: (bi, i, 0)),
        scratch_shapes=[pltpu.VMEM((NSA_KV, LANES, tq), f32), pltpu.VMEM((NSA_KV, rows, 2 * LANES), bf16),
                        pltpu.VMEM((NSA_KV, rows, LANES), f32), pltpu.VMEM((NSA_KV, rows, 1), f32)]
        + _flash_scratch(rows) * NSA_KV,
        compiler_params=_cparams(("parallel", "parallel")),
        name="nsa_attention_prompt",
    )(q_n, g_n, kct, cmp, agg, kst, onehot_t, nsa_bf, kwt, win_bf)


def _block_mean_body(k_ref, o_ref):
    o_ref[...] = jnp.mean(k_ref[...], axis=0, keepdims=True)


def _block_mean(rows, width, blk):
    b, l, _ = rows.shape
    return pl.pallas_call(
        _block_mean_body,
        out_shape=jax.ShapeDtypeStruct((b, l // blk, 1, width), f32),
        grid=(b, l // blk),
        in_specs=[pl.BlockSpec((None, blk, width), lambda bi, j: (bi, j, 0))],
        out_specs=pl.BlockSpec((None, None, 1, width), lambda bi, j: (bi, j, 0, 0)),
        compiler_params=_cparams(("parallel", "parallel")),
        name="moba_block_mean",
    )(rows)


def _moba_prompt_body(q_ref, kmt_ref, kt_ref, oh_ref, v_ref, o_ref, qa_scr, *flash_scr, tq, tk, nb_pad):
    i = pl.program_id(1)
    t0 = i * tq
    q = q_ref[...]
    qf = q.astype(f32)
    lane = lax.broadcasted_iota(jnp.int32, (tq, LANES), 1)

    gate_t = _nt(kmt_ref[...].astype(bf16), q)
    blk_id = lax.broadcasted_iota(jnp.int32, (LANES, tq), 0) % nb_pad
    own = (t0 + lax.broadcasted_iota(jnp.int32, (LANES, tq), 1)) // MOBA_BLOCK
    past = blk_id < own
    score = jnp.where(past, gate_t, NEG)
    parts = [_topk_rows(score[nb_pad * h:nb_pad * (h + 1)], blk_id[nb_pad * h:nb_pad * (h + 1)], MOBA_TOPK, nb_pad)
             for h in range(MOBA_HEADS)]
    sel_t = jnp.where(past, jnp.concatenate(parts, axis=0), 0.0)
    sel_t = jnp.where(blk_id == own, 1.0, sel_t)
    notsel = (1.0 - sel_t).T

    nch = MOBA_HEADS // 2
    for c in range(nch):
        q_rows = []
        for e in range(2):
            h = 2 * c + e
            qh = jnp.where(lane // HEAD_DIM == e, qf[:, LANES * c:LANES * (c + 1)], 0.0)
            ns = notsel if h == 0 else pltpu.roll(notsel, LANES - nb_pad * h, 1)
            ns = jnp.where(lane < nb_pad, ns, 0.0)
            q_rows.append(jnp.concatenate([qh, ns], axis=1))
        qa_scr[c] = jnp.concatenate(q_rows, axis=0).astype(bf16)
    flashes = [_Flash(flash_scr[FLASH_SCRATCH * c:FLASH_SCRATCH * (c + 1)],
                      MOBA_SLOPES[2 * c:2 * c + 2], tq, t0) for c in range(nch)]

    def step(kt, masked):
        k0 = pl.multiple_of(kt * tk, tk)
        onehot = oh_ref[:, pl.ds(k0, tk)]
        _flash_tiles([(flashes[c], qa_scr[c], jnp.concatenate([kt_ref[c, :, pl.ds(k0, tk)], onehot], axis=0),
                       _with_ones(v_ref[pl.ds(k0, tk), LANES * c:LANES * (c + 1)])) for c in range(nch)], k0, masked)

    def loop(kt, carry):
        step(kt, False)
        return carry

    lax.fori_loop(0, t0 // tk, loop, 0)
    step(t0 // tk, True)
    out_chunks = []
    for c in range(nch):
        o = flashes[c].result()
        out_chunks.append(jnp.where(lane < HEAD_DIM, o[0:tq], o[tq:2 * tq]))
    o_ref[...] = jnp.concatenate(out_chunks, axis=1).astype(o_ref.dtype)


def _moba_prompt(mb_q, kmt, moba_bf, tq, tk):
    b, l, _ = mb_q.shape
    nb_pad = LANES // MOBA_HEADS
    nch = MOBA_HEADS // 2
    onehot_t = _block_onehot_t(MOBA_BLOCK, l)
    k_t = jnp.swapaxes(jnp.swapaxes(moba_bf[:, :, 0:nch * LANES].reshape(b, l, nch, LANES), 1, 3), 1, 2)
    return pl.pallas_call(
        functools.partial(_moba_prompt_body, tq=tq, tk=tk, nb_pad=nb_pad),
        out_shape=jax.ShapeDtypeStruct((b, l, MOBA_HEADS * HEAD_DIM), bf16),
        grid=(b, l // tq),
        in_specs=[pl.BlockSpec((None, tq, 256), lambda bi, i: (bi, i, 0)),
                  pl.BlockSpec((None, LANES, 256), lambda bi, i: (bi, 0, 0)),
                  pl.BlockSpec((None, nch, LANES, l), lambda bi, i: (bi, 0, 0, 0)),
                  _const_spec(onehot_t.shape),
                  pl.BlockSpec((None, l, nch * LANES), lambda bi, i: (bi, 0, 1))],
        out_specs=pl.BlockSpec((None, tq, 256), lambda bi, i: (bi, i, 0)),
        scratch_shapes=[pltpu.VMEM((nch, 2 * tq, 2 * LANES), bf16)] + _flash_scratch(2 * tq) * nch,
        compiler_params=_cparams(("parallel", "parallel")),
        name="moba_attention_prompt",
    )(mb_q, kmt, k_t, onehot_t, moba_bf)


def _log_keep(z):
    return -(jnp.maximum(z, 0.0) + jnp.log(1.0 + jnp.exp(-jnp.abs(z))))


def _suffix_sums(lk, uu):
    hi = lk.astype(bf16)
    lo = (lk - hi.astype(f32)).astype(bf16)
    r = _dot(jnp.concatenate([hi, lo], axis=1), uu)
    return r[:, 0:LANES], r[:, LANES:2 * LANES]


def _sb_prompt_body(q_ref, uu_ref, k_ref, v_ref, o_ref, carry_scr, acc_scr, *, tq, tk):
    i = pl.program_id(2)
    t0 = i * tq
    qf = q_ref[...].astype(f32)
    lane = lax.broadcasted_iota(jnp.int32, (tq, LANES), 1)
    q2 = jnp.concatenate([jnp.where(lane // HEAD_DIM == e, qf, 0.0) for e in range(2)], axis=0).astype(bf16)
    t_row = t0 + lax.broadcasted_iota(jnp.int32, (2 * tq, 1), 0) % tq
    uu = uu_ref[...]
    carry_scr[...] = jnp.zeros(carry_scr.shape, f32)
    acc_scr[...] = jnp.zeros(acc_scr.shape, f32)

    def step(kt, diag):
        k0 = pl.multiple_of(kt * tk, tk)
        z = _dot(q2, k_ref[:, pl.ds(k0, tk)])
        lk = _log_keep(z)
        if diag:
            is_past = (k0 + lax.broadcasted_iota(jnp.int32, (1, tk), 1)) < t_row
            lk = jnp.where(is_past, lk, 0.0)
        carry = carry_scr[...]
        between = [None] * (tk // LANES)
        for c in reversed(range(tk // LANES)):
            later, total = _suffix_sums(lk[:, LANES * c:LANES * (c + 1)], uu)
            between[c] = later + carry
            carry = carry + total
        w = jnp.exp(z + lk + jnp.concatenate(between, axis=1))
        if diag:
            w = jnp.where(is_past, w, 0.0)
        acc_scr[...] = acc_scr[...] + _dot(w.astype(bf16), v_ref[pl.ds(k0, tk), :])
        carry_scr[...] = carry

    kd = t0 // tk
    step(kd, True)

    def more(state):
        j, top = state
        return (j < kd) & (top > SB_CUTOFF)

    def walk(state):
        j, _ = state
        step(kd - 1 - j, False)
        return j + 1, jnp.max(carry_scr[...])

    lax.while_loop(more, walk, (0, jnp.max(carry_scr[...])))
    o = acc_scr[...]
    o_ref[...] = jnp.where(lane < HEAD_DIM, o[0:tq], o[tq:2 * tq]).astype(o_ref.dtype)


def _sb_prompt(sb_q, uu, sb_bf, tq, tk):
    b, l, _ = sb_q.shape
    nch = SB_HEADS // 2
    k_t = jnp.swapaxes(jnp.swapaxes(sb_bf[:, :, 0:nch * LANES].reshape(b, l, nch, LANES), 1, 3), 1, 2)
    return pl.pallas_call(
        functools.partial(_sb_prompt_body, tq=tq, tk=tk),
        out_shape=jax.ShapeDtypeStruct((b, l, SB_HEADS * HEAD_DIM), bf16),
        grid=(b, nch, l // tq),
        in_specs=[pl.BlockSpec((None, tq, LANES), lambda bi, c, i: (bi, i, c)),
                  _const_spec(uu.shape),
                  pl.BlockSpec((None, None, LANES, l), lambda bi, c, i: (bi, c, 0, 0)),
                  pl.BlockSpec((None, l, LANES), lambda bi, c, i: (bi, 0, nch + c))],
        out_specs=pl.BlockSpec((None, tq, LANES), lambda bi, c, i: (bi, i, c)),
        scratch_shapes=[pltpu.VMEM((2 * tq, LANES), f32), pltpu.VMEM((2 * tq, LANES), f32)],
        compiler_params=_cparams(("parallel", "parallel", "parallel")),
        name="stickbreak_attention_prompt",
    )(sb_q, uu, k_t, sb_bf)


def _merge_body(oa_ref, ob_ref, oc_ref, gm_ref, x_ref, gate_ref, wa_ref, wb_ref, wc_ref, wo_ref, o_ref):
    d = x_ref.shape[1]
    g = jax.nn.sigmoid(gm_ref[...])
    merged = (g[:, 0:d] * _dot(oa_ref[...], wa_ref[...]) + g[:, d:2 * d] * _dot(ob_ref[...], wb_ref[...])
              + g[:, 2 * d:3 * d] * _dot(oc_ref[...], wc_ref[...]))
    o_ref[...] = x_ref[...] + gate_ref[...] * _dot(merged.astype(bf16), wo_ref[...])


def _merge_out(o_a, o_b, o_c, g_m, x, gate, wa, wb, wc, wo, tm):
    r, d = x.shape
    nb, rb, _ = gate.shape
    tiles_per_b = (r // nb) // tm
    row = lambda w: pl.BlockSpec((tm, w), lambda i: (i, 0))
    return pl.pallas_call(
        _merge_body,
        out_shape=jax.ShapeDtypeStruct((r, d), f32),
        grid=(r // tm,),
        in_specs=[row(o_a.shape[1]), row(o_b.shape[1]), row(o_c.shape[1]), row(3 * d), row(d),
                  pl.BlockSpec((None, rb, d), lambda i: (i // tiles_per_b, 0, 0)),
                  _const_spec(wa.shape), _const_spec(wb.shape), _const_spec(wc.shape), _const_spec(wo.shape)],
        out_specs=row(d),
        compiler_params=_cparams(("parallel",)),
        name="merge_out_proj",
    )(o_a, o_b, o_c, g_m, x, gate, wa, wb, wc, wo)


def _ffn_tail(a, a_m1, a_m2, b, cw_ref, cb_ref, wd_ref, x_ref, gate_ref, o_ref):
    conv = cb_ref[...] + a_m2 * cw_ref[0:1, :]
    conv = conv + a_m1 * cw_ref[1:2, :]
    conv = conv + a * cw_ref[2:3, :]
    y = _dot((_gelu_tanh(conv) * b).astype(bf16), wd_ref[...])
    o_ref[...] = x_ref[...] + gate_ref[...] * y


def _ffn_seq_body(a_ref, halo_ref, b_ref, cw_ref, cb_ref, wd_ref, x_ref, gate_ref, o_ref, *, tiles_per_b):
    a = a_ref[...]
    first = pl.program_id(0) % tiles_per_b == 0
    halo = jnp.where(first, 0.0, halo_ref[...])
    row = lax.broadcasted_iota(jnp.int32, a.shape, 0)
    a_m1 = jnp.where(row < 1, halo[7:8, :], pltpu.roll(a, 1, 0))
    a_m2 = jnp.where(row < 1, halo[6:7, :], jnp.where(row < 2, halo[7:8, :], pltpu.roll(a, 2, 0)))
    _ffn_tail(a, a_m1, a_m2, b_ref[...], cw_ref, cb_ref, wd_ref, x_ref, gate_ref, o_ref)


def _ffn_step_body(a_ref, am1_ref, am2_ref, b_ref, cw_ref, cb_ref, wd_ref, x_ref, gate_ref, o_ref):
    _ffn_tail(a_ref[...], am1_ref[...], am2_ref[...], b_ref[...], cw_ref, cb_ref, wd_ref, x_ref, gate_ref, o_ref)


def _ffn_down_seq(u_a, u_b, cw8, cb, wd, x, gate, tm):
    r, d = x.shape
    ff = u_a.shape[1]
    nb = gate.shape[0]
    tiles_per_b = (r // nb) // tm
    row = lambda w: pl.BlockSpec((tm, w), lambda i: (i, 0))
    return pl.pallas_call(
        functools.partial(_ffn_seq_body, tiles_per_b=tiles_per_b),
        out_shape=jax.ShapeDtypeStruct((r, d), f32),
        grid=(r // tm,),
        in_specs=[row(ff), pl.BlockSpec((8, ff), lambda i: (jnp.maximum(i * (tm // 8) - 1, 0), 0)), row(ff),
                  _const_spec(cw8.shape), _const_spec(cb.shape), _const_spec(wd.shape), row(d),
                  pl.BlockSpec((None, 1, d), lambda i: (i // tiles_per_b, 0, 0))],
        out_specs=row(d),
        compiler_params=_cparams(("parallel",)),
        name="conv_ffn_down_seq",
    )(u_a, u_a, u_b, cw8, cb, wd, x, gate)


def _ffn_down_step(u_a, a_m1, a_m2, u_b, cw8, cb, wd, x, gate):
    r, d = x.shape
    full = lambda a: pl.BlockSpec(a.shape, lambda i: (0,) * a.ndim)
    return pl.pallas_call(
        _ffn_step_body,
        out_shape=jax.ShapeDtypeStruct((r, d), f32),
        grid=(1,),
        in_specs=[full(u_a), full(a_m1), full(a_m2), full(u_b), full(cw8), full(cb), full(wd), full(x),
                  pl.BlockSpec((None, r, d), lambda i: (0, 0, 0))],
        out_specs=full(x),
        compiler_params=_cparams(("arbitrary",)),
        name="conv_ffn_down_step",
    )(u_a, a_m1, a_m2, u_b, cw8, cb, wd, x, gate)


def _final_norm_body(x_ref, g_ref, o_ref):
    x = x_ref[...]
    o_ref[...] = x * lax.rsqrt(jnp.mean(x * x, axis=-1, keepdims=True) + RMS_EPS) * g_ref[...]


def _final_norm(x, g, tm):
    r, d = x.shape
    return pl.pallas_call(
        _final_norm_body,
        out_shape=jax.ShapeDtypeStruct((r, d), f32),
        grid=(r // tm,),
        in_specs=[pl.BlockSpec((tm, d), lambda i: (i, 0)), _const_spec((1, d))],
        out_specs=pl.BlockSpec((tm, d), lambda i: (i, 0)),
        compiler_params=_cparams(("parallel",)),
        name="final_rmsnorm",
    )(x, g.reshape(1, d))


def _page_specs(n_pages, layer_base, per_step=1, k=0):
    return [pl.BlockSpec((None, 512, PAGE_SIZE),
                         functools.partial(lambda s, pt, j: (layer_base + pt[(s * per_step + k) * n_pages + j], 0, 0),
                                           j=j))
            for j in range(n_pages)]


def _per_seq(width):
    return pl.BlockSpec((None, 1, width), lambda s, pt: (s, 0, 0))


def _dec_const(shape):
    nd = len(shape)
    return pl.BlockSpec(shape, lambda s, pt: (0,) * nd, pipeline_mode=pl.Buffered(1))


def _head_rows(q_row):
    row = lax.broadcasted_iota(jnp.int32, (8, LANES), 0)
    lane = lax.broadcasted_iota(jnp.int32, (8, LANES), 1)
    q8 = jnp.broadcast_to(q_row, (8, q_row.shape[1]))
    qsel = jnp.zeros((8, LANES), f32)
    for c in range(NSA_HEADS // 2):
        qsel = qsel + jnp.where(row // 2 == c, q8[:, LANES * c:LANES * (c + 1)], 0.0)
    swap = (row % 2) != (row // HPG)
    qm = jnp.where(swap, pltpu.roll(qsel, HEAD_DIM, 1), qsel)
    return jnp.where(lane // HEAD_DIM == row // HPG, qm, 0.0), swap


def _rank_select(score_row, k):
    a = jnp.broadcast_to(score_row, (LANES, LANES))
    b = a.T
    ii = lax.broadcasted_iota(jnp.int32, (LANES, LANES), 0)
    jj = lax.broadcasted_iota(jnp.int32, (LANES, LANES), 1)
    ahead = (b > a) | ((b == a) & (ii < jj))
    rank = jnp.sum(jnp.where(ahead, 1.0, 0.0), axis=0, keepdims=True)
    return jnp.where(rank < k, 1.0, 0.0)


def _nsa_dec_body(pt_ref, q_ref, gn_ref, new_ref, wnew_ref, sw_ref, pe_ref, w1_ref, w2_ref, agg_ref, e_ref,
                  *rest, n_pages, per_step):
    pages, o_ref = rest[:n_pages * per_step], rest[n_pages * per_step]
    kc_scr, vc_scr = rest[n_pages * per_step + 1:]
    past = n_pages * PAGE_SIZE
    for j in range(n_pages * per_step):
        kc_scr[PAGE_SIZE * j:PAGE_SIZE * (j + 1), :] = pages[j][0:LANES, :].T
        vc_scr[PAGE_SIZE * j:PAGE_SIZE * (j + 1), :] = pages[j][LANES:2 * LANES, :].T

    nc = past // CMP_STRIDE
    cmp_all = _cmp_core((kc_scr, vc_scr), pe_ref, w1_ref, w2_ref, nc * per_step)
    for k in range(per_step):
        o_ref[k] = _nsa_dec_one(q_ref[k], gn_ref[k], new_ref[k], wnew_ref[k], sw_ref.at[k],
                                cmp_all[nc * k:nc * (k + 1)], pages[n_pages * k:n_pages * (k + 1)], agg_ref, e_ref)


def _nsa_dec_one(q_row, gn_row, new_row, wnew, sw_ref, cmpv, pages, agg_ref, e_ref):
    n_pages = len(pages)
    past = n_pages * PAGE_SIZE
    n_cmp = (past + 1 - CMP_LEN) // CMP_STRIDE + 1
    row = lax.broadcasted_iota(jnp.int32, (8, LANES), 0)
    lane = lax.broadcasted_iota(jnp.int32, (8, LANES), 1)
    qm, swap = _head_rows(q_row)
    qmb = qm.astype(bf16)
    slope = jnp.zeros((8, 1), f32)
    row1 = lax.broadcasted_iota(jnp.int32, (8, 1), 0)
    for h in range(NSA_HEADS):
        slope = jnp.where(row1 == h, NSA_SLOPES[h], slope)
    grp0 = row < HPG

    d_c = past - (lane[0:1] * CMP_STRIDE + CMP_LEN - 1)
    s_c = _nt(qmb, cmpv[:, 0:LANES].astype(bf16)) - slope * d_c.astype(f32)
    p_c = _ref_softmax(s_c, (d_c >= 0) & (lane[0:1] < n_cmp))
    o_c = _dot(p_c.astype(bf16), cmpv[:, LANES:2 * LANES].astype(bf16))
    aggb = agg_ref[...].astype(bf16)
    p_hi = p_c.astype(bf16)
    p_lo = (p_c - p_hi.astype(f32)).astype(bf16)
    imp_rows = _dot(p_hi, aggb) + _dot(p_lo, aggb)

    cur = past // SEL_BLOCK
    blk = lane[0:1]
    forced = (blk == 0) | (blk == cur) | (blk == cur - 1)
    causal = blk <= cur
    notsel_g = []
    for g in range(NSA_KV):
        imp = jnp.sum(imp_rows[HPG * g:HPG * (g + 1)], axis=0, keepdims=True)
        score = jnp.where(causal, jnp.where(forced, FORCE, imp), NEG)
        sel = jnp.where(causal, _rank_select(score, min(SEL_TOPN, -(-(past + 1) // SEL_BLOCK))), 0.0)
        notsel_g.append(jnp.broadcast_to(1.0 - sel, (8, LANES)))
    notsel = jnp.where(grp0, notsel_g[0], notsel_g[1])

    dist = past - lax.broadcasted_iota(jnp.int32, (1, past), 1)
    picked = _dot(notsel.astype(bf16), e_ref[...]) < 0.5
    ks_t = jnp.concatenate([pages[j][2 * LANES:3 * LANES, :].astype(bf16) for j in range(n_pages)], axis=1)
    vs_t = jnp.concatenate([pages[j][3 * LANES:4 * LANES, :].astype(bf16) for j in range(n_pages)], axis=1)
    s_s = jnp.where(picked, _dot(qmb, ks_t) - slope * dist.astype(f32), NEG)
    s_n = jnp.sum(qm * new_row[:, 2 * LANES:3 * LANES], axis=-1, keepdims=True)
    m_s = jnp.maximum(jnp.max(s_s, axis=-1, keepdims=True), s_n)
    e_s = jnp.where(picked, jnp.exp(s_s - m_s), 0.0)
    e_n = jnp.exp(s_n - m_s)
    den = jnp.maximum(jnp.sum(e_s, axis=-1, keepdims=True) + e_n, 1e-30)
    o_s = (e_n * new_row[:, 3 * LANES:4 * LANES] + _nt(e_s.astype(bf16), vs_t)) / den

    wb = sw_ref.shape[1]
    d_w = wb - lax.broadcasted_iota(jnp.int32, (1, wb), 1)
    s_w = _dot(qmb, sw_ref[0:LANES, :].astype(bf16)) - slope * d_w.astype(f32)
    valid_w = (d_w < WINDOW) & (d_w >= 0)
    s_w = jnp.where(valid_w, s_w, NEG)
    s_n = jnp.sum(qm * wnew[:, 0:LANES], axis=-1, keepdims=True)
    m_w = jnp.maximum(jnp.max(s_w, axis=-1, keepdims=True), s_n)
    e_w = jnp.where(valid_w, jnp.exp(s_w - m_w), 0.0)
    e_n = jnp.exp(s_n - m_w)
    den = jnp.maximum(jnp.sum(e_w, axis=-1, keepdims=True) + e_n, 1e-30)
    o_w = (_nt(e_w.astype(bf16), sw_ref[LANES:2 * LANES, :].astype(bf16)) + e_n * wnew[:, LANES:2 * LANES]) / den

    sig = jnp.broadcast_to(jax.nn.sigmoid(gn_row), (8, LANES))
    gates = [jnp.sum(jnp.where(lane == br * NSA_HEADS + row, sig, 0.0), axis=-1, keepdims=True) for br in range(3)]
    o = gates[0] * o_c + gates[1] * o_s + gates[2] * o_w
    o = jnp.where(swap, pltpu.roll(o, HEAD_DIM, 1), o)
    o = jnp.where(lane // HEAD_DIM == row % 2, o, 0.0)
    return jnp.concatenate([o[2 * c:2 * c + 1] + o[2 * c + 1:2 * c + 2] for c in range(NSA_HEADS // 2)], axis=1)


def _nsa_decode(page_table, cache_t, layer_base, q, g_n, nsa_new, win_new, state_win_t, win_base, pe4, w1bd, w2bd,
                agg, e_sel, per_step):
    s, n_pages = page_table.shape
    past = n_pages * PAGE_SIZE
    consts = (pe4, w1bd, w2bd, agg, e_sel)
    seqs = lambda width: pl.BlockSpec((per_step, 1, width), lambda si, pt: (si, 0, 0))
    page_specs = [spec for k in range(per_step) for spec in _page_specs(n_pages, layer_base, per_step, k)]
    grid_spec = pltpu.PrefetchScalarGridSpec(
        num_scalar_prefetch=1,
        grid=(s // per_step,),
        in_specs=[seqs(512), seqs(LANES), seqs(512), seqs(256),
                  pl.BlockSpec((per_step,) + state_win_t.shape[1:],
                               lambda si, pt: (win_base // per_step + si, 0, 0))]
        + [_dec_const(c.shape) for c in consts] + page_specs,
        out_specs=seqs(512),
        scratch_shapes=[pltpu.VMEM((per_step * past, LANES), f32), pltpu.VMEM((per_step * past, LANES), f32)],
    )
    return pl.pallas_call(
        functools.partial(_nsa_dec_body, n_pages=n_pages, per_step=per_step),
        out_shape=jax.ShapeDtypeStruct((s, 1, 512), f32),
        grid_spec=grid_spec,
        compiler_params=_cparams(("parallel",)),
        name="nsa_attention_decode",
    )(page_table.reshape(-1), q, g_n, nsa_new, win_new, state_win_t, *consts,
      *([cache_t] * (n_pages * per_step)))


def _win_shift_body(sw_ref, new_ref, o_ref):
    n, feat, wb = sw_ref.shape
    lane = lax.broadcasted_iota(jnp.int32, (feat, wb), 1)
    for k in range(n):
        col = jnp.broadcast_to(new_ref[k], (LANES, feat)).T
        col = jnp.concatenate([col] * (wb // LANES), axis=1)
        o_ref[k] = jnp.where(lane == wb - 1, col, pltpu.roll(sw_ref[k], wb - 1, 1))


def _win_shift(state_win_t, new_rows, per_step):
    n, feat, wb = state_win_t.shape
    return pl.pallas_call(
        _win_shift_body,
        out_shape=jax.ShapeDtypeStruct((n, feat, wb), f32),
        grid=(n // per_step,),
        in_specs=[pl.BlockSpec((per_step, feat, wb), lambda i: (i, 0, 0)),
                  pl.BlockSpec((per_step, 1, feat), lambda i: (i, 0, 0))],
        out_specs=pl.BlockSpec((per_step, feat, wb), lambda i: (i, 0, 0)),
        compiler_params=_cparams(("parallel",)),
        name="window_state_shift",
    )(state_win_t, new_rows)


def _q_head_rows4(q_row):
    row = lax.broadcasted_iota(jnp.int32, (8, 256), 0)
    lane = lax.broadcasted_iota(jnp.int32, (8, 256), 1)
    own = lane // HEAD_DIM == row
    return jnp.where(own, jnp.broadcast_to(q_row, (8, 256)), 0.0), own


def _moba_dec_body(pt_ref, q_ref, new_ref, seg_ref, e_ref, *rest, n_pages):
    pages, o_ref = rest[:n_pages], rest[n_pages]
    past = n_pages * PAGE_SIZE
    nb_past = past // MOBA_BLOCK
    kw = MOBA_HEADS * HEAD_DIM
    qm, own_lanes = _q_head_rows4(q_ref[...])
    qmb = qm.astype(bf16)
    k_t = jnp.concatenate([pages[j][0:kw, :].astype(bf16) for j in range(n_pages)], axis=1)
    v_t = jnp.concatenate([pages[j][kw:2 * kw, :].astype(bf16) for j in range(n_pages)], axis=1)
    raw = _dot(qmb, k_t)
    raw_hi = raw.astype(bf16)
    raw_lo = (raw - raw_hi.astype(f32)).astype(bf16)
    gate = _dot(raw_hi, seg_ref[...]) + _dot(raw_lo, seg_ref[...])

    lane = lax.broadcasted_iota(jnp.int32, (8, LANES), 1)
    is_past = lane < nb_past
    score = jnp.where(is_past, gate, NEG)
    rank = jnp.zeros((8, LANES), f32)
    for i in range(nb_past):
        gi = score[:, i:i + 1]
        rank = rank + jnp.where((gi > score) | ((gi == score) & (i < lane)), 1.0, 0.0)
    sel = jnp.where(is_past & (rank < min(MOBA_TOPK, nb_past)), 1.0, 0.0)
    picked = _dot(sel.astype(bf16), e_ref[...]) > 0.5

    row1 = lax.broadcasted_iota(jnp.int32, (8, 1), 0)
    slope = jnp.zeros((8, 1), f32)
    for h in range(MOBA_HEADS):
        slope = jnp.where(row1 == h, MOBA_SLOPES[h], slope)
    dist = past - lax.broadcasted_iota(jnp.int32, (1, past), 1)
    new_row = new_ref[...]
    s = jnp.where(picked, raw - slope * dist.astype(f32), NEG)
    s_n = jnp.sum(qm * new_row[:, 0:kw], axis=-1, keepdims=True)
    m = jnp.maximum(jnp.max(s, axis=-1, keepdims=True), s_n)
    e = jnp.where(picked, jnp.exp(s - m), 0.0)
    e_n = jnp.exp(s_n - m)
    den = jnp.maximum(jnp.sum(e, axis=-1, keepdims=True) + e_n, 1e-30)
    o = (e_n * new_row[:, kw:2 * kw] + _nt(e.astype(bf16), v_t)) / den
    o_ref[...] = jnp.sum(jnp.where(own_lanes, o, 0.0), axis=0, keepdims=True)


def _moba_decode(page_table, cache_t, layer_base, q, moba_new, seg_mean, e_blk):
    s, n_pages = page_table.shape
    grid_spec = pltpu.PrefetchScalarGridSpec(
        num_scalar_prefetch=1,
        grid=(s,),
        in_specs=[_per_seq(256), _per_seq(512), _dec_const(seg_mean.shape), _dec_const(e_blk.shape)]
        + _page_specs(n_pages, layer_base),
        out_specs=_per_seq(256),
    )
    return pl.pallas_call(
        functools.partial(_moba_dec_body, n_pages=n_pages),
        out_shape=jax.ShapeDtypeStruct((s, 1, 256), f32),
        grid_spec=grid_spec,
        compiler_params=_cparams(("parallel",)),
        name="moba_attention_decode",
    )(page_table.reshape(-1), q, moba_new, seg_mean, e_blk, *([cache_t] * n_pages))


def _sb_dec_body(pt_ref, q_ref, uu_ref, *rest, n_pages):
    pages, o_ref = rest[:n_pages], rest[n_pages]
    kw = SB_HEADS * HEAD_DIM
    qm, own_lanes = _q_head_rows4(q_ref[...])
    qmb = qm.astype(bf16)
    k_t = jnp.concatenate([pages[j][0:kw, :].astype(bf16) for j in range(n_pages)], axis=1)
    v_t = jnp.concatenate([pages[j][kw:2 * kw, :].astype(bf16) for j in range(n_pages)], axis=1)
    z = _dot(qmb, k_t)
    lk = _log_keep(z)
    stacked = jnp.concatenate([lk[:, PAGE_SIZE * j:PAGE_SIZE * (j + 1)] for j in range(n_pages)], axis=0)
    later, total = _suffix_sums(stacked, uu_ref[...])
    carry = jnp.zeros((8, LANES), f32)
    between = [None] * n_pages
    for j in reversed(range(n_pages)):
        between[j] = later[8 * j:8 * (j + 1)] + carry
        carry = carry + total[8 * j:8 * (j + 1)]
    w = jnp.exp(z + lk + jnp.concatenate(between, axis=1))
    acc = _nt(w.astype(bf16), v_t)
    o_ref[...] = jnp.sum(jnp.where(own_lanes, acc, 0.0), axis=0, keepdims=True)


def _sb_decode(page_table, cache_t, layer_base, q, uu):
    s, n_pages = page_table.shape
    grid_spec = pltpu.PrefetchScalarGridSpec(
        num_scalar_prefetch=1,
        grid=(s,),
        in_specs=[_per_seq(256), _dec_const(uu.shape)] + _page_specs(n_pages, layer_base),
        out_specs=_per_seq(256),
    )
    return pl.pallas_call(
        functools.partial(_sb_dec_body, n_pages=n_pages),
        out_shape=jax.ShapeDtypeStruct((s, 1, 256), f32),
        grid_spec=grid_spec,
        compiler_params=_cparams(("parallel",)),
        name="stickbreak_attention_decode",
    )(page_table.reshape(-1), q, uu, *([cache_t] * n_pages))


def _agg_matrix(nc, n_cmp):
    c0 = np.arange(nc)[:, None] * CMP_STRIDE
    s0 = np.arange(LANES)[None, :] * SEL_BLOCK
    ov = np.clip(np.minimum(c0 + CMP_LEN, s0 + SEL_BLOCK) - np.maximum(c0, s0), 0, None) / CMP_LEN
    ov[n_cmp:] = 0.0
    return jnp.asarray(ov, f32)


def _block_onehot_t(block, l):
    e = (np.arange(l)[None, :] // block) == np.arange(LANES)[:, None]
    return jnp.asarray(np.where(e, -MASK_BIG, 0.0), bf16)


def _expand_matrix(block, kp):
    e = (np.arange(kp)[None, :] // block) == np.arange(LANES)[:, None]
    return jnp.asarray(e, bf16)


def _suffix_matrix():
    j = np.arange(2 * LANES)[:, None] % LANES
    s = np.arange(2 * LANES)[None, :]
    return jnp.asarray((s >= LANES) | (j > s), bf16)


def _layer_weights(l, w_in, cmp_pe, cmp_w1, cmp_w2, w_br_a, w_br_b, w_br_c, w_o, w_up, conv_w, conv_b, w_down):
    d = w_in.shape[1]
    w = w_in[l]
    w_proj = jnp.concatenate([w[:, 0:1280], w[:, 1304:], w[:, 1280:1304], jnp.zeros((d, _PROJ_W - 5912), f32)],
                             axis=1).astype(bf16)
    pe4 = jnp.concatenate([cmp_pe[l], cmp_pe[l]], axis=2)
    w1 = cmp_w1[l].reshape(2, CMP_LEN, HEAD_DIM, CMP_HID)
    w1bd = jnp.zeros((2, CMP_LEN, NSA_KV * HEAD_DIM, NSA_KV * CMP_HID), f32)
    w2bd = jnp.zeros((2, NSA_KV * CMP_HID, NSA_KV * HEAD_DIM), f32)
    for g in range(NSA_KV):
        w1bd = w1bd.at[:, :, HEAD_DIM * g:HEAD_DIM * (g + 1), CMP_HID * g:CMP_HID * (g + 1)].set(w1)
        w2bd = w2bd.at[:, CMP_HID * g:CMP_HID * (g + 1), HEAD_DIM * g:HEAD_DIM * (g + 1)].set(cmp_w2[l])
    cw8 = jnp.concatenate([conv_w[l], jnp.zeros((8 - CONV_W, conv_w.shape[2]), f32)], axis=0)
    return dict(w_proj=w_proj, pe4=pe4, w1bd=w1bd.astype(bf16), w2bd=w2bd.astype(bf16),
                wa=w_br_a[l].astype(bf16), wb=w_br_b[l].astype(bf16), wc=w_br_c[l].astype(bf16),
                wo=w_o[l].astype(bf16), w_up=w_up[l].astype(bf16), cw8=cw8, cb=conv_b[l].reshape(1, -1),
                wd=w_down[l].astype(bf16))


def _mod_parts(mod_rows, per_row):
    r = mod_rows.shape[0]
    parts = mod_rows.reshape(r, 6, D_MODEL)
    return [parts[:, k].reshape((1, r, D_MODEL) if per_row else (r, 1, D_MODEL)) for k in range(6)]


def kernel(x_prompt, x_sample, cache_nsa, cache_moba, cache_sb, state_win, state_conv, page_table, c_prompt,
           c_sample, norm1_g, norm2_g, w_ada, b_ada, w_in, cmp_pe, cmp_w1, cmp_w2, w_br_a, w_br_b, w_br_c, w_o,
           w_up, conv_w, conv_b, w_down, final_g):
    b, t, d = x_prompt.shape
    s = x_sample.shape[0]
    depth = w_in.shape[0]
    n_phys = cache_nsa.shape[1]
    n_pages = page_table.shape[1]
    past = n_pages * PAGE_SIZE
    tm = 512
    tq = 128
    tk = 512

    n_c = b + s
    c_all = jnp.concatenate([c_prompt, c_sample, jnp.zeros((-n_c % 8, d), f32)], axis=0)
    mod = _ada_mod(c_all, w_ada, b_ada)

    nc_p = t // CMP_STRIDE
    n_cmp_p = (t - CMP_LEN) // CMP_STRIDE + 1
    agg_p = _agg_matrix(nc_p, n_cmp_p)
    nc_s = past // CMP_STRIDE
    agg_s = _agg_matrix(nc_s, (past + 1 - CMP_LEN) // CMP_STRIDE + 1)
    e_sel = _expand_matrix(SEL_BLOCK, past)
    e_blk = _expand_matrix(MOBA_BLOCK, past)
    seg_mean = (jnp.swapaxes(e_blk, 0, 1).astype(f32) * (1.0 / MOBA_BLOCK)).astype(bf16)
    uu = _suffix_matrix()
    nb_pad = LANES // MOBA_HEADS

    feat_major = lambda a: jnp.transpose(a, (0, 1, 3, 4, 5, 2))
    caches = [feat_major(c).reshape(depth * n_phys, 512, PAGE_SIZE) for c in (cache_nsa, cache_moba, cache_sb)]
    state_win_t = feat_major(state_win).reshape(depth * s, 256, state_win.shape[2])

    xp = x_prompt.reshape(b * t, d)
    xs = x_sample.reshape(s, d)
    outs_p = [[] for _ in range(5)]
    outs_s = [[] for _ in range(5)]
    for l in range(depth):
        w = _layer_weights(l, w_in, cmp_pe, cmp_w1, cmp_w2, w_br_a, w_br_b, w_br_c, w_o, w_up, conv_w, conv_b,
                           w_down)
        mp = _mod_parts(mod[l, 0:b], per_row=False)
        ms = _mod_parts(mod[l, b:b + s], per_row=True)

        (q_n, nsa_rows, nsa_bf, win_rows, win_bf, mb_q, moba_rows, moba_bf, sb_q, sb_rows, sb_bf, g_m, g_n) = \
            _norm_mod_matmul(xp, norm1_g[l], mp[0], mp[1], w["w_proj"], _IN_SEGS, _IN_DTYPES, tm, "in_proj_prompt")
        r3 = lambda a: a.reshape(b, t, a.shape[1])
        cmp = _cmp_prompt(r3(nsa_rows), w["pe4"], w["w1bd"], w["w2bd"])
        o_a = _nsa_prompt(r3(q_n), r3(g_n), cmp, agg_p, r3(nsa_bf), r3(win_bf), n_cmp_p, tq, tk)
        kmean = _block_mean(r3(moba_rows), 256, MOBA_BLOCK).reshape(b, t // MOBA_BLOCK, MOBA_HEADS, HEAD_DIM)
        kmt = jnp.zeros((b, MOBA_HEADS, nb_pad, MOBA_HEADS, HEAD_DIM), f32)
        for h in range(MOBA_HEADS):
            kmt = kmt.at[:, h, 0:t // MOBA_BLOCK, h].set(kmean[:, :, h])
        o_b = _moba_prompt(r3(mb_q), kmt.reshape(b, LANES, 256), r3(moba_bf), tq, 2 * tk)
        o_c = _sb_prompt(r3(sb_q), uu, r3(sb_bf), 2 * tq, tk)
        xp = _merge_out(o_a.reshape(b * t, -1), o_b.reshape(b * t, -1), o_c.reshape(b * t, -1), g_m, xp, mp[2],
                        w["wa"], w["wb"], w["wc"], w["wo"], tm)
        u_a, u_b = _norm_mod_matmul(xp, norm2_g[l], mp[3], mp[4], w["w_up"], _UP_SEGS, _UP_DTYPES, tm,
                                    "ffn_up_prompt")
        xp = _ffn_down_seq(u_a, u_b, w["cw8"], w["cb"], w["wd"], xp, mp[5], tm)
        keep = min(WINDOW, t)
        outs_p[0].append(nsa_rows.reshape(b, t // PAGE_SIZE, PAGE_SIZE, 4, NSA_KV, HEAD_DIM))
        outs_p[1].append(moba_rows.reshape(b, t // PAGE_SIZE, PAGE_SIZE, 2, MOBA_HEADS, HEAD_DIM))
        outs_p[2].append(sb_rows.reshape(b, t // PAGE_SIZE, PAGE_SIZE, 2, SB_HEADS, HEAD_DIM))
        outs_p[3].append(r3(win_rows)[:, t - keep:].reshape(b, keep, 2, NSA_KV, HEAD_DIM))
        outs_p[4].append(r3(u_a)[:, t - (CONV_W - 1):])

        (q_n, nsa_rows, _, win_rows, _, mb_q, moba_rows, _, sb_q, sb_rows, _, g_m, g_n) = \
            _norm_mod_matmul(xs, norm1_g[l], ms[0], ms[1], w["w_proj"], _IN_SEGS, _IN_DTYPES, s, "in_proj_sample")
        s3 = lambda a: a.astype(f32).reshape(s, 1, a.shape[1])
        base = l * n_phys
        o_a = _nsa_decode(page_table, caches[0], base, s3(q_n), s3(g_n), s3(nsa_rows), s3(win_rows), state_win_t,
                          l * s, w["pe4"], w["w1bd"], w["w2bd"], agg_s, e_sel, 2)
        o_b = _moba_decode(page_table, caches[1], base, s3(mb_q), s3(moba_rows), seg_mean, e_blk)
        o_c = _sb_decode(page_table, caches[2], base, s3(sb_q), uu)
        xs = _merge_out(o_a.reshape(s, -1).astype(bf16), o_b.reshape(s, -1).astype(bf16),
                        o_c.reshape(s, -1).astype(bf16), g_m, xs, ms[2], w["wa"], w["wb"], w["wc"], w["wo"], s)
        u_a, u_b = _norm_mod_matmul(xs, norm2_g[l], ms[3], ms[4], w["w_up"], _UP_SEGS, _UP_DTYPES, s,
                                    "ffn_up_sample")
        xs = _ffn_down_step(u_a, state_conv[l, :, 1], state_conv[l, :, 0], u_b, w["cw8"], w["cb"], w["wd"], xs,
                            ms[5])
        conv_full = jnp.concatenate([state_conv[l], u_a.reshape(s, 1, -1)], axis=1)
        outs_s[0].append(nsa_rows.reshape(s, 1, 4, NSA_KV, HEAD_DIM))
        outs_s[1].append(moba_rows.reshape(s, 1, 2, MOBA_HEADS, HEAD_DIM))
        outs_s[2].append(sb_rows.reshape(s, 1, 2, SB_HEADS, HEAD_DIM))
        outs_s[3].append(win_rows.reshape(s, 1, 256))
        outs_s[4].append(conv_full[:, conv_full.shape[1] - (CONV_W - 1):])

    y_prompt = _final_norm(xp, final_g, tm).reshape(b, t, d)
    y_sample = _final_norm(xs, final_g, s).reshape(s, 1, d)
    wb = state_win.shape[2]
    win_s = _win_shift(state_win_t, jnp.concatenate(outs_s[3], axis=0), 4)
    win_s = jnp.transpose(win_s.reshape(depth, s, 2, NSA_KV, HEAD_DIM, wb), (0, 1, 5, 2, 3, 4))
    st = lambda lst: jnp.stack(lst)
    return (y_prompt, y_sample, st(outs_p[0]), st(outs_s[0]), st(outs_p[1]), st(outs_s[1]), st(outs_p[2]),
            st(outs_s[2]), st(outs_p[3]), win_s, st(outs_p[4]), st(outs_s[4]))
```

```python
import functools

import numpy as np
import jax
import jax.numpy as jnp
from jax import lax
from jax.experimental import pallas as pl
from jax.experimental.pallas import tpu as pltpu

f32 = jnp.float32
bf16 = jnp.bfloat16

D_MODEL = 1024
HEAD_DIM = 64
NSA_HEADS = 8
NSA_KV = 2
HPG = NSA_HEADS // NSA_KV
CMP_LEN = 32
CMP_STRIDE = 16
CMP_HID = 128
SEL_BLOCK = 64
SEL_TOPN = 16
WINDOW = 512
MOBA_HEADS = 4
MOBA_BLOCK = 256
MOBA_TOPK = 3
SB_HEADS = 4
D_FF = 2816
CONV_W = 3
PAGE_SIZE = 128
RMS_EPS = 1e-6
NEG = -1e30
FORCE = 1e9

LANES = 128
MASK_BIG = 2.0 ** 100
M_INIT = -1e29
REMOVED = -3e38
SB_CUTOFF = -110.0
VMEM_LIMIT_MB = 56

NSA_SLOPES = [2.0 ** (-8.0 * (h + 1) / NSA_HEADS) for h in range(NSA_HEADS)]
MOBA_SLOPES = [2.0 ** (-8.0 * (h + 1) / MOBA_HEADS) for h in range(MOBA_HEADS)]

_Q_N, _NSA, _WIN, _MB_Q, _MOBA, _SB_Q, _SB, _G_M, _G_N, _PROJ_W = 0, 512, 1024, 1280, 1536, 2048, 2304, 2816, 5888, 6016


def _cparams(sem, vmem_mb=VMEM_LIMIT_MB):
    return pltpu.CompilerParams(dimension_semantics=sem, vmem_limit_bytes=vmem_mb * 2 ** 20)


def _const_spec(shape):
    nd = len(shape)
    return pl.BlockSpec(shape, lambda *_: (0,) * nd, pipeline_mode=pl.Buffered(1))


def _nt(a, b):
    return lax.dot_general(a, b, (((1,), (1,)), ((), ())), preferred_element_type=f32)


def _dot(a, b):
    return jnp.dot(a, b, preferred_element_type=f32)


def _gelu_tanh(x):
    return x * (0.5 * (1.0 + jnp.tanh(np.sqrt(2.0 / np.pi) * (x + 0.044715 * (x * x * x)))))


def _ref_softmax(s, valid):
    s = jnp.where(valid, s, NEG)
    e = jnp.where(valid, jnp.exp(s - jnp.max(s, axis=-1, keepdims=True)), 0.0)
    return e / jnp.maximum(jnp.sum(e, axis=-1, keepdims=True), 1e-30)


def _ada_body(c_ref, w_ref, b_ref, o_ref):
    c = c_ref[...]
    s = c * jax.nn.sigmoid(c)
    o_ref[...] = _dot(s.astype(bf16), w_ref[...].astype(bf16)) + b_ref[...]


def _ada_mod(c_all, w_ada, b_ada):
    depth, d, n = w_ada.shape
    r = c_all.shape[0]
    tn = 1536
    return pl.pallas_call(
        _ada_body,
        out_shape=jax.ShapeDtypeStruct((depth, r, n), f32),
        grid=(depth, n // tn),
        in_specs=[pl.BlockSpec((r, d), lambda l, j: (0, 0)),
                  pl.BlockSpec((None, d, tn), lambda l, j: (l, 0, j)),
                  pl.BlockSpec((None, 1, tn), lambda l, j: (l, 0, j))],
        out_specs=pl.BlockSpec((None, r, tn), lambda l, j: (l, 0, j)),
        compiler_params=_cparams(("parallel", "parallel")),
        name="ada_mod",
    )(c_all, w_ada, b_ada.reshape(depth, 1, n))


def _nmm_body(x_ref, g_ref, sh_ref, sc_ref, w_ref, *o_refs, segs):
    x = x_ref[...]
    y = x * lax.rsqrt(jnp.mean(x * x, axis=-1, keepdims=True) + RMS_EPS)
    h = (y * g_ref[...]) * (1.0 + sc_ref[...]) + sh_ref[...]
    hb = h.astype(bf16)
    k = 0
    for off, width, scales in segs:
        outs = o_refs[k:k + len(scales)]
        k += len(scales)
        for c0 in range(0, width, 512):
            cw = min(512, width - c0)
            acc = _dot(hb, w_ref[:, off + c0:off + c0 + cw])
            for o, scale in zip(outs, scales):
                o[:, c0:c0 + cw] = (acc if scale == 1.0 else acc * scale).astype(o.dtype)


def _norm_mod_matmul(x, g, shift, scale, w_bf, segs, out_dtypes, tm, name):
    r, d = x.shape
    nb, rb, _ = shift.shape
    tiles_per_b = (r // nb) // tm
    out_shape, out_specs = [], []
    k = 0
    for off, width, scales in segs:
        for _ in scales:
            out_shape.append(jax.ShapeDtypeStruct((r, width), out_dtypes[k]))
            out_specs.append(pl.BlockSpec((tm, width), lambda i: (i, 0)))
            k += 1
    mod_spec = pl.BlockSpec((None, rb, d), lambda i: (i // tiles_per_b, 0, 0))
    return pl.pallas_call(
        functools.partial(_nmm_body, segs=segs),
        out_shape=out_shape,
        grid=(r // tm,),
        in_specs=[pl.BlockSpec((tm, d), lambda i: (i, 0)), _const_spec((1, d)), mod_spec, mod_spec,
                  _const_spec(w_bf.shape)],
        out_specs=out_specs,
        compiler_params=_cparams(("parallel",)),
        name=name,
    )(x, g.reshape(1, d), shift, scale, w_bf)


_IN_SEGS = ((_Q_N, 512, (0.125,)), (_NSA, 512, (1.0, 1.0)), (_WIN, 256, (1.0, 1.0)), (_MB_Q, 256, (0.125,)),
            (_MOBA, 512, (1.0, 1.0)), (_SB_Q, 256, (0.125,)), (_SB, 512, (1.0, 1.0)), (_G_M, 3072, (1.0,)),
            (_G_N, 128, (1.0,)))
_IN_DTYPES = (bf16, f32, bf16, f32, bf16, bf16, f32, bf16, bf16, f32, bf16, f32, f32)
_UP_SEGS = ((0, D_FF, (1.0,)), (D_FF, D_FF, (1.0,)))
_UP_DTYPES = (f32, f32)


def _cmp_core(rows_refs, pe_ref, w1_ref, w2_ref, nc):
    outs = []
    for kv in range(2):
        acc_a = jnp.zeros((nc, NSA_KV * CMP_HID), f32)
        acc_b = jnp.zeros((nc, NSA_KV * CMP_HID), f32)
        for r in range(CMP_STRIDE):
            y = rows_refs[kv][pl.ds(r, nc, stride=CMP_STRIDE), :]
            acc_a = acc_a + _dot((y + pe_ref[kv, r:r + 1, :]).astype(bf16), w1_ref[kv, r])
            acc_b = acc_b + _dot((y + pe_ref[kv, CMP_STRIDE + r:CMP_STRIDE + r + 1, :]).astype(bf16),
                                 w1_ref[kv, CMP_STRIDE + r])
        pre = acc_a + pltpu.roll(acc_b, nc - 1, 0)
        outs.append(_dot(_gelu_tanh(pre).astype(bf16), w2_ref[kv]))
    return jnp.concatenate(outs, axis=1)


def _cmp_prompt_body(k_ref, v_ref, pe_ref, w1_ref, w2_ref, o_ref, *, nc):
    o_ref[...] = _cmp_core((k_ref, v_ref), pe_ref, w1_ref, w2_ref, nc)


def _cmp_prompt(nsa_rows, pe2, w1bd, w2bd):
    b, l, _ = nsa_rows.shape
    nc = l // CMP_STRIDE
    return pl.pallas_call(
        functools.partial(_cmp_prompt_body, nc=nc),
        out_shape=jax.ShapeDtypeStruct((b, nc, 256), f32),
        grid=(b,),
        in_specs=[pl.BlockSpec((None, l, LANES), lambda i: (i, 0, 0)),
                  pl.BlockSpec((None, l, LANES), lambda i: (i, 0, 1)), _const_spec(pe2.shape),
                  _const_spec(w1bd.shape), _const_spec(w2bd.shape)],
        out_specs=pl.BlockSpec((None, nc, 256), lambda i: (i, 0, 0)),
        compiler_params=_cparams(("parallel",)),
        name="nsa_compress_prompt",
    )(nsa_rows, nsa_rows, pe2, w1bd, w2bd)


SLAB = 64
FLASH_SCRATCH = 2


class _Flash:
    def __init__(self, scratch, slopes, tq, t0):
        self.m, self.acc = scratch
        self.slope8 = jnp.concatenate(
            [jnp.full((1, 1), v, f32) for v in slopes] + [jnp.zeros((8 - len(slopes), 1), f32)], axis=0)
        self.tq, self.t0 = tq, t0
        self.m[...] = jnp.full(self.m.shape, M_INIT, f32)
        self.acc[...] = jnp.zeros(self.acc.shape, f32)

    def slab(self, j, s_all, bias, rel, masked):
        r0 = j * SLAB
        rows = slice(r0, r0 + SLAB)
        head = r0 // self.tq
        s = s_all[rows, :] + bias[head:head + 1, :]
        if masked:
            t_rel = (r0 + lax.broadcasted_iota(jnp.int32, (SLAB, 1), 0)) % self.tq
            s = jnp.where(rel <= t_rel, s, -MASK_BIG)
        m_old = self.m[rows, :]
        m_new = jnp.maximum(m_old, jnp.max(s, axis=-1, keepdims=True))
        self.m[rows, :] = m_new
        p = jnp.exp(s - jnp.concatenate([m_new] * (s.shape[1] // LANES), axis=1))
        return p.astype(bf16), jnp.exp(m_old - m_new)

    def result(self):
        acc = self.acc[...]
        return acc[:, 0:LANES] / acc[:, LANES:2 * LANES]


def _flash_tiles(chains, k0, masked):
    first = chains[0][0]
    rows = first.m.shape[0]
    tk = chains[0][2].shape[1]
    rel = k0 - first.t0 + lax.broadcasted_iota(jnp.int32, (1, tk), 1)
    logits = [(_dot(q_aug, kt_aug), f.slope8 * rel.astype(f32)) for f, q_aug, kt_aug, _ in chains]
    parts = [[] for _ in chains]
    for j in range(rows // SLAB):
        for c, (f, _, _, _) in enumerate(chains):
            parts[c].append(f.slab(j, logits[c][0], logits[c][1], rel, masked))
    for c, (f, _, _, v_aug) in enumerate(chains):
        p = jnp.concatenate([x[0] for x in parts[c]], axis=0)
        alpha = jnp.concatenate([x[1] for x in parts[c]], axis=0)
        f.acc[...] = jnp.concatenate([alpha, alpha], axis=1) * f.acc[...] + _dot(p, v_aug)


def _flash_scratch(rows):
    return [pltpu.VMEM((rows, LANES), f32), pltpu.VMEM((rows, 2 * LANES), f32)]


def _with_ones(v):
    return jnp.concatenate([v, jnp.ones(v.shape, v.dtype)], axis=-1)


def _masked_exp(s_all, slopes, rel, valid_fn, tq):
    out = []
    for j in range(s_all.shape[0] // SLAB):
        r0 = j * SLAB
        t_rel = r0 % tq + lax.broadcasted_iota(jnp.int32, (SLAB, 1), 0)
        valid = valid_fn(t_rel)
        s = jnp.where(valid, s_all[r0:r0 + SLAB, :] + slopes[r0 // tq] * rel.astype(f32), NEG)
        out.append(jnp.exp(s - jnp.maximum(jnp.max(s, axis=-1, keepdims=True), M_INIT)))
    return jnp.concatenate(out, axis=0)


def _topk_rows(score, ids, k, n_ids):
    picked = jnp.zeros(score.shape, f32)
    for _ in range(k):
        mx = jnp.max(score, axis=0, keepdims=True)
        idx = jnp.min(jnp.where(score == mx, ids, n_ids), axis=0, keepdims=True)
        pick = ids == idx
        picked = jnp.where(pick, 1.0, picked)
        score = jnp.where(pick, REMOVED, score)
    return picked


def _nsa_prompt_body(q_ref, gn_ref, kct_ref, vc_ref, agg_ref, kst_ref, oh_ref, vs_ref, kwt_ref, vw_ref, o_ref,
                     sel_scr, qa_scr, part_scr, gate_scr, *flash_scr, tq, tk, n_cmp):
    i = pl.program_id(1)
    t0 = i * tq
    nc = vc_ref.shape[0]
    qf = q_ref[...].astype(f32)
    lane = lax.broadcasted_iota(jnp.int32, (tq, LANES), 1)
    sig = jax.nn.sigmoid(gn_ref[...])
    kct = kct_ref[...]
    vc = vc_ref[...].astype(bf16)
    aggb = agg_ref[...].astype(bf16)
    ones_c = jnp.ones((nc, LANES), bf16)
    n_id = lax.broadcasted_iota(jnp.int32, (1, nc), 1)
    cend_rel = n_id * CMP_STRIDE + (CMP_LEN - 1) - t0
    cend_masked = jnp.where(n_id < n_cmp, cend_rel, 2 ** 30)
    blk_id = lax.broadcasted_iota(jnp.int32, (LANES, tq), 0)
    cur = (t0 + lax.broadcasted_iota(jnp.int32, (LANES, tq), 1)) // SEL_BLOCK
    causal_blk = blk_id <= cur
    forced = (blk_id == 0) | (blk_id == cur) | (blk_id == cur - 1)
    chunks = [jnp.zeros((tq, LANES), f32) for _ in range(NSA_HEADS // 2)]

    for g in range(NSA_KV):
        heads = [HPG * g + hh for hh in range(HPG)]
        pieces = []
        for h in heads:
            blk = qf[:, LANES * (h // 2):LANES * (h // 2 + 1)]
            if h % 2 != g:
                blk = pltpu.roll(blk, HEAD_DIM, 1)
            pieces.append(jnp.where(lane // HEAD_DIM == g, blk, 0.0))
        qg = jnp.concatenate(pieces, axis=0).astype(bf16)
        slopes = [NSA_SLOPES[h] for h in heads]

        valid_c = lambda t_rel: cend_masked <= t_rel
        e_c = _masked_exp(_dot(qg, kct), slopes, cend_rel, valid_c, tq)
        e_hi = e_c.astype(bf16)
        e_lo = (e_c - e_hi.astype(f32)).astype(bf16)
        r_hi = _dot(e_hi, jnp.concatenate([vc, ones_c, aggb], axis=1))
        r_lo = _dot(e_lo, jnp.concatenate([aggb, ones_c], axis=1))
        o_c = r_hi[:, 0:LANES] / jnp.maximum(r_hi[:, LANES:2 * LANES], 1e-30)
        imp_rows = (r_hi[:, 2 * LANES:3 * LANES] + r_lo[:, 0:LANES]) / jnp.maximum(
            r_hi[:, LANES:2 * LANES] + r_lo[:, LANES:2 * LANES], 1e-30)
        imp = imp_rows[0:tq] + imp_rows[tq:2 * tq] + imp_rows[2 * tq:3 * tq] + imp_rows[3 * tq:4 * tq]

        score = jnp.where(causal_blk, jnp.where(forced, FORCE, imp.T), NEG)
        sel_t = jnp.where(causal_blk, _topk_rows(score, blk_id, SEL_TOPN, LANES), 0.0)
        notsel = (1.0 - sel_t).T
        qa_scr[g] = jnp.concatenate([qg, jnp.concatenate([notsel] * HPG, axis=0).astype(bf16)], axis=1)
        sel_scr[g] = sel_t

        wl = WINDOW + tq
        s0 = pl.multiple_of(jnp.maximum(t0 - WINDOW, 0), tq)
        w_rel = s0 - t0 + lax.broadcasted_iota(jnp.int32, (1, wl), 1)
        valid_w = lambda t_rel: (w_rel <= t_rel) & (w_rel > t_rel - WINDOW)
        e_w = _masked_exp(_dot(qg, kwt_ref[:, pl.ds(s0, wl)]), slopes, w_rel, valid_w, tq)
        r_w = _dot(e_w.astype(bf16), _with_ones(vw_ref[pl.ds(s0, wl), :]))
        o_w = r_w[:, 0:LANES] / r_w[:, LANES:2 * LANES]

        gates = [jnp.concatenate([sig[:, br * NSA_HEADS + h:br * NSA_HEADS + h + 1] for h in heads], axis=0)
                 for br in range(3)]
        part_scr[g] = gates[0] * o_c + gates[2] * o_w
        gate_scr[g] = gates[1]

    flashes = [_Flash(flash_scr[FLASH_SCRATCH * g:FLASH_SCRATCH * (g + 1)],
                      NSA_SLOPES[HPG * g:HPG * (g + 1)], tq, t0) for g in range(NSA_KV)]

    def sel_step(kt, masked):
        k0 = pl.multiple_of(kt * tk, tk)
        kt_aug = jnp.concatenate([kst_ref[:, pl.ds(k0, tk)], oh_ref[:, pl.ds(k0, tk)]], axis=0)
        v_aug = _with_ones(vs_ref[pl.ds(k0, tk), :])
        _flash_tiles([(flashes[g], qa_scr[g], kt_aug, v_aug) for g in range(NSA_KV)], k0, masked)

    def sel_loop(kt, carry):
        blk0 = pl.multiple_of(kt * (tk // SEL_BLOCK), tk // SEL_BLOCK)
        picked = jnp.maximum(sel_scr[0, pl.ds(blk0, tk // SEL_BLOCK), :], sel_scr[1, pl.ds(blk0, tk // SEL_BLOCK), :])

        @pl.when(jnp.max(picked) > 0.0)
        def _():
            sel_step(kt, False)

        return carry

    kd = t0 // tk
    lax.fori_loop(0, kd, sel_loop, 0)
    sel_step(kd, True)

    for g in range(NSA_KV):
        heads = [HPG * g + hh for hh in range(HPG)]
        o = part_scr[g] + gate_scr[g] * flashes[g].result()
        for hh, h in enumerate(heads):
            piece = o[hh * tq:(hh + 1) * tq]
            if h % 2 != g:
                piece = pltpu.roll(piece, HEAD_DIM, 1)
            chunks[h // 2] = chunks[h // 2] + jnp.where(lane // HEAD_DIM == h % 2, piece, 0.0)

    o_ref[...] = jnp.concatenate(chunks, axis=1).astype(o_ref.dtype)


def _nsa_prompt(q_n, g_n, cmp, agg, nsa_bf, win_bf, n_cmp, tq, tk):
    b, l, _ = q_n.shape
    nc = cmp.shape[1]
    rows = HPG * tq
    kct = jnp.swapaxes(cmp[:, :, 0:LANES], 1, 2).astype(bf16)
    onehot_t = _block_onehot_t(SEL_BLOCK, l)
    kst = jnp.swapaxes(nsa_bf[:, :, 2 * LANES:3 * LANES], 1, 2)
    kwt = jnp.swapaxes(win_bf[:, :, 0:LANES], 1, 2)
    per_b = lambda shape, col=0: pl.BlockSpec((None,) + shape, lambda bi, i: (bi, 0, col))
    return pl.pallas_call(
        functools.partial(_nsa_prompt_body, tq=tq, tk=tk, n_cmp=n_cmp),
        out_shape=jax.ShapeDtypeStruct((b, l, NSA_HEADS * HEAD_DIM), bf16),
        grid=(b, l // tq),
        in_specs=[pl.BlockSpec((None, tq, 512), lambda bi, i: (bi, i, 0)),
                  pl.BlockSpec((None, tq, LANES), lambda bi, i: (bi, i, 0)),
                  per_b((LANES, nc)), per_b((nc, LANES), 1), _const_spec(agg.shape),
                  per_b((LANES, l)), _const_spec(onehot_t.shape), per_b((l, LANES), 3), per_b((LANES, l)),
                  per_b((l, LANES), 1)],
        out_specs=pl.BlockSpec((None, tq, 512), lambda bi, i: (bi, i, 0)),
        scratch_shapes=[pltpu.VMEM((NSA_KV, LANES, tq), f32), pltpu.VMEM((NSA_KV, rows, 2 * LANES), bf16),
                        pltpu.VMEM((NSA_KV, rows, LANES), f32), pltpu.VMEM((NSA_KV, rows, 1), f32)]
        + _flash_scratch(rows) * NSA_KV,
        compiler_params=_cparams(("parallel", "parallel")),
        name="nsa_attention_prompt",
    )(q_n, g_n, kct, cmp, agg, kst, onehot_t, nsa_bf, kwt, win_bf)


def _block_mean_body(k_ref, o_ref):
    o_ref[...] = jnp.mean(k_ref[...], axis=0, keepdims=True)


def _block_mean(rows, width, blk):
    b, l, _ = rows.shape
    return pl.pallas_call(
        _block_mean_body,
        out_shape=jax.ShapeDtypeStruct((b, l // blk, 1, width), f32),
        grid=(b, l // blk),
        in_specs=[pl.BlockSpec((None, blk, width), lambda bi, j: (bi, j, 0))],
        out_specs=pl.BlockSpec((None, None, 1, width), lambda bi, j: (bi, j, 0, 0)),
        compiler_params=_cparams(("parallel", "parallel")),
        name="moba_block_mean",
    )(rows)


def _moba_prompt_body(q_ref, kmt_ref, kt_ref, oh_ref, v_ref, o_ref, qa_scr, *flash_scr, tq, tk, nb_pad):
    i = pl.program_id(1)
    t0 = i * tq
    q = q_ref[...]
    qf = q.astype(f32)
    lane = lax.broadcasted_iota(jnp.int32, (tq, LANES), 1)

    gate_t = _nt(kmt_ref[...].astype(bf16), q)
    blk_id = lax.broadcasted_iota(jnp.int32, (LANES, tq), 0) % nb_pad
    own = (t0 + lax.broadcasted_iota(jnp.int32, (LANES, tq), 1)) // MOBA_BLOCK
    past = blk_id < own
    score = jnp.where(past, gate_t, NEG)
    parts = [_topk_rows(score[nb_pad * h:nb_pad * (h + 1)], blk_id[nb_pad * h:nb_pad * (h + 1)], MOBA_TOPK, nb_pad)
             for h in range(MOBA_HEADS)]
    sel_t = jnp.where(past, jnp.concatenate(parts, axis=0), 0.0)
    sel_t = jnp.where(blk_id == own, 1.0, sel_t)
    notsel = (1.0 - sel_t).T

    nch = MOBA_HEADS // 2
    for c in range(nch):
        q_rows = []
        for e in range(2):
            h = 2 * c + e
            qh = jnp.where(lane // HEAD_DIM == e, qf[:, LANES * c:LANES * (c + 1)], 0.0)
            ns = notsel if h == 0 else pltpu.roll(notsel, LANES - nb_pad * h, 1)
            ns = jnp.where(lane < nb_pad, ns, 0.0)
            q_rows.append(jnp.concatenate([qh, ns], axis=1))
        qa_scr[c] = jnp.concatenate(q_rows, axis=0).astype(bf16)
    flashes = [_Flash(flash_scr[FLASH_SCRATCH * c:FLASH_SCRATCH * (c + 1)],
                      MOBA_SLOPES[2 * c:2 * c + 2], tq, t0) for c in range(nch)]

    def step(kt, masked):
        k0 = pl.multiple_of(kt * tk, tk)
        onehot = oh_ref[:, pl.ds(k0, tk)]
        _flash_tiles([(flashes[c], qa_scr[c], jnp.concatenate([kt_ref[c, :, pl.ds(k0, tk)], onehot], axis=0),
                       _with_ones(v_ref[pl.ds(k0, tk), LANES * c:LANES * (c + 1)])) for c in range(nch)], k0, masked)

    def loop(kt, carry):
        step(kt, False)
        return carry

    lax.fori_loop(0, t0 // tk, loop, 0)
    step(t0 // tk, True)
    out_chunks = []
    for c in range(nch):
        o = flashes[c].result()
        out_chunks.append(jnp.where(lane < HEAD_DIM, o[0:tq], o[tq:2 * tq]))
    o_ref[...] = jnp.concatenate(out_chunks, axis=1).astype(o_ref.dtype)


def _moba_prompt(mb_q, kmt, moba_bf, tq, tk):
    b, l, _ = mb_q.shape
    nb_pad = LANES // MOBA_HEADS
    nch = MOBA_HEADS // 2
    onehot_t = _block_onehot_t(MOBA_BLOCK, l)
    k_t = jnp.swapaxes(jnp.swapaxes(moba_bf[:, :, 0:nch * LANES].reshape(b, l, nch, LANES), 1, 3), 1, 2)
    return pl.pallas_call(
        functools.partial(_moba_prompt_body, tq=tq, tk=tk, nb_pad=nb_pad),
        out_shape=jax.ShapeDtypeStruct((b, l, MOBA_HEADS * HEAD_DIM), bf16),
        grid=(b, l // tq),
        in_specs=[pl.BlockSpec((None, tq, 256), lambda bi, i: (bi, i, 0)),
                  pl.BlockSpec((None, LANES, 256), lambda bi, i: (bi, 0, 0)),
                  pl.BlockSpec((None, nch, LANES, l), lambda bi, i: (bi, 0, 0, 0)),
                  _const_spec(onehot_t.shape),
                  pl.BlockSpec((None, l, nch * LANES), lambda bi, i: (bi, 0, 1))],
        out_specs=pl.BlockSpec((None, tq, 256), lambda bi, i: (bi, i, 0)),
        scratch_shapes=[pltpu.VMEM((nch, 2 * tq, 2 * LANES), bf16)] + _flash_scratch(2 * tq) * nch,
        compiler_params=_cparams(("parallel", "parallel")),
        name="moba_attention_prompt",
    )(mb_q, kmt, k_t, onehot_t, moba_bf)


def _log_keep(z):
    return -(jnp.maximum(z, 0.0) + jnp.log(1.0 + jnp.exp(-jnp.abs(z))))


def _suffix_sums(lk, uu):
    hi = lk.astype(bf16)
    lo = (lk - hi.astype(f32)).astype(bf16)
    r = _dot(jnp.concatenate([hi, lo], axis=1), uu)
    return r[:, 0:LANES], r[:, LANES:2 * LANES]


def _sb_prompt_body(q_ref, uu_ref, k_ref, v_ref, o_ref, carry_scr, acc_scr, *, tq, tk):
    i = pl.program_id(2)
    t0 = i * tq
    qf = q_ref[...].astype(f32)
    lane = lax.broadcasted_iota(jnp.int32, (tq, LANES), 1)
    q2 = jnp.concatenate([jnp.where(lane // HEAD_DIM == e, qf, 0.0) for e in range(2)], axis=0).astype(bf16)
    t_row = t0 + lax.broadcasted_iota(jnp.int32, (2 * tq, 1), 0) % tq
    uu = uu_ref[...]
    carry_scr[...] = jnp.zeros(carry_scr.shape, f32)
    acc_scr[...] = jnp.zeros(acc_scr.shape, f32)

    def step(kt, diag):
        k0 = pl.multiple_of(kt * tk, tk)
        z = _dot(q2, k_ref[:, pl.ds(k0, tk)])
        lk = _log_keep(z)
        if diag:
            is_past = (k0 + lax.broadcasted_iota(jnp.int32, (1, tk), 1)) < t_row
            lk = jnp.where(is_past, lk, 0.0)
        carry = carry_scr[...]
        between = [None] * (tk // LANES)
        for c in reversed(range(tk // LANES)):
            later, total = _suffix_sums(lk[:, LANES * c:LANES * (c + 1)], uu)
            between[c] = later + carry
            carry = carry + total
        w = jnp.exp(z + lk + jnp.concatenate(between, axis=1))
        if diag:
            w = jnp.where(is_past, w, 0.0)
        acc_scr[...] = acc_scr[...] + _dot(w.astype(bf16), v_ref[pl.ds(k0, tk), :])
        carry_scr[...] = carry

    kd = t0 // tk
    step(kd, True)

    def more(state):
        j, top = state
        return (j < kd) & (top > SB_CUTOFF)

    def walk(state):
        j, _ = state
        step(kd - 1 - j, False)
        return j + 1, jnp.max(carry_scr[...])

    lax.while_loop(more, walk, (0, jnp.max(carry_scr[...])))
    o = acc_scr[...]
    o_ref[...] = jnp.where(lane < HEAD_DIM, o[0:tq], o[tq:2 * tq]).astype(o_ref.dtype)


def _sb_prompt(sb_q, uu, sb_bf, tq, tk):
    b, l, _ = sb_q.shape
    nch = SB_HEADS // 2
    k_t = jnp.swapaxes(jnp.swapaxes(sb_bf[:, :, 0:nch * LANES].reshape(b, l, nch, LANES), 1, 3), 1, 2)
    return pl.pallas_call(
        functools.partial(_sb_prompt_body, tq=tq, tk=tk),
        out_shape=jax.ShapeDtypeStruct((b, l, SB_HEADS * HEAD_DIM), bf16),
        grid=(b, nch, l // tq),
        in_specs=[pl.BlockSpec((None, tq, LANES), lambda bi, c, i: (bi, i, c)),
                  _const_spec(uu.shape),
                  pl.BlockSpec((None, None, LANES, l), lambda bi, c, i: (bi, c, 0, 0)),
                  pl.BlockSpec((None, l, LANES), lambda bi, c, i: (bi, 0, nch + c))],
        out_specs=pl.BlockSpec((None, tq, LANES), lambda bi, c, i: (bi, i, c)),
        scratch_shapes=[pltpu.VMEM((2 * tq, LANES), f32), pltpu.VMEM((2 * tq, LANES), f32)],
        compiler_params=_cparams(("parallel", "parallel", "parallel")),
        name="stickbreak_attention_prompt",
    )(sb_q, uu, k_t, sb_bf)


def _merge_body(oa_ref, ob_ref, oc_ref, gm_ref, x_ref, gate_ref, wa_ref, wb_ref, wc_ref, wo_ref, o_ref):
    d = x_ref.shape[1]
    g = jax.nn.sigmoid(gm_ref[...])
    merged = (g[:, 0:d] * _dot(oa_ref[...], wa_ref[...]) + g[:, d:2 * d] * _dot(ob_ref[...], wb_ref[...])
              + g[:, 2 * d:3 * d] * _dot(oc_ref[...], wc_ref[...]))
    o_ref[...] = x_ref[...] + gate_ref[...] * _dot(merged.astype(bf16), wo_ref[...])


def _merge_out(o_a, o_b, o_c, g_m, x, gate, wa, wb, wc, wo, tm):
    r, d = x.shape
    nb, rb, _ = gate.shape
    tiles_per_b = (r // nb) // tm
    row = lambda w: pl.BlockSpec((tm, w), lambda i: (i, 0))
    return pl.pallas_call(
        _merge_body,
        out_shape=jax.ShapeDtypeStruct((r, d), f32),
        grid=(r // tm,),
        in_specs=[row(o_a.shape[1]), row(o_b.shape[1]), row(o_c.shape[1]), row(3 * d), row(d),
                  pl.BlockSpec((None, rb, d), lambda i: (i // tiles_per_b, 0, 0)),
                  _const_spec(wa.shape), _const_spec(wb.shape), _const_spec(wc.shape), _const_spec(wo.shape)],
        out_specs=row(d),
        compiler_params=_cparams(("parallel",)),
        name="merge_out_proj",
    )(o_a, o_b, o_c, g_m, x, gate, wa, wb, wc, wo)


def _ffn_tail(a, a_m1, a_m2, b, cw_ref, cb_ref, wd_ref, x_ref, gate_ref, o_ref):
    conv = cb_ref[...] + a_m2 * cw_ref[0:1, :]
    conv = conv + a_m1 * cw_ref[1:2, :]
    conv = conv + a * cw_ref[2:3, :]
    y = _dot((_gelu_tanh(conv) * b).astype(bf16), wd_ref[...])
    o_ref[...] = x_ref[...] + gate_ref[...] * y


def _ffn_seq_body(a_ref, halo_ref, b_ref, cw_ref, cb_ref, wd_ref, x_ref, gate_ref, o_ref, *, tiles_per_b):
    a = a_ref[...]
    first = pl.program_id(0) % tiles_per_b == 0
    halo = jnp.where(first, 0.0, halo_ref[...])
    row = lax.broadcasted_iota(jnp.int32, a.shape, 0)
    a_m1 = jnp.where(row < 1, halo[7:8, :], pltpu.roll(a, 1, 0))
    a_m2 = jnp.where(row < 1, halo[6:7, :], jnp.where(row < 2, halo[7:8, :], pltpu.roll(a, 2, 0)))
    _ffn_tail(a, a_m1, a_m2, b_ref[...], cw_ref, cb_ref, wd_ref, x_ref, gate_ref, o_ref)


def _ffn_step_body(a_ref, am1_ref, am2_ref, b_ref, cw_ref, cb_ref, wd_ref, x_ref, gate_ref, o_ref):
    _ffn_tail(a_ref[...], am1_ref[...], am2_ref[...], b_ref[...], cw_ref, cb_ref, wd_ref, x_ref, gate_ref, o_ref)


def _ffn_down_seq(u_a, u_b, cw8, cb, wd, x, gate, tm):
    r, d = x.shape
    ff = u_a.shape[1]
    nb = gate.shape[0]
    tiles_per_b = (r // nb) // tm
    row = lambda w: pl.BlockSpec((tm, w), lambda i: (i, 0))
    return pl.pallas_call(
        functools.partial(_ffn_seq_body, tiles_per_b=tiles_per_b),
        out_shape=jax.ShapeDtypeStruct((r, d), f32),
        grid=(r // tm,),
        in_specs=[row(ff), pl.BlockSpec((8, ff), lambda i: (jnp.maximum(i * (tm // 8) - 1, 0), 0)), row(ff),
                  _const_spec(cw8.shape), _const_spec(cb.shape), _const_spec(wd.shape), row(d),
                  pl.BlockSpec((None, 1, d), lambda i: (i // tiles_per_b, 0, 0))],
        out_specs=row(d),
        compiler_params=_cparams(("parallel",)),
        name="conv_ffn_down_seq",
    )(u_a, u_a, u_b, cw8, cb, wd, x, gate)


def _ffn_down_step(u_a, a_m1, a_m2, u_b, cw8, cb, wd, x, gate):
    r, d = x.shape
    full = lambda a: pl.BlockSpec(a.shape, lambda i: (0,) * a.ndim)
    return pl.pallas_call(
        _ffn_step_body,
        out_shape=jax.ShapeDtypeStruct((r, d), f32),
        grid=(1,),
        in_specs=[full(u_a), full(a_m1), full(a_m2), full(u_b), full(cw8), full(cb), full(wd), full(x),
                  pl.BlockSpec((None, r, d), lambda i: (0, 0, 0))],
        out_specs=full(x),
        compiler_params=_cparams(("arbitrary",)),
        name="conv_ffn_down_step",
    )(u_a, a_m1, a_m2, u_b, cw8, cb, wd, x, gate)


def _final_norm_body(x_ref, g_ref, o_ref):
    x = x_ref[...]
    o_ref[...] = x * lax.rsqrt(jnp.mean(x * x, axis=-1, keepdims=True) + RMS_EPS) * g_ref[...]


def _final_norm(x, g, tm):
    r, d = x.shape
    return pl.pallas_call(
        _final_norm_body,
        out_shape=jax.ShapeDtypeStruct((r, d), f32),
        grid=(r // tm,),
        in_specs=[pl.BlockSpec((tm, d), lambda i: (i, 0)), _const_spec((1, d))],
        out_specs=pl.BlockSpec((tm, d), lambda i: (i, 0)),
        compiler_params=_cparams(("parallel",)),
        name="final_rmsnorm",
    )(x, g.reshape(1, d))


def _page_specs(n_pages, layer_base, per_step=1, k=0):
    return [pl.BlockSpec((None, 512, PAGE_SIZE),
                         functools.partial(lambda s, pt, j: (layer_base + pt[(s * per_step + k) * n_pages + j], 0, 0),
                                           j=j))
            for j in range(n_pages)]


def _dec_const(shape):
    nd = len(shape)
    return pl.BlockSpec(shape, lambda s, pt: (0,) * nd, pipeline_mode=pl.Buffered(1))


def _head_rows(q_row):
    row = lax.broadcasted_iota(jnp.int32, (8, LANES), 0)
    lane = lax.broadcasted_iota(jnp.int32, (8, LANES), 1)
    q8 = jnp.broadcast_to(q_row, (8, q_row.shape[1]))
    qsel = jnp.zeros((8, LANES), f32)
    for c in range(NSA_HEADS // 2):
        qsel = qsel + jnp.where(row // 2 == c, q8[:, LANES * c:LANES * (c + 1)], 0.0)
    swap = (row % 2) != (row // HPG)
    qm = jnp.where(swap, pltpu.roll(qsel, HEAD_DIM, 1), qsel)
    return jnp.where(lane // HEAD_DIM == row // HPG, qm, 0.0), swap


def _rank_select(score_row, k):
    a = jnp.broadcast_to(score_row, (LANES, LANES))
    b = a.T
    ii = lax.broadcasted_iota(jnp.int32, (LANES, LANES), 0)
    jj = lax.broadcasted_iota(jnp.int32, (LANES, LANES), 1)
    ahead = (b > a) | ((b == a) & (ii < jj))
    rank = jnp.sum(jnp.where(ahead, 1.0, 0.0), axis=0, keepdims=True)
    return jnp.where(rank < k, 1.0, 0.0)


def _nsa_dec_body(pt_ref, q_ref, gn_ref, new_ref, wnew_ref, sw_ref, pe_ref, w1_ref, w2_ref, agg_ref, e_ref,
                  *rest, n_pages, per_step):
    pages, o_ref = rest[:n_pages * per_step], rest[n_pages * per_step]
    kc_scr, vc_scr = rest[n_pages * per_step + 1:]
    past = n_pages * PAGE_SIZE
    for j in range(n_pages * per_step):
        kc_scr[PAGE_SIZE * j:PAGE_SIZE * (j + 1), :] = pages[j][0:LANES, :].T
        vc_scr[PAGE_SIZE * j:PAGE_SIZE * (j + 1), :] = pages[j][LANES:2 * LANES, :].T

    nc = past // CMP_STRIDE
    cmp_all = _cmp_core((kc_scr, vc_scr), pe_ref, w1_ref, w2_ref, nc * per_step)
    for k in range(per_step):
        o_ref[k] = _nsa_dec_one(q_ref[k], gn_ref[k], new_ref[k], wnew_ref[k], sw_ref.at[k],
                                cmp_all[nc * k:nc * (k + 1)], pages[n_pages * k:n_pages * (k + 1)], agg_ref, e_ref)


def _nsa_dec_one(q_row, gn_row, new_row, wnew, sw_ref, cmpv, pages, agg_ref, e_ref):
    n_pages = len(pages)
    past = n_pages * PAGE_SIZE
    n_cmp = (past + 1 - CMP_LEN) // CMP_STRIDE + 1
    row = lax.broadcasted_iota(jnp.int32, (8, LANES), 0)
    lane = lax.broadcasted_iota(jnp.int32, (8, LANES), 1)
    qm, swap = _head_rows(q_row)
    qmb = qm.astype(bf16)
    slope = jnp.zeros((8, 1), f32)
    row1 = lax.broadcasted_iota(jnp.int32, (8, 1), 0)
    for h in range(NSA_HEADS):
        slope = jnp.where(row1 == h, NSA_SLOPES[h], slope)
    grp0 = row < HPG

    d_c = past - (lane[0:1] * CMP_STRIDE + CMP_LEN - 1)
    s_c = _nt(qmb, cmpv[:, 0:LANES].astype(bf16)) - slope * d_c.astype(f32)
    p_c = _ref_softmax(s_c, (d_c >= 0) & (lane[0:1] < n_cmp))
    o_c = _dot(p_c.astype(bf16), cmpv[:, LANES:2 * LANES].astype(bf16))
    aggb = agg_ref[...].astype(bf16)
    p_hi = p_c.astype(bf16)
    p_lo = (p_c - p_hi.astype(f32)).astype(bf16)
    imp_rows = _dot(p_hi, aggb) + _dot(p_lo, aggb)

    cur = past // SEL_BLOCK
    blk = lane[0:1]
    forced = (blk == 0) | (blk == cur) | (blk == cur - 1)
    causal = blk <= cur
    notsel_g = []
    for g in range(NSA_KV):
        imp = jnp.sum(imp_rows[HPG * g:HPG * (g + 1)], axis=0, keepdims=True)
        score = jnp.where(causal, jnp.where(forced, FORCE, imp), NEG)
        sel = jnp.where(causal, _rank_select(score, min(SEL_TOPN, -(-(past + 1) // SEL_BLOCK))), 0.0)
        notsel_g.append(jnp.broadcast_to(1.0 - sel, (8, LANES)))
    notsel = jnp.where(grp0, notsel_g[0], notsel_g[1])

    dist = past - lax.broadcasted_iota(jnp.int32, (1, past), 1)
    picked = _dot(notsel.astype(bf16), e_ref[...]) < 0.5
    ks_t = jnp.concatenate([pages[j][2 * LANES:3 * LANES, :].astype(bf16) for j in range(n_pages)], axis=1)
    vs_t = jnp.concatenate([pages[j][3 * LANES:4 * LANES, :].astype(bf16) for j in range(n_pages)], axis=1)
    s_s = jnp.where(picked, _dot(qmb, ks_t) - slope * dist.astype(f32), NEG)
    s_n = jnp.sum(qm * new_row[:, 2 * LANES:3 * LANES], axis=-1, keepdims=True)
    m_s = jnp.maximum(jnp.max(s_s, axis=-1, keepdims=True), s_n)
    e_s = jnp.where(picked, jnp.exp(s_s - m_s), 0.0)
    e_n = jnp.exp(s_n - m_s)
    den = jnp.maximum(jnp.sum(e_s, axis=-1, keepdims=True) + e_n, 1e-30)
    o_s = (e_n * new_row[:, 3 * LANES:4 * LANES] + _nt(e_s.astype(bf16), vs_t)) / den

    wb = sw_ref.shape[1]
    d_w = wb - lax.broadcasted_iota(jnp.int32, (1, wb), 1)
    s_w = _dot(qmb, sw_ref[0:LANES, :].astype(bf16)) - slope * d_w.astype(f32)
    valid_w = (d_w < WINDOW) & (d_w >= 0)
    s_w = jnp.where(valid_w, s_w, NEG)
    s_n = jnp.sum(qm * wnew[:, 0:LANES], axis=-1, keepdims=True)
    m_w = jnp.maximum(jnp.max(s_w, axis=-1, keepdims=True), s_n)
    e_w = jnp.where(valid_w, jnp.exp(s_w - m_w), 0.0)
    e_n = jnp.exp(s_n - m_w)
    den = jnp.maximum(jnp.sum(e_w, axis=-1, keepdims=True) + e_n, 1e-30)
    o_w = (_nt(e_w.astype(bf16), sw_ref[LANES:2 * LANES, :].astype(bf16)) + e_n * wnew[:, LANES:2 * LANES]) / den

    sig = jnp.broadcast_to(jax.nn.sigmoid(gn_row), (8, LANES))
    gates = [jnp.sum(jnp.where(lane == br * NSA_HEADS + row, sig, 0.0), axis=-1, keepdims=True) for br in range(3)]
    o = gates[0] * o_c + gates[1] * o_s + gates[2] * o_w
    o = jnp.where(swap, pltpu.roll(o, HEAD_DIM, 1), o)
    o = jnp.where(lane // HEAD_DIM == row % 2, o, 0.0)
    return jnp.concatenate([o[2 * c:2 * c + 1] + o[2 * c + 1:2 * c + 2] for c in range(NSA_HEADS // 2)], axis=1)


def _nsa_decode(page_table, cache_t, layer_base, q, g_n, nsa_new, win_new, state_win_t, win_base, pe4, w1bd, w2bd,
                agg, e_sel, per_step):
    s, n_pages = page_table.shape
    past = n_pages * PAGE_SIZE
    consts = (pe4, w1bd, w2bd, agg, e_sel)
    seqs = lambda width: pl.BlockSpec((per_step, 1, width), lambda si, pt: (si, 0, 0))
    page_specs = [spec for k in range(per_step) for spec in _page_specs(n_pages, layer_base, per_step, k)]
    grid_spec = pltpu.PrefetchScalarGridSpec(
        num_scalar_prefetch=1,
        grid=(s // per_step,),
        in_specs=[seqs(512), seqs(LANES), seqs(512), seqs(256),
                  pl.BlockSpec((per_step,) + state_win_t.shape[1:],
                               lambda si, pt: (win_base // per_step + si, 0, 0))]
        + [_dec_const(c.shape) for c in consts] + page_specs,
        out_specs=seqs(512),
        scratch_shapes=[pltpu.VMEM((per_step * past, LANES), f32), pltpu.VMEM((per_step * past, LANES), f32)],
    )
    return pl.pallas_call(
        functools.partial(_nsa_dec_body, n_pages=n_pages, per_step=per_step),
        out_shape=jax.ShapeDtypeStruct((s, 1, 512), f32),
        grid_spec=grid_spec,
        compiler_params=_cparams(("parallel",)),
        name="nsa_attention_decode",
    )(page_table.reshape(-1), q, g_n, nsa_new, win_new, state_win_t, *consts,
      *([cache_t] * (n_pages * per_step)))


def _win_shift_body(sw_ref, new_ref, o_ref):
    n, feat, wb = sw_ref.shape
    lane = lax.broadcasted_iota(jnp.int32, (feat, wb), 1)
    for k in range(n):
        col = jnp.broadcast_to(new_ref[k], (LANES, feat)).T
        col = jnp.concatenate([col] * (wb // LANES), axis=1)
        o_ref[k] = jnp.where(lane == wb - 1, col, pltpu.roll(sw_ref[k], wb - 1, 1))


def _win_shift(state_win_t, new_rows, per_step):
    n, feat, wb = state_win_t.shape
    return pl.pallas_call(
        _win_shift_body,
        out_shape=jax.ShapeDtypeStruct((n, feat, wb), f32),
        grid=(n // per_step,),
        in_specs=[pl.BlockSpec((per_step, feat, wb), lambda i: (i, 0, 0)),
                  pl.BlockSpec((per_step, 1, feat), lambda i: (i, 0, 0))],
        out_specs=pl.BlockSpec((per_step, feat, wb), lambda i: (i, 0, 0)),
        compiler_params=_cparams(("parallel",)),
        name="window_state_shift",
    )(state_win_t, new_rows)


def _q_head_rows4(q_row):
    row = lax.broadcasted_iota(jnp.int32, (8, 256), 0)
    lane = lax.broadcasted_iota(jnp.int32, (8, 256), 1)
    own = lane // HEAD_DIM == row
    return jnp.where(own, jnp.broadcast_to(q_row, (8, 256)), 0.0), own


def _moba_dec_body(pt_ref, q_ref, new_ref, seg_ref, e_ref, *rest, n_pages, per_step):
    pages, o_ref = rest[:n_pages * per_step], rest[n_pages * per_step]
    for k in range(per_step):
        o_ref[k] = _moba_dec_one(q_ref[k], new_ref[k], pages[n_pages * k:n_pages * (k + 1)], seg_ref, e_ref)


def _moba_dec_one(q_row, new_row, pages, seg_ref, e_ref):
    n_pages = len(pages)
    past = n_pages * PAGE_SIZE
    nb_past = past // MOBA_BLOCK
    kw = MOBA_HEADS * HEAD_DIM
    qm, own_lanes = _q_head_rows4(q_row)
    qmb = qm.astype(bf16)
    k_t = jnp.concatenate([pages[j][0:kw, :].astype(bf16) for j in range(n_pages)], axis=1)
    v_t = jnp.concatenate([pages[j][kw:2 * kw, :].astype(bf16) for j in range(n_pages)], axis=1)
    raw = _dot(qmb, k_t)
    raw_hi = raw.astype(bf16)
    raw_lo = (raw - raw_hi.astype(f32)).astype(bf16)
    gate = _dot(raw_hi, seg_ref[...]) + _dot(raw_lo, seg_ref[...])

    lane = lax.broadcasted_iota(jnp.int32, (8, LANES), 1)
    is_past = lane < nb_past
    score = jnp.where(is_past, gate, NEG)
    rank = jnp.zeros((8, LANES), f32)
    for i in range(nb_past):
        gi = score[:, i:i + 1]
        rank = rank + jnp.where((gi > score) | ((gi == score) & (i < lane)), 1.0, 0.0)
    sel = jnp.where(is_past & (rank < min(MOBA_TOPK, nb_past)), 1.0, 0.0)
    picked = _dot(sel.astype(bf16), e_ref[...]) > 0.5

    row1 = lax.broadcasted_iota(jnp.int32, (8, 1), 0)
    slope = jnp.zeros((8, 1), f32)
    for h in range(MOBA_HEADS):
        slope = jnp.where(row1 == h, MOBA_SLOPES[h], slope)
    dist = past - lax.broadcasted_iota(jnp.int32, (1, past), 1)
    s = jnp.where(picked, raw - slope * dist.astype(f32), NEG)
    s_n = jnp.sum(qm * new_row[:, 0:kw], axis=-1, keepdims=True)
    m = jnp.maximum(jnp.max(s, axis=-1, keepdims=True), s_n)
    e = jnp.where(picked, jnp.exp(s - m), 0.0)
    e_n = jnp.exp(s_n - m)
    den = jnp.maximum(jnp.sum(e, axis=-1, keepdims=True) + e_n, 1e-30)
    o = (e_n * new_row[:, kw:2 * kw] + _nt(e.astype(bf16), v_t)) / den
    return jnp.sum(jnp.where(own_lanes, o, 0.0), axis=0, keepdims=True)


def _seqs_spec(per_step, width):
    return pl.BlockSpec((per_step, 1, width), lambda si, pt: (si, 0, 0))


def _all_page_specs(n_pages, layer_base, per_step):
    return [spec for k in range(per_step) for spec in _page_specs(n_pages, layer_base, per_step, k)]


def _moba_decode(page_table, cache_t, layer_base, q, moba_new, seg_mean, e_blk, per_step):
    s, n_pages = page_table.shape
    grid_spec = pltpu.PrefetchScalarGridSpec(
        num_scalar_prefetch=1,
        grid=(s // per_step,),
        in_specs=[_seqs_spec(per_step, 256), _seqs_spec(per_step, 512), _dec_const(seg_mean.shape),
                  _dec_const(e_blk.shape)] + _all_page_specs(n_pages, layer_base, per_step),
        out_specs=_seqs_spec(per_step, 256),
    )
    return pl.pallas_call(
        functools.partial(_moba_dec_body, n_pages=n_pages, per_step=per_step),
        out_shape=jax.ShapeDtypeStruct((s, 1, 256), f32),
        grid_spec=grid_spec,
        compiler_params=_cparams(("parallel",)),
        name="moba_attention_decode",
    )(page_table.reshape(-1), q, moba_new, seg_mean, e_blk, *([cache_t] * (n_pages * per_step)))


def _sb_dec_body(pt_ref, q_ref, uu_ref, *rest, n_pages, per_step):
    pages, o_ref = rest[:n_pages * per_step], rest[n_pages * per_step]
    for k in range(per_step):
        o_ref[k] = _sb_dec_one(q_ref[k], pages[n_pages * k:n_pages * (k + 1)], uu_ref)


def _sb_dec_one(q_row, pages, uu_ref):
    n_pages = len(pages)
    kw = SB_HEADS * HEAD_DIM
    qm, own_lanes = _q_head_rows4(q_row)
    qmb = qm.astype(bf16)
    k_t = jnp.concatenate([pages[j][0:kw, :].astype(bf16) for j in range(n_pages)], axis=1)
    v_t = jnp.concatenate([pages[j][kw:2 * kw, :].astype(bf16) for j in range(n_pages)], axis=1)
    z = _dot(qmb, k_t)
    lk = _log_keep(z)
    stacked = jnp.concatenate([lk[:, PAGE_SIZE * j:PAGE_SIZE * (j + 1)] for j in range(n_pages)], axis=0)
    later, total = _suffix_sums(stacked, uu_ref[...])
    carry = jnp.zeros((8, LANES), f32)
    between = [None] * n_pages
    for j in reversed(range(n_pages)):
        between[j] = later[8 * j:8 * (j + 1)] + carry
        carry = carry + total[8 * j:8 * (j + 1)]
    w = jnp.exp(z + lk + jnp.concatenate(between, axis=1))
    acc = _nt(w.astype(bf16), v_t)
    return jnp.sum(jnp.where(own_lanes, acc, 0.0), axis=0, keepdims=True)


def _sb_decode(page_table, cache_t, layer_base, q, uu, per_step):
    s, n_pages = page_table.shape
    grid_spec = pltpu.PrefetchScalarGridSpec(
        num_scalar_prefetch=1,
        grid=(s // per_step,),
        in_specs=[_seqs_spec(per_step, 256), _dec_const(uu.shape)] + _all_page_specs(n_pages, layer_base, per_step),
        out_specs=_seqs_spec(per_step, 256),
    )
    return pl.pallas_call(
        functools.partial(_sb_dec_body, n_pages=n_pages, per_step=per_step),
        out_shape=jax.ShapeDtypeStruct((s, 1, 256), f32),
        grid_spec=grid_spec,
        compiler_params=_cparams(("parallel",)),
        name="stickbreak_attention_decode",
    )(page_table.reshape(-1), q, uu, *([cache_t] * (n_pages * per_step)))


def _agg_matrix(nc, n_cmp):
    c0 = np.arange(nc)[:, None] * CMP_STRIDE
    s0 = np.arange(LANES)[None, :] * SEL_BLOCK
    ov = np.clip(np.minimum(c0 + CMP_LEN, s0 + SEL_BLOCK) - np.maximum(c0, s0), 0, None) / CMP_LEN
    ov[n_cmp:] = 0.0
    return jnp.asarray(ov, f32)


def _block_onehot_t(block, l):
    e = (np.arange(l)[None, :] // block) == np.arange(LANES)[:, None]
    return jnp.asarray(np.where(e, -MASK_BIG, 0.0), bf16)


def _expand_matrix(block, kp):
    e = (np.arange(kp)[None, :] // block) == np.arange(LANES)[:, None]
    return jnp.asarray(e, bf16)


def _suffix_matrix():
    j = np.arange(2 * LANES)[:, None] % LANES
    s = np.arange(2 * LANES)[None, :]
    return jnp.asarray((s >= LANES) | (j > s), bf16)


def _layer_weights(l, w_in, cmp_pe, cmp_w1, cmp_w2, w_br_a, w_br_b, w_br_c, w_o, w_up, conv_w, conv_b, w_down):
    d = w_in.shape[1]
    w = w_in[l]
    w_proj = jnp.concatenate([w[:, 0:1280], w[:, 1304:], w[:, 1280:1304], jnp.zeros((d, _PROJ_W - 5912), f32)],
                             axis=1).astype(bf16)
    pe4 = jnp.concatenate([cmp_pe[l], cmp_pe[l]], axis=2)
    w1 = cmp_w1[l].reshape(2, CMP_LEN, HEAD_DIM, CMP_HID)
    w1bd = jnp.zeros((2, CMP_LEN, NSA_KV * HEAD_DIM, NSA_KV * CMP_HID), f32)
    w2bd = jnp.zeros((2, NSA_KV * CMP_HID, NSA_KV * HEAD_DIM), f32)
    for g in range(NSA_KV):
        w1bd = w1bd.at[:, :, HEAD_DIM * g:HEAD_DIM * (g + 1), CMP_HID * g:CMP_HID * (g + 1)].set(w1)
        w2bd = w2bd.at[:, CMP_HID * g:CMP_HID * (g + 1), HEAD_DIM * g:HEAD_DIM * (g + 1)].set(cmp_w2[l])
    cw8 = jnp.concatenate([conv_w[l], jnp.zeros((8 - CONV_W, conv_w.shape[2]), f32)], axis=0)
    return dict(w_proj=w_proj, pe4=pe4, w1bd=w1bd.astype(bf16), w2bd=w2bd.astype(bf16),
                wa=w_br_a[l].astype(bf16), wb=w_br_b[l].astype(bf16), wc=w_br_c[l].astype(bf16),
                wo=w_o[l].astype(bf16), w_up=w_up[l].astype(bf16), cw8=cw8, cb=conv_b[l].reshape(1, -1),
                wd=w_down[l].astype(bf16))


def _mod_parts(mod_rows, per_row):
    r = mod_rows.shape[0]
    parts = mod_rows.reshape(r, 6, D_MODEL)
    return [parts[:, k].reshape((1, r, D_MODEL) if per_row else (r, 1, D_MODEL)) for k in range(6)]


def kernel(x_prompt, x_sample, cache_nsa, cache_moba, cache_sb, state_win, state_conv, page_table, c_prompt,
           c_sample, norm1_g, norm2_g, w_ada, b_ada, w_in, cmp_pe, cmp_w1, cmp_w2, w_br_a, w_br_b, w_br_c, w_o,
           w_up, conv_w, conv_b, w_down, final_g):
    b, t, d = x_prompt.shape
    s = x_sample.shape[0]
    depth = w_in.shape[0]
    n_phys = cache_nsa.shape[1]
    n_pages = page_table.shape[1]
    past = n_pages * PAGE_SIZE
    tm = 512
    tq = 128
    tk = 512

    n_c = b + s
    c_all = jnp.concatenate([c_prompt, c_sample, jnp.zeros((-n_c % 8, d), f32)], axis=0)
    mod = _ada_mod(c_all, w_ada, b_ada)

    nc_p = t // CMP_STRIDE
    n_cmp_p = (t - CMP_LEN) // CMP_STRIDE + 1
    agg_p = _agg_matrix(nc_p, n_cmp_p)
    nc_s = past // CMP_STRIDE
    agg_s = _agg_matrix(nc_s, (past + 1 - CMP_LEN) // CMP_STRIDE + 1)
    e_sel = _expand_matrix(SEL_BLOCK, past)
    e_blk = _expand_matrix(MOBA_BLOCK, past)
    seg_mean = (jnp.swapaxes(e_blk, 0, 1).astype(f32) * (1.0 / MOBA_BLOCK)).astype(bf16)
    uu = _suffix_matrix()
    nb_pad = LANES // MOBA_HEADS

    feat_major = lambda a: jnp.transpose(a, (0, 1, 3, 4, 5, 2))
    caches = [feat_major(c).reshape(depth * n_phys, 512, PAGE_SIZE) for c in (cache_nsa, cache_moba, cache_sb)]
    state_win_t = feat_major(state_win).reshape(depth * s, 256, state_win.shape[2])

    xp = x_prompt.reshape(b * t, d)
    xs = x_sample.reshape(s, d)
    outs_p = [[] for _ in range(5)]
    outs_s = [[] for _ in range(5)]
    for l in range(depth):
        w = _layer_weights(l, w_in, cmp_pe, cmp_w1, cmp_w2, w_br_a, w_br_b, w_br_c, w_o, w_up, conv_w, conv_b,
                           w_down)
        mp = _mod_parts(mod[l, 0:b], per_row=False)
        ms = _mod_parts(mod[l, b:b + s], per_row=True)

        (q_n, nsa_rows, nsa_bf, win_rows, win_bf, mb_q, moba_rows, moba_bf, sb_q, sb_rows, sb_bf, g_m, g_n) = \
            _norm_mod_matmul(xp, norm1_g[l], mp[0], mp[1], w["w_proj"], _IN_SEGS, _IN_DTYPES, tm, "in_proj_prompt")
        r3 = lambda a: a.reshape(b, t, a.shape[1])
        cmp = _cmp_prompt(r3(nsa_rows), w["pe4"], w["w1bd"], w["w2bd"])
        o_a = _nsa_prompt(r3(q_n), r3(g_n), cmp, agg_p, r3(nsa_bf), r3(win_bf), n_cmp_p, tq, tk)
        kmean = _block_mean(r3(moba_rows), 256, MOBA_BLOCK).reshape(b, t // MOBA_BLOCK, MOBA_HEADS, HEAD_DIM)
        kmt = jnp.zeros((b, MOBA_HEADS, nb_pad, MOBA_HEADS, HEAD_DIM), f32)
        for h in range(MOBA_HEADS):
            kmt = kmt.at[:, h, 0:t // MOBA_BLOCK, h].set(kmean[:, :, h])
        o_b = _moba_prompt(r3(mb_q), kmt.reshape(b, LANES, 256), r3(moba_bf), tq, 2 * tk)
        o_c = _sb_prompt(r3(sb_q), uu, r3(sb_bf), 2 * tq, tk)
        xp = _merge_out(o_a.reshape(b * t, -1), o_b.reshape(b * t, -1), o_c.reshape(b * t, -1), g_m, xp, mp[2],
                        w["wa"], w["wb"], w["wc"], w["wo"], tm)
        u_a, u_b = _norm_mod_matmul(xp, norm2_g[l], mp[3], mp[4], w["w_up"], _UP_SEGS, _UP_DTYPES, tm,
                                    "ffn_up_prompt")
        xp = _ffn_down_seq(u_a, u_b, w["cw8"], w["cb"], w["wd"], xp, mp[5], tm)
        keep = min(WINDOW, t)
        outs_p[0].append(nsa_rows.reshape(b, t // PAGE_SIZE, PAGE_SIZE, 4, NSA_KV, HEAD_DIM))
        outs_p[1].append(moba_rows.reshape(b, t // PAGE_SIZE, PAGE_SIZE, 2, MOBA_HEADS, HEAD_DIM))
        outs_p[2].append(sb_rows.reshape(b, t // PAGE_SIZE, PAGE_SIZE, 2, SB_HEADS, HEAD_DIM))
        outs_p[3].append(r3(win_rows)[:, t - keep:].reshape(b, keep, 2, NSA_KV, HEAD_DIM))
        outs_p[4].append(r3(u_a)[:, t - (CONV_W - 1):])

        (q_n, nsa_rows, _, win_rows, _, mb_q, moba_rows, _, sb_q, sb_rows, _, g_m, g_n) = \
            _norm_mod_matmul(xs, norm1_g[l], ms[0], ms[1], w["w_proj"], _IN_SEGS, _IN_DTYPES, s, "in_proj_sample")
        s3 = lambda a: a.astype(f32).reshape(s, 1, a.shape[1])
        base = l * n_phys
        o_a = _nsa_decode(page_table, caches[0], base, s3(q_n), s3(g_n), s3(nsa_rows), s3(win_rows), state_win_t,
                          l * s, w["pe4"], w["w1bd"], w["w2bd"], agg_s, e_sel, 2)
        o_b = _moba_decode(page_table, caches[1], base, s3(mb_q), s3(moba_rows), seg_mean, e_blk, 2)
        o_c = _sb_decode(page_table, caches[2], base, s3(sb_q), uu, 2)
        xs = _merge_out(o_a.reshape(s, -1).astype(bf16), o_b.reshape(s, -1).astype(bf16),
                        o_c.reshape(s, -1).astype(bf16), g_m, xs, ms[2], w["wa"], w["wb"], w["wc"], w["wo"], s)
        u_a, u_b = _norm_mod_matmul(xs, norm2_g[l], ms[3], ms[4], w["w_up"], _UP_SEGS, _UP_DTYPES, s,
                                    "ffn_up_sample")
        xs = _ffn_down_step(u_a, state_conv[l, :, 1], state_conv[l, :, 0], u_b, w["cw8"], w["cb"], w["wd"], xs,
                            ms[5])
        conv_full = jnp.concatenate([state_conv[l], u_a.reshape(s, 1, -1)], axis=1)
        outs_s[0].append(nsa_rows.reshape(s, 1, 4, NSA_KV, HEAD_DIM))
        outs_s[1].append(moba_rows.reshape(s, 1, 2, MOBA_HEADS, HEAD_DIM))
        outs_s[2].append(sb_rows.reshape(s, 1, 2, SB_HEADS, HEAD_DIM))
        outs_s[3].append(win_rows.reshape(s, 1, 256))
        outs_s[4].append(conv_full[:, conv_full.shape[1] - (CONV_W - 1):])

    y_prompt = _final_norm(xp, final_g, tm).reshape(b, t, d)
    y_sample = _final_norm(xs, final_g, s).reshape(s, 1, d)
    wb = state_win.shape[2]
    win_s = _win_shift(state_win_t, jnp.concatenate(outs_s[3], axis=0), 4)
    win_s = jnp.transpose(win_s.reshape(depth, s, 2, NSA_KV, HEAD_DIM, wb), (0, 1, 5, 2, 3, 4))
    st = lambda lst: jnp.stack(lst)
    return (y_prompt, y_sample, st(outs_p[0]), st(outs_s[0]), st(outs_p[1]), st(outs_s[1]), st(outs_p[2]),
            st(outs_s[2]), st(outs_p[3]), win_s, st(outs_p[4]), st(outs_s[4]))
```

```python
import functools

import numpy as np
import jax
import jax.numpy as jnp
from jax import lax
from jax.experimental import pallas as pl
from jax.experimental.pallas import tpu as pltpu

f32 = jnp.float32
bf16 = jnp.bfloat16

D_MODEL = 1024
HEAD_DIM = 64
NSA_HEADS = 8
NSA_KV = 2
HPG = NSA_HEADS // NSA_KV
CMP_LEN = 32
CMP_STRIDE = 16
CMP_HID = 128
SEL_BLOCK = 64
SEL_TOPN = 16
WINDOW = 512
MOBA_HEADS = 4
MOBA_BLOCK = 256
MOBA_TOPK = 3
SB_HEADS = 4
D_FF = 2816
CONV_W = 3
PAGE_SIZE = 128
RMS_EPS = 1e-6
NEG = -1e30
FORCE = 1e9

LANES = 128
MASK_BIG = 2.0 ** 100
M_INIT = -1e29
REMOVED = -3e38
SB_CUTOFF = -110.0
VMEM_LIMIT_MB = 56

NSA_SLOPES = [2.0 ** (-8.0 * (h + 1) / NSA_HEADS) for h in range(NSA_HEADS)]
MOBA_SLOPES = [2.0 ** (-8.0 * (h + 1) / MOBA_HEADS) for h in range(MOBA_HEADS)]

_Q_N, _NSA, _WIN, _MB_Q, _MOBA, _SB_Q, _SB, _G_M, _G_N, _PROJ_W = 0, 512, 1024, 1280, 1536, 2048, 2304, 2816, 5888, 6016


def _cparams(sem, vmem_mb=VMEM_LIMIT_MB):
    return pltpu.CompilerParams(dimension_semantics=sem, vmem_limit_bytes=vmem_mb * 2 ** 20)


def _const_spec(shape):
    nd = len(shape)
    return pl.BlockSpec(shape, lambda *_: (0,) * nd, pipeline_mode=pl.Buffered(1))


def _nt(a, b):
    return lax.dot_general(a, b, (((1,), (1,)), ((), ())), preferred_element_type=f32)


def _dot(a, b):
    return jnp.dot(a, b, preferred_element_type=f32)


def _gelu_tanh(x):
    return x * (0.5 * (1.0 + jnp.tanh(np.sqrt(2.0 / np.pi) * (x + 0.044715 * (x * x * x)))))


def _ref_softmax(s, valid):
    s = jnp.where(valid, s, NEG)
    e = jnp.where(valid, jnp.exp(s - jnp.max(s, axis=-1, keepdims=True)), 0.0)
    return e / jnp.maximum(jnp.sum(e, axis=-1, keepdims=True), 1e-30)


def _ada_body(c_ref, w_ref, b_ref, o_ref):
    c = c_ref[...]
    s = c * jax.nn.sigmoid(c)
    o_ref[...] = _dot(s.astype(bf16), w_ref[...].astype(bf16)) + b_ref[...]


def _ada_mod(c_all, w_ada, b_ada):
    depth, d, n = w_ada.shape
    r = c_all.shape[0]
    tn = 1536
    return pl.pallas_call(
        _ada_body,
        out_shape=jax.ShapeDtypeStruct((depth, r, n), f32),
        grid=(depth, n // tn),
        in_specs=[pl.BlockSpec((r, d), lambda l, j: (0, 0)),
                  pl.BlockSpec((None, d, tn), lambda l, j: (l, 0, j)),
                  pl.BlockSpec((None, 1, tn), lambda l, j: (l, 0, j))],
        out_specs=pl.BlockSpec((None, r, tn), lambda l, j: (l, 0, j)),
        compiler_params=_cparams(("parallel", "parallel")),
        name="ada_mod",
    )(c_all, w_ada, b_ada.reshape(depth, 1, n))


def _nmm_body(x_ref, g_ref, sh_ref, sc_ref, w_ref, *o_refs, segs):
    x = x_ref[...]
    y = x * lax.rsqrt(jnp.mean(x * x, axis=-1, keepdims=True) + RMS_EPS)
    h = (y * g_ref[...]) * (1.0 + sc_ref[...]) + sh_ref[...]
    hb = h.astype(bf16)
    k = 0
    for off, width, scales in segs:
        outs = o_refs[k:k + len(scales)]
        k += len(scales)
        for c0 in range(0, width, 512):
            cw = min(512, width - c0)
            acc = _dot(hb, w_ref[:, off + c0:off + c0 + cw])
            for o, scale in zip(outs, scales):
                o[:, c0:c0 + cw] = (acc if scale == 1.0 else acc * scale).astype(o.dtype)


def _norm_mod_matmul(x, g, shift, scale, w_bf, segs, out_dtypes, tm, name):
    r, d = x.shape
    nb, rb, _ = shift.shape
    tiles_per_b = (r // nb) // tm
    out_shape, out_specs = [], []
    k = 0
    for off, width, scales in segs:
        for _ in scales:
            out_shape.append(jax.ShapeDtypeStruct((r, width), out_dtypes[k]))
            out_specs.append(pl.BlockSpec((tm, width), lambda i: (i, 0)))
            k += 1
    mod_spec = pl.BlockSpec((None, rb, d), lambda i: (i // tiles_per_b, 0, 0))
    return pl.pallas_call(
        functools.partial(_nmm_body, segs=segs),
        out_shape=out_shape,
        grid=(r // tm,),
        in_specs=[pl.BlockSpec((tm, d), lambda i: (i, 0)), _const_spec((1, d)), mod_spec, mod_spec,
                  _const_spec(w_bf.shape)],
        out_specs=out_specs,
        compiler_params=_cparams(("parallel",)),
        name=name,
    )(x, g.reshape(1, d), shift, scale, w_bf)


_IN_SEGS = ((_Q_N, 512, (0.125,)), (_NSA, 512, (1.0, 1.0)), (_WIN, 256, (1.0, 1.0)), (_MB_Q, 256, (0.125,)),
            (_MOBA, 512, (1.0, 1.0)), (_SB_Q, 256, (0.125,)), (_SB, 512, (1.0, 1.0)), (_G_M, 3072, (1.0,)),
            (_G_N, 128, (1.0,)))
_IN_DTYPES = (bf16, f32, bf16, f32, bf16, bf16, f32, bf16, bf16, f32, bf16, f32, f32)
_UP_SEGS = ((0, D_FF, (1.0,)), (D_FF, D_FF, (1.0,)))
_UP_DTYPES = (f32, f32)


def _cmp_core(rows_refs, pe_ref, w1_ref, w2_ref, nc):
    outs = []
    for kv in range(2):
        acc_a = jnp.zeros((nc, NSA_KV * CMP_HID), f32)
        acc_b = jnp.zeros((nc, NSA_KV * CMP_HID), f32)
        for r in range(CMP_STRIDE):
            y = rows_refs[kv][pl.ds(r, nc, stride=CMP_STRIDE), :]
            acc_a = acc_a + _dot((y + pe_ref[kv, r:r + 1, :]).astype(bf16), w1_ref[kv, r])
            acc_b = acc_b + _dot((y + pe_ref[kv, CMP_STRIDE + r:CMP_STRIDE + r + 1, :]).astype(bf16),
                                 w1_ref[kv, CMP_STRIDE + r])
        pre = acc_a + pltpu.roll(acc_b, nc - 1, 0)
        outs.append(_dot(_gelu_tanh(pre).astype(bf16), w2_ref[kv]))
    return jnp.concatenate(outs, axis=1)


def _cmp_prompt_body(k_ref, v_ref, pe_ref, w1_ref, w2_ref, o_ref, *, nc):
    o_ref[...] = _cmp_core((k_ref, v_ref), pe_ref, w1_ref, w2_ref, nc)


def _cmp_prompt(nsa_rows, pe2, w1bd, w2bd):
    b, l, _ = nsa_rows.shape
    nc = l // CMP_STRIDE
    return pl.pallas_call(
        functools.partial(_cmp_prompt_body, nc=nc),
        out_shape=jax.ShapeDtypeStruct((b, nc, 256), f32),
        grid=(b,),
        in_specs=[pl.BlockSpec((None, l, LANES), lambda i: (i, 0, 0)),
                  pl.BlockSpec((None, l, LANES), lambda i: (i, 0, 1)), _const_spec(pe2.shape),
                  _const_spec(w1bd.shape), _const_spec(w2bd.shape)],
        out_specs=pl.BlockSpec((None, nc, 256), lambda i: (i, 0, 0)),
        compiler_params=_cparams(("parallel",)),
        name="nsa_compress_prompt",
    )(nsa_rows, nsa_rows, pe2, w1bd, w2bd)


SLAB = 64
FLASH_SCRATCH = 2


class _Flash:
    def __init__(self, scratch, slopes, tq, t0):
        self.m, self.acc = scratch
        self.slope8 = jnp.concatenate(
            [jnp.full((1, 1), v, f32) for v in slopes] + [jnp.zeros((8 - len(slopes), 1), f32)], axis=0)
        self.tq, self.t0 = tq, t0
        self.m[...] = jnp.full(self.m.shape, M_INIT, f32)
        self.acc[...] = jnp.zeros(self.acc.shape, f32)

    def slab(self, j, s_all, bias, rel, masked):
        r0 = j * SLAB
        rows = slice(r0, r0 + SLAB)
        head = r0 // self.tq
        s = s_all[rows, :] + bias[head:head + 1, :]
        if masked:
            t_rel = (r0 + lax.broadcasted_iota(jnp.int32, (SLAB, 1), 0)) % self.tq
            s = jnp.where(rel <= t_rel, s, -MASK_BIG)
        m_old = self.m[rows, :]
        m_new = jnp.maximum(m_old, jnp.max(s, axis=-1, keepdims=True))
        self.m[rows, :] = m_new
        p = jnp.exp(s - jnp.concatenate([m_new] * (s.shape[1] // LANES), axis=1))
        return p.astype(bf16), jnp.exp(m_old - m_new)

    def result(self):
        acc = self.acc[...]
        return acc[:, 0:LANES] / acc[:, LANES:2 * LANES]


def _flash_tiles(chains, k0, masked):
    first = chains[0][0]
    rows = first.m.shape[0]
    tk = chains[0][2].shape[1]
    rel = k0 - first.t0 + lax.broadcasted_iota(jnp.int32, (1, tk), 1)
    logits = [(_dot(q_aug, kt_aug), f.slope8 * rel.astype(f32)) for f, q_aug, kt_aug, _ in chains]
    parts = [[] for _ in chains]
    for j in range(rows // SLAB):
        for c, (f, _, _, _) in enumerate(chains):
            parts[c].append(f.slab(j, logits[c][0], logits[c][1], rel, masked))
    for c, (f, _, _, v_aug) in enumerate(chains):
        p = jnp.concatenate([x[0] for x in parts[c]], axis=0)
        alpha = jnp.concatenate([x[1] for x in parts[c]], axis=0)
        f.acc[...] = jnp.concatenate([alpha, alpha], axis=1) * f.acc[...] + _dot(p, v_aug)


def _flash_scratch(rows):
    return [pltpu.VMEM((rows, LANES), f32), pltpu.VMEM((rows, 2 * LANES), f32)]


def _with_ones(v):
    return jnp.concatenate([v, jnp.ones(v.shape, v.dtype)], axis=-1)


def _masked_exp(s_all, slopes, rel, valid_fn, tq):
    out = []
    for j in range(s_all.shape[0] // SLAB):
        r0 = j * SLAB
        t_rel = r0 % tq + lax.broadcasted_iota(jnp.int32, (SLAB, 1), 0)
        valid = valid_fn(t_rel)
        s = jnp.where(valid, s_all[r0:r0 + SLAB, :] + slopes[r0 // tq] * rel.astype(f32), NEG)
        out.append(jnp.exp(s - jnp.maximum(jnp.max(s, axis=-1, keepdims=True), M_INIT)))
    return jnp.concatenate(out, axis=0)


def _topk_rows(score, ids, k, n_ids):
    picked = jnp.zeros(score.shape, f32)
    for _ in range(k):
        mx = jnp.max(score, axis=0, keepdims=True)
        idx = jnp.min(jnp.where(score == mx, ids, n_ids), axis=0, keepdims=True)
        pick = ids == idx
        picked = jnp.where(pick, 1.0, picked)
        score = jnp.where(pick, REMOVED, score)
    return picked


def _nsa_prompt_body(q_ref, gn_ref, kct_ref, vc_ref, agg_ref, kst_ref, oh_ref, vs_ref, kwt_ref, vw_ref, o_ref,
                     sel_scr, qa_scr, part_scr, gate_scr, *flash_scr, tq, tk, n_cmp):
    i = pl.program_id(1)
    t0 = i * tq
    nc = vc_ref.shape[0]
    qf = q_ref[...].astype(f32)
    lane = lax.broadcasted_iota(jnp.int32, (tq, LANES), 1)
    sig = jax.nn.sigmoid(gn_ref[...])
    kct = kct_ref[...]
    vc = vc_ref[...].astype(bf16)
    aggb = agg_ref[...].astype(bf16)
    ones_c = jnp.ones((nc, LANES), bf16)
    n_id = lax.broadcasted_iota(jnp.int32, (1, nc), 1)
    cend_rel = n_id * CMP_STRIDE + (CMP_LEN - 1) - t0
    cend_masked = jnp.where(n_id < n_cmp, cend_rel, 2 ** 30)
    blk_id = lax.broadcasted_iota(jnp.int32, (LANES, tq), 0)
    cur = (t0 + lax.broadcasted_iota(jnp.int32, (LANES, tq), 1)) // SEL_BLOCK
    causal_blk = blk_id <= cur
    forced = (blk_id == 0) | (blk_id == cur) | (blk_id == cur - 1)
    chunks = [jnp.zeros((tq, LANES), f32) for _ in range(NSA_HEADS // 2)]

    for g in range(NSA_KV):
        heads = [HPG * g + hh for hh in range(HPG)]
        pieces = []
        for h in heads:
            blk = qf[:, LANES * (h // 2):LANES * (h // 2 + 1)]
            if h % 2 != g:
                blk = pltpu.roll(blk, HEAD_DIM, 1)
            pieces.append(jnp.where(lane // HEAD_DIM == g, blk, 0.0))
        qg = jnp.concatenate(pieces, axis=0).astype(bf16)
        slopes = [NSA_SLOPES[h] for h in heads]

        valid_c = lambda t_rel: cend_masked <= t_rel
        e_c = _masked_exp(_dot(qg, kct), slopes, cend_rel, valid_c, tq)
        e_hi = e_c.astype(bf16)
        e_lo = (e_c - e_hi.astype(f32)).astype(bf16)
        r_hi = _dot(e_hi, jnp.concatenate([vc, ones_c, aggb], axis=1))
        r_lo = _dot(e_lo, jnp.concatenate([aggb, ones_c], axis=1))
        o_c = r_hi[:, 0:LANES] / jnp.maximum(r_hi[:, LANES:2 * LANES], 1e-30)
        imp_rows = (r_hi[:, 2 * LANES:3 * LANES] + r_lo[:, 0:LANES]) / jnp.maximum(
            r_hi[:, LANES:2 * LANES] + r_lo[:, LANES:2 * LANES], 1e-30)
        imp = imp_rows[0:tq] + imp_rows[tq:2 * tq] + imp_rows[2 * tq:3 * tq] + imp_rows[3 * tq:4 * tq]

        score = jnp.where(causal_blk, jnp.where(forced, FORCE, imp.T), NEG)
        sel_t = jnp.where(causal_blk, _topk_rows(score, blk_id, SEL_TOPN, LANES), 0.0)
        notsel = (1.0 - sel_t).T
        qa_scr[g] = jnp.concatenate([qg, jnp.concatenate([notsel] * HPG, axis=0).astype(bf16)], axis=1)
        sel_scr[g] = sel_t

        wl = WINDOW + tq
        s0 = pl.multiple_of(jnp.maximum(t0 - WINDOW, 0), tq)
        w_rel = s0 - t0 + lax.broadcasted_iota(jnp.int32, (1, wl), 1)
        valid_w = lambda t_rel: (w_rel <= t_rel) & (w_rel > t_rel - WINDOW)
        e_w = _masked_exp(_dot(qg, kwt_ref[:, pl.ds(s0, wl)]), slopes, w_rel, valid_w, tq)
        r_w = _dot(e_w.astype(bf16), _with_ones(vw_ref[pl.ds(s0, wl), :]))
        o_w = r_w[:, 0:LANES] / r_w[:, LANES:2 * LANES]

        gates = [jnp.concatenate([sig[:, br * NSA_HEADS + h:br * NSA_HEADS + h + 1] for h in heads], axis=0)
                 for br in range(3)]
        part_scr[g] = gates[0] * o_c + gates[2] * o_w
        gate_scr[g] = gates[1]

    flashes = [_Flash(flash_scr[FLASH_SCRATCH * g:FLASH_SCRATCH * (g + 1)],
                      NSA_SLOPES[HPG * g:HPG * (g + 1)], tq, t0) for g in range(NSA_KV)]

    def sel_step(kt, masked):
        k0 = pl.multiple_of(kt * tk, tk)
        kt_aug = jnp.concatenate([kst_ref[:, pl.ds(k0, tk)], oh_ref[:, pl.ds(k0, tk)]], axis=0)
        v_aug = _with_ones(vs_ref[pl.ds(k0, tk), :])
        _flash_tiles([(flashes[g], qa_scr[g], kt_aug, v_aug) for g in range(NSA_KV)], k0, masked)

    def sel_loop(kt, carry):
        blk0 = pl.multiple_of(kt * (tk // SEL_BLOCK), tk // SEL_BLOCK)
        picked = jnp.maximum(sel_scr[0, pl.ds(blk0, tk // SEL_BLOCK), :], sel_scr[1, pl.ds(blk0, tk // SEL_BLOCK), :])

        @pl.when(jnp.max(picked) > 0.0)
        def _():
            sel_step(kt, False)

        return carry

    kd = t0 // tk
    lax.fori_loop(0, kd, sel_loop, 0)
    sel_step(kd, True)

    for g in range(NSA_KV):
        heads = [HPG * g + hh for hh in range(HPG)]
        o = part_scr[g] + gate_scr[g] * flashes[g].result()
        for hh, h in enumerate(heads):
            piece = o[hh * tq:(hh + 1) * tq]
            if h % 2 != g:
                piece = pltpu.roll(piece, HEAD_DIM, 1)
            chunks[h // 2] = chunks[h // 2] + jnp.where(lane // HEAD_DIM == h % 2, piece, 0.0)

    o_ref[...] = jnp.concatenate(chunks, axis=1).astype(o_ref.dtype)


def _nsa_prompt(q_n, g_n, cmp, agg, nsa_bf, win_bf, n_cmp, tq, tk):
    b, l, _ = q_n.shape
    nc = cmp.shape[1]
    rows = HPG * tq
    kct = jnp.swapaxes(cmp[:, :, 0:LANES], 1, 2).astype(bf16)
    onehot_t = _block_onehot_t(SEL_BLOCK, l)
    kst = jnp.swapaxes(nsa_bf[:, :, 2 * LANES:3 * LANES], 1, 2)
    kwt = jnp.swapaxes(win_bf[:, :, 0:LANES], 1, 2)
    per_b = lambda shape, col=0: pl.BlockSpec((None,) + shape, lambda bi, i: (bi, 0, col))
    return pl.pallas_call(
        functools.partial(_nsa_prompt_body, tq=tq, tk=tk, n_cmp=n_cmp),
        out_shape=jax.ShapeDtypeStruct((b, l, NSA_HEADS * HEAD_DIM), bf16),
        grid=(b, l // tq),
        in_specs=[pl.BlockSpec((None, tq, 512), lambda bi, i: (bi, i, 0)),
                  pl.BlockSpec((None, tq, LANES), lambda bi, i: (bi, i, 0)),
                  per_b((LANES, nc)), per_b((nc, LANES), 1), _const_spec(agg.shape),
                  per_b((LANES, l)), _const_spec(onehot_t.shape), per_b((l, LANES), 3), per_b((LANES, l)),
                  per_b((l, LANES), 1)],
        out_specs=pl.BlockSpec((None, tq, 512), lambda bi, i: (bi, i, 0)),
        scratch_shapes=[pltpu.VMEM((NSA_KV, LANES, tq), f32), pltpu.VMEM((NSA_KV, rows, 2 * LANES), bf16),
                        pltpu.VMEM((NSA_KV, rows, LANES), f32), pltpu.VMEM((NSA_KV, rows, 1), f32)]
        + _flash_scratch(rows) * NSA_KV,
        compiler_params=_cparams(("parallel", "parallel")),
        name="nsa_attention_prompt",
    )(q_n, g_n, kct, cmp, agg, kst, onehot_t, nsa_bf, kwt, win_bf)


def _block_mean_body(k_ref, o_ref):
    o_ref[...] = jnp.mean(k_ref[...], axis=0, keepdims=True)


def _block_mean(rows, width, blk):
    b, l, _ = rows.shape
    return pl.pallas_call(
        _block_mean_body,
        out_shape=jax.ShapeDtypeStruct((b, l // blk, 1, width), f32),
        grid=(b, l // blk),
        in_specs=[pl.BlockSpec((None, blk, width), lambda bi, j: (bi, j, 0))],
        out_specs=pl.BlockSpec((None, None, 1, width), lambda bi, j: (bi, j, 0, 0)),
        compiler_params=_cparams(("parallel", "parallel")),
        name="moba_block_mean",
    )(rows)


def _moba_prompt_body(q_ref, kmt_ref, kt_ref, oh_ref, v_ref, o_ref, qa_scr, *flash_scr, tq, tk, nb_pad):
    i = pl.program_id(1)
    t0 = i * tq
    q = q_ref[...]
    qf = q.astype(f32)
    lane = lax.broadcasted_iota(jnp.int32, (tq, LANES), 1)

    gate_t = _nt(kmt_ref[...].astype(bf16), q)
    blk_id = lax.broadcasted_iota(jnp.int32, (LANES, tq), 0) % nb_pad
    own = (t0 + lax.broadcasted_iota(jnp.int32, (LANES, tq), 1)) // MOBA_BLOCK
    past = blk_id < own
    score = jnp.where(past, gate_t, NEG)
    parts = [_topk_rows(score[nb_pad * h:nb_pad * (h + 1)], blk_id[nb_pad * h:nb_pad * (h + 1)], MOBA_TOPK, nb_pad)
             for h in range(MOBA_HEADS)]
    sel_t = jnp.where(past, jnp.concatenate(parts, axis=0), 0.0)
    sel_t = jnp.where(blk_id == own, 1.0, sel_t)
    notsel = (1.0 - sel_t).T

    nch = MOBA_HEADS // 2
    for c in range(nch):
        q_rows = []
        for e in range(2):
            h = 2 * c + e
            qh = jnp.where(lane // HEAD_DIM == e, qf[:, LANES * c:LANES * (c + 1)], 0.0)
            ns = notsel if h == 0 else pltpu.roll(notsel, LANES - nb_pad * h, 1)
            ns = jnp.where(lane < nb_pad, ns, 0.0)
            q_rows.append(jnp.concatenate([qh, ns], axis=1))
        qa_scr[c] = jnp.concatenate(q_rows, axis=0).astype(bf16)
    flashes = [_Flash(flash_scr[FLASH_SCRATCH * c:FLASH_SCRATCH * (c + 1)],
                      MOBA_SLOPES[2 * c:2 * c + 2], tq, t0) for c in range(nch)]

    def step(kt, masked):
        k0 = pl.multiple_of(kt * tk, tk)
        onehot = oh_ref[:, pl.ds(k0, tk)]
        _flash_tiles([(flashes[c], qa_scr[c], jnp.concatenate([kt_ref[c, :, pl.ds(k0, tk)], onehot], axis=0),
                       _with_ones(v_ref[pl.ds(k0, tk), LANES * c:LANES * (c + 1)])) for c in range(nch)], k0, masked)

    def loop(kt, carry):
        step(kt, False)
        return carry

    lax.fori_loop(0, t0 // tk, loop, 0)
    step(t0 // tk, True)
    out_chunks = []
    for c in range(nch):
        o = flashes[c].result()
        out_chunks.append(jnp.where(lane < HEAD_DIM, o[0:tq], o[tq:2 * tq]))
    o_ref[...] = jnp.concatenate(out_chunks, axis=1).astype(o_ref.dtype)


def _moba_prompt(mb_q, kmt, moba_bf, tq, tk):
    b, l, _ = mb_q.shape
    nb_pad = LANES // MOBA_HEADS
    nch = MOBA_HEADS // 2
    onehot_t = _block_onehot_t(MOBA_BLOCK, l)
    k_t = jnp.swapaxes(jnp.swapaxes(moba_bf[:, :, 0:nch * LANES].reshape(b, l, nch, LANES), 1, 3), 1, 2)
    return pl.pallas_call(
        functools.partial(_moba_prompt_body, tq=tq, tk=tk, nb_pad=nb_pad),
        out_shape=jax.ShapeDtypeStruct((b, l, MOBA_HEADS * HEAD_DIM), bf16),
        grid=(b, l // tq),
        in_specs=[pl.BlockSpec((None, tq, 256), lambda bi, i: (bi, i, 0)),
                  pl.BlockSpec((None, LANES, 256), lambda bi, i: (bi, 0, 0)),
                  pl.BlockSpec((None, nch, LANES, l), lambda bi, i: (bi, 0, 0, 0)),
                  _const_spec(onehot_t.shape),
                  pl.BlockSpec((None, l, nch * LANES), lambda bi, i: (bi, 0, 1))],
        out_specs=pl.BlockSpec((None, tq, 256), lambda bi, i: (bi, i, 0)),
        scratch_shapes=[pltpu.VMEM((nch, 2 * tq, 2 * LANES), bf16)] + _flash_scratch(2 * tq) * nch,
        compiler_params=_cparams(("parallel", "parallel")),
        name="moba_attention_prompt",
    )(mb_q, kmt, k_t, onehot_t, moba_bf)


def _log_keep(z):
    return -(jnp.maximum(z, 0.0) + jnp.log(1.0 + jnp.exp(-jnp.abs(z))))


def _suffix_sums(lk, uu):
    hi = lk.astype(bf16)
    lo = (lk - hi.astype(f32)).astype(bf16)
    r = _dot(jnp.concatenate([hi, lo], axis=1), uu)
    return r[:, 0:LANES], r[:, LANES:2 * LANES]


def _sb_prompt_body(q_ref, uu_ref, k_ref, v_ref, o_ref, carry_scr, acc_scr, *, tq, tk):
    i = pl.program_id(2)
    t0 = i * tq
    qf = q_ref[...].astype(f32)
    lane = lax.broadcasted_iota(jnp.int32, (tq, LANES), 1)
    q2 = jnp.concatenate([jnp.where(lane // HEAD_DIM == e, qf, 0.0) for e in range(2)], axis=0).astype(bf16)
    t_row = t0 + lax.broadcasted_iota(jnp.int32, (2 * tq, 1), 0) % tq
    uu = uu_ref[...]
    carry_scr[...] = jnp.zeros(carry_scr.shape, f32)
    acc_scr[...] = jnp.zeros(acc_scr.shape, f32)

    def step(kt, diag):
        k0 = pl.multiple_of(kt * tk, tk)
        z = _dot(q2, k_ref[:, pl.ds(k0, tk)])
        lk = _log_keep(z)
        if diag:
            is_past = (k0 + lax.broadcasted_iota(jnp.int32, (1, tk), 1)) < t_row
            lk = jnp.where(is_past, lk, 0.0)
        carry = carry_scr[...]
        between = [None] * (tk // LANES)
        for c in reversed(range(tk // LANES)):
            later, total = _suffix_sums(lk[:, LANES * c:LANES * (c + 1)], uu)
            between[c] = later + carry
            carry = carry + total
        w = jnp.exp(z + lk + jnp.concatenate(between, axis=1))
        if diag:
            w = jnp.where(is_past, w, 0.0)
        acc_scr[...] = acc_scr[...] + _dot(w.astype(bf16), v_ref[pl.ds(k0, tk), :])
        carry_scr[...] = carry

    kd = t0 // tk
    step(kd, True)

    def more(state):
        j, top = state
        return (j < kd) & (top > SB_CUTOFF)

    def walk(state):
        j, _ = state
        step(kd - 1 - j, False)
        return j + 1, jnp.max(carry_scr[...])

    lax.while_loop(more, walk, (0, jnp.max(carry_scr[...])))
    o = acc_scr[...]
    o_ref[...] = jnp.where(lane < HEAD_DIM, o[0:tq], o[tq:2 * tq]).astype(o_ref.dtype)


def _sb_prompt(sb_q, uu, sb_bf, tq, tk):
    b, l, _ = sb_q.shape
    nch = SB_HEADS // 2
    k_t = jnp.swapaxes(jnp.swapaxes(sb_bf[:, :, 0:nch * LANES].reshape(b, l, nch, LANES), 1, 3), 1, 2)
    return pl.pallas_call(
        functools.partial(_sb_prompt_body, tq=tq, tk=tk),
        out_shape=jax.ShapeDtypeStruct((b, l, SB_HEADS * HEAD_DIM), bf16),
        grid=(b, nch, l // tq),
        in_specs=[pl.BlockSpec((None, tq, LANES), lambda bi, c, i: (bi, i, c)),
                  _const_spec(uu.shape),
                  pl.BlockSpec((None, None, LANES, l), lambda bi, c, i: (bi, c, 0, 0)),
                  pl.BlockSpec((None, l, LANES), lambda bi, c, i: (bi, 0, nch + c))],
        out_specs=pl.BlockSpec((None, tq, LANES), lambda bi, c, i: (bi, i, c)),
        scratch_shapes=[pltpu.VMEM((2 * tq, LANES), f32), pltpu.VMEM((2 * tq, LANES), f32)],
        compiler_params=_cparams(("parallel", "parallel", "parallel")),
        name="stickbreak_attention_prompt",
    )(sb_q, uu, k_t, sb_bf)


def _merge_body(oa_ref, ob_ref, oc_ref, gm_ref, x_ref, gate_ref, wa_ref, wb_ref, wc_ref, wo_ref, o_ref):
    d = x_ref.shape[1]
    g = jax.nn.sigmoid(gm_ref[...])
    merged = (g[:, 0:d] * _dot(oa_ref[...], wa_ref[...]) + g[:, d:2 * d] * _dot(ob_ref[...], wb_ref[...])
              + g[:, 2 * d:3 * d] * _dot(oc_ref[...], wc_ref[...]))
    o_ref[...] = x_ref[...] + gate_ref[...] * _dot(merged.astype(bf16), wo_ref[...])


def _merge_out(o_a, o_b, o_c, g_m, x, gate, wa, wb, wc, wo, tm):
    r, d = x.shape
    nb, rb, _ = gate.shape
    tiles_per_b = (r // nb) // tm
    row = lambda w: pl.BlockSpec((tm, w), lambda i: (i, 0))
    return pl.pallas_call(
        _merge_body,
        out_shape=jax.ShapeDtypeStruct((r, d), f32),
        grid=(r // tm,),
        in_specs=[row(o_a.shape[1]), row(o_b.shape[1]), row(o_c.shape[1]), row(3 * d), row(d),
                  pl.BlockSpec((None, rb, d), lambda i: (i // tiles_per_b, 0, 0)),
                  _const_spec(wa.shape), _const_spec(wb.shape), _const_spec(wc.shape), _const_spec(wo.shape)],
        out_specs=row(d),
        compiler_params=_cparams(("parallel",)),
        name="merge_out_proj",
    )(o_a, o_b, o_c, g_m, x, gate, wa, wb, wc, wo)


def _ffn_tail(a, a_m1, a_m2, b, cw_ref, cb_ref, wd_ref, x_ref, gate_ref, o_ref):
    conv = cb_ref[...] + a_m2 * cw_ref[0:1, :]
    conv = conv + a_m1 * cw_ref[1:2, :]
    conv = conv + a * cw_ref[2:3, :]
    y = _dot((_gelu_tanh(conv) * b).astype(bf16), wd_ref[...])
    o_ref[...] = x_ref[...] + gate_ref[...] * y


def _ffn_seq_body(a_ref, halo_ref, b_ref, cw_ref, cb_ref, wd_ref, x_ref, gate_ref, o_ref, *, tiles_per_b):
    a = a_ref[...]
    first = pl.program_id(0) % tiles_per_b == 0
    halo = jnp.where(first, 0.0, halo_ref[...])
    row = lax.broadcasted_iota(jnp.int32, a.shape, 0)
    a_m1 = jnp.where(row < 1, halo[7:8, :], pltpu.roll(a, 1, 0))
    a_m2 = jnp.where(row < 1, halo[6:7, :], jnp.where(row < 2, halo[7:8, :], pltpu.roll(a, 2, 0)))
    _ffn_tail(a, a_m1, a_m2, b_ref[...], cw_ref, cb_ref, wd_ref, x_ref, gate_ref, o_ref)


def _ffn_step_body(a_ref, am1_ref, am2_ref, b_ref, cw_ref, cb_ref, wd_ref, x_ref, gate_ref, o_ref):
    _ffn_tail(a_ref[...], am1_ref[...], am2_ref[...], b_ref[...], cw_ref, cb_ref, wd_ref, x_ref, gate_ref, o_ref)


def _ffn_down_seq(u_a, u_b, cw8, cb, wd, x, gate, tm):
    r, d = x.shape
    ff = u_a.shape[1]
    nb = gate.shape[0]
    tiles_per_b = (r // nb) // tm
    row = lambda w: pl.BlockSpec((tm, w), lambda i: (i, 0))
    return pl.pallas_call(
        functools.partial(_ffn_seq_body, tiles_per_b=tiles_per_b),
        out_shape=jax.ShapeDtypeStruct((r, d), f32),
        grid=(r // tm,),
        in_specs=[row(ff), pl.BlockSpec((8, ff), lambda i: (jnp.maximum(i * (tm // 8) - 1, 0), 0)), row(ff),
                  _const_spec(cw8.shape), _const_spec(cb.shape), _const_spec(wd.shape), row(d),
                  pl.BlockSpec((None, 1, d), lambda i: (i // tiles_per_b, 0, 0))],
        out_specs=row(d),
        compiler_params=_cparams(("parallel",)),
        name="conv_ffn_down_seq",
    )(u_a, u_a, u_b, cw8, cb, wd, x, gate)


def _ffn_down_step(u_a, a_m1, a_m2, u_b, cw8, cb, wd, x, gate):
    r, d = x.shape
    full = lambda a: pl.BlockSpec(a.shape, lambda i: (0,) * a.ndim)
    return pl.pallas_call(
        _ffn_step_body,
        out_shape=jax.ShapeDtypeStruct((r, d), f32),
        grid=(1,),
        in_specs=[full(u_a), full(a_m1), full(a_m2), full(u_b), full(cw8), full(cb), full(wd), full(x),
                  pl.BlockSpec((None, r, d), lambda i: (0, 0, 0))],
        out_specs=full(x),
        compiler_params=_cparams(("arbitrary",)),
        name="conv_ffn_down_step",
    )(u_a, a_m1, a_m2, u_b, cw8, cb, wd, x, gate)


def _final_norm_body(x_ref, g_ref, o_ref):
    x = x_ref[...]
    o_ref[...] = x * lax.rsqrt(jnp.mean(x * x, axis=-1, keepdims=True) + RMS_EPS) * g_ref[...]


def _final_norm(x, g, tm):
    r, d = x.shape
    return pl.pallas_call(
        _final_norm_body,
        out_shape=jax.ShapeDtypeStruct((r, d), f32),
        grid=(r // tm,),
        in_specs=[pl.BlockSpec((tm, d), lambda i: (i, 0)), _const_spec((1, d))],
        out_specs=pl.BlockSpec((tm, d), lambda i: (i, 0)),
        compiler_params=_cparams(("parallel",)),
        name="final_rmsnorm",
    )(x, g.reshape(1, d))


def _page_specs(n_pages, layer_base, per_step=1, k=0):
    return [pl.BlockSpec((None, 512, PAGE_SIZE),
                         functools.partial(lambda s, pt, j: (layer_base + pt[(s * per_step + k) * n_pages + j], 0, 0),
                                           j=j))
            for j in range(n_pages)]


def _dec_const(shape):
    nd = len(shape)
    return pl.BlockSpec(shape, lambda s, pt: (0,) * nd, pipeline_mode=pl.Buffered(1))


def _head_rows(q_row):
    row = lax.broadcasted_iota(jnp.int32, (8, LANES), 0)
    lane = lax.broadcasted_iota(jnp.int32, (8, LANES), 1)
    q8 = jnp.broadcast_to(q_row, (8, q_row.shape[1]))
    qsel = jnp.zeros((8, LANES), f32)
    for c in range(NSA_HEADS // 2):
        qsel = qsel + jnp.where(row // 2 == c, q8[:, LANES * c:LANES * (c + 1)], 0.0)
    swap = (row % 2) != (row // HPG)
    qm = jnp.where(swap, pltpu.roll(qsel, HEAD_DIM, 1), qsel)
    return jnp.where(lane // HEAD_DIM == row // HPG, qm, 0.0), swap


def _rank_select(score_row, k):
    a = jnp.broadcast_to(score_row, (LANES, LANES))
    b = a.T
    ii = lax.broadcasted_iota(jnp.int32, (LANES, LANES), 0)
    jj = lax.broadcasted_iota(jnp.int32, (LANES, LANES), 1)
    ahead = (b > a) | ((b == a) & (ii < jj))
    rank = jnp.sum(jnp.where(ahead, 1.0, 0.0), axis=0, keepdims=True)
    return jnp.where(rank < k, 1.0, 0.0)


def _nsa_dec_body(pt_ref, q_ref, gn_ref, new_ref, wnew_ref, sw_ref, pe_ref, w1_ref, w2_ref, agg_ref, e_ref,
                  *rest, n_pages, per_step):
    pages, o_ref = rest[:n_pages * per_step], rest[n_pages * per_step]
    kc_scr, vc_scr = rest[n_pages * per_step + 1:]
    past = n_pages * PAGE_SIZE
    for j in range(n_pages * per_step):
        kc_scr[PAGE_SIZE * j:PAGE_SIZE * (j + 1), :] = pages[j][0:LANES, :].T
        vc_scr[PAGE_SIZE * j:PAGE_SIZE * (j + 1), :] = pages[j][LANES:2 * LANES, :].T

    nc = past // CMP_STRIDE
    cmp_all = _cmp_core((kc_scr, vc_scr), pe_ref, w1_ref, w2_ref, nc * per_step)
    for k in range(per_step):
        o_ref[k] = _nsa_dec_one(q_ref[k], gn_ref[k], new_ref[k], wnew_ref[k], sw_ref.at[k],
                                cmp_all[nc * k:nc * (k + 1)], pages[n_pages * k:n_pages * (k + 1)], agg_ref, e_ref)


def _nsa_dec_one(q_row, gn_row, new_row, wnew, sw_ref, cmpv, pages, agg_ref, e_ref):
    n_pages = len(pages)
    past = n_pages * PAGE_SIZE
    n_cmp = (past + 1 - CMP_LEN) // CMP_STRIDE + 1
    row = lax.broadcasted_iota(jnp.int32, (8, LANES), 0)
    lane = lax.broadcasted_iota(jnp.int32, (8, LANES), 1)
    qm, swap = _head_rows(q_row)
    qmb = qm.astype(bf16)
    slope = jnp.zeros((8, 1), f32)
    row1 = lax.broadcasted_iota(jnp.int32, (8, 1), 0)
    for h in range(NSA_HEADS):
        slope = jnp.where(row1 == h, NSA_SLOPES[h], slope)
    grp0 = row < HPG

    d_c = past - (lane[0:1] * CMP_STRIDE + CMP_LEN - 1)
    s_c = _nt(qmb, cmpv[:, 0:LANES].astype(bf16)) - slope * d_c.astype(f32)
    p_c = _ref_softmax(s_c, (d_c >= 0) & (lane[0:1] < n_cmp))
    o_c = _dot(p_c.astype(bf16), cmpv[:, LANES:2 * LANES].astype(bf16))
    aggb = agg_ref[...].astype(bf16)
    p_hi = p_c.astype(bf16)
    p_lo = (p_c - p_hi.astype(f32)).astype(bf16)
    imp_rows = _dot(p_hi, aggb) + _dot(p_lo, aggb)

    cur = past // SEL_BLOCK
    blk = lane[0:1]
    forced = (blk == 0) | (blk == cur) | (blk == cur - 1)
    causal = blk <= cur
    notsel_g = []
    for g in range(NSA_KV):
        imp = jnp.sum(imp_rows[HPG * g:HPG * (g + 1)], axis=0, keepdims=True)
        score = jnp.where(causal, jnp.where(forced, FORCE, imp), NEG)
        sel = jnp.where(causal, _rank_select(score, min(SEL_TOPN, -(-(past + 1) // SEL_BLOCK))), 0.0)
        notsel_g.append(jnp.broadcast_to(1.0 - sel, (8, LANES)))
    notsel = jnp.where(grp0, notsel_g[0], notsel_g[1])

    dist = past - lax.broadcasted_iota(jnp.int32, (1, past), 1)
    picked = _dot(notsel.astype(bf16), e_ref[...]) < 0.5
    ks_t = jnp.concatenate([pages[j][2 * LANES:3 * LANES, :].astype(bf16) for j in range(n_pages)], axis=1)
    vs_t = jnp.concatenate([pages[j][3 * LANES:4 * LANES, :].astype(bf16) for j in range(n_pages)], axis=1)
    s_s = jnp.where(picked, _dot(qmb, ks_t) - slope * dist.astype(f32), NEG)
    s_n = jnp.sum(qm * new_row[:, 2 * LANES:3 * LANES], axis=-1, keepdims=True)
    m_s = jnp.maximum(jnp.max(s_s, axis=-1, keepdims=True), s_n)
    e_s = jnp.where(picked, jnp.exp(s_s - m_s), 0.0)
    e_n = jnp.exp(s_n - m_s)
    den = jnp.maximum(jnp.sum(e_s, axis=-1, keepdims=True) + e_n, 1e-30)
    o_s = (e_n * new_row[:, 3 * LANES:4 * LANES] + _nt(e_s.astype(bf16), vs_t)) / den

    wb = sw_ref.shape[1]
    d_w = wb - lax.broadcasted_iota(jnp.int32, (1, wb), 1)
    s_w = _dot(qmb, sw_ref[0:LANES, :].astype(bf16)) - slope * d_w.astype(f32)
    valid_w = (d_w < WINDOW) & (d_w >= 0)
    s_w = jnp.where(valid_w, s_w, NEG)
    s_n = jnp.sum(qm * wnew[:, 0:LANES], axis=-1, keepdims=True)
    m_w = jnp.maximum(jnp.max(s_w, axis=-1, keepdims=True), s_n)
    e_w = jnp.where(valid_w, jnp.exp(s_w - m_w), 0.0)
    e_n = jnp.exp(s_n - m_w)
    den = jnp.maximum(jnp.sum(e_w, axis=-1, keepdims=True) + e_n, 1e-30)
    o_w = (_nt(e_w.astype(bf16), sw_ref[LANES:2 * LANES, :].astype(bf16)) + e_n * wnew[:, LANES:2 * LANES]) / den

    sig = jnp.broadcast_to(jax.nn.sigmoid(gn_row), (8, LANES))
    gates = [jnp.sum(jnp.where(lane == br * NSA_HEADS + row, sig, 0.0), axis=-1, keepdims=True) for br in range(3)]
    o = gates[0] * o_c + gates[1] * o_s + gates[2] * o_w
    o = jnp.where(swap, pltpu.roll(o, HEAD_DIM, 1), o)
    o = jnp.where(lane // HEAD_DIM == row % 2, o, 0.0)
    return jnp.concatenate([o[2 * c:2 * c + 1] + o[2 * c + 1:2 * c + 2] for c in range(NSA_HEADS // 2)], axis=1)


def _nsa_decode(page_table, cache_t, layer_base, q, g_n, nsa_new, win_new, state_win_t, win_base, pe4, w1bd, w2bd,
                agg, e_sel, per_step):
    s, n_pages = page_table.shape
    past = n_pages * PAGE_SIZE
    consts = (pe4, w1bd, w2bd, agg, e_sel)
    seqs = lambda width: pl.BlockSpec((per_step, 1, width), lambda si, pt: (si, 0, 0))
    page_specs = [spec for k in range(per_step) for spec in _page_specs(n_pages, layer_base, per_step, k)]
    grid_spec = pltpu.PrefetchScalarGridSpec(
        num_scalar_prefetch=1,
        grid=(s // per_step,),
        in_specs=[seqs(512), seqs(LANES), seqs(512), seqs(256),
                  pl.BlockSpec((per_step,) + state_win_t.shape[1:],
                               lambda si, pt: (win_base // per_step + si, 0, 0))]
        + [_dec_const(c.shape) for c in consts] + page_specs,
        out_specs=seqs(512),
        scratch_shapes=[pltpu.VMEM((per_step * past, LANES), f32), pltpu.VMEM((per_step * past, LANES), f32)],
    )
    return pl.pallas_call(
        functools.partial(_nsa_dec_body, n_pages=n_pages, per_step=per_step),
        out_shape=jax.ShapeDtypeStruct((s, 1, 512), f32),
        grid_spec=grid_spec,
        compiler_params=_cparams(("parallel",)),
        name="nsa_attention_decode",
    )(page_table.reshape(-1), q, g_n, nsa_new, win_new, state_win_t, *consts,
      *([cache_t] * (n_pages * per_step)))


def _win_shift_body(sw_ref, new_ref, o_ref):
    n, feat, wb = sw_ref.shape
    lane = lax.broadcasted_iota(jnp.int32, (feat, wb), 1)
    for k in range(n):
        col = jnp.broadcast_to(new_ref[k], (LANES, feat)).T
        col = jnp.concatenate([col] * (wb // LANES), axis=1)
        o_ref[k] = jnp.where(lane == wb - 1, col, pltpu.roll(sw_ref[k], wb - 1, 1))


def _win_shift(state_win_t, new_rows, per_step):
    n, feat, wb = state_win_t.shape
    return pl.pallas_call(
        _win_shift_body,
        out_shape=jax.ShapeDtypeStruct((n, feat, wb), f32),
        grid=(n // per_step,),
        in_specs=[pl.BlockSpec((per_step, feat, wb), lambda i: (i, 0, 0)),
                  pl.BlockSpec((per_step, 1, feat), lambda i: (i, 0, 0))],
        out_specs=pl.BlockSpec((per_step, feat, wb), lambda i: (i, 0, 0)),
        compiler_params=_cparams(("parallel",)),
        name="window_state_shift",
    )(state_win_t, new_rows)


def _q_head_rows4(q_row):
    row = lax.broadcasted_iota(jnp.int32, (8, 256), 0)
    lane = lax.broadcasted_iota(jnp.int32, (8, 256), 1)
    own = lane // HEAD_DIM == row
    return jnp.where(own, jnp.broadcast_to(q_row, (8, 256)), 0.0), own


def _moba_dec_body(pt_ref, q_ref, new_ref, seg_ref, e_ref, *rest, n_pages, per_step):
    pages, o_ref = rest[:n_pages * per_step], rest[n_pages * per_step]
    for k in range(per_step):
        o_ref[k] = _moba_dec_one(q_ref[k], new_ref[k], pages[n_pages * k:n_pages * (k + 1)], seg_ref, e_ref)


def _moba_dec_one(q_row, new_row, pages, seg_ref, e_ref):
    n_pages = len(pages)
    past = n_pages * PAGE_SIZE
    nb_past = past // MOBA_BLOCK
    kw = MOBA_HEADS * HEAD_DIM
    qm, own_lanes = _q_head_rows4(q_row)
    qmb = qm.astype(bf16)
    k_t = jnp.concatenate([pages[j][0:kw, :].astype(bf16) for j in range(n_pages)], axis=1)
    v_t = jnp.concatenate([pages[j][kw:2 * kw, :].astype(bf16) for j in range(n_pages)], axis=1)
    raw = _dot(qmb, k_t)
    raw_hi = raw.astype(bf16)
    raw_lo = (raw - raw_hi.astype(f32)).astype(bf16)
    gate = _dot(raw_hi, seg_ref[...]) + _dot(raw_lo, seg_ref[...])

    lane = lax.broadcasted_iota(jnp.int32, (8, LANES), 1)
    is_past = lane < nb_past
    score = jnp.where(is_past, gate, NEG)
    rank = jnp.zeros((8, LANES), f32)
    for i in range(nb_past):
        gi = score[:, i:i + 1]
        rank = rank + jnp.where((gi > score) | ((gi == score) & (i < lane)), 1.0, 0.0)
    sel = jnp.where(is_past & (rank < min(MOBA_TOPK, nb_past)), 1.0, 0.0)
    picked = _dot(sel.astype(bf16), e_ref[...]) > 0.5

    row1 = lax.broadcasted_iota(jnp.int32, (8, 1), 0)
    slope = jnp.zeros((8, 1), f32)
    for h in range(MOBA_HEADS):
        slope = jnp.where(row1 == h, MOBA_SLOPES[h], slope)
    dist = past - lax.broadcasted_iota(jnp.int32, (1, past), 1)
    s = jnp.where(picked, raw - slope * dist.astype(f32), NEG)
    s_n = jnp.sum(qm * new_row[:, 0:kw], axis=-1, keepdims=True)
    m = jnp.maximum(jnp.max(s, axis=-1, keepdims=True), s_n)
    e = jnp.where(picked, jnp.exp(s - m), 0.0)
    e_n = jnp.exp(s_n - m)
    den = jnp.maximum(jnp.sum(e, axis=-1, keepdims=True) + e_n, 1e-30)
    o = (e_n * new_row[:, kw:2 * kw] + _nt(e.astype(bf16), v_t)) / den
    return jnp.sum(jnp.where(own_lanes, o, 0.0), axis=0, keepdims=True)


def _seqs_spec(per_step, width):
    return pl.BlockSpec((per_step, 1, width), lambda si, pt: (si, 0, 0))


def _all_page_specs(n_pages, layer_base, per_step):
    return [spec for k in range(per_step) for spec in _page_specs(n_pages, layer_base, per_step, k)]


def _moba_decode(page_table, cache_t, layer_base, q, moba_new, seg_mean, e_blk, per_step):
    s, n_pages = page_table.shape
    grid_spec = pltpu.PrefetchScalarGridSpec(
        num_scalar_prefetch=1,
        grid=(s // per_step,),
        in_specs=[_seqs_spec(per_step, 256), _seqs_spec(per_step, 512), _dec_const(seg_mean.shape),
                  _dec_const(e_blk.shape)] + _all_page_specs(n_pages, layer_base, per_step),
        out_specs=_seqs_spec(per_step, 256),
    )
    return pl.pallas_call(
        functools.partial(_moba_dec_body, n_pages=n_pages, per_step=per_step),
        out_shape=jax.ShapeDtypeStruct((s, 1, 256), f32),
        grid_spec=grid_spec,
        compiler_params=_cparams(("parallel",)),
        name="moba_attention_decode",
    )(page_table.reshape(-1), q, moba_new, seg_mean, e_blk, *([cache_t] * (n_pages * per_step)))


def _sb_dec_body(pt_ref, q_ref, uu_ref, *rest, n_pages, per_step):
    pages, o_ref = rest[:n_pages * per_step], rest[n_pages * per_step]
    for k in range(per_step):
        o_ref[k] = _sb_dec_one(q_ref[k], pages[n_pages * k:n_pages * (k + 1)], uu_ref)


def _sb_dec_one(q_row, pages, uu_ref):
    n_pages = len(pages)
    kw = SB_HEADS * HEAD_DIM
    qm, own_lanes = _q_head_rows4(q_row)
    qmb = qm.astype(bf16)
    k_t = jnp.concatenate([pages[j][0:kw, :].astype(bf16) for j in range(n_pages)], axis=1)
    v_t = jnp.concatenate([pages[j][kw:2 * kw, :].astype(bf16) for j in range(n_pages)], axis=1)
    z = _dot(qmb, k_t)
    lk = _log_keep(z)
    stacked = jnp.concatenate([lk[:, PAGE_SIZE * j:PAGE_SIZE * (j + 1)] for j in range(n_pages)], axis=0)
    later, total = _suffix_sums(stacked, uu_ref[...])
    carry = jnp.zeros((8, LANES), f32)
    between = [None] * n_pages
    for j in reversed(range(n_pages)):
        between[j] = later[8 * j:8 * (j + 1)] + carry
        carry = carry + total[8 * j:8 * (j + 1)]
    w = jnp.exp(z + lk + jnp.concatenate(between, axis=1))
    acc = _nt(w.astype(bf16), v_t)
    return jnp.sum(jnp.where(own_lanes, acc, 0.0), axis=0, keepdims=True)


def _sb_decode(page_table, cache_t, layer_base, q, uu, per_step):
    s, n_pages = page_table.shape
    grid_spec = pltpu.PrefetchScalarGridSpec(
        num_scalar_prefetch=1,
        grid=(s // per_step,),
        in_specs=[_seqs_spec(per_step, 256), _dec_const(uu.shape)] + _all_page_specs(n_pages, layer_base, per_step),
        out_specs=_seqs_spec(per_step, 256),
    )
    return pl.pallas_call(
        functools.partial(_sb_dec_body, n_pages=n_pages, per_step=per_step),
        out_shape=jax.ShapeDtypeStruct((s, 1, 256), f32),
        grid_spec=grid_spec,
        compiler_params=_cparams(("parallel",)),
        name="stickbreak_attention_decode",
    )(page_table.reshape(-1), q, uu, *([cache_t] * (n_pages * per_step)))


def _agg_matrix(nc, n_cmp):
    c0 = np.arange(nc)[:, None] * CMP_STRIDE
    s0 = np.arange(LANES)[None, :] * SEL_BLOCK
    ov = np.clip(np.minimum(c0 + CMP_LEN, s0 + SEL_BLOCK) - np.maximum(c0, s0), 0, None) / CMP_LEN
    ov[n_cmp:] = 0.0
    return jnp.asarray(ov, f32)


def _block_onehot_t(block, l):
    e = (np.arange(l)[None, :] // block) == np.arange(LANES)[:, None]
    return jnp.asarray(np.where(e, -MASK_BIG, 0.0), bf16)


def _expand_matrix(block, kp):
    e = (np.arange(kp)[None, :] // block) == np.arange(LANES)[:, None]
    return jnp.asarray(e, bf16)


def _suffix_matrix():
    j = np.arange(2 * LANES)[:, None] % LANES
    s = np.arange(2 * LANES)[None, :]
    return jnp.asarray((s >= LANES) | (j > s), bf16)


def _layer_weights(l, w_in, cmp_pe, cmp_w1, cmp_w2, w_br_a, w_br_b, w_br_c, w_o, w_up, conv_w, conv_b, w_down):
    d = w_in.shape[1]
    w = w_in[l]
    w_proj = jnp.concatenate([w[:, 0:1280], w[:, 1304:], w[:, 1280:1304], jnp.zeros((d, _PROJ_W - 5912), f32)],
                             axis=1).astype(bf16)
    pe4 = jnp.concatenate([cmp_pe[l], cmp_pe[l]], axis=2)
    w1 = cmp_w1[l].reshape(2, CMP_LEN, HEAD_DIM, CMP_HID)
    w1bd = jnp.zeros((2, CMP_LEN, NSA_KV * HEAD_DIM, NSA_KV * CMP_HID), f32)
    w2bd = jnp.zeros((2, NSA_KV * CMP_HID, NSA_KV * HEAD_DIM), f32)
    for g in range(NSA_KV):
        w1bd = w1bd.at[:, :, HEAD_DIM * g:HEAD_DIM * (g + 1), CMP_HID * g:CMP_HID * (g + 1)].set(w1)
        w2bd = w2bd.at[:, CMP_HID * g:CMP_HID * (g + 1), HEAD_DIM * g:HEAD_DIM * (g + 1)].set(cmp_w2[l])
    cw8 = jnp.concatenate([conv_w[l], jnp.zeros((8 - CONV_W, conv_w.shape[2]), f32)], axis=0)
    return dict(w_proj=w_proj, pe4=pe4, w1bd=w1bd.astype(bf16), w2bd=w2bd.astype(bf16),
                wa=w_br_a[l].astype(bf16), wb=w_br_b[l].astype(bf16), wc=w_br_c[l].astype(bf16),
                wo=w_o[l].astype(bf16), w_up=w_up[l].astype(bf16), cw8=cw8, cb=conv_b[l].reshape(1, -1),
                wd=w_down[l].astype(bf16))


def _mod_parts(mod_rows, per_row):
    r = mod_rows.shape[0]
    parts = mod_rows.reshape(r, 6, D_MODEL)
    return [parts[:, k].reshape((1, r, D_MODEL) if per_row else (r, 1, D_MODEL)) for k in range(6)]


def kernel(x_prompt, x_sample, cache_nsa, cache_moba, cache_sb, state_win, state_conv, page_table, c_prompt,
           c_sample, norm1_g, norm2_g, w_ada, b_ada, w_in, cmp_pe, cmp_w1, cmp_w2, w_br_a, w_br_b, w_br_c, w_o,
           w_up, conv_w, conv_b, w_down, final_g):
    b, t, d = x_prompt.shape
    s = x_sample.shape[0]
    depth = w_in.shape[0]
    n_phys = cache_nsa.shape[1]
    n_pages = page_table.shape[1]
    past = n_pages * PAGE_SIZE
    tm = 512
    tq = 128
    tk = 512

    n_c = b + s
    c_all = jnp.concatenate([c_prompt, c_sample, jnp.zeros((-n_c % 8, d), f32)], axis=0)
    mod = _ada_mod(c_all, w_ada, b_ada)

    nc_p = t // CMP_STRIDE
    n_cmp_p = (t - CMP_LEN) // CMP_STRIDE + 1
    agg_p = _agg_matrix(nc_p, n_cmp_p)
    nc_s = past // CMP_STRIDE
    agg_s = _agg_matrix(nc_s, (past + 1 - CMP_LEN) // CMP_STRIDE + 1)
    e_sel = _expand_matrix(SEL_BLOCK, past)
    e_blk = _expand_matrix(MOBA_BLOCK, past)
    seg_mean = (jnp.swapaxes(e_blk, 0, 1).astype(f32) * (1.0 / MOBA_BLOCK)).astype(bf16)
    uu = _suffix_matrix()
    nb_pad = LANES // MOBA_HEADS

    feat_major = lambda a: jnp.transpose(a, (0, 1, 3, 4, 5, 2))
    caches = [feat_major(c).reshape(depth * n_phys, 512, PAGE_SIZE) for c in (cache_nsa, cache_moba, cache_sb)]
    state_win_t = feat_major(state_win).reshape(depth * s, 256, state_win.shape[2])

    xp = x_prompt.reshape(b * t, d)
    xs = x_sample.reshape(s, d)
    outs_p = [[] for _ in range(5)]
    outs_s = [[] for _ in range(5)]
    for l in range(depth):
        w = _layer_weights(l, w_in, cmp_pe, cmp_w1, cmp_w2, w_br_a, w_br_b, w_br_c, w_o, w_up, conv_w, conv_b,
                           w_down)
        mp = _mod_parts(mod[l, 0:b], per_row=False)
        ms = _mod_parts(mod[l, b:b + s], per_row=True)

        (q_n, nsa_rows, nsa_bf, win_rows, win_bf, mb_q, moba_rows, moba_bf, sb_q, sb_rows, sb_bf, g_m, g_n) = \
            _norm_mod_matmul(xp, norm1_g[l], mp[0], mp[1], w["w_proj"], _IN_SEGS, _IN_DTYPES, tm, "in_proj_prompt")
        r3 = lambda a: a.reshape(b, t, a.shape[1])
        cmp = _cmp_prompt(r3(nsa_rows), w["pe4"], w["w1bd"], w["w2bd"])
        o_a = _nsa_prompt(r3(q_n), r3(g_n), cmp, agg_p, r3(nsa_bf), r3(win_bf), n_cmp_p, tq, tk)
        kmean = _block_mean(r3(moba_rows), 256, MOBA_BLOCK).reshape(b, t // MOBA_BLOCK, MOBA_HEADS, HEAD_DIM)
        kmt = jnp.zeros((b, MOBA_HEADS, nb_pad, MOBA_HEADS, HEAD_DIM), f32)
        for h in range(MOBA_HEADS):
            kmt = kmt.at[:, h, 0:t // MOBA_BLOCK, h].set(kmean[:, :, h])
        o_b = _moba_prompt(r3(mb_q), kmt.reshape(b, LANES, 256), r3(moba_bf), tq, 2 * tk)
        o_c = _sb_prompt(r3(sb_q), uu, r3(sb_bf), 2 * tq, tk)
        xp = _merge_out(o_a.reshape(b * t, -1), o_b.reshape(b * t, -1), o_c.reshape(b * t, -1), g_m, xp, mp[2],
                        w["wa"], w["wb"], w["wc"], w["wo"], tm)
        u_a, u_b = _norm_mod_matmul(xp, norm2_g[l], mp[3], mp[4], w["w_up"], _UP_SEGS, _UP_DTYPES, tm,
                                    "ffn_up_prompt")
        xp = _ffn_down_seq(u_a, u_b, w["cw8"], w["cb"], w["wd"], xp, mp[5], tm)
        keep = min(WINDOW, t)
        outs_p[0].append(nsa_rows.reshape(b, t // PAGE_SIZE, PAGE_SIZE, 4, NSA_KV, HEAD_DIM))
        outs_p[1].append(moba_rows.reshape(b, t // PAGE_SIZE, PAGE_SIZE, 2, MOBA_HEADS, HEAD_DIM))
        outs_p[2].append(sb_rows.reshape(b, t // PAGE_SIZE, PAGE_SIZE, 2, SB_HEADS, HEAD_DIM))
        outs_p[3].append(r3(win_rows)[:, t - keep:].reshape(b, keep, 2, NSA_KV, HEAD_DIM))
        outs_p[4].append(r3(u_a)[:, t - (CONV_W - 1):])

        (q_n, nsa_rows, _, win_rows, _, mb_q, moba_rows, _, sb_q, sb_rows, _, g_m, g_n) = \
            _norm_mod_matmul(xs, norm1_g[l], ms[0], ms[1], w["w_proj"], _IN_SEGS, _IN_DTYPES, s, "in_proj_sample")
        s3 = lambda a: a.astype(f32).reshape(s, 1, a.shape[1])
        base = l * n_phys
        o_a = _nsa_decode(page_table, caches[0], base, s3(q_n), s3(g_n), s3(nsa_rows), s3(win_rows), state_win_t,
                          l * s, w["pe4"], w["w1bd"], w["w2bd"], agg_s, e_sel, 2)
        o_b = _moba_decode(page_table, caches[1], base, s3(mb_q), s3(moba_rows), seg_mean, e_blk, 4)
        o_c = _sb_decode(page_table, caches[2], base, s3(sb_q), uu, 4)
        xs = _merge_out(o_a.reshape(s, -1).astype(bf16), o_b.reshape(s, -1).astype(bf16),
                        o_c.reshape(s, -1).astype(bf16), g_m, xs, ms[2], w["wa"], w["wb"], w["wc"], w["wo"], s)
        u_a, u_b = _norm_mod_matmul(xs, norm2_g[l], ms[3], ms[4], w["w_up"], _UP_SEGS, _UP_DTYPES, s,
                                    "ffn_up_sample")
        xs = _ffn_down_step(u_a, state_conv[l, :, 1], state_conv[l, :, 0], u_b, w["cw8"], w["cb"], w["wd"], xs,
                            ms[5])
        conv_full = jnp.concatenate([state_conv[l], u_a.reshape(s, 1, -1)], axis=1)
        outs_s[0].append(nsa_rows.reshape(s, 1, 4, NSA_KV, HEAD_DIM))
        outs_s[1].append(moba_rows.reshape(s, 1, 2, MOBA_HEADS, HEAD_DIM))
        outs_s[2].append(sb_rows.reshape(s, 1, 2, SB_HEADS, HEAD_DIM))
        outs_s[3].append(win_rows.reshape(s, 1, 256))
        outs_s[4].append(conv_full[:, conv_full.shape[1] - (CONV_W - 1):])

    y_prompt = _final_norm(xp, final_g, tm).reshape(b, t, d)
    y_sample = _final_norm(xs, final_g, s).reshape(s, 1, d)
    wb = state_win.shape[2]
    win_s = _win_shift(state_win_t, jnp.concatenate(outs_s[3], axis=0), 4)
    win_s = jnp.transpose(win_s.reshape(depth, s, 2, NSA_KV, HEAD_DIM, wb), (0, 1, 5, 2, 3, 4))
    st = lambda lst: jnp.stack(lst)
    return (y_prompt, y_sample, st(outs_p[0]), st(outs_s[0]), st(outs_p[1]), st(outs_s[1]), st(outs_p[2]),
            st(outs_s[2]), st(outs_p[3]), win_s, st(outs_p[4]), st(outs_s[4]))
```
